```python
import jax, jax.numpy as jnp
from jax import lax
import numpy as np

D_MODEL = 1024
BATCH = 2
SEQ = 8192
DEPTH = 2

N_A_LAYERS = DEPTH // 2
N_B_LAYERS = DEPTH - N_A_LAYERS

RET_HEADS = 4
RET_DK = D_MODEL // 8
RET_DV = 2 * RET_DK
RET_QK_W = RET_HEADS * RET_DK
RET_V_W = RET_HEADS * RET_DV
RET_CHUNK = 128
ROPE_BASE = 10000.0

SB_HEADS = 8
SB_DH = D_MODEL // SB_HEADS
SB_W = SB_HEADS * SB_DH
SB_BLOCK = 128

MEM_LEN = 256
MEM_HEADS = 4
MEM_DH = D_MODEL // 8
MEM_W = MEM_HEADS * MEM_DH

A_IN_W = RET_QK_W + RET_QK_W + RET_V_W + RET_V_W + MEM_W + MEM_W
A_OUT_IN = RET_V_W + MEM_W
B_IN_W = SB_W + SB_W + MEM_W + MEM_W
B_OUT_IN = SB_W + MEM_W

EPS = 1e-6

kernel_name = 'yoco_retention_stickbreaking_memory_trunk'


def rms_norm(x, g):
    xf = x.astype(jnp.float32)
    y = xf * lax.rsqrt(jnp.mean(xf * xf, axis=-1, keepdims=True) + EPS)
    return (y * g.astype(jnp.float32)).astype(x.dtype)


def split_heads(t, n_heads):
    b, s, w = t.shape
    return t.reshape(b, s, n_heads, w // n_heads).transpose(0, 2, 1, 3)


def merge_heads(t):
    b, h, s, d = t.shape
    return t.transpose(0, 2, 1, 3).reshape(b, s, h * d)


def apply_rotary(t, positions):
    d = t.shape[-1]
    inv_freq = ROPE_BASE ** (-jnp.arange(0, d // 2, dtype=jnp.float32) * 2.0 / d)
    ang = positions.astype(jnp.float32)[:, None, :, None] * inv_freq
    cos, sin = jnp.cos(ang), jnp.sin(ang)
    tf = t.astype(jnp.float32)
    t1, t2 = tf[..., : d // 2], tf[..., d // 2:]
    return jnp.concatenate([t1 * cos - t2 * sin, t1 * sin + t2 * cos], axis=-1)


def retention_chunkwise(q, k, v):
    b, h, s, dk = q.shape
    dv = v.shape[-1]
    c = RET_CHUNK
    nc = s // c
    gamma = (1.0 - np.exp2(-5.0 - np.arange(h))).astype(np.float32)
    lg = jnp.log(jnp.asarray(gamma))
    k = k * (dk ** -0.5)
    qc = q.reshape(b, h, nc, c, dk)
    kc = k.reshape(b, h, nc, c, dk)
    vc = v.astype(jnp.float32).reshape(b, h, nc, c, dv)
    idx = jnp.arange(c, dtype=jnp.float32)
    rel = idx[:, None] - idx[None, :]
    decay = jnp.where(rel[None] >= 0, jnp.exp(jnp.maximum(rel, 0.0)[None] * lg[:, None, None]), 0.0)
    scores = jnp.einsum('bhncd,bhnmd->bhncm', qc, kc) * decay[None, :, None]
    intra = jnp.einsum('bhncm,bhnme->bhnce', scores, vc)
    zeta = jnp.exp((c - 1.0 - idx)[None, :] * lg[:, None])
    chunk_kv = jnp.einsum('bhnmd,bhnme->bhnde', kc * zeta[None, :, None, :, None], vc)
    gamma_c = jnp.exp(c * lg)[None, :, None, None]

    def step(state, kv_n):
        return gamma_c * state + kv_n, state

    _, prev = lax.scan(step, jnp.zeros((b, h, dk, dv), jnp.float32), chunk_kv.transpose(2, 0, 1, 3, 4))
    prev = prev.transpose(1, 2, 0, 3, 4)
    xi = jnp.exp((idx + 1.0)[None, :] * lg[:, None])
    cross = jnp.einsum('bhncd,bhnde->bhnce', qc, prev) * xi[None, :, None, :, None]
    return (intra + cross).reshape(b, h, s, dv)


def stick_breaking_attention(q, k, v):
    b, h, s, d = q.shape
    bq = SB_BLOCK
    nb = s // bq
    scale = d ** -0.5
    q_blocks = q.reshape(b, h, nb, bq, d).transpose(2, 0, 1, 3, 4)
    ar = jnp.arange(bq)

    def per_block(args):
        qb, qi = args
        t_idx = qi * bq + ar

        def body(n, carry):
            acc, logsum = carry
            start = (qi - n) * bq
            kb = lax.dynamic_slice_in_dim(k, start, bq, axis=2)
            vb = lax.dynamic_slice_in_dim(v, start, bq, axis=2)
            z = jnp.einsum('bhqd,bhkd->bhqk', qb, kb).astype(jnp.float32) * scale
            mask = (start + ar)[None, :] < t_idx[:, None]
            l_fail = jnp.where(mask, jax.nn.log_sigmoid(-z), 0.0)
            suffix = lax.cumsum(l_fail, axis=3, reverse=True) - l_fail
            log_a = jax.nn.log_sigmoid(z) + suffix + logsum[..., None]
            a = jnp.where(mask, jnp.exp(log_a), 0.0)
            acc = acc + jnp.einsum('bhqk,bhkd->bhqd', a, vb.astype(jnp.float32))
            return acc, logsum + jnp.sum(l_fail, axis=-1)

        init = (jnp.zeros((b, h, bq, d), jnp.float32), jnp.zeros((b, h, bq), jnp.float32))
        acc, _ = lax.fori_loop(0, qi + 1, body, init)
        return acc

    out = lax.map(per_block, (q_blocks, jnp.arange(nb)))
    return out.transpose(1, 2, 0, 3, 4).reshape(b, h, s, d).astype(q.dtype)


def memory_kv(mem, g_mem_l, w_mem_kv_l):
    kv = rms_norm(mem, g_mem_l) @ w_mem_kv_l
    return split_heads(kv[..., :MEM_W], MEM_HEADS), split_heads(kv[..., MEM_W:], MEM_HEADS)


def memory_attention(q, mk, mv, g_q, g_k):
    qh = rms_norm(split_heads(q, MEM_HEADS), g_q).astype(jnp.float32)
    kh = rms_norm(mk, g_k).astype(jnp.float32)
    logits = jnp.einsum('bhsd,bhmd->bhsm', qh, kh) * (MEM_DH ** -0.5)
    p = jax.nn.softmax(logits, axis=-1)
    o = jnp.einsum('bhsm,bhmd->bhsd', p, mv.astype(jnp.float32))
    return merge_heads(o).astype(q.dtype)


def setup_inputs(seed: int = 0) -> dict:
    key = jax.random.key(seed)
    ks = jax.random.split(key, 20)

    def w(k, shape, fan_in):
        return jax.random.normal(k, shape, jnp.float32) * (fan_in ** -0.5)

    def gain(k, shape):
        return 1.0 + 0.02 * jax.random.normal(k, shape, jnp.float32)

    x = jax.random.normal(ks[0], (BATCH, SEQ, D_MODEL), jnp.float32)
    mem = jax.random.normal(ks[1], (BATCH, MEM_LEN, D_MODEL), jnp.float32)
    offset = jax.random.randint(ks[2], (BATCH, 1), 0, 1024, dtype=jnp.int32)
    positions = (jnp.arange(SEQ, dtype=jnp.int32)[None, :] + offset).astype(jnp.int32)
    return {
        'x': x,
        'mem': mem,
        'positions': positions,
        'g_norm_a': gain(ks[3], (N_A_LAYERS, D_MODEL)),
        'w_in_a': w(ks[4], (N_A_LAYERS, D_MODEL, A_IN_W), D_MODEL),
        'g_ret_head': gain(ks[5], (N_A_LAYERS, RET_DV)),
        'w_out_a': w(ks[6], (N_A_LAYERS, A_OUT_IN, D_MODEL), A_OUT_IN),
        'g_kv': gain(ks[7], (D_MODEL,)),
        'w_kv': w(ks[8], (D_MODEL, 2 * SB_W), D_MODEL),
        'g_norm_b': gain(ks[9], (N_B_LAYERS, D_MODEL)),
        'w_in_b': w(ks[10], (N_B_LAYERS, D_MODEL, B_IN_W), D_MODEL),
        'w_out_b': w(ks[11], (N_B_LAYERS, B_OUT_IN, D_MODEL), B_OUT_IN),
        'g_mem': gain(ks[12], (DEPTH, D_MODEL)),
        'w_mem_kv': w(ks[13], (DEPTH, D_MODEL, 2 * MEM_W), D_MODEL),
        'g_mem_q': gain(ks[14], (DEPTH, MEM_DH)),
        'g_mem_k': gain(ks[15], (DEPTH, MEM_DH)),
    }


def reference(x, mem, positions, g_norm_a, w_in_a, g_ret_head, w_out_a, g_kv, w_kv,
              g_norm_b, w_in_b, w_out_b, g_mem, w_mem_kv, g_mem_q, g_mem_k):
    k_shared = None
    v_shared = None
    for layer in range(DEPTH):
        mk, mv = memory_kv(mem, g_mem[layer], w_mem_kv[layer])
        if layer < N_A_LAYERS:
            la = layer
            u = rms_norm(x, g_norm_a[la]) @ w_in_a[la]
            o1 = RET_QK_W
            o2 = o1 + RET_QK_W
            o3 = o2 + RET_V_W
            o4 = o3 + RET_V_W
            o5 = o4 + MEM_W
            q = apply_rotary(split_heads(u[..., :o1], RET_HEADS), positions)
            k = apply_rotary(split_heads(u[..., o1:o2], RET_HEADS), positions)
            v = split_heads(u[..., o2:o3], RET_HEADS)
            ret = rms_norm(retention_chunkwise(q, k, v), g_ret_head[la])
            ret = merge_heads(ret).astype(x.dtype) * jax.nn.silu(u[..., o3:o4])
            mo = memory_attention(u[..., o4:o5], mk, mv, g_mem_q[layer], g_mem_k[layer])
            mo = mo * jax.nn.silu(u[..., o5:])
            x = x + jnp.concatenate([ret, mo], axis=-1) @ w_out_a[la]
        else:
            lb = layer - N_A_LAYERS
            if layer == N_A_LAYERS:
                kv = rms_norm(x, g_kv) @ w_kv
                k_shared = split_heads(kv[..., :SB_W], SB_HEADS)
                v_shared = split_heads(kv[..., SB_W:], SB_HEADS)
            u = rms_norm(x, g_norm_b[lb]) @ w_in_b[lb]
            o1 = SB_W
            o2 = o1 + SB_W
            o3 = o2 + MEM_W
            q = split_heads(u[..., :o1], SB_HEADS)
            sb = merge_heads(stick_breaking_attention(q, k_shared, v_shared)) * jax.nn.silu(u[..., o1:o2])
            mo = memory_attention(u[..., o2:o3], mk, mv, g_mem_q[layer], g_mem_k[layer])
            mo = mo * jax.nn.silu(u[..., o3:])
            x = x + jnp.concatenate([sb, mo], axis=-1) @ w_out_b[lb]
    return x
```

```python
import functools
import math

import numpy as np
import jax
import jax.numpy as jnp
from jax import lax
from jax.experimental import pallas as pl
from jax.experimental.pallas import tpu as pltpu

F32 = jnp.float32
BF16 = jnp.bfloat16

D_MODEL = 1024
RET_HEADS = 4
RET_DK = 128
RET_DV = 256
RET_QK_W = RET_HEADS * RET_DK
RET_V_W = RET_HEADS * RET_DV
ROPE_BASE = 10000.0
SB_HEADS = 8
SB_DH = 128
SB_W = SB_HEADS * SB_DH
MEM_HEADS = 4
MEM_DH = 128
MEM_W = MEM_HEADS * MEM_DH
EPS = 1e-6

RET_LOG_GAMMA = tuple(
    math.log(float(np.float32(1.0 - 2.0 ** (-5.0 - h)))) for h in range(RET_HEADS))

VMEM_LIMIT_BYTES = 48 * 1024 * 1024

ROW_TILE_A = 256
ROW_TILE_B = 512
SB_TILE = 256


def _rms(x, g):
    return x * lax.rsqrt(jnp.mean(x * x, axis=-1, keepdims=True) + EPS) * g


def _silu(x):
    return x * (1.0 / (1.0 + jnp.exp(-x)))


def _dot(a, b):
    return jnp.dot(a, b, preferred_element_type=F32)


def _dot_nt(a, b):
    return lax.dot_general(a, b, (((1,), (1,)), ((), ())), preferred_element_type=F32)


def _const_spec(shape, index_map):
    return pl.BlockSpec(shape, index_map, pipeline_mode=pl.Buffered(1))


def _memkv_kernel(mem_ref, g_ref, w_ref, gk_ref, kt_ref, v_ref):
    mn = _rms(mem_ref[0], g_ref[0]).astype(BF16)
    kv = _dot(mn, w_ref[0])
    gk = gk_ref[0]
    for h in range(MEM_HEADS):
        kh = _rms(kv[:, h * MEM_DH:(h + 1) * MEM_DH], gk)
        kt_ref[0, 0, h * MEM_DH:(h + 1) * MEM_DH, :] = kh.T.astype(BF16)
    v_ref[0, 0] = kv[:, MEM_W:].astype(BF16)


def _memory_kv(mem, g_mem, w_mem_kv_bf16, g_mem_k):
    depth = g_mem.shape[0]
    batch, mem_len, _ = mem.shape
    return pl.pallas_call(
        _memkv_kernel,
        grid=(depth, batch),
        in_specs=[
            pl.BlockSpec((1, mem_len, D_MODEL), lambda l, b: (b, 0, 0)),
            pl.BlockSpec((1, 1, D_MODEL), lambda l, b: (l, 0, 0)),
            pl.BlockSpec((1, D_MODEL, 2 * MEM_W), lambda l, b: (l, 0, 0)),
            pl.BlockSpec((1, 1, MEM_DH), lambda l, b: (l, 0, 0)),
        ],
        out_specs=[
            pl.BlockSpec((1, 1, MEM_W, mem_len), lambda l, b: (l, b, 0, 0)),
            pl.BlockSpec((1, 1, mem_len, MEM_W), lambda l, b: (l, b, 0, 0)),
        ],
        out_shape=[
            jax.ShapeDtypeStruct((depth, batch, MEM_W, mem_len), BF16),
            jax.ShapeDtypeStruct((depth, batch, mem_len, MEM_W), BF16),
        ],
        compiler_params=pltpu.CompilerParams(
            dimension_semantics=("arbitrary", "arbitrary"),
            vmem_limit_bytes=VMEM_LIMIT_BYTES),
        name="memkv",
    )(mem, g_mem.reshape(depth, 1, D_MODEL), w_mem_kv_bf16, g_mem_k.reshape(depth, 1, MEM_DH))


def _mem_attention_heads(memq, memgate, kt_ref, mv_ref, gq):
    outs = []
    for h in range(MEM_HEADS):
        sl = slice(h * MEM_DH, (h + 1) * MEM_DH)
        qn = _rms(memq[:, sl], gq).astype(BF16)
        logits = _dot(qn, kt_ref[0, 0, sl, :]) * (MEM_DH ** -0.5)
        e = jnp.exp(logits - jnp.max(logits, axis=-1, keepdims=True))
        o = _dot(e.astype(BF16), mv_ref[0, 0, :, sl]) / jnp.sum(e, axis=-1, keepdims=True)
        outs.append((o * _silu(memgate[:, sl])).astype(BF16))
    return outs


def _layer_a_kernel(x_ref, pos_ref, invf_ref, sgn_ref, gn_ref, win_ref, gret_ref, wout_ref,
                    kt_ref, mv_ref, gq_ref, o_ref, state_ref):
    tile = x_ref.shape[1]

    @pl.when(pl.program_id(1) == 0)
    def _():
        state_ref[...] = jnp.zeros_like(state_ref)

    x = x_ref[0]
    xn = _rms(x, gn_ref[0]).astype(BF16)

    def proj(c0, c1):
        return _dot(xn, win_ref[0, :, c0:c1])

    ang = pos_ref[0] * invf_ref[...]
    cos_full = jnp.cos(ang)
    sin_signed = jnp.sin(ang) * sgn_ref[...]

    def rotary(t):
        return t * cos_full + pltpu.roll(t, RET_DK // 2, 1) * sin_signed

    rel = (lax.broadcasted_iota(jnp.int32, (tile, tile), 0)
           - lax.broadcasted_iota(jnp.int32, (tile, tile), 1)).astype(F32)
    idx = lax.broadcasted_iota(jnp.int32, (tile, 1), 0).astype(F32)
    g_ret = gret_ref[0]

    pieces = []
    o_v = 2 * RET_QK_W
    o_g = o_v + RET_V_W
    for h in range(RET_HEADS):
        lg = RET_LOG_GAMMA[h]
        q = rotary(proj(h * RET_DK, (h + 1) * RET_DK)).astype(BF16)
        k = rotary(proj(RET_QK_W + h * RET_DK, RET_QK_W + (h + 1) * RET_DK)) * (RET_DK ** -0.5)
        v = proj(o_v + h * RET_DV, o_v + (h + 1) * RET_DV).astype(BF16)
        gate = proj(o_g + h * RET_DV, o_g + (h + 1) * RET_DV)
        decay = jnp.where(rel >= 0.0, jnp.exp(jnp.maximum(rel, 0.0) * lg), 0.0)
        scores = _dot_nt(q, k.astype(BF16)) * decay
        intra = _dot(scores.astype(BF16), v)
        state = state_ref[h]
        cross = _dot(q, state.astype(BF16)) * jnp.exp((idx + 1.0) * lg)
        kz = k * jnp.exp((tile - 1.0 - idx) * lg)
        state_ref[h] = math.exp(tile * lg) * state + _dot(kz.T.astype(BF16), v)
        ret = _rms(intra + cross, g_ret)
        pieces.append((ret * _silu(gate)).astype(BF16))

    o_mq = o_g + RET_V_W
    pieces += _mem_attention_heads(proj(o_mq, o_mq + MEM_W), proj(o_mq + MEM_W, o_mq + 2 * MEM_W),
                                   kt_ref, mv_ref, gq_ref[0])
    cat = jnp.concatenate(pieces, axis=1)
    o_ref[0] = x + _dot(cat, wout_ref[0])


def _layer_a(x, pos_f32, invf, sgn, g_norm, w_in_bf16, g_ret_head, w_out_bf16, kt, mv, g_mem_q,
             la, layer):
    batch, seq, _ = x.shape
    tile = ROW_TILE_A
    n_a = g_norm.shape[0]
    depth = g_mem_q.shape[0]
    a_in_w = w_in_bf16.shape[2]
    a_out_in = w_out_bf16.shape[1]
    mem_len = mv.shape[2]
    return pl.pallas_call(
        _layer_a_kernel,
        grid=(batch, seq // tile),
        in_specs=[
            pl.BlockSpec((1, tile, D_MODEL), lambda b, t: (b, t, 0)),
            pl.BlockSpec((1, tile, 1), lambda b, t: (b, t, 0)),
            _const_spec((1, RET_DK), lambda b, t: (0, 0)),
            _const_spec((1, RET_DK), lambda b, t: (0, 0)),
            _const_spec((1, 1, D_MODEL), lambda b, t: (la, 0, 0)),
            _const_spec((1, D_MODEL, a_in_w), lambda b, t: (la, 0, 0)),
            _const_spec((1, 1, RET_DV), lambda b, t: (la, 0, 0)),
            _const_spec((1, a_out_in, D_MODEL), lambda b, t: (la, 0, 0)),
            pl.BlockSpec((1, 1, MEM_W, mem_len), lambda b, t: (layer, b, 0, 0)),
            pl.BlockSpec((1, 1, mem_len, MEM_W), lambda b, t: (layer, b, 0, 0)),
            _const_spec((1, 1, MEM_DH), lambda b, t: (layer, 0, 0)),
        ],
        out_specs=pl.BlockSpec((1, tile, D_MODEL), lambda b, t: (b, t, 0)),
        out_shape=jax.ShapeDtypeStruct(x.shape, F32),
        scratch_shapes=[pltpu.VMEM((RET_HEADS, RET_DK, RET_DV), F32)],
        compiler_params=pltpu.CompilerParams(
            dimension_semantics=("arbitrary", "arbitrary"),
            vmem_limit_bytes=VMEM_LIMIT_BYTES),
        name="layer_a",
    )(x, pos_f32, invf, sgn, g_norm.reshape(n_a, 1, D_MODEL), w_in_bf16,
      g_ret_head.reshape(n_a, 1, RET_DV), w_out_bf16, kt, mv, g_mem_q.reshape(depth, 1, MEM_DH))


def _proj_b_kernel(x_ref, gkv_ref, gnb_ref, wkv_ref, win_ref, q_ref, k_ref, v_ref, rest_ref):
    x = x_ref[0]
    xr = x * lax.rsqrt(jnp.mean(x * x, axis=-1, keepdims=True) + EPS)
    xkv = (xr * gkv_ref[...]).astype(BF16)
    xb = (xr * gnb_ref[0]).astype(BF16)
    k_ref[0] = _dot(xkv, wkv_ref[:, :SB_W]).astype(BF16)
    v_ref[0] = _dot(xkv, wkv_ref[:, SB_W:]).astype(BF16)
    q_ref[0] = _dot(xb, win_ref[0, :, :SB_W]).astype(BF16)
    rest_ref[0] = _dot(xb, win_ref[0, :, SB_W:])


def _proj_b(x, g_kv, g_norm_b, w_kv_bf16, w_in_bf16, lb):
    batch, seq, _ = x.shape
    tile = ROW_TILE_B
    n_b = g_norm_b.shape[0]
    b_in_w = w_in_bf16.shape[2]
    rest_w = b_in_w - SB_W
    row_spec = lambda w: pl.BlockSpec((1, tile, w), lambda b, t: (b, t, 0))
    return pl.pallas_call(
        _proj_b_kernel,
        grid=(batch, seq // tile),
        in_specs=[
            row_spec(D_MODEL),
            _const_spec((1, D_MODEL), lambda b, t: (0, 0)),
            _const_spec((1, 1, D_MODEL), lambda b, t: (lb, 0, 0)),
            _const_spec((D_MODEL, 2 * SB_W), lambda b, t: (0, 0)),
            _const_spec((1, D_MODEL, b_in_w), lambda b, t: (lb, 0, 0)),
        ],
        out_specs=[row_spec(SB_W), row_spec(SB_W), row_spec(SB_W), row_spec(rest_w)],
        out_shape=[
            jax.ShapeDtypeStruct((batch, seq, SB_W), BF16),
            jax.ShapeDtypeStruct((batch, seq, SB_W), BF16),
            jax.ShapeDtypeStruct((batch, seq, SB_W), BF16),
            jax.ShapeDtypeStruct((batch, seq, rest_w), F32),
        ],
        compiler_params=pltpu.CompilerParams(
            dimension_semantics=("arbitrary", "arbitrary"),
            vmem_limit_bytes=VMEM_LIMIT_BYTES),
        name="proj_b",
    )(x, g_kv.reshape(1, D_MODEL), g_norm_b.reshape(n_b, 1, D_MODEL), w_kv_bf16, w_in_bf16)


def _sb_kernel(q_ref, k_ref, v_ref, o_ref):
    tile = q_ref.shape[1]
    qi = pl.program_id(2)
    q = q_ref[0]
    scale = SB_DH ** -0.5

    row = lax.broadcasted_iota(jnp.int32, (tile, tile), 0)
    col = lax.broadcasted_iota(jnp.int32, (tile, tile), 1)
    causal = col < row
    urow = lax.broadcasted_iota(jnp.int32, (2 * tile, tile), 0)
    ucol = lax.broadcasted_iota(jnp.int32, (2 * tile, tile), 1)
    urow = jnp.where(urow >= tile, urow - tile, urow)
    suffix_mat = jnp.where(urow > ucol, 1.0, 0.0).astype(BF16)

    def key_tile(start, logsum, acc, masked):
        kb = k_ref[0, pl.ds(start, tile), :]
        vb = v_ref[0, pl.ds(start, tile), :]
        z = _dot_nt(q, kb) * scale
        softplus = jnp.maximum(z, 0.0) + jnp.log(1.0 + jnp.exp(-jnp.abs(z)))
        l_fail = -softplus
        if masked:
            l_fail = jnp.where(causal, l_fail, 0.0)
        hi = l_fail.astype(BF16)
        lo = (l_fail - hi.astype(F32)).astype(BF16)
        suffix = _dot(jnp.concatenate([hi, lo], axis=1), suffix_mat)
        a = jnp.exp((z - softplus) + suffix + logsum)
        if masked:
            a = jnp.where(causal, a, 0.0)
        acc = acc + _dot(a.astype(BF16), vb)
        logsum = logsum + (suffix[:, 0:1] + l_fail[:, 0:1])
        return logsum, acc

    logsum0 = jnp.zeros((tile, 1), F32)
    acc0 = jnp.zeros((tile, SB_DH), F32)
    carry = key_tile(pl.multiple_of(qi * tile, tile), logsum0, acc0, True)

    def body(n, carry):
        start = pl.multiple_of((qi - n) * tile, tile)
        return key_tile(start, carry[0], carry[1], False)

    _, acc = lax.fori_loop(1, qi + 1, body, carry)
    o_ref[0] = acc.astype(o_ref.dtype)


def _sb_attention(q, k, v):
    batch, seq, _ = q.shape
    tile = SB_TILE
    return pl.pallas_call(
        _sb_kernel,
        grid=(batch, SB_HEADS, seq // tile),
        in_specs=[
            pl.BlockSpec((1, tile, SB_DH), lambda b, h, i: (b, i, h)),
            pl.BlockSpec((1, seq, SB_DH), lambda b, h, i: (b, 0, h)),
            pl.BlockSpec((1, seq, SB_DH), lambda b, h, i: (b, 0, h)),
        ],
        out_specs=pl.BlockSpec((1, tile, SB_DH), lambda b, h, i: (b, i, h)),
        out_shape=jax.ShapeDtypeStruct((batch, seq, SB_W), F32),
        compiler_params=pltpu.CompilerParams(
            dimension_semantics=("arbitrary", "arbitrary", "arbitrary"),
            vmem_limit_bytes=VMEM_LIMIT_BYTES),
        name="sb_attn",
    )(q, k, v)


def _out_b_kernel(x_ref, sb_ref, rest_ref, kt_ref, mv_ref, gq_ref, wout_ref, o_ref):
    rest = rest_ref[0]
    pieces = [(sb_ref[0] * _silu(rest[:, :SB_W])).astype(BF16)]
    pieces += _mem_attention_heads(rest[:, SB_W:SB_W + MEM_W], rest[:, SB_W + MEM_W:],
                                   kt_ref, mv_ref, gq_ref[0])
    cat = jnp.concatenate(pieces, axis=1)
    o_ref[0] = x_ref[0] + _dot(cat, wout_ref[0])


def _out_b(x, sb, rest, kt, mv, g_mem_q, w_out_bf16, lb, layer):
    batch, seq, _ = x.shape
    tile = ROW_TILE_B
    depth = g_mem_q.shape[0]
    mem_len = mv.shape[2]
    b_out_in = w_out_bf16.shape[1]
    row_spec = lambda w: pl.BlockSpec((1, tile, w), lambda b, t: (b, t, 0))
    return pl.pallas_call(
        _out_b_kernel,
        grid=(batch, seq // tile),
        in_specs=[
            row_spec(D_MODEL), row_spec(SB_W), row_spec(rest.shape[2]),
            pl.BlockSpec((1, 1, MEM_W, mem_len), lambda b, t: (layer, b, 0, 0)),
            pl.BlockSpec((1, 1, mem_len, MEM_W), lambda b, t: (layer, b, 0, 0)),
            _const_spec((1, 1, MEM_DH), lambda b, t: (layer, 0, 0)),
            _const_spec((1, b_out_in, D_MODEL), lambda b, t: (lb, 0, 0)),
        ],
        out_specs=row_spec(D_MODEL),
        out_shape=jax.ShapeDtypeStruct(x.shape, F32),
        compiler_params=pltpu.CompilerParams(
            dimension_semantics=("arbitrary", "arbitrary"),
            vmem_limit_bytes=VMEM_LIMIT_BYTES),
        name="out_b",
    )(x, sb, rest, kt, mv, g_mem_q.reshape(depth, 1, MEM_DH), w_out_bf16)


def kernel(x, mem, positions, g_norm_a, w_in_a, g_ret_head, w_out_a, g_kv, w_kv, g_norm_b, w_in_b,
           w_out_b, g_mem, w_mem_kv, g_mem_q, g_mem_k):
    n_a = g_norm_a.shape[0]
    n_b = g_norm_b.shape[0]
    batch, seq, _ = x.shape

    kt, mv = _memory_kv(mem, g_mem, w_mem_kv.astype(BF16), g_mem_k)

    inv_freq = ROPE_BASE ** (-jnp.arange(0, RET_DK // 2, dtype=F32) * 2.0 / RET_DK)
    invf = jnp.concatenate([inv_freq, inv_freq]).reshape(1, RET_DK)
    sgn = jnp.concatenate([-jnp.ones((RET_DK // 2,), F32), jnp.ones((RET_DK // 2,), F32)]
                          ).reshape(1, RET_DK)
    pos_f32 = positions.astype(F32).reshape(batch, seq, 1)

    w_in_a_bf16 = w_in_a.astype(BF16)
    w_out_a_bf16 = w_out_a.astype(BF16)
    w_in_b_bf16 = w_in_b.astype(BF16)
    w_out_b_bf16 = w_out_b.astype(BF16)
    w_kv_bf16 = w_kv.astype(BF16)

    for la in range(n_a):
        x = _layer_a(x, pos_f32, invf, sgn, g_norm_a, w_in_a_bf16, g_ret_head, w_out_a_bf16,
                     kt, mv, g_mem_q, la, la)

    k_shared = v_shared = None
    for lb in range(n_b):
        layer = n_a + lb
        q, k_new, v_new, rest = _proj_b(x, g_kv, g_norm_b, w_kv_bf16, w_in_b_bf16, lb)
        if lb == 0:
            k_shared, v_shared = k_new, v_new
        sb = _sb_attention(q, k_shared, v_shared)
        x = _out_b(x, sb, rest, kt, mv, g_mem_q, w_out_b_bf16, lb, layer)
    return x
```

```python
import functools
import math

import numpy as np
import jax
import jax.numpy as jnp
from jax import lax
from jax.experimental import pallas as pl
from jax.experimental.pallas import tpu as pltpu

F32 = jnp.float32
BF16 = jnp.bfloat16

D_MODEL = 1024
RET_HEADS = 4
RET_DK = 128
RET_DV = 256
RET_QK_W = RET_HEADS * RET_DK
RET_V_W = RET_HEADS * RET_DV
ROPE_BASE = 10000.0
SB_HEADS = 8
SB_DH = 128
SB_W = SB_HEADS * SB_DH
MEM_HEADS = 4
MEM_DH = 128
MEM_W = MEM_HEADS * MEM_DH
EPS = 1e-6

RET_LOG_GAMMA = tuple(
    math.log(float(np.float32(1.0 - 2.0 ** (-5.0 - h)))) for h in range(RET_HEADS))

VMEM_LIMIT_BYTES = 48 * 1024 * 1024

ROW_TILE_A = 256
ROW_TILE_B = 512
SB_TILE = 256
SB_LOG_WEIGHT_FLOOR = -104.0


def _rms(x, g):
    return x * lax.rsqrt(jnp.mean(x * x, axis=-1, keepdims=True) + EPS) * g


def _silu(x):
    return x * (1.0 / (1.0 + jnp.exp(-x)))


def _dot(a, b):
    return jnp.dot(a, b, preferred_element_type=F32)


def _dot_nt(a, b):
    return lax.dot_general(a, b, (((1,), (1,)), ((), ())), preferred_element_type=F32)


def _const_spec(shape, index_map):
    return pl.BlockSpec(shape, index_map, pipeline_mode=pl.Buffered(1))


def _memkv_kernel(mem_ref, g_ref, w_ref, gk_ref, kt_ref, v_ref):
    mn = _rms(mem_ref[0], g_ref[0]).astype(BF16)
    kv = _dot(mn, w_ref[0])
    gk = gk_ref[0]
    for h in range(MEM_HEADS):
        kh = _rms(kv[:, h * MEM_DH:(h + 1) * MEM_DH], gk)
        kt_ref[0, 0, h * MEM_DH:(h + 1) * MEM_DH, :] = kh.T.astype(BF16)
    v_ref[0, 0] = kv[:, MEM_W:].astype(BF16)


def _memory_kv(mem, g_mem, w_mem_kv_bf16, g_mem_k):
    depth = g_mem.shape[0]
    batch, mem_len, _ = mem.shape
    return pl.pallas_call(
        _memkv_kernel,
        grid=(depth, batch),
        in_specs=[
            pl.BlockSpec((1, mem_len, D_MODEL), lambda l, b: (b, 0, 0)),
            pl.BlockSpec((1, 1, D_MODEL), lambda l, b: (l, 0, 0)),
            pl.BlockSpec((1, D_MODEL, 2 * MEM_W), lambda l, b: (l, 0, 0)),
            pl.BlockSpec((1, 1, MEM_DH), lambda l, b: (l, 0, 0)),
        ],
        out_specs=[
            pl.BlockSpec((1, 1, MEM_W, mem_len), lambda l, b: (l, b, 0, 0)),
            pl.BlockSpec((1, 1, mem_len, MEM_W), lambda l, b: (l, b, 0, 0)),
        ],
        out_shape=[
            jax.ShapeDtypeStruct((depth, batch, MEM_W, mem_len), BF16),
            jax.ShapeDtypeStruct((depth, batch, mem_len, MEM_W), BF16),
        ],
        compiler_params=pltpu.CompilerParams(
            dimension_semantics=("arbitrary", "arbitrary"),
            vmem_limit_bytes=VMEM_LIMIT_BYTES),
        name="memkv",
    )(mem, g_mem.reshape(depth, 1, D_MODEL), w_mem_kv_bf16, g_mem_k.reshape(depth, 1, MEM_DH))


def _mem_attention_heads(memq, memgate, kt_ref, mv_ref, gq):
    outs = []
    for h in range(MEM_HEADS):
        sl = slice(h * MEM_DH, (h + 1) * MEM_DH)
        qn = _rms(memq[:, sl], gq).astype(BF16)
        logits = _dot(qn, kt_ref[0, 0, sl, :]) * (MEM_DH ** -0.5)
        e = jnp.exp(logits - jnp.max(logits, axis=-1, keepdims=True))
        o = _dot(e.astype(BF16), mv_ref[0, 0, :, sl]) / jnp.sum(e, axis=-1, keepdims=True)
        outs.append((o * _silu(memgate[:, sl])).astype(BF16))
    return outs


def _layer_a_kernel(x_ref, pos_ref, invf_ref, sgn_ref, gn_ref, win_ref, gret_ref, wout_ref,
                    kt_ref, mv_ref, gq_ref, o_ref, state_ref):
    tile = x_ref.shape[1]

    @pl.when(pl.program_id(1) == 0)
    def _():
        state_ref[...] = jnp.zeros_like(state_ref)

    x = x_ref[0]
    xn = _rms(x, gn_ref[0]).astype(BF16)

    def proj(c0, c1):
        return _dot(xn, win_ref[0, :, c0:c1])

    ang = pos_ref[0] * invf_ref[...]
    cos_full = jnp.cos(ang)
    sin_signed = jnp.sin(ang) * sgn_ref[...]

    def rotary(t):
        return t * cos_full + pltpu.roll(t, RET_DK // 2, 1) * sin_signed

    rel = (lax.broadcasted_iota(jnp.int32, (tile, tile), 0)
           - lax.broadcasted_iota(jnp.int32, (tile, tile), 1)).astype(F32)
    idx = lax.broadcasted_iota(jnp.int32, (tile, 1), 0).astype(F32)
    g_ret = gret_ref[0]

    pieces = []
    o_v = 2 * RET_QK_W
    o_g = o_v + RET_V_W
    for h in range(RET_HEADS):
        lg = RET_LOG_GAMMA[h]
        q = rotary(proj(h * RET_DK, (h + 1) * RET_DK)).astype(BF16)
        k = rotary(proj(RET_QK_W + h * RET_DK, RET_QK_W + (h + 1) * RET_DK)) * (RET_DK ** -0.5)
        v = proj(o_v + h * RET_DV, o_v + (h + 1) * RET_DV).astype(BF16)
        gate = proj(o_g + h * RET_DV, o_g + (h + 1) * RET_DV)
        decay = jnp.where(rel >= 0.0, jnp.exp(jnp.maximum(rel, 0.0) * lg), 0.0)
        scores = _dot_nt(q, k.astype(BF16)) * decay
        intra = _dot(scores.astype(BF16), v)
        state = state_ref[h]
        cross = _dot(q, state.astype(BF16)) * jnp.exp((idx + 1.0) * lg)
        kz = k * jnp.exp((tile - 1.0 - idx) * lg)
        state_ref[h] = math.exp(tile * lg) * state + _dot(kz.T.astype(BF16), v)
        ret = _rms(intra + cross, g_ret)
        pieces.append((ret * _silu(gate)).astype(BF16))

    o_mq = o_g + RET_V_W
    pieces += _mem_attention_heads(proj(o_mq, o_mq + MEM_W), proj(o_mq + MEM_W, o_mq + 2 * MEM_W),
                                   kt_ref, mv_ref, gq_ref[0])
    cat = jnp.concatenate(pieces, axis=1)
    o_ref[0] = x + _dot(cat, wout_ref[0])


def _layer_a(x, pos_f32, invf, sgn, g_norm, w_in_bf16, g_ret_head, w_out_bf16, kt, mv, g_mem_q,
             la, layer):
    batch, seq, _ = x.shape
    tile = ROW_TILE_A
    n_a = g_norm.shape[0]
    depth = g_mem_q.shape[0]
    a_in_w = w_in_bf16.shape[2]
    a_out_in = w_out_bf16.shape[1]
    mem_len = mv.shape[2]
    return pl.pallas_call(
        _layer_a_kernel,
        grid=(batch, seq // tile),
        in_specs=[
            pl.BlockSpec((1, tile, D_MODEL), lambda b, t: (b, t, 0)),
            pl.BlockSpec((1, tile, 1), lambda b, t: (b, t, 0)),
            _const_spec((1, RET_DK), lambda b, t: (0, 0)),
            _const_spec((1, RET_DK), lambda b, t: (0, 0)),
            _const_spec((1, 1, D_MODEL), lambda b, t: (la, 0, 0)),
            _const_spec((1, D_MODEL, a_in_w), lambda b, t: (la, 0, 0)),
            _const_spec((1, 1, RET_DV), lambda b, t: (la, 0, 0)),
            _const_spec((1, a_out_in, D_MODEL), lambda b, t: (la, 0, 0)),
            pl.BlockSpec((1, 1, MEM_W, mem_len), lambda b, t: (layer, b, 0, 0)),
            pl.BlockSpec((1, 1, mem_len, MEM_W), lambda b, t: (layer, b, 0, 0)),
            _const_spec((1, 1, MEM_DH), lambda b, t: (layer, 0, 0)),
        ],
        out_specs=pl.BlockSpec((1, tile, D_MODEL), lambda b, t: (b, t, 0)),
        out_shape=jax.ShapeDtypeStruct(x.shape, F32),
        scratch_shapes=[pltpu.VMEM((RET_HEADS, RET_DK, RET_DV), F32)],
        compiler_params=pltpu.CompilerParams(
            dimension_semantics=("arbitrary", "arbitrary"),
            vmem_limit_bytes=VMEM_LIMIT_BYTES),
        name="layer_a",
    )(x, pos_f32, invf, sgn, g_norm.reshape(n_a, 1, D_MODEL), w_in_bf16,
      g_ret_head.reshape(n_a, 1, RET_DV), w_out_bf16, kt, mv, g_mem_q.reshape(depth, 1, MEM_DH))


def _proj_b_kernel(x_ref, gkv_ref, gnb_ref, wkv_ref, win_ref, q_ref, k_ref, v_ref, rest_ref):
    x = x_ref[0]
    xr = x * lax.rsqrt(jnp.mean(x * x, axis=-1, keepdims=True) + EPS)
    xkv = (xr * gkv_ref[...]).astype(BF16)
    xb = (xr * gnb_ref[0]).astype(BF16)
    k_ref[0] = _dot(xkv, wkv_ref[:, :SB_W]).astype(BF16)
    v_ref[0] = _dot(xkv, wkv_ref[:, SB_W:]).astype(BF16)
    q_ref[0] = _dot(xb, win_ref[0, :, :SB_W]).astype(BF16)
    rest_ref[0] = _dot(xb, win_ref[0, :, SB_W:])


def _proj_b(x, g_kv, g_norm_b, w_kv_bf16, w_in_bf16, lb):
    batch, seq, _ = x.shape
    tile = ROW_TILE_B
    n_b = g_norm_b.shape[0]
    b_in_w = w_in_bf16.shape[2]
    rest_w = b_in_w - SB_W
    row_spec = lambda w: pl.BlockSpec((1, tile, w), lambda b, t: (b, t, 0))
    return pl.pallas_call(
        _proj_b_kernel,
        grid=(batch, seq // tile),
        in_specs=[
            row_spec(D_MODEL),
            _const_spec((1, D_MODEL), lambda b, t: (0, 0)),
            _const_spec((1, 1, D_MODEL), lambda b, t: (lb, 0, 0)),
            _const_spec((D_MODEL, 2 * SB_W), lambda b, t: (0, 0)),
            _const_spec((1, D_MODEL, b_in_w), lambda b, t: (lb, 0, 0)),
        ],
        out_specs=[row_spec(SB_W), row_spec(SB_W), row_spec(SB_W), row_spec(rest_w)],
        out_shape=[
            jax.ShapeDtypeStruct((batch, seq, SB_W), BF16),
            jax.ShapeDtypeStruct((batch, seq, SB_W), BF16),
            jax.ShapeDtypeStruct((batch, seq, SB_W), BF16),
            jax.ShapeDtypeStruct((batch, seq, rest_w), F32),
        ],
        compiler_params=pltpu.CompilerParams(
            dimension_semantics=("arbitrary", "arbitrary"),
            vmem_limit_bytes=VMEM_LIMIT_BYTES),
        name="proj_b",
    )(x, g_kv.reshape(1, D_MODEL), g_norm_b.reshape(n_b, 1, D_MODEL), w_kv_bf16, w_in_bf16)


def _sb_kernel(q_ref, k_ref, v_ref, o_ref):
    tile = q_ref.shape[1]
    qi = pl.program_id(2)
    q = q_ref[0]
    scale = SB_DH ** -0.5

    row = lax.broadcasted_iota(jnp.int32, (tile, tile), 0)
    col = lax.broadcasted_iota(jnp.int32, (tile, tile), 1)
    causal = col < row
    urow = lax.broadcasted_iota(jnp.int32, (2 * tile, tile), 0)
    ucol = lax.broadcasted_iota(jnp.int32, (2 * tile, tile), 1)
    urow = jnp.where(urow >= tile, urow - tile, urow)
    suffix_mat = jnp.where(urow > ucol, 1.0, 0.0).astype(BF16)

    def key_tile(start, logsum, acc, masked):
        kb = k_ref[0, pl.ds(start, tile), :]
        vb = v_ref[0, pl.ds(start, tile), :]
        z = _dot_nt(q, kb) * scale
        softplus = jnp.maximum(z, 0.0) + jnp.log(1.0 + jnp.exp(-jnp.abs(z)))
        l_fail = -softplus
        if masked:
            l_fail = jnp.where(causal, l_fail, 0.0)
        hi = l_fail.astype(BF16)
        lo = (l_fail - hi.astype(F32)).astype(BF16)
        suffix = _dot(jnp.concatenate([hi, lo], axis=1), suffix_mat)
        a = jnp.exp((z - softplus) + suffix + logsum)
        if masked:
            a = jnp.where(causal, a, 0.0)
        acc = acc + _dot(a.astype(BF16), vb)
        logsum = logsum + (suffix[:, 0:1] + l_fail[:, 0:1])
        return logsum, acc

    logsum0 = jnp.zeros((tile, 1), F32)
    acc0 = jnp.zeros((tile, SB_DH), F32)

    @pl.when(qi == 0)
    def _():
        _, acc = key_tile(0, logsum0, acc0, True)
        o_ref[0] = acc.astype(o_ref.dtype)

    @pl.when(qi > 0)
    def _():
        logsum, acc = key_tile(pl.multiple_of(qi * tile, tile), logsum0, acc0, True)
        logsum, acc = key_tile(pl.multiple_of((qi - 1) * tile, tile), logsum, acc, False)

        def cond(carry):
            return jnp.logical_and(carry[0] <= qi, carry[3] > SB_LOG_WEIGHT_FLOOR)

        def body(carry):
            n, logsum, acc, _ = carry
            start = pl.multiple_of((qi - n) * tile, tile)
            logsum, acc = key_tile(start, logsum, acc, False)
            return n + 1, logsum, acc, jnp.max(logsum)

        _, _, acc, _ = lax.while_loop(cond, body, (jnp.int32(2), logsum, acc, jnp.max(logsum)))
        o_ref[0] = acc.astype(o_ref.dtype)


def _sb_attention(q, k, v):
    batch, seq, _ = q.shape
    tile = SB_TILE
    return pl.pallas_call(
        _sb_kernel,
        grid=(batch, SB_HEADS, seq // tile),
        in_specs=[
            pl.BlockSpec((1, tile, SB_DH), lambda b, h, i: (b, i, h)),
            pl.BlockSpec((1, seq, SB_DH), lambda b, h, i: (b, 0, h)),
            pl.BlockSpec((1, seq, SB_DH), lambda b, h, i: (b, 0, h)),
        ],
        out_specs=pl.BlockSpec((1, tile, SB_DH), lambda b, h, i: (b, i, h)),
        out_shape=jax.ShapeDtypeStruct((batch, seq, SB_W), F32),
        compiler_params=pltpu.CompilerParams(
            dimension_semantics=("arbitrary", "arbitrary", "arbitrary"),
            vmem_limit_bytes=VMEM_LIMIT_BYTES),
        name="sb_attn",
    )(q, k, v)


def _out_b_kernel(x_ref, sb_ref, rest_ref, kt_ref, mv_ref, gq_ref, wout_ref, o_ref):
    rest = rest_ref[0]
    pieces = [(sb_ref[0] * _silu(rest[:, :SB_W])).astype(BF16)]
    pieces += _mem_attention_heads(rest[:, SB_W:SB_W + MEM_W], rest[:, SB_W + MEM_W:],
                                   kt_ref, mv_ref, gq_ref[0])
    cat = jnp.concatenate(pieces, axis=1)
    o_ref[0] = x_ref[0] + _dot(cat, wout_ref[0])


def _out_b(x, sb, rest, kt, mv, g_mem_q, w_out_bf16, lb, layer):
    batch, seq, _ = x.shape
    tile = ROW_TILE_B
    depth = g_mem_q.shape[0]
    mem_len = mv.shape[2]
    b_out_in = w_out_bf16.shape[1]
    row_spec = lambda w: pl.BlockSpec((1, tile, w), lambda b, t: (b, t, 0))
    return pl.pallas_call(
        _out_b_kernel,
        grid=(batch, seq // tile),
        in_specs=[
            row_spec(D_MODEL), row_spec(SB_W), row_spec(rest.shape[2]),
            pl.BlockSpec((1, 1, MEM_W, mem_len), lambda b, t: (layer, b, 0, 0)),
            pl.BlockSpec((1, 1, mem_len, MEM_W), lambda b, t: (layer, b, 0, 0)),
            _const_spec((1, 1, MEM_DH), lambda b, t: (layer, 0, 0)),
            _const_spec((1, b_out_in, D_MODEL), lambda b, t: (lb, 0, 0)),
        ],
        out_specs=row_spec(D_MODEL),
        out_shape=jax.ShapeDtypeStruct(x.shape, F32),
        compiler_params=pltpu.CompilerParams(
            dimension_semantics=("arbitrary", "arbitrary"),
            vmem_limit_bytes=VMEM_LIMIT_BYTES),
        name="out_b",
    )(x, sb, rest, kt, mv, g_mem_q.reshape(depth, 1, MEM_DH), w_out_bf16)


def kernel(x, mem, positions, g_norm_a, w_in_a, g_ret_head, w_out_a, g_kv, w_kv, g_norm_b, w_in_b,
           w_out_b, g_mem, w_mem_kv, g_mem_q, g_mem_k):
    n_a = g_norm_a.shape[0]
    n_b = g_norm_b.shape[0]
    batch, seq, _ = x.shape

    kt, mv = _memory_kv(mem, g_mem, w_mem_kv.astype(BF16), g_mem_k)

    inv_freq = ROPE_BASE ** (-jnp.arange(0, RET_DK // 2, dtype=F32) * 2.0 / RET_DK)
    invf = jnp.concatenate([inv_freq, inv_freq]).reshape(1, RET_DK)
    sgn = jnp.concatenate([-jnp.ones((RET_DK // 2,), F32), jnp.ones((RET_DK // 2,), F32)]
                          ).reshape(1, RET_DK)
    pos_f32 = positions.astype(F32).reshape(batch, seq, 1)

    w_in_a_bf16 = w_in_a.astype(BF16)
    w_out_a_bf16 = w_out_a.astype(BF16)
    w_in_b_bf16 = w_in_b.astype(BF16)
    w_out_b_bf16 = w_out_b.astype(BF16)
    w_kv_bf16 = w_kv.astype(BF16)

    for la in range(n_a):
        x = _layer_a(x, pos_f32, invf, sgn, g_norm_a, w_in_a_bf16, g_ret_head, w_out_a_bf16,
                     kt, mv, g_mem_q, la, la)

    k_shared = v_shared = None
    for lb in range(n_b):
        layer = n_a + lb
        q, k_new, v_new, rest = _proj_b(x, g_kv, g_norm_b, w_kv_bf16, w_in_b_bf16, lb)
        if lb == 0:
            k_shared, v_shared = k_new, v_new
        sb = _sb_attention(q, k_shared, v_shared)
        x = _out_b(x, sb, rest, kt, mv, g_mem_q, w_out_b_bf16, lb, layer)
    return x
```

```python
import functools
import math

import numpy as np
import jax
import jax.numpy as jnp
from jax import lax
from jax.experimental import pallas as pl
from jax.experimental.pallas import tpu as pltpu

F32 = jnp.float32
BF16 = jnp.bfloat16

D_MODEL = 1024
RET_HEADS = 4
RET_DK = 128
RET_DV = 256
RET_QK_W = RET_HEADS * RET_DK
RET_V_W = RET_HEADS * RET_DV
ROPE_BASE = 10000.0
SB_HEADS = 8
SB_DH = 128
SB_W = SB_HEADS * SB_DH
MEM_HEADS = 4
MEM_DH = 128
MEM_W = MEM_HEADS * MEM_DH
EPS = 1e-6

RET_LOG_GAMMA = tuple(
    math.log(float(np.float32(1.0 - 2.0 ** (-5.0 - h)))) for h in range(RET_HEADS))

VMEM_LIMIT_BYTES = 48 * 1024 * 1024

ROW_TILE_A = 256
ROW_TILE_B = 512
SB_TILE = 256
SB_LOG_WEIGHT_FLOOR = -104.0
SB_HEAD_GROUP = 4
LOG2E = math.log2(math.e)


def _rms(x, g):
    return x * lax.rsqrt(jnp.mean(x * x, axis=-1, keepdims=True) + EPS) * g


def _silu(x):
    return x * (1.0 / (1.0 + jnp.exp(-x)))


def _dot(a, b):
    return jnp.dot(a, b, preferred_element_type=F32)


def _dot_nt(a, b):
    return lax.dot_general(a, b, (((1,), (1,)), ((), ())), preferred_element_type=F32)


def _const_spec(shape, index_map):
    return pl.BlockSpec(shape, index_map, pipeline_mode=pl.Buffered(1))


def _memkv_kernel(mem_ref, g_ref, w_ref, gk_ref, kt_ref, v_ref):
    mn = _rms(mem_ref[0], g_ref[0]).astype(BF16)
    kv = _dot(mn, w_ref[0])
    gk = gk_ref[0]
    for h in range(MEM_HEADS):
        kh = _rms(kv[:, h * MEM_DH:(h + 1) * MEM_DH], gk)
        kt_ref[0, 0, h * MEM_DH:(h + 1) * MEM_DH, :] = kh.T.astype(BF16)
    v_ref[0, 0] = kv[:, MEM_W:].astype(BF16)


def _memory_kv(mem, g_mem, w_mem_kv_bf16, g_mem_k):
    depth = g_mem.shape[0]
    batch, mem_len, _ = mem.shape
    return pl.pallas_call(
        _memkv_kernel,
        grid=(depth, batch),
        in_specs=[
            pl.BlockSpec((1, mem_len, D_MODEL), lambda l, b: (b, 0, 0)),
            pl.BlockSpec((1, 1, D_MODEL), lambda l, b: (l, 0, 0)),
            pl.BlockSpec((1, D_MODEL, 2 * MEM_W), lambda l, b: (l, 0, 0)),
            pl.BlockSpec((1, 1, MEM_DH), lambda l, b: (l, 0, 0)),
        ],
        out_specs=[
            pl.BlockSpec((1, 1, MEM_W, mem_len), lambda l, b: (l, b, 0, 0)),
            pl.BlockSpec((1, 1, mem_len, MEM_W), lambda l, b: (l, b, 0, 0)),
        ],
        out_shape=[
            jax.ShapeDtypeStruct((depth, batch, MEM_W, mem_len), BF16),
            jax.ShapeDtypeStruct((depth, batch, mem_len, MEM_W), BF16),
        ],
        compiler_params=pltpu.CompilerParams(
            dimension_semantics=("arbitrary", "arbitrary"),
            vmem_limit_bytes=VMEM_LIMIT_BYTES),
        name="memkv",
    )(mem, g_mem.reshape(depth, 1, D_MODEL), w_mem_kv_bf16, g_mem_k.reshape(depth, 1, MEM_DH))


def _mem_attention_heads(memq, memgate, kt_ref, mv_ref, gq):
    outs = []
    for h in range(MEM_HEADS):
        sl = slice(h * MEM_DH, (h + 1) * MEM_DH)
        qn = _rms(memq[:, sl], gq).astype(BF16)
        logits = _dot(qn, kt_ref[0, 0, sl, :]) * (MEM_DH ** -0.5)
        e = jnp.exp(logits - jnp.max(logits, axis=-1, keepdims=True))
        o = _dot(e.astype(BF16), mv_ref[0, 0, :, sl]) / jnp.sum(e, axis=-1, keepdims=True)
        outs.append((o * _silu(memgate[:, sl])).astype(BF16))
    return outs


def _layer_a_kernel(x_ref, pos_ref, invf_ref, sgn_ref, gn_ref, win_ref, gret_ref, wout_ref,
                    kt_ref, mv_ref, gq_ref, o_ref, state_ref):
    tile = x_ref.shape[1]

    @pl.when(pl.program_id(1) == 0)
    def _():
        state_ref[...] = jnp.zeros_like(state_ref)

    x = x_ref[0]
    xn = _rms(x, gn_ref[0]).astype(BF16)

    def proj(c0, c1):
        return _dot(xn, win_ref[0, :, c0:c1])

    ang = pos_ref[0] * invf_ref[...]
    cos_full = jnp.cos(ang)
    sin_signed = jnp.sin(ang) * sgn_ref[...]

    def rotary(t):
        return t * cos_full + pltpu.roll(t, RET_DK // 2, 1) * sin_signed

    rel = (lax.broadcasted_iota(jnp.int32, (tile, tile), 0)
           - lax.broadcasted_iota(jnp.int32, (tile, tile), 1)).astype(F32)
    idx = lax.broadcasted_iota(jnp.int32, (tile, 1), 0).astype(F32)
    g_ret = gret_ref[0]

    pieces = []
    o_v = 2 * RET_QK_W
    o_g = o_v + RET_V_W
    for h in range(RET_HEADS):
        lg = RET_LOG_GAMMA[h]
        q = rotary(proj(h * RET_DK, (h + 1) * RET_DK)).astype(BF16)
        k = rotary(proj(RET_QK_W + h * RET_DK, RET_QK_W + (h + 1) * RET_DK)) * (RET_DK ** -0.5)
        v = proj(o_v + h * RET_DV, o_v + (h + 1) * RET_DV).astype(BF16)
        gate = proj(o_g + h * RET_DV, o_g + (h + 1) * RET_DV)
        decay = jnp.where(rel >= 0.0, jnp.exp(jnp.maximum(rel, 0.0) * lg), 0.0)
        scores = _dot_nt(q, k.astype(BF16)) * decay
        intra = _dot(scores.astype(BF16), v)
        state = state_ref[h]
        cross = _dot(q, state.astype(BF16)) * jnp.exp((idx + 1.0) * lg)
        kz = k * jnp.exp((tile - 1.0 - idx) * lg)
        state_ref[h] = math.exp(tile * lg) * state + _dot(kz.T.astype(BF16), v)
        ret = _rms(intra + cross, g_ret)
        pieces.append((ret * _silu(gate)).astype(BF16))

    o_mq = o_g + RET_V_W
    pieces += _mem_attention_heads(proj(o_mq, o_mq + MEM_W), proj(o_mq + MEM_W, o_mq + 2 * MEM_W),
                                   kt_ref, mv_ref, gq_ref[0])
    cat = jnp.concatenate(pieces, axis=1)
    o_ref[0] = x + _dot(cat, wout_ref[0])


def _layer_a(x, pos_f32, invf, sgn, g_norm, w_in_bf16, g_ret_head, w_out_bf16, kt, mv, g_mem_q,
             la, layer):
    batch, seq, _ = x.shape
    tile = ROW_TILE_A
    n_a = g_norm.shape[0]
    depth = g_mem_q.shape[0]
    a_in_w = w_in_bf16.shape[2]
    a_out_in = w_out_bf16.shape[1]
    mem_len = mv.shape[2]
    return pl.pallas_call(
        _layer_a_kernel,
        grid=(batch, seq // tile),
        in_specs=[
            pl.BlockSpec((1, tile, D_MODEL), lambda b, t: (b, t, 0)),
            pl.BlockSpec((1, tile, 1), lambda b, t: (b, t, 0)),
            _const_spec((1, RET_DK), lambda b, t: (0, 0)),
            _const_spec((1, RET_DK), lambda b, t: (0, 0)),
            _const_spec((1, 1, D_MODEL), lambda b, t: (la, 0, 0)),
            _const_spec((1, D_MODEL, a_in_w), lambda b, t: (la, 0, 0)),
            _const_spec((1, 1, RET_DV), lambda b, t: (la, 0, 0)),
            _const_spec((1, a_out_in, D_MODEL), lambda b, t: (la, 0, 0)),
            pl.BlockSpec((1, 1, MEM_W, mem_len), lambda b, t: (layer, b, 0, 0)),
            pl.BlockSpec((1, 1, mem_len, MEM_W), lambda b, t: (layer, b, 0, 0)),
            _const_spec((1, 1, MEM_DH), lambda b, t: (layer, 0, 0)),
        ],
        out_specs=pl.BlockSpec((1, tile, D_MODEL), lambda b, t: (b, t, 0)),
        out_shape=jax.ShapeDtypeStruct(x.shape, F32),
        scratch_shapes=[pltpu.VMEM((RET_HEADS, RET_DK, RET_DV), F32)],
        compiler_params=pltpu.CompilerParams(
            dimension_semantics=("arbitrary", "arbitrary"),
            vmem_limit_bytes=VMEM_LIMIT_BYTES),
        name="layer_a",
    )(x, pos_f32, invf, sgn, g_norm.reshape(n_a, 1, D_MODEL), w_in_bf16,
      g_ret_head.reshape(n_a, 1, RET_DV), w_out_bf16, kt, mv, g_mem_q.reshape(depth, 1, MEM_DH))


def _proj_b_kernel(x_ref, gkv_ref, gnb_ref, wkv_ref, win_ref, q_ref, k_ref, v_ref, rest_ref):
    x = x_ref[0]
    xr = x * lax.rsqrt(jnp.mean(x * x, axis=-1, keepdims=True) + EPS)
    xkv = (xr * gkv_ref[...]).astype(BF16)
    xb = (xr * gnb_ref[0]).astype(BF16)
    k_ref[0] = _dot(xkv, wkv_ref[:, :SB_W]).astype(BF16)
    v_ref[0] = _dot(xkv, wkv_ref[:, SB_W:]).astype(BF16)
    q_ref[0] = _dot(xb, win_ref[0, :, :SB_W]).astype(BF16)
    rest_ref[0] = _dot(xb, win_ref[0, :, SB_W:])


def _proj_b(x, g_kv, g_norm_b, w_kv_bf16, w_in_bf16, lb):
    batch, seq, _ = x.shape
    tile = ROW_TILE_B
    n_b = g_norm_b.shape[0]
    b_in_w = w_in_bf16.shape[2]
    rest_w = b_in_w - SB_W
    row_spec = lambda w: pl.BlockSpec((1, tile, w), lambda b, t: (b, t, 0))
    return pl.pallas_call(
        _proj_b_kernel,
        grid=(batch, seq // tile),
        in_specs=[
            row_spec(D_MODEL),
            _const_spec((1, D_MODEL), lambda b, t: (0, 0)),
            _const_spec((1, 1, D_MODEL), lambda b, t: (lb, 0, 0)),
            _const_spec((D_MODEL, 2 * SB_W), lambda b, t: (0, 0)),
            _const_spec((1, D_MODEL, b_in_w), lambda b, t: (lb, 0, 0)),
        ],
        out_specs=[row_spec(SB_W), row_spec(SB_W), row_spec(SB_W), row_spec(rest_w)],
        out_shape=[
            jax.ShapeDtypeStruct((batch, seq, SB_W), BF16),
            jax.ShapeDtypeStruct((batch, seq, SB_W), BF16),
            jax.ShapeDtypeStruct((batch, seq, SB_W), BF16),
            jax.ShapeDtypeStruct((batch, seq, rest_w), F32),
        ],
        compiler_params=pltpu.CompilerParams(
            dimension_semantics=("arbitrary", "arbitrary"),
            vmem_limit_bytes=VMEM_LIMIT_BYTES),
        name="proj_b",
    )(x, g_kv.reshape(1, D_MODEL), g_norm_b.reshape(n_b, 1, D_MODEL), w_kv_bf16, w_in_bf16)


def _sb_kernel(q_ref, k_ref, v_ref, o_ref):
    tile = q_ref.shape[1]
    heads = q_ref.shape[2] // SB_DH
    qi = pl.program_id(2)
    scale = SB_DH ** -0.5

    row = lax.broadcasted_iota(jnp.int32, (tile, tile), 0)
    col = lax.broadcasted_iota(jnp.int32, (tile, tile), 1)
    causal = col < row
    urow = lax.broadcasted_iota(jnp.int32, (2 * tile, tile), 0)
    ucol = lax.broadcasted_iota(jnp.int32, (2 * tile, tile), 1)
    urow = jnp.where(urow >= tile, urow - tile, urow)
    suffix_mat = jnp.where(urow > ucol, 1.0, 0.0).astype(BF16)

    def window(g, start, nblk, diag, penalty):
        hs = slice(g * SB_DH, (g + 1) * SB_DH)
        q = q_ref[0, :, hs]
        kb = k_ref[0, pl.ds(start, nblk * tile), hs]
        vb = v_ref[0, pl.ds(start, nblk * tile), hs]
        zr = _dot_nt(q, kb)
        z = zr * scale
        sp = jnp.maximum(z, 0.0) + jnp.log(1.0 + jnp.exp2(jnp.abs(zr) * (-scale * LOG2E)))
        blocks = [sp[:, j * tile:(j + 1) * tile] for j in range(nblk)]
        if diag:
            blocks[-1] = jnp.where(causal, blocks[-1], 0.0)
        split = []
        for blk in blocks:
            hi = blk.astype(BF16)
            lo = (blk - hi.astype(F32)).astype(BF16)
            split.append(jnp.concatenate([hi, lo], axis=1))
        suffix = _dot(jnp.concatenate(split, axis=0), suffix_mat)
        weights = [None] * nblk
        for j in reversed(range(nblk)):
            sfx = suffix[j * tile:(j + 1) * tile]
            log_a = (z[:, j * tile:(j + 1) * tile] - blocks[j]) - sfx
            if penalty is not None:
                log_a = log_a - penalty
            a = jnp.exp(log_a)
            if diag and j == nblk - 1:
                a = jnp.where(causal, a, 0.0)
            weights[j] = a.astype(BF16)
            total = sfx[:, 0:1] + blocks[j][:, 0:1]
            penalty = total if penalty is None else penalty + total
        acc = _dot(jnp.concatenate(weights, axis=1), vb)
        return acc, penalty

    def store(accs):
        for g in range(heads):
            o_ref[0, :, g * SB_DH:(g + 1) * SB_DH] = accs[g].astype(o_ref.dtype)

    @pl.when(qi == 0)
    def _():
        store([window(g, 0, 1, True, None)[0] for g in range(heads)])

    @pl.when(qi > 0)
    def _():
        near = [window(g, pl.multiple_of((qi - 1) * tile, tile), 2, True, None)
                for g in range(heads)]
        accs = [r[0] for r in near]
        penalties = [r[1] for r in near]
        store(accs)

        def log_weight_bound(pens):
            return -functools.reduce(jnp.minimum, [jnp.min(p) for p in pens])

        def cond(carry):
            return jnp.logical_and(carry[0] <= qi, carry[1] > SB_LOG_WEIGHT_FLOOR)

        def body(carry):
            n, _, accs, pens = carry
            start = pl.multiple_of((qi - n) * tile, tile)
            far = [window(g, start, 1, False, pens[g]) for g in range(heads)]
            accs = [accs[g] + far[g][0] for g in range(heads)]
            pens = [far[g][1] for g in range(heads)]
            return n + 1, log_weight_bound(pens), accs, pens

        bound = log_weight_bound(penalties)

        @pl.when(jnp.logical_and(qi >= 2, bound > SB_LOG_WEIGHT_FLOOR))
        def _():
            _, _, accs_far, _ = lax.while_loop(cond, body, (jnp.int32(2), bound, accs, penalties))
            store(accs_far)


def _sb_attention(q, k, v):
    batch, seq, _ = q.shape
    tile = SB_TILE
    gw = SB_HEAD_GROUP * SB_DH
    return pl.pallas_call(
        _sb_kernel,
        grid=(batch, SB_HEADS // SB_HEAD_GROUP, seq // tile),
        in_specs=[
            pl.BlockSpec((1, tile, gw), lambda b, h, i: (b, i, h)),
            pl.BlockSpec((1, seq, gw), lambda b, h, i: (b, 0, h)),
            pl.BlockSpec((1, seq, gw), lambda b, h, i: (b, 0, h)),
        ],
        out_specs=pl.BlockSpec((1, tile, gw), lambda b, h, i: (b, i, h)),
        out_shape=jax.ShapeDtypeStruct((batch, seq, SB_W), F32),
        compiler_params=pltpu.CompilerParams(
            dimension_semantics=("arbitrary", "arbitrary", "arbitrary"),
            vmem_limit_bytes=VMEM_LIMIT_BYTES),
        name="sb_attn",
    )(q, k, v)


def _out_b_kernel(x_ref, sb_ref, rest_ref, kt_ref, mv_ref, gq_ref, wout_ref, o_ref):
    rest = rest_ref[0]
    pieces = [(sb_ref[0] * _silu(rest[:, :SB_W])).astype(BF16)]
    pieces += _mem_attention_heads(rest[:, SB_W:SB_W + MEM_W], rest[:, SB_W + MEM_W:],
                                   kt_ref, mv_ref, gq_ref[0])
    cat = jnp.concatenate(pieces, axis=1)
    o_ref[0] = x_ref[0] + _dot(cat, wout_ref[0])


def _out_b(x, sb, rest, kt, mv, g_mem_q, w_out_bf16, lb, layer):
    batch, seq, _ = x.shape
    tile = ROW_TILE_B
    depth = g_mem_q.shape[0]
    mem_len = mv.shape[2]
    b_out_in = w_out_bf16.shape[1]
    row_spec = lambda w: pl.BlockSpec((1, tile, w), lambda b, t: (b, t, 0))
    return pl.pallas_call(
        _out_b_kernel,
        grid=(batch, seq // tile),
        in_specs=[
            row_spec(D_MODEL), row_spec(SB_W), row_spec(rest.shape[2]),
            pl.BlockSpec((1, 1, MEM_W, mem_len), lambda b, t: (layer, b, 0, 0)),
            pl.BlockSpec((1, 1, mem_len, MEM_W), lambda b, t: (layer, b, 0, 0)),
            _const_spec((1, 1, MEM_DH), lambda b, t: (layer, 0, 0)),
            _const_spec((1, b_out_in, D_MODEL), lambda b, t: (lb, 0, 0)),
        ],
        out_specs=row_spec(D_MODEL),
        out_shape=jax.ShapeDtypeStruct(x.shape, F32),
        compiler_params=pltpu.CompilerParams(
            dimension_semantics=("arbitrary", "arbitrary"),
            vmem_limit_bytes=VMEM_LIMIT_BYTES),
        name="out_b",
    )(x, sb, rest, kt, mv, g_mem_q.reshape(depth, 1, MEM_DH), w_out_bf16)


def kernel(x, mem, positions, g_norm_a, w_in_a, g_ret_head, w_out_a, g_kv, w_kv, g_norm_b, w_in_b,
           w_out_b, g_mem, w_mem_kv, g_mem_q, g_mem_k):
    n_a = g_norm_a.shape[0]
    n_b = g_norm_b.shape[0]
    batch, seq, _ = x.shape

    kt, mv = _memory_kv(mem, g_mem, w_mem_kv.astype(BF16), g_mem_k)

    inv_freq = ROPE_BASE ** (-jnp.arange(0, RET_DK // 2, dtype=F32) * 2.0 / RET_DK)
    invf = jnp.concatenate([inv_freq, inv_freq]).reshape(1, RET_DK)
    sgn = jnp.concatenate([-jnp.ones((RET_DK // 2,), F32), jnp.ones((RET_DK // 2,), F32)]
                          ).reshape(1, RET_DK)
    pos_f32 = positions.astype(F32).reshape(batch, seq, 1)

    w_in_a_bf16 = w_in_a.astype(BF16)
    w_out_a_bf16 = w_out_a.astype(BF16)
    w_in_b_bf16 = w_in_b.astype(BF16)
    w_out_b_bf16 = w_out_b.astype(BF16)
    w_kv_bf16 = w_kv.astype(BF16)

    for la in range(n_a):
        x = _layer_a(x, pos_f32, invf, sgn, g_norm_a, w_in_a_bf16, g_ret_head, w_out_a_bf16,
                     kt, mv, g_mem_q, la, la)

    k_shared = v_shared = None
    for lb in range(n_b):
        layer = n_a + lb
        q, k_new, v_new, rest = _proj_b(x, g_kv, g_norm_b, w_kv_bf16, w_in_b_bf16, lb)
        if lb == 0:
            k_shared, v_shared = k_new, v_new
        sb = _sb_attention(q, k_shared, v_shared)
        x = _out_b(x, sb, rest, kt, mv, g_mem_q, w_out_b_bf16, lb, layer)
    return x
```

```python
import functools
import math

import numpy as np
import jax
import jax.numpy as jnp
from jax import lax
from jax.experimental import pallas as pl
from jax.experimental.pallas import tpu as pltpu

F32 = jnp.float32
BF16 = jnp.bfloat16

D_MODEL = 1024
RET_HEADS = 4
RET_DK = 128
RET_DV = 256
RET_QK_W = RET_HEADS * RET_DK
RET_V_W = RET_HEADS * RET_DV
ROPE_BASE = 10000.0
SB_HEADS = 8
SB_DH = 128
SB_W = SB_HEADS * SB_DH
MEM_HEADS = 4
MEM_DH = 128
MEM_W = MEM_HEADS * MEM_DH
EPS = 1e-6

RET_LOG_GAMMA = tuple(
    math.log(float(np.float32(1.0 - 2.0 ** (-5.0 - h)))) for h in range(RET_HEADS))

VMEM_LIMIT_BYTES = 48 * 1024 * 1024

ROW_TILE_A = 512
RET_CHUNK = 256
ROW_TILE_B = 512
SB_TILE = 256
SB_LOG_WEIGHT_FLOOR = -104.0
SB_VMEM_LIMIT_BYTES = 58 * 1024 * 1024
LOG2E = math.log2(math.e)


def _rms(x, g):
    return x * lax.rsqrt(jnp.mean(x * x, axis=-1, keepdims=True) + EPS) * g


def _silu(x):
    return x * (1.0 / (1.0 + jnp.exp(-x)))


def _dot(a, b):
    return jnp.dot(a, b, preferred_element_type=F32)


def _dot_nt(a, b):
    return lax.dot_general(a, b, (((1,), (1,)), ((), ())), preferred_element_type=F32)


def _const_spec(shape, index_map):
    return pl.BlockSpec(shape, index_map, pipeline_mode=pl.Buffered(1))


def _memkv_kernel(mem_ref, g_ref, w_ref, gk_ref, kt_ref, v_ref):
    mn = _rms(mem_ref[0], g_ref[0]).astype(BF16)
    kv = _dot(mn, w_ref[0])
    gk = gk_ref[0]
    for h in range(MEM_HEADS):
        kh = _rms(kv[:, h * MEM_DH:(h + 1) * MEM_DH], gk)
        kt_ref[0, 0, h * MEM_DH:(h + 1) * MEM_DH, :] = kh.T.astype(BF16)
    v_ref[0, 0] = kv[:, MEM_W:].astype(BF16)


def _memory_kv(mem, g_mem, w_mem_kv_bf16, g_mem_k):
    depth = g_mem.shape[0]
    batch, mem_len, _ = mem.shape
    return pl.pallas_call(
        _memkv_kernel,
        grid=(depth, batch),
        in_specs=[
            pl.BlockSpec((1, mem_len, D_MODEL), lambda l, b: (b, 0, 0)),
            pl.BlockSpec((1, 1, D_MODEL), lambda l, b: (l, 0, 0)),
            pl.BlockSpec((1, D_MODEL, 2 * MEM_W), lambda l, b: (l, 0, 0)),
            pl.BlockSpec((1, 1, MEM_DH), lambda l, b: (l, 0, 0)),
        ],
        out_specs=[
            pl.BlockSpec((1, 1, MEM_W, mem_len), lambda l, b: (l, b, 0, 0)),
            pl.BlockSpec((1, 1, mem_len, MEM_W), lambda l, b: (l, b, 0, 0)),
        ],
        out_shape=[
            jax.ShapeDtypeStruct((depth, batch, MEM_W, mem_len), BF16),
            jax.ShapeDtypeStruct((depth, batch, mem_len, MEM_W), BF16),
        ],
        compiler_params=pltpu.CompilerParams(
            dimension_semantics=("arbitrary", "arbitrary"),
            vmem_limit_bytes=VMEM_LIMIT_BYTES),
        name="memkv",
    )(mem, g_mem.reshape(depth, 1, D_MODEL), w_mem_kv_bf16, g_mem_k.reshape(depth, 1, MEM_DH))


def _mem_attention_heads(memq, memgate, kt_ref, mv_ref, gq):
    outs = []
    for h in range(MEM_HEADS):
        sl = slice(h * MEM_DH, (h + 1) * MEM_DH)
        qn = _rms(memq[:, sl], gq).astype(BF16)
        logits = _dot(qn, kt_ref[0, 0, sl, :]) * (MEM_DH ** -0.5)
        e = jnp.exp(logits - jnp.max(logits, axis=-1, keepdims=True))
        o = _dot(e.astype(BF16), mv_ref[0, 0, :, sl]) / jnp.sum(e, axis=-1, keepdims=True)
        outs.append((o * _silu(memgate[:, sl])).astype(BF16))
    return outs


def _layer_a_kernel(x_ref, pos_ref, invf_ref, sgn_ref, gn_ref, win_ref, gret_ref, wout_ref,
                    kt_ref, mv_ref, gq_ref, o_ref, state_ref):
    tile = x_ref.shape[1]

    @pl.when(pl.program_id(1) == 0)
    def _():
        state_ref[...] = jnp.zeros_like(state_ref)

    x = x_ref[0]
    u = _dot(_rms(x, gn_ref[0]).astype(BF16), win_ref[0])

    def proj(c0, c1):
        return u[:, c0:c1]

    ang = pos_ref[0] * invf_ref[...]
    cos_full = jnp.cos(ang)
    sin_signed = jnp.sin(ang) * sgn_ref[...]

    def rotary(t):
        return t * cos_full + pltpu.roll(t, RET_DK // 2, 1) * sin_signed

    chunk = RET_CHUNK
    rel = (lax.broadcasted_iota(jnp.int32, (chunk, chunk), 0)
           - lax.broadcasted_iota(jnp.int32, (chunk, chunk), 1)).astype(F32)
    idx = lax.broadcasted_iota(jnp.int32, (chunk, 1), 0).astype(F32)
    g_ret = gret_ref[0]

    pieces = []
    o_v = 2 * RET_QK_W
    o_g = o_v + RET_V_W
    for h in range(RET_HEADS):
        lg = RET_LOG_GAMMA[h]
        q_all = rotary(proj(h * RET_DK, (h + 1) * RET_DK)).astype(BF16)
        k_all = rotary(proj(RET_QK_W + h * RET_DK, RET_QK_W + (h + 1) * RET_DK)) * (RET_DK ** -0.5)
        v_all = proj(o_v + h * RET_DV, o_v + (h + 1) * RET_DV).astype(BF16)
        decay = jnp.where(rel >= 0.0, jnp.exp(jnp.maximum(rel, 0.0) * lg), 0.0)
        cross_decay = jnp.exp((idx + 1.0) * lg)
        state_decay = jnp.exp((chunk - 1.0 - idx) * lg)
        state = state_ref[h]
        rets = []
        for c in range(tile // chunk):
            rows = slice(c * chunk, (c + 1) * chunk)
            q, k, v = q_all[rows], k_all[rows], v_all[rows]
            scores = _dot_nt(q, k.astype(BF16)) * decay
            intra = _dot(scores.astype(BF16), v)
            cross = _dot(q, state.astype(BF16)) * cross_decay
            state = (math.exp(chunk * lg) * state
                     + _dot((k * state_decay).T.astype(BF16), v))
            rets.append(intra + cross)
        state_ref[h] = state
        ret = _rms(jnp.concatenate(rets, axis=0), g_ret)
        gate = proj(o_g + h * RET_DV, o_g + (h + 1) * RET_DV)
        pieces.append((ret * _silu(gate)).astype(BF16))

    o_mq = o_g + RET_V_W
    pieces += _mem_attention_heads(proj(o_mq, o_mq + MEM_W), proj(o_mq + MEM_W, o_mq + 2 * MEM_W),
                                   kt_ref, mv_ref, gq_ref[0])
    cat = jnp.concatenate(pieces, axis=1)
    o_ref[0] = x + _dot(cat, wout_ref[0])


def _layer_a(x, pos_f32, invf, sgn, g_norm, w_in_bf16, g_ret_head, w_out_bf16, kt, mv, g_mem_q,
             la, layer):
    batch, seq, _ = x.shape
    tile = ROW_TILE_A
    n_a = g_norm.shape[0]
    depth = g_mem_q.shape[0]
    a_in_w = w_in_bf16.shape[2]
    a_out_in = w_out_bf16.shape[1]
    mem_len = mv.shape[2]
    return pl.pallas_call(
        _layer_a_kernel,
        grid=(batch, seq // tile),
        in_specs=[
            pl.BlockSpec((1, tile, D_MODEL), lambda b, t: (b, t, 0)),
            pl.BlockSpec((1, tile, 1), lambda b, t: (b, t, 0)),
            _const_spec((1, RET_DK), lambda b, t: (0, 0)),
            _const_spec((1, RET_DK), lambda b, t: (0, 0)),
            _const_spec((1, 1, D_MODEL), lambda b, t: (la, 0, 0)),
            _const_spec((1, D_MODEL, a_in_w), lambda b, t: (la, 0, 0)),
            _const_spec((1, 1, RET_DV), lambda b, t: (la, 0, 0)),
            _const_spec((1, a_out_in, D_MODEL), lambda b, t: (la, 0, 0)),
            pl.BlockSpec((1, 1, MEM_W, mem_len), lambda b, t: (layer, b, 0, 0)),
            pl.BlockSpec((1, 1, mem_len, MEM_W), lambda b, t: (layer, b, 0, 0)),
            _const_spec((1, 1, MEM_DH), lambda b, t: (layer, 0, 0)),
        ],
        out_specs=pl.BlockSpec((1, tile, D_MODEL), lambda b, t: (b, t, 0)),
        out_shape=jax.ShapeDtypeStruct(x.shape, F32),
        scratch_shapes=[pltpu.VMEM((RET_HEADS, RET_DK, RET_DV), F32)],
        compiler_params=pltpu.CompilerParams(
            dimension_semantics=("arbitrary", "arbitrary"),
            vmem_limit_bytes=VMEM_LIMIT_BYTES),
        name="layer_a",
    )(x, pos_f32, invf, sgn, g_norm.reshape(n_a, 1, D_MODEL), w_in_bf16,
      g_ret_head.reshape(n_a, 1, RET_DV), w_out_bf16, kt, mv, g_mem_q.reshape(depth, 1, MEM_DH))


def _proj_b_kernel(x_ref, gkv_ref, gnb_ref, wkv_ref, win_ref, kt_ref, mv_ref, gq_ref,
                   q_ref, k_ref, v_ref, gate_ref, mo_ref):
    x = x_ref[0]
    xr = x * lax.rsqrt(jnp.mean(x * x, axis=-1, keepdims=True) + EPS)
    kv = _dot((xr * gkv_ref[...]).astype(BF16), wkv_ref[...])
    k_ref[0] = kv[:, :SB_W].astype(BF16)
    v_ref[0] = kv[:, SB_W:].astype(BF16)
    u = _dot((xr * gnb_ref[0]).astype(BF16), win_ref[0])
    q_ref[0] = u[:, :SB_W].astype(BF16)
    gate_ref[0] = _silu(u[:, SB_W:2 * SB_W]).astype(BF16)
    o_mq = 2 * SB_W
    mo = _mem_attention_heads(u[:, o_mq:o_mq + MEM_W], u[:, o_mq + MEM_W:], kt_ref, mv_ref, gq_ref[0])
    mo_ref[0] = jnp.concatenate(mo, axis=1)


def _proj_b(x, g_kv, g_norm_b, w_kv_bf16, w_in_bf16, kt, mv, g_mem_q, lb, layer):
    batch, seq, _ = x.shape
    tile = ROW_TILE_B
    n_b = g_norm_b.shape[0]
    depth = g_mem_q.shape[0]
    mem_len = mv.shape[2]
    b_in_w = w_in_bf16.shape[2]
    row_spec = lambda w: pl.BlockSpec((1, tile, w), lambda b, t: (b, t, 0))
    widths = (SB_W, SB_W, SB_W, SB_W, MEM_W)
    return pl.pallas_call(
        _proj_b_kernel,
        grid=(batch, seq // tile),
        in_specs=[
            row_spec(D_MODEL),
            _const_spec((1, D_MODEL), lambda b, t: (0, 0)),
            _const_spec((1, 1, D_MODEL), lambda b, t: (lb, 0, 0)),
            _const_spec((D_MODEL, 2 * SB_W), lambda b, t: (0, 0)),
            _const_spec((1, D_MODEL, b_in_w), lambda b, t: (lb, 0, 0)),
            pl.BlockSpec((1, 1, MEM_W, mem_len), lambda b, t: (layer, b, 0, 0)),
            pl.BlockSpec((1, 1, mem_len, MEM_W), lambda b, t: (layer, b, 0, 0)),
            _const_spec((1, 1, MEM_DH), lambda b, t: (layer, 0, 0)),
        ],
        out_specs=[row_spec(w) for w in widths],
        out_shape=[jax.ShapeDtypeStruct((batch, seq, w), BF16) for w in widths],
        compiler_params=pltpu.CompilerParams(
            dimension_semantics=("arbitrary", "arbitrary"),
            vmem_limit_bytes=VMEM_LIMIT_BYTES),
        name="proj_b",
    )(x, g_kv.reshape(1, D_MODEL), g_norm_b.reshape(n_b, 1, D_MODEL), w_kv_bf16, w_in_bf16,
      kt, mv, g_mem_q.reshape(depth, 1, MEM_DH))


def _sb_out_kernel(x_ref, q_ref, k_ref, v_ref, gate_ref, mo_ref, wout_ref, o_ref, cat_ref):
    tile = q_ref.shape[1]
    heads = q_ref.shape[2] // SB_DH
    qi = pl.program_id(1)
    scale = SB_DH ** -0.5

    row = lax.broadcasted_iota(jnp.int32, (tile, tile), 0)
    col = lax.broadcasted_iota(jnp.int32, (tile, tile), 1)
    causal = col < row
    urow = lax.broadcasted_iota(jnp.int32, (2 * tile, tile), 0)
    ucol = lax.broadcasted_iota(jnp.int32, (2 * tile, tile), 1)
    urow = jnp.where(urow >= tile, urow - tile, urow)
    suffix_mat = jnp.where(urow > ucol, 1.0, 0.0).astype(BF16)

    def window(g, start, nblk, diag, penalty):
        hs = slice(g * SB_DH, (g + 1) * SB_DH)
        q = q_ref[0, :, hs]
        kb = k_ref[0, pl.ds(start, nblk * tile), hs]
        vb = v_ref[0, pl.ds(start, nblk * tile), hs]
        zr = _dot_nt(q, kb)
        z = zr * scale
        sp = jnp.maximum(z, 0.0) + jnp.log(1.0 + jnp.exp2(jnp.abs(zr) * (-scale * LOG2E)))
        blocks = [sp[:, j * tile:(j + 1) * tile] for j in range(nblk)]
        if diag:
            blocks[-1] = jnp.where(causal, blocks[-1], 0.0)
        split = []
        for blk in blocks:
            hi = blk.astype(BF16)
            lo = (blk - hi.astype(F32)).astype(BF16)
            split.append(jnp.concatenate([hi, lo], axis=1))
        suffix = _dot(jnp.concatenate(split, axis=0), suffix_mat)
        weights = [None] * nblk
        for j in reversed(range(nblk)):
            sfx = suffix[j * tile:(j + 1) * tile]
            log_a = (z[:, j * tile:(j + 1) * tile] - blocks[j]) - sfx
            if penalty is not None:
                log_a = log_a - penalty
            a = jnp.exp(log_a)
            if diag and j == nblk - 1:
                a = jnp.where(causal, a, 0.0)
            weights[j] = a.astype(BF16)
            total = sfx[:, 0:1] + blocks[j][:, 0:1]
            penalty = total if penalty is None else penalty + total
        acc = _dot(jnp.concatenate(weights, axis=1), vb)
        return acc, penalty

    def store(accs):
        for g in range(heads):
            hs = slice(g * SB_DH, (g + 1) * SB_DH)
            cat_ref[:, hs] = (accs[g] * gate_ref[0, :, hs].astype(F32)).astype(BF16)

    @pl.when(qi == 0)
    def _():
        store([window(g, 0, 1, True, None)[0] for g in range(heads)])

    @pl.when(qi > 0)
    def _():
        near = [window(g, pl.multiple_of((qi - 1) * tile, tile), 2, True, None)
                for g in range(heads)]
        accs = [r[0] for r in near]
        penalties = [r[1] for r in near]
        store(accs)

        def log_weight_bound(pens):
            return -functools.reduce(jnp.minimum, [jnp.min(p) for p in pens])

        def cond(carry):
            return jnp.logical_and(carry[0] <= qi, carry[1] > SB_LOG_WEIGHT_FLOOR)

        def body(carry):
            n, _, accs, pens = carry
            start = pl.multiple_of((qi - n) * tile, tile)
            far = [window(g, start, 1, False, pens[g]) for g in range(heads)]
            accs = [accs[g] + far[g][0] for g in range(heads)]
            pens = [far[g][1] for g in range(heads)]
            return n + 1, log_weight_bound(pens), accs, pens

        bound = log_weight_bound(penalties)

        @pl.when(jnp.logical_and(qi >= 2, bound > SB_LOG_WEIGHT_FLOOR))
        def _():
            _, _, accs_far, _ = lax.while_loop(cond, body, (jnp.int32(2), bound, accs, penalties))
            store(accs_far)

    cat_ref[:, SB_W:] = mo_ref[0]
    o_ref[0] = x_ref[0] + _dot(cat_ref[...], wout_ref[0])


def _sb_out(x, q, k, v, gate, mo, w_out_bf16, lb):
    batch, seq, _ = x.shape
    tile = SB_TILE
    b_out_in = w_out_bf16.shape[1]
    row_spec = lambda w: pl.BlockSpec((1, tile, w), lambda b, i: (b, i, 0))
    kv_spec = _const_spec((1, seq, SB_W), lambda b, i: (b, 0, 0))
    return pl.pallas_call(
        _sb_out_kernel,
        grid=(batch, seq // tile),
        in_specs=[
            row_spec(D_MODEL), row_spec(SB_W), kv_spec, kv_spec, row_spec(SB_W), row_spec(MEM_W),
            _const_spec((1, b_out_in, D_MODEL), lambda b, i: (lb, 0, 0)),
        ],
        out_specs=row_spec(D_MODEL),
        out_shape=jax.ShapeDtypeStruct(x.shape, F32),
        scratch_shapes=[pltpu.VMEM((tile, b_out_in), BF16)],
        compiler_params=pltpu.CompilerParams(
            dimension_semantics=("arbitrary", "arbitrary"),
            vmem_limit_bytes=SB_VMEM_LIMIT_BYTES),
        name="sb_out",
    )(x, q, k, v, gate, mo, w_out_bf16)


def kernel(x, mem, positions, g_norm_a, w_in_a, g_ret_head, w_out_a, g_kv, w_kv, g_norm_b, w_in_b,
           w_out_b, g_mem, w_mem_kv, g_mem_q, g_mem_k):
    n_a = g_norm_a.shape[0]
    n_b = g_norm_b.shape[0]
    batch, seq, _ = x.shape

    kt, mv = _memory_kv(mem, g_mem, w_mem_kv.astype(BF16), g_mem_k)

    inv_freq = ROPE_BASE ** (-jnp.arange(0, RET_DK // 2, dtype=F32) * 2.0 / RET_DK)
    invf = jnp.concatenate([inv_freq, inv_freq]).reshape(1, RET_DK)
    sgn = jnp.concatenate([-jnp.ones((RET_DK // 2,), F32), jnp.ones((RET_DK // 2,), F32)]
                          ).reshape(1, RET_DK)
    pos_f32 = positions.astype(F32).reshape(batch, seq, 1)

    w_in_a_bf16 = w_in_a.astype(BF16)
    w_out_a_bf16 = w_out_a.astype(BF16)
    w_in_b_bf16 = w_in_b.astype(BF16)
    w_out_b_bf16 = w_out_b.astype(BF16)
    w_kv_bf16 = w_kv.astype(BF16)

    for la in range(n_a):
        x = _layer_a(x, pos_f32, invf, sgn, g_norm_a, w_in_a_bf16, g_ret_head, w_out_a_bf16,
                     kt, mv, g_mem_q, la, la)

    k_shared = v_shared = None
    for lb in range(n_b):
        layer = n_a + lb
        q, k_new, v_new, gate, mo = _proj_b(x, g_kv, g_norm_b, w_kv_bf16, w_in_b_bf16, kt, mv,
                                            g_mem_q, lb, layer)
        if lb == 0:
            k_shared, v_shared = k_new, v_new
        x = _sb_out(x, q, k_shared, v_shared, gate, mo, w_out_b_bf16, lb)
    return x
```

```python
import functools
import math

import numpy as np
import jax
import jax.numpy as jnp
from jax import lax
from jax.experimental import pallas as pl
from jax.experimental.pallas import tpu as pltpu

F32 = jnp.float32
BF16 = jnp.bfloat16

D_MODEL = 1024
RET_HEADS = 4
RET_DK = 128
RET_DV = 256
RET_QK_W = RET_HEADS * RET_DK
RET_V_W = RET_HEADS * RET_DV
ROPE_BASE = 10000.0
SB_HEADS = 8
SB_DH = 128
SB_W = SB_HEADS * SB_DH
MEM_HEADS = 4
MEM_DH = 128
MEM_W = MEM_HEADS * MEM_DH
EPS = 1e-6

RET_LOG_GAMMA = tuple(
    math.log(float(np.float32(1.0 - 2.0 ** (-5.0 - h)))) for h in range(RET_HEADS))

VMEM_LIMIT_BYTES = 48 * 1024 * 1024

ROW_TILE_A = 512
RET_CHUNK = 256
ROW_TILE_B = 512
SB_TILE = 256
SB_LOG_WEIGHT_FLOOR = -104.0
SB_VMEM_LIMIT_BYTES = 58 * 1024 * 1024
LOG2E = math.log2(math.e)


def _rms(x, g):
    return x * lax.rsqrt(jnp.mean(x * x, axis=-1, keepdims=True) + EPS) * g


def _silu(x):
    return x * (1.0 / (1.0 + jnp.exp(-x)))


def _dot(a, b):
    return jnp.dot(a, b, preferred_element_type=F32)


def _dot_nt(a, b):
    return lax.dot_general(a, b, (((1,), (1,)), ((), ())), preferred_element_type=F32)


def _const_spec(shape, index_map):
    return pl.BlockSpec(shape, index_map, pipeline_mode=pl.Buffered(1))


def _memkv_kernel(mem_ref, g_ref, w_ref, gk_ref, kt_ref, v_ref):
    mn = _rms(mem_ref[0], g_ref[0]).astype(BF16)
    kv = _dot(mn, w_ref[0])
    gk = gk_ref[0]
    for h in range(MEM_HEADS):
        kh = _rms(kv[:, h * MEM_DH:(h + 1) * MEM_DH], gk)
        kt_ref[0, 0, h * MEM_DH:(h + 1) * MEM_DH, :] = kh.T.astype(BF16)
    v_ref[0, 0] = kv[:, MEM_W:].astype(BF16)


def _memory_kv(mem, g_mem, w_mem_kv_bf16, g_mem_k):
    depth = g_mem.shape[0]
    batch, mem_len, _ = mem.shape
    return pl.pallas_call(
        _memkv_kernel,
        grid=(depth, batch),
        in_specs=[
            pl.BlockSpec((1, mem_len, D_MODEL), lambda l, b: (b, 0, 0)),
            pl.BlockSpec((1, 1, D_MODEL), lambda l, b: (l, 0, 0)),
            pl.BlockSpec((1, D_MODEL, 2 * MEM_W), lambda l, b: (l, 0, 0)),
            pl.BlockSpec((1, 1, MEM_DH), lambda l, b: (l, 0, 0)),
        ],
        out_specs=[
            pl.BlockSpec((1, 1, MEM_W, mem_len), lambda l, b: (l, b, 0, 0)),
            pl.BlockSpec((1, 1, mem_len, MEM_W), lambda l, b: (l, b, 0, 0)),
        ],
        out_shape=[
            jax.ShapeDtypeStruct((depth, batch, MEM_W, mem_len), BF16),
            jax.ShapeDtypeStruct((depth, batch, mem_len, MEM_W), BF16),
        ],
        compiler_params=pltpu.CompilerParams(
            dimension_semantics=("arbitrary", "arbitrary"),
            vmem_limit_bytes=VMEM_LIMIT_BYTES),
        name="memkv",
    )(mem, g_mem.reshape(depth, 1, D_MODEL), w_mem_kv_bf16, g_mem_k.reshape(depth, 1, MEM_DH))


def _mem_attention_heads(memq, memgate, kt_ref, mv_ref, gq):
    outs = []
    for h in range(MEM_HEADS):
        sl = slice(h * MEM_DH, (h + 1) * MEM_DH)
        qn = _rms(memq[:, sl], gq).astype(BF16)
        logits = _dot(qn, kt_ref[0, 0, sl, :]) * (MEM_DH ** -0.5)
        e = jnp.exp(logits - jnp.max(logits, axis=-1, keepdims=True))
        o = _dot(e.astype(BF16), mv_ref[0, 0, :, sl]) / jnp.sum(e, axis=-1, keepdims=True)
        outs.append((o * _silu(memgate[:, sl])).astype(BF16))
    return outs


def _layer_a_kernel(x_ref, pos_ref, invf_ref, sgn_ref, gn_ref, win_ref, gret_ref, wout_ref,
                    kt_ref, mv_ref, gq_ref, o_ref, state_ref):
    tile = x_ref.shape[1]

    @pl.when(pl.program_id(1) == 0)
    def _():
        state_ref[...] = jnp.zeros_like(state_ref)

    x = x_ref[0]
    u = _dot(_rms(x, gn_ref[0]).astype(BF16), win_ref[0])

    def proj(c0, c1):
        return u[:, c0:c1]

    ang = pos_ref[0] * invf_ref[...]
    cos_full = jnp.cos(ang)
    sin_signed = jnp.sin(ang) * sgn_ref[...]

    def rotary(t):
        return t * cos_full + pltpu.roll(t, RET_DK // 2, 1) * sin_signed

    chunk = RET_CHUNK
    rel = (lax.broadcasted_iota(jnp.int32, (chunk, chunk), 0)
           - lax.broadcasted_iota(jnp.int32, (chunk, chunk), 1)).astype(F32)
    idx = lax.broadcasted_iota(jnp.int32, (chunk, 1), 0).astype(F32)
    g_ret = gret_ref[0]

    pieces = []
    o_v = 2 * RET_QK_W
    o_g = o_v + RET_V_W
    for h in range(RET_HEADS):
        lg = RET_LOG_GAMMA[h]
        q_all = rotary(proj(h * RET_DK, (h + 1) * RET_DK)).astype(BF16)
        k_all = rotary(proj(RET_QK_W + h * RET_DK, RET_QK_W + (h + 1) * RET_DK)) * (RET_DK ** -0.5)
        v_all = proj(o_v + h * RET_DV, o_v + (h + 1) * RET_DV).astype(BF16)
        decay = jnp.where(rel >= 0.0, jnp.exp(jnp.maximum(rel, 0.0) * lg), 0.0)
        cross_decay = jnp.exp((idx + 1.0) * lg)
        state_decay = jnp.exp((chunk - 1.0 - idx) * lg)
        state = state_ref[h]
        rets = []
        for c in range(tile // chunk):
            rows = slice(c * chunk, (c + 1) * chunk)
            q, k, v = q_all[rows], k_all[rows], v_all[rows]
            scores = _dot_nt(q, k.astype(BF16)) * decay
            intra = _dot(scores.astype(BF16), v)
            cross = _dot(q, state.astype(BF16)) * cross_decay
            state = (math.exp(chunk * lg) * state
                     + _dot((k * state_decay).T.astype(BF16), v))
            rets.append(intra + cross)
        state_ref[h] = state
        ret = _rms(jnp.concatenate(rets, axis=0), g_ret)
        gate = proj(o_g + h * RET_DV, o_g + (h + 1) * RET_DV)
        pieces.append((ret * _silu(gate)).astype(BF16))

    o_mq = o_g + RET_V_W
    pieces += _mem_attention_heads(proj(o_mq, o_mq + MEM_W), proj(o_mq + MEM_W, o_mq + 2 * MEM_W),
                                   kt_ref, mv_ref, gq_ref[0])
    cat = jnp.concatenate(pieces, axis=1)
    o_ref[0] = x + _dot(cat, wout_ref[0])


def _layer_a(x, pos_f32, invf, sgn, g_norm, w_in_bf16, g_ret_head, w_out_bf16, kt, mv, g_mem_q,
             la, layer):
    batch, seq, _ = x.shape
    tile = ROW_TILE_A
    n_a = g_norm.shape[0]
    depth = g_mem_q.shape[0]
    a_in_w = w_in_bf16.shape[2]
    a_out_in = w_out_bf16.shape[1]
    mem_len = mv.shape[2]
    return pl.pallas_call(
        _layer_a_kernel,
        grid=(batch, seq // tile),
        in_specs=[
            pl.BlockSpec((1, tile, D_MODEL), lambda b, t: (b, t, 0)),
            pl.BlockSpec((1, tile, 1), lambda b, t: (b, t, 0)),
            _const_spec((1, RET_DK), lambda b, t: (0, 0)),
            _const_spec((1, RET_DK), lambda b, t: (0, 0)),
            _const_spec((1, 1, D_MODEL), lambda b, t: (la, 0, 0)),
            _const_spec((1, D_MODEL, a_in_w), lambda b, t: (la, 0, 0)),
            _const_spec((1, 1, RET_DV), lambda b, t: (la, 0, 0)),
            _const_spec((1, a_out_in, D_MODEL), lambda b, t: (la, 0, 0)),
            pl.BlockSpec((1, 1, MEM_W, mem_len), lambda b, t: (layer, b, 0, 0)),
            pl.BlockSpec((1, 1, mem_len, MEM_W), lambda b, t: (layer, b, 0, 0)),
            _const_spec((1, 1, MEM_DH), lambda b, t: (layer, 0, 0)),
        ],
        out_specs=pl.BlockSpec((1, tile, D_MODEL), lambda b, t: (b, t, 0)),
        out_shape=jax.ShapeDtypeStruct(x.shape, F32),
        scratch_shapes=[pltpu.VMEM((RET_HEADS, RET_DK, RET_DV), F32)],
        compiler_params=pltpu.CompilerParams(
            dimension_semantics=("arbitrary", "arbitrary"),
            vmem_limit_bytes=VMEM_LIMIT_BYTES),
        name="layer_a",
    )(x, pos_f32, invf, sgn, g_norm.reshape(n_a, 1, D_MODEL), w_in_bf16,
      g_ret_head.reshape(n_a, 1, RET_DV), w_out_bf16, kt, mv, g_mem_q.reshape(depth, 1, MEM_DH))


def _proj_b_kernel(x_ref, gkv_ref, gnb_ref, wkv_ref, win_ref, kt_ref, mv_ref, gq_ref,
                   q_ref, k_ref, v_ref, gate_ref, mo_ref):
    x = x_ref[0]
    xr = x * lax.rsqrt(jnp.mean(x * x, axis=-1, keepdims=True) + EPS)
    kv = _dot((xr * gkv_ref[...]).astype(BF16), wkv_ref[...])
    k_ref[0] = kv[:, :SB_W].astype(BF16)
    v_ref[0] = kv[:, SB_W:].astype(BF16)
    u = _dot((xr * gnb_ref[0]).astype(BF16), win_ref[0])
    q_ref[0] = u[:, :SB_W].astype(BF16)
    gate_ref[0] = _silu(u[:, SB_W:2 * SB_W]).astype(BF16)
    o_mq = 2 * SB_W
    mo = _mem_attention_heads(u[:, o_mq:o_mq + MEM_W], u[:, o_mq + MEM_W:], kt_ref, mv_ref, gq_ref[0])
    mo_ref[0] = jnp.concatenate(mo, axis=1)


def _proj_b(x, g_kv, g_norm_b, w_kv_bf16, w_in_bf16, kt, mv, g_mem_q, lb, layer):
    batch, seq, _ = x.shape
    tile = ROW_TILE_B
    n_b = g_norm_b.shape[0]
    depth = g_mem_q.shape[0]
    mem_len = mv.shape[2]
    b_in_w = w_in_bf16.shape[2]
    row_spec = lambda w: pl.BlockSpec((1, tile, w), lambda b, t: (b, t, 0))
    widths = (SB_W, SB_W, SB_W, SB_W, MEM_W)
    return pl.pallas_call(
        _proj_b_kernel,
        grid=(batch, seq // tile),
        in_specs=[
            row_spec(D_MODEL),
            _const_spec((1, D_MODEL), lambda b, t: (0, 0)),
            _const_spec((1, 1, D_MODEL), lambda b, t: (lb, 0, 0)),
            _const_spec((D_MODEL, 2 * SB_W), lambda b, t: (0, 0)),
            _const_spec((1, D_MODEL, b_in_w), lambda b, t: (lb, 0, 0)),
            pl.BlockSpec((1, 1, MEM_W, mem_len), lambda b, t: (layer, b, 0, 0)),
            pl.BlockSpec((1, 1, mem_len, MEM_W), lambda b, t: (layer, b, 0, 0)),
            _const_spec((1, 1, MEM_DH), lambda b, t: (layer, 0, 0)),
        ],
        out_specs=[row_spec(w) for w in widths],
        out_shape=[jax.ShapeDtypeStruct((batch, seq, w), BF16) for w in widths],
        compiler_params=pltpu.CompilerParams(
            dimension_semantics=("arbitrary", "arbitrary"),
            vmem_limit_bytes=VMEM_LIMIT_BYTES),
        name="proj_b",
    )(x, g_kv.reshape(1, D_MODEL), g_norm_b.reshape(n_b, 1, D_MODEL), w_kv_bf16, w_in_bf16,
      kt, mv, g_mem_q.reshape(depth, 1, MEM_DH))


def _sb_out_kernel(x_ref, q_ref, k_ref, v_ref, gate_ref, mo_ref, wout_ref, o_ref, cat_ref):
    tile = q_ref.shape[1]
    heads = q_ref.shape[2] // SB_DH
    qi = pl.program_id(1)
    scale = SB_DH ** -0.5

    row = lax.broadcasted_iota(jnp.int32, (tile, tile), 0)
    col = lax.broadcasted_iota(jnp.int32, (tile, tile), 1)
    causal = col < row
    urow = lax.broadcasted_iota(jnp.int32, (2 * tile, tile), 0)
    ucol = lax.broadcasted_iota(jnp.int32, (2 * tile, tile), 1)
    urow = jnp.where(urow >= tile, urow - tile, urow)
    suffix_mat = jnp.where(urow > ucol, 1.0, 0.0).astype(BF16)


    def stage_scores(g, start, nblk, diag):
        hs = slice(g * SB_DH, (g + 1) * SB_DH)
        zr = _dot_nt(q_ref[0, :, hs], k_ref[0, pl.ds(start, nblk * tile), hs])
        z = zr * scale
        sp = jnp.maximum(z, 0.0) + jnp.log(1.0 + jnp.exp2(jnp.abs(zr) * (-scale * LOG2E)))
        blocks = [sp[:, j * tile:(j + 1) * tile] for j in range(nblk)]
        if diag:
            blocks[-1] = jnp.where(causal, blocks[-1], 0.0)
        split = []
        for blk in blocks:
            hi = blk.astype(BF16)
            lo = (blk - hi.astype(F32)).astype(BF16)
            split.append(jnp.concatenate([hi, lo], axis=1))
        return z, blocks, jnp.concatenate(split, axis=0)

    def stage_weights(scores, nblk, diag, penalty):
        z, blocks, split = scores
        suffix = _dot(split, suffix_mat)
        weights = [None] * nblk
        for j in reversed(range(nblk)):
            sfx = suffix[j * tile:(j + 1) * tile]
            log_a = (z[:, j * tile:(j + 1) * tile] - blocks[j]) - sfx
            if penalty is not None:
                log_a = log_a - penalty
            a = jnp.exp(log_a)
            if diag and j == nblk - 1:
                a = jnp.where(causal, a, 0.0)
            weights[j] = a.astype(BF16)
            total = sfx[:, 0:1] + blocks[j][:, 0:1]
            penalty = total if penalty is None else penalty + total
        return jnp.concatenate(weights, axis=1), penalty

    def stage_values(g, start, nblk, weights):
        hs = slice(g * SB_DH, (g + 1) * SB_DH)
        return _dot(weights, v_ref[0, pl.ds(start, nblk * tile), hs])

    def windows(start, nblk, diag, penalties):
        scores, weights, out = {}, {}, [None] * heads
        for step in range(heads + 2):
            if step < heads:
                scores[step] = stage_scores(step, start, nblk, diag)
            g = step - 1
            if 0 <= g < heads:
                weights[g] = stage_weights(scores.pop(g), nblk, diag,
                                           None if penalties is None else penalties[g])
            g = step - 2
            if 0 <= g < heads:
                w, penalty = weights.pop(g)
                out[g] = (stage_values(g, start, nblk, w), penalty)
        return out

    def store(accs):
        for g in range(heads):
            hs = slice(g * SB_DH, (g + 1) * SB_DH)
            cat_ref[:, hs] = (accs[g] * gate_ref[0, :, hs].astype(F32)).astype(BF16)

    @pl.when(qi == 0)
    def _():
        store([r[0] for r in windows(0, 1, True, None)])

    @pl.when(qi > 0)
    def _():
        near = windows(pl.multiple_of((qi - 1) * tile, tile), 2, True, None)
        accs = [r[0] for r in near]
        penalties = [r[1] for r in near]
        store(accs)

        def log_weight_bound(pens):
            return -functools.reduce(jnp.minimum, [jnp.min(p) for p in pens])

        def cond(carry):
            return jnp.logical_and(carry[0] <= qi, carry[1] > SB_LOG_WEIGHT_FLOOR)

        def body(carry):
            n, _, accs, pens = carry
            start = pl.multiple_of((qi - n) * tile, tile)
            far = windows(start, 1, False, pens)
            accs = [accs[g] + far[g][0] for g in range(heads)]
            pens = [far[g][1] for g in range(heads)]
            return n + 1, log_weight_bound(pens), accs, pens

        bound = log_weight_bound(penalties)

        @pl.when(jnp.logical_and(qi >= 2, bound > SB_LOG_WEIGHT_FLOOR))
        def _():
            _, _, accs_far, _ = lax.while_loop(cond, body, (jnp.int32(2), bound, accs, penalties))
            store(accs_far)

    cat_ref[:, SB_W:] = mo_ref[0]
    o_ref[0] = x_ref[0] + _dot(cat_ref[...], wout_ref[0])


def _sb_out(x, q, k, v, gate, mo, w_out_bf16, lb):
    batch, seq, _ = x.shape
    tile = SB_TILE
    b_out_in = w_out_bf16.shape[1]
    row_spec = lambda w: pl.BlockSpec((1, tile, w), lambda b, i: (b, i, 0))
    kv_spec = _const_spec((1, seq, SB_W), lambda b, i: (b, 0, 0))
    return pl.pallas_call(
        _sb_out_kernel,
        grid=(batch, seq // tile),
        in_specs=[
            row_spec(D_MODEL), row_spec(SB_W), kv_spec, kv_spec, row_spec(SB_W), row_spec(MEM_W),
            _const_spec((1, b_out_in, D_MODEL), lambda b, i: (lb, 0, 0)),
        ],
        out_specs=row_spec(D_MODEL),
        out_shape=jax.ShapeDtypeStruct(x.shape, F32),
        scratch_shapes=[pltpu.VMEM((tile, b_out_in), BF16)],
        compiler_params=pltpu.CompilerParams(
            dimension_semantics=("arbitrary", "arbitrary"),
            vmem_limit_bytes=SB_VMEM_LIMIT_BYTES),
        name="sb_out",
    )(x, q, k, v, gate, mo, w_out_bf16)


def kernel(x, mem, positions, g_norm_a, w_in_a, g_ret_head, w_out_a, g_kv, w_kv, g_norm_b, w_in_b,
           w_out_b, g_mem, w_mem_kv, g_mem_q, g_mem_k):
    n_a = g_norm_a.shape[0]
    n_b = g_norm_b.shape[0]
    batch, seq, _ = x.shape

    kt, mv = _memory_kv(mem, g_mem, w_mem_kv.astype(BF16), g_mem_k)

    inv_freq = ROPE_BASE ** (-jnp.arange(0, RET_DK // 2, dtype=F32) * 2.0 / RET_DK)
    invf = jnp.concatenate([inv_freq, inv_freq]).reshape(1, RET_DK)
    sgn = jnp.concatenate([-jnp.ones((RET_DK // 2,), F32), jnp.ones((RET_DK // 2,), F32)]
                          ).reshape(1, RET_DK)
    pos_f32 = positions.astype(F32).reshape(batch, seq, 1)

    w_in_a_bf16 = w_in_a.astype(BF16)
    w_out_a_bf16 = w_out_a.astype(BF16)
    w_in_b_bf16 = w_in_b.astype(BF16)
    w_out_b_bf16 = w_out_b.astype(BF16)
    w_kv_bf16 = w_kv.astype(BF16)

    for la in range(n_a):
        x = _layer_a(x, pos_f32, invf, sgn, g_norm_a, w_in_a_bf16, g_ret_head, w_out_a_bf16,
                     kt, mv, g_mem_q, la, la)

    k_shared = v_shared = None
    for lb in range(n_b):
        layer = n_a + lb
        q, k_new, v_new, gate, mo = _proj_b(x, g_kv, g_norm_b, w_kv_bf16, w_in_b_bf16, kt, mv,
                                            g_mem_q, lb, layer)
        if lb == 0:
            k_shared, v_shared = k_new, v_new
        x = _sb_out(x, q, k_shared, v_shared, gate, mo, w_out_b_bf16, lb)
    return x
```

```python
import functools
import math

import numpy as np
import jax
import jax.numpy as jnp
from jax import lax
from jax.experimental import pallas as pl
from jax.experimental.pallas import tpu as pltpu

F32 = jnp.float32
BF16 = jnp.bfloat16

D_MODEL = 1024
RET_HEADS = 4
RET_DK = 128
RET_DV = 256
RET_QK_W = RET_HEADS * RET_DK
RET_V_W = RET_HEADS * RET_DV
ROPE_BASE = 10000.0
SB_HEADS = 8
SB_DH = 128
SB_W = SB_HEADS * SB_DH
MEM_HEADS = 4
MEM_DH = 128
MEM_W = MEM_HEADS * MEM_DH
EPS = 1e-6

RET_LOG_GAMMA = tuple(
    math.log(float(np.float32(1.0 - 2.0 ** (-5.0 - h)))) for h in range(RET_HEADS))

VMEM_LIMIT_BYTES = 48 * 1024 * 1024

ROW_TILE_A = 512
RET_CHUNK = 256
IN_PIECE_A = 512
OUT_PIECE_A = 256
ROW_TILE_B = 512
SB_TILE = 256
SB_LOG_WEIGHT_FLOOR = -151.0
SB_VMEM_LIMIT_BYTES = 58 * 1024 * 1024
LOG2E = math.log2(math.e)


def _rms(x, g):
    return x * lax.rsqrt(jnp.mean(x * x, axis=-1, keepdims=True) + EPS) * g


def _silu(x):
    return x * (1.0 / (1.0 + jnp.exp(-x)))


def _dot(a, b):
    return jnp.dot(a, b, preferred_element_type=F32)


def _dot_nt(a, b):
    return lax.dot_general(a, b, (((1,), (1,)), ((), ())), preferred_element_type=F32)


def _const_spec(shape, index_map):
    return pl.BlockSpec(shape, index_map, pipeline_mode=pl.Buffered(1))


def _interleaved(*task_lists):
    tagged = [((i + 0.5) / len(tasks), k, task)
              for k, tasks in enumerate(task_lists) for i, task in enumerate(tasks)]
    return [task for _, _, task in sorted(tagged, key=lambda e: e[:2])]


def _memkv_kernel(mem_ref, g_ref, w_ref, gk_ref, kt_ref, v_ref):
    mn = _rms(mem_ref[0], g_ref[0]).astype(BF16)
    kv = _dot(mn, w_ref[0])
    gk = gk_ref[0]
    for h in range(MEM_HEADS):
        kh = _rms(kv[:, h * MEM_DH:(h + 1) * MEM_DH], gk)
        kt_ref[0, 0, h * MEM_DH:(h + 1) * MEM_DH, :] = kh.T.astype(BF16)
    v_ref[0, 0] = kv[:, MEM_W:].astype(BF16)


def _memory_kv(mem, g_mem, w_mem_kv_bf16, g_mem_k):
    depth = g_mem.shape[0]
    batch, mem_len, _ = mem.shape
    return pl.pallas_call(
        _memkv_kernel,
        grid=(depth, batch),
        in_specs=[
            pl.BlockSpec((1, mem_len, D_MODEL), lambda l, b: (b, 0, 0)),
            pl.BlockSpec((1, 1, D_MODEL), lambda l, b: (l, 0, 0)),
            pl.BlockSpec((1, D_MODEL, 2 * MEM_W), lambda l, b: (l, 0, 0)),
            pl.BlockSpec((1, 1, MEM_DH), lambda l, b: (l, 0, 0)),
        ],
        out_specs=[
            pl.BlockSpec((1, 1, MEM_W, mem_len), lambda l, b: (l, b, 0, 0)),
            pl.BlockSpec((1, 1, mem_len, MEM_W), lambda l, b: (l, b, 0, 0)),
        ],
        out_shape=[
            jax.ShapeDtypeStruct((depth, batch, MEM_W, mem_len), BF16),
            jax.ShapeDtypeStruct((depth, batch, mem_len, MEM_W), BF16),
        ],
        compiler_params=pltpu.CompilerParams(
            dimension_semantics=("arbitrary", "arbitrary"),
            vmem_limit_bytes=VMEM_LIMIT_BYTES),
        name="memkv",
    )(mem, g_mem.reshape(depth, 1, D_MODEL), w_mem_kv_bf16, g_mem_k.reshape(depth, 1, MEM_DH))


def _mem_attention_head(h, memq, memgate, kt_ref, mv_ref, gq):
    sl = slice(h * MEM_DH, (h + 1) * MEM_DH)
    qn = _rms(memq, gq).astype(BF16)
    logits = _dot(qn, kt_ref[0, 0, sl, :]) * (MEM_DH ** -0.5)
    e = jnp.exp(logits - jnp.max(logits, axis=-1, keepdims=True))
    o = _dot(e.astype(BF16), mv_ref[0, 0, :, sl]) / jnp.sum(e, axis=-1, keepdims=True)
    return (o * _silu(memgate)).astype(BF16)


def _mem_attention_heads(memq, memgate, kt_ref, mv_ref, gq):
    return [_mem_attention_head(h, memq[:, h * MEM_DH:(h + 1) * MEM_DH],
                                memgate[:, h * MEM_DH:(h + 1) * MEM_DH], kt_ref, mv_ref, gq)
            for h in range(MEM_HEADS)]


def _layer_a_kernel(x_ref, pos_ref, invf_ref, gn_ref, win_ref, gret_ref, wout_ref,
                    kt_ref, mv_ref, gq_ref, o_ref, state_ref):
    tile = x_ref.shape[1]
    chunk = RET_CHUNK
    n_chunks = tile // chunk
    half = RET_DK // 2
    n_in = win_ref.shape[2] // IN_PIECE_A
    n_out = wout_ref.shape[2] // OUT_PIECE_A
    o_v = 2 * RET_QK_W
    o_g = o_v + RET_V_W
    o_mq = o_g + RET_V_W

    @pl.when(pl.program_id(1) == 0)
    def _():
        state_ref[...] = jnp.zeros_like(state_ref)

    tables = {}
    states = [state_ref[h] for h in range(RET_HEADS)]
    xs, xns = {}, {}
    u = {}
    mixed = {}
    cats = {}

    def rotary_tables():
        ang_t = invf_ref[...] * pos_ref[0]
        cos_sin = jnp.concatenate([jnp.cos(ang_t), jnp.sin(ang_t)], axis=0).T
        sin_cos = pltpu.roll(cos_sin, half, 1)
        low_lanes = lax.broadcasted_iota(jnp.int32, (tile, RET_DK), 1) < half
        tables["cos"] = jnp.where(low_lanes, cos_sin, sin_cos)
        tables["sin"] = jnp.where(low_lanes, -sin_cos, cos_sin)

    def decay_tables(h):
        def task():
            lg = RET_LOG_GAMMA[h]
            rel = (lax.broadcasted_iota(jnp.int32, (chunk, chunk), 0)
                   - lax.broadcasted_iota(jnp.int32, (chunk, chunk), 1)).astype(F32)
            idx = lax.broadcasted_iota(jnp.int32, (chunk, 1), 0).astype(F32)
            tables[h] = (jnp.where(rel >= 0.0, jnp.exp(jnp.maximum(rel, 0.0) * lg), 0.0),
                         jnp.exp((idx + 1.0) * lg),
                         jnp.exp((chunk - 1.0 - idx) * lg))
        return task

    def normalize(c):
        xs[c] = x_ref[0, c * chunk:(c + 1) * chunk, :]
        xns[c] = _rms(xs[c], gn_ref[0]).astype(BF16)

    def in_piece(c, p):
        def task():
            u[c, p] = _dot(xns[c], win_ref[0, :, p * IN_PIECE_A:(p + 1) * IN_PIECE_A])
        return task

    def cols(c, lo, width):
        p, off = divmod(lo, IN_PIECE_A)
        return u[c, p][:, off:off + width]

    def retention_head(c, h):
        def task():
            rows = slice(c * chunk, (c + 1) * chunk)
            cos_c, sin_c = tables["cos"][rows], tables["sin"][rows]

            def rotary(t):
                return t * cos_c + pltpu.roll(t, half, 1) * sin_c

            decay, cross_decay, state_decay = tables[h]
            q = rotary(cols(c, h * RET_DK, RET_DK)).astype(BF16)
            k = rotary(cols(c, RET_QK_W + h * RET_DK, RET_DK)) * (RET_DK ** -0.5)
            v = cols(c, o_v + h * RET_DV, RET_DV).astype(BF16)
            scores = _dot_nt(q, k.astype(BF16)) * decay
            intra = _dot(scores.astype(BF16), v)
            cross = _dot(q, states[h].astype(BF16)) * cross_decay
            states[h] = (math.exp(chunk * RET_LOG_GAMMA[h]) * states[h]
                         + _dot((k * state_decay).T.astype(BF16), v))
            ret = _rms(intra + cross, gret_ref[0])
            mixed[c, h] = (ret * _silu(cols(c, o_g + h * RET_DV, RET_DV))).astype(BF16)
        return task

    def memory_head(c, h):
        def task():
            mixed[c, RET_HEADS + h] = _mem_attention_head(
                h, cols(c, o_mq + h * MEM_DH, MEM_DH), cols(c, o_mq + MEM_W + h * MEM_DH, MEM_DH),
                kt_ref, mv_ref, gq_ref[0])
        return task

    def out_piece(c, j):
        def task():
            if c not in cats:
                cats[c] = jnp.concatenate(
                    [mixed.pop((c, s)) for s in range(RET_HEADS + MEM_HEADS)], axis=1)
            sl = slice(j * OUT_PIECE_A, (j + 1) * OUT_PIECE_A)
            o_ref[0, c * chunk:(c + 1) * chunk, sl] = xs[c][:, sl] + _dot(cats[c], wout_ref[0, :, sl])
        return task

    setup = [rotary_tables] + [decay_tables(h) for h in range(RET_HEADS)]
    for step in range(n_chunks + 2):
        stages = [setup] if step == 0 else []
        if step < n_chunks:
            normalize(step)
            stages.append([in_piece(step, p) for p in range(n_in)])
        if 0 <= step - 1 < n_chunks:
            stages.append([retention_head(step - 1, h) for h in range(RET_HEADS)]
                          + [memory_head(step - 1, h) for h in range(MEM_HEADS)])
        if 0 <= step - 2 < n_chunks:
            stages.append([out_piece(step - 2, j) for j in range(n_out)])
        for task in _interleaved(*stages):
            task()

    for h in range(RET_HEADS):
        state_ref[h] = states[h]


def _layer_a(x, pos_f32, invf, g_norm, w_in_bf16, g_ret_head, w_out_bf16, kt, mv, g_mem_q,
             la, layer):
    batch, seq, _ = x.shape
    tile = ROW_TILE_A
    n_a = g_norm.shape[0]
    depth = g_mem_q.shape[0]
    a_in_w = w_in_bf16.shape[2]
    a_out_in = w_out_bf16.shape[1]
    mem_len = mv.shape[2]
    return pl.pallas_call(
        _layer_a_kernel,
        grid=(batch, seq // tile),
        in_specs=[
            pl.BlockSpec((1, tile, D_MODEL), lambda b, t: (b, t, 0)),
            pl.BlockSpec((1, 1, tile), lambda b, t: (b, 0, t)),
            _const_spec((RET_DK // 2, 1), lambda b, t: (0, 0)),
            _const_spec((1, 1, D_MODEL), lambda b, t: (la, 0, 0)),
            _const_spec((1, D_MODEL, a_in_w), lambda b, t: (la, 0, 0)),
            _const_spec((1, 1, RET_DV), lambda b, t: (la, 0, 0)),
            _const_spec((1, a_out_in, D_MODEL), lambda b, t: (la, 0, 0)),
            pl.BlockSpec((1, 1, MEM_W, mem_len), lambda b, t: (layer, b, 0, 0)),
            pl.BlockSpec((1, 1, mem_len, MEM_W), lambda b, t: (layer, b, 0, 0)),
            _const_spec((1, 1, MEM_DH), lambda b, t: (layer, 0, 0)),
        ],
        out_specs=pl.BlockSpec((1, tile, D_MODEL), lambda b, t: (b, t, 0)),
        out_shape=jax.ShapeDtypeStruct(x.shape, F32),
        scratch_shapes=[pltpu.VMEM((RET_HEADS, RET_DK, RET_DV), F32)],
        compiler_params=pltpu.CompilerParams(
            dimension_semantics=("arbitrary", "arbitrary"),
            vmem_limit_bytes=VMEM_LIMIT_BYTES),
        name="layer_a",
    )(x, pos_f32, invf, g_norm.reshape(n_a, 1, D_MODEL), w_in_bf16,
      g_ret_head.reshape(n_a, 1, RET_DV), w_out_bf16, kt, mv, g_mem_q.reshape(depth, 1, MEM_DH))


def _proj_b_kernel(x_ref, gkv_ref, gnb_ref, wkv_ref, win_ref, kt_ref, mv_ref, gq_ref,
                   q_ref, k_ref, v_ref, gate_ref, mo_ref):
    x = x_ref[0]
    xr = x * lax.rsqrt(jnp.mean(x * x, axis=-1, keepdims=True) + EPS)
    kv = _dot((xr * gkv_ref[...]).astype(BF16), wkv_ref[...])
    k_ref[0] = kv[:, :SB_W].astype(BF16)
    v_ref[0] = kv[:, SB_W:].astype(BF16)
    u = _dot((xr * gnb_ref[0]).astype(BF16), win_ref[0])
    q_ref[0] = u[:, :SB_W].astype(BF16)
    gate_ref[0] = _silu(u[:, SB_W:2 * SB_W]).astype(BF16)
    o_mq = 2 * SB_W
    mo = _mem_attention_heads(u[:, o_mq:o_mq + MEM_W], u[:, o_mq + MEM_W:], kt_ref, mv_ref, gq_ref[0])
    mo_ref[0] = jnp.concatenate(mo, axis=1)


def _proj_b(x, g_kv, g_norm_b, w_kv_bf16, w_in_bf16, kt, mv, g_mem_q, lb, layer):
    batch, seq, _ = x.shape
    tile = ROW_TILE_B
    n_b = g_norm_b.shape[0]
    depth = g_mem_q.shape[0]
    mem_len = mv.shape[2]
    b_in_w = w_in_bf16.shape[2]
    row_spec = lambda w: pl.BlockSpec((1, tile, w), lambda b, t: (b, t, 0))
    widths = (SB_W, SB_W, SB_W, SB_W, MEM_W)
    return pl.pallas_call(
        _proj_b_kernel,
        grid=(batch, seq // tile),
        in_specs=[
            row_spec(D_MODEL),
            _const_spec((1, D_MODEL), lambda b, t: (0, 0)),
            _const_spec((1, 1, D_MODEL), lambda b, t: (lb, 0, 0)),
            _const_spec((D_MODEL, 2 * SB_W), lambda b, t: (0, 0)),
            _const_spec((1, D_MODEL, b_in_w), lambda b, t: (lb, 0, 0)),
            pl.BlockSpec((1, 1, MEM_W, mem_len), lambda b, t: (layer, b, 0, 0)),
            pl.BlockSpec((1, 1, mem_len, MEM_W), lambda b, t: (layer, b, 0, 0)),
            _const_spec((1, 1, MEM_DH), lambda b, t: (layer, 0, 0)),
        ],
        out_specs=[row_spec(w) for w in widths],
        out_shape=[jax.ShapeDtypeStruct((batch, seq, w), BF16) for w in widths],
        compiler_params=pltpu.CompilerParams(
            dimension_semantics=("arbitrary", "arbitrary"),
            vmem_limit_bytes=VMEM_LIMIT_BYTES),
        name="proj_b",
    )(x, g_kv.reshape(1, D_MODEL), g_norm_b.reshape(n_b, 1, D_MODEL), w_kv_bf16, w_in_bf16,
      kt, mv, g_mem_q.reshape(depth, 1, MEM_DH))


def _sb_out_kernel(x_ref, q_ref, k_ref, v_ref, gate_ref, mo_ref, wout_ref, o_ref, cat_ref):
    tile = q_ref.shape[1]
    heads = q_ref.shape[2] // SB_DH
    qi = pl.program_id(1)
    scale = SB_DH ** -0.5

    row = lax.broadcasted_iota(jnp.int32, (tile, tile), 0)
    col = lax.broadcasted_iota(jnp.int32, (tile, tile), 1)
    causal = col < row
    suffix_mat = jnp.where(row > col, 1.0, 0.0).astype(BF16)


    def stage_scores(g, start, nblk, diag):
        hs = slice(g * SB_DH, (g + 1) * SB_DH)
        zr = _dot_nt(q_ref[0, :, hs], k_ref[0, pl.ds(start, nblk * tile), hs])
        z = zr * (scale * LOG2E)
        sp = jnp.maximum(z, 0.0) + jnp.log(1.0 + jnp.exp2(-jnp.abs(z))) * LOG2E
        blocks = [sp[:, j * tile:(j + 1) * tile] for j in range(nblk)]
        if diag:
            blocks[-1] = jnp.where(causal, blocks[-1], 0.0)
        his, los = [], []
        for blk in blocks:
            hi = blk.astype(BF16)
            his.append(hi)
            los.append((blk - hi.astype(F32)).astype(BF16))
        return z, blocks, jnp.concatenate(his, axis=0), jnp.concatenate(los, axis=0)

    def stage_weights(scores, nblk, diag, penalty):
        z, blocks, hi, lo = scores
        suffix = _dot(hi, suffix_mat) + _dot(lo, suffix_mat)
        weights = [None] * nblk
        for j in reversed(range(nblk)):
            sfx = suffix[j * tile:(j + 1) * tile]
            log_a = (z[:, j * tile:(j + 1) * tile] - blocks[j]) - sfx
            if penalty is not None:
                log_a = log_a - penalty
            a = jnp.exp2(log_a)
            if diag and j == nblk - 1:
                a = jnp.where(causal, a, 0.0)
            weights[j] = a.astype(BF16)
            total = sfx[:, 0:1] + blocks[j][:, 0:1]
            penalty = total if penalty is None else penalty + total
        return jnp.concatenate(weights, axis=1), penalty

    def stage_values(g, start, nblk, weights):
        hs = slice(g * SB_DH, (g + 1) * SB_DH)
        return _dot(weights, v_ref[0, pl.ds(start, nblk * tile), hs])

    def windows(start, nblk, diag, penalties):
        scores, weights, out = {}, {}, [None] * heads
        for step in range(heads + 2):
            if step < heads:
                scores[step] = stage_scores(step, start, nblk, diag)
            g = step - 1
            if 0 <= g < heads:
                weights[g] = stage_weights(scores.pop(g), nblk, diag,
                                           None if penalties is None else penalties[g])
            g = step - 2
            if 0 <= g < heads:
                w, penalty = weights.pop(g)
                out[g] = (stage_values(g, start, nblk, w), penalty)
        return out

    def store(accs):
        for g in range(heads):
            hs = slice(g * SB_DH, (g + 1) * SB_DH)
            cat_ref[:, hs] = (accs[g] * gate_ref[0, :, hs].astype(F32)).astype(BF16)

    @pl.when(qi == 0)
    def _():
        store([r[0] for r in windows(0, 1, True, None)])

    @pl.when(qi > 0)
    def _():
        near = windows(pl.multiple_of((qi - 1) * tile, tile), 2, True, None)
        accs = [r[0] for r in near]
        penalties = [r[1] for r in near]
        store(accs)

        def log_weight_bound(pens):
            return -functools.reduce(jnp.minimum, [jnp.min(p) for p in pens])

        def cond(carry):
            return jnp.logical_and(carry[0] <= qi, carry[1] > SB_LOG_WEIGHT_FLOOR)

        def body(carry):
            n, _, accs, pens = carry
            start = pl.multiple_of((qi - n) * tile, tile)
            far = windows(start, 1, False, pens)
            accs = [accs[g] + far[g][0] for g in range(heads)]
            pens = [far[g][1] for g in range(heads)]
            return n + 1, log_weight_bound(pens), accs, pens

        bound = log_weight_bound(penalties)

        @pl.when(jnp.logical_and(qi >= 2, bound > SB_LOG_WEIGHT_FLOOR))
        def _():
            _, _, accs_far, _ = lax.while_loop(cond, body, (jnp.int32(2), bound, accs, penalties))
            store(accs_far)

    cat_ref[:, SB_W:] = mo_ref[0]
    o_ref[0] = x_ref[0] + _dot(cat_ref[...], wout_ref[0])


def _sb_out(x, q, k, v, gate, mo, w_out_bf16, lb):
    batch, seq, _ = x.shape
    tile = SB_TILE
    b_out_in = w_out_bf16.shape[1]
    row_spec = lambda w: pl.BlockSpec((1, tile, w), lambda b, i: (b, i, 0))
    kv_spec = _const_spec((1, seq, SB_W), lambda b, i: (b, 0, 0))
    return pl.pallas_call(
        _sb_out_kernel,
        grid=(batch, seq // tile),
        in_specs=[
            row_spec(D_MODEL), row_spec(SB_W), kv_spec, kv_spec, row_spec(SB_W), row_spec(MEM_W),
            _const_spec((1, b_out_in, D_MODEL), lambda b, i: (lb, 0, 0)),
        ],
        out_specs=row_spec(D_MODEL),
        out_shape=jax.ShapeDtypeStruct(x.shape, F32),
        scratch_shapes=[pltpu.VMEM((tile, b_out_in), BF16)],
        compiler_params=pltpu.CompilerParams(
            dimension_semantics=("arbitrary", "arbitrary"),
            vmem_limit_bytes=SB_VMEM_LIMIT_BYTES),
        name="sb_out",
    )(x, q, k, v, gate, mo, w_out_bf16)


def kernel(x, mem, positions, g_norm_a, w_in_a, g_ret_head, w_out_a, g_kv, w_kv, g_norm_b, w_in_b,
           w_out_b, g_mem, w_mem_kv, g_mem_q, g_mem_k):
    n_a = g_norm_a.shape[0]
    n_b = g_norm_b.shape[0]
    batch, seq, _ = x.shape

    kt, mv = _memory_kv(mem, g_mem, w_mem_kv.astype(BF16), g_mem_k)

    inv_freq = ROPE_BASE ** (-jnp.arange(0, RET_DK // 2, dtype=F32) * 2.0 / RET_DK)
    invf = inv_freq.reshape(RET_DK // 2, 1)
    pos_f32 = positions.astype(F32).reshape(batch, 1, seq)

    w_in_a_bf16 = w_in_a.astype(BF16)
    w_out_a_bf16 = w_out_a.astype(BF16)
    w_in_b_bf16 = w_in_b.astype(BF16)
    w_out_b_bf16 = w_out_b.astype(BF16)
    w_kv_bf16 = w_kv.astype(BF16)

    for la in range(n_a):
        x = _layer_a(x, pos_f32, invf, g_norm_a, w_in_a_bf16, g_ret_head, w_out_a_bf16,
                     kt, mv, g_mem_q, la, la)

    k_shared = v_shared = None
    for lb in range(n_b):
        layer = n_a + lb
        q, k_new, v_new, gate, mo = _proj_b(x, g_kv, g_norm_b, w_kv_bf16, w_in_b_bf16, kt, mv,
                                            g_mem_q, lb, layer)
        if lb == 0:
            k_shared, v_shared = k_new, v_new
        x = _sb_out(x, q, k_shared, v_shared, gate, mo, w_out_b_bf16, lb)
    return x
```

```python
import functools
import math

import numpy as np
import jax
import jax.numpy as jnp
from jax import lax
from jax.experimental import pallas as pl
from jax.experimental.pallas import tpu as pltpu

F32 = jnp.float32
BF16 = jnp.bfloat16

D_MODEL = 1024
RET_HEADS = 4
RET_DK = 128
RET_DV = 256
RET_QK_W = RET_HEADS * RET_DK
RET_V_W = RET_HEADS * RET_DV
ROPE_BASE = 10000.0
SB_HEADS = 8
SB_DH = 128
SB_W = SB_HEADS * SB_DH
MEM_HEADS = 4
MEM_DH = 128
MEM_W = MEM_HEADS * MEM_DH
EPS = 1e-6

RET_LOG_GAMMA = tuple(
    math.log(float(np.float32(1.0 - 2.0 ** (-5.0 - h)))) for h in range(RET_HEADS))

VMEM_LIMIT_BYTES = 48 * 1024 * 1024

ROW_TILE_A = 512
RET_CHUNK = 256
IN_PIECE_A = 512
OUT_PIECE_A = 256
ROW_TILE_B = 512
OUT_PIECE_B = 256
SB_TILE = 256
SB_LOG_WEIGHT_FLOOR = -151.0
SB_VMEM_LIMIT_BYTES = 58 * 1024 * 1024
LOG2E = math.log2(math.e)


def _rms(x, g):
    return x * lax.rsqrt(jnp.mean(x * x, axis=-1, keepdims=True) + EPS) * g


def _silu(x):
    return x * (1.0 / (1.0 + jnp.exp(-x)))


def _dot(a, b):
    return jnp.dot(a, b, preferred_element_type=F32)


def _dot_nt(a, b):
    return lax.dot_general(a, b, (((1,), (1,)), ((), ())), preferred_element_type=F32)


def _const_spec(shape, index_map):
    return pl.BlockSpec(shape, index_map, pipeline_mode=pl.Buffered(1))


def _interleaved(*task_lists):
    tagged = [((i + 0.5) / len(tasks), k, task)
              for k, tasks in enumerate(task_lists) for i, task in enumerate(tasks)]
    return [task for _, _, task in sorted(tagged, key=lambda e: e[:2])]


def _memkv_kernel(mem_ref, g_ref, w_ref, gk_ref, kt_ref, v_ref):
    mn = _rms(mem_ref[0], g_ref[0]).astype(BF16)
    kv = _dot(mn, w_ref[0])
    gk = gk_ref[0]
    for h in range(MEM_HEADS):
        kh = _rms(kv[:, h * MEM_DH:(h + 1) * MEM_DH], gk)
        kt_ref[0, 0, h * MEM_DH:(h + 1) * MEM_DH, :] = kh.T.astype(BF16)
    v_ref[0, 0] = kv[:, MEM_W:].astype(BF16)


def _memory_kv(mem, g_mem, w_mem_kv_bf16, g_mem_k):
    depth = g_mem.shape[0]
    batch, mem_len, _ = mem.shape
    return pl.pallas_call(
        _memkv_kernel,
        grid=(depth, batch),
        in_specs=[
            pl.BlockSpec((1, mem_len, D_MODEL), lambda l, b: (b, 0, 0)),
            pl.BlockSpec((1, 1, D_MODEL), lambda l, b: (l, 0, 0)),
            pl.BlockSpec((1, D_MODEL, 2 * MEM_W), lambda l, b: (l, 0, 0)),
            pl.BlockSpec((1, 1, MEM_DH), lambda l, b: (l, 0, 0)),
        ],
        out_specs=[
            pl.BlockSpec((1, 1, MEM_W, mem_len), lambda l, b: (l, b, 0, 0)),
            pl.BlockSpec((1, 1, mem_len, MEM_W), lambda l, b: (l, b, 0, 0)),
        ],
        out_shape=[
            jax.ShapeDtypeStruct((depth, batch, MEM_W, mem_len), BF16),
            jax.ShapeDtypeStruct((depth, batch, mem_len, MEM_W), BF16),
        ],
        compiler_params=pltpu.CompilerParams(
            dimension_semantics=("arbitrary", "arbitrary"),
            vmem_limit_bytes=VMEM_LIMIT_BYTES),
        name="memkv",
    )(mem, g_mem.reshape(depth, 1, D_MODEL), w_mem_kv_bf16, g_mem_k.reshape(depth, 1, MEM_DH))


def _mem_attention_head(h, memq, memgate, kt_ref, mv_ref, gq):
    sl = slice(h * MEM_DH, (h + 1) * MEM_DH)
    qn = _rms(memq, gq).astype(BF16)
    logits = _dot(qn, kt_ref[0, 0, sl, :]) * (MEM_DH ** -0.5)
    e = jnp.exp(logits - jnp.max(logits, axis=-1, keepdims=True))
    o = _dot(e.astype(BF16), mv_ref[0, 0, :, sl]) / jnp.sum(e, axis=-1, keepdims=True)
    return (o * _silu(memgate)).astype(BF16)


def _mem_attention_heads(memq, memgate, kt_ref, mv_ref, gq):
    return [_mem_attention_head(h, memq[:, h * MEM_DH:(h + 1) * MEM_DH],
                                memgate[:, h * MEM_DH:(h + 1) * MEM_DH], kt_ref, mv_ref, gq)
            for h in range(MEM_HEADS)]


def _layer_a_kernel(x_ref, pos_ref, invf_ref, gn_ref, win_ref, gret_ref, wout_ref,
                    kt_ref, mv_ref, gq_ref, o_ref, state_ref):
    tile = x_ref.shape[1]
    chunk = RET_CHUNK
    n_chunks = tile // chunk
    half = RET_DK // 2
    n_in = win_ref.shape[2] // IN_PIECE_A
    n_out = wout_ref.shape[2] // OUT_PIECE_A
    o_v = 2 * RET_QK_W
    o_g = o_v + RET_V_W
    o_mq = o_g + RET_V_W

    @pl.when(pl.program_id(1) == 0)
    def _():
        state_ref[...] = jnp.zeros_like(state_ref)

    tables = {}
    states = [state_ref[h] for h in range(RET_HEADS)]
    xs, xns = {}, {}
    u = {}
    mixed = {}
    cats = {}

    def rotary_tables():
        ang_t = invf_ref[...] * pos_ref[0]
        cos_sin = jnp.concatenate([jnp.cos(ang_t), jnp.sin(ang_t)], axis=0).T
        sin_cos = pltpu.roll(cos_sin, half, 1)
        low_lanes = lax.broadcasted_iota(jnp.int32, (tile, RET_DK), 1) < half
        tables["cos"] = jnp.where(low_lanes, cos_sin, sin_cos)
        tables["sin"] = jnp.where(low_lanes, -sin_cos, cos_sin)

    def decay_tables(h):
        def task():
            lg = RET_LOG_GAMMA[h]
            rel = (lax.broadcasted_iota(jnp.int32, (chunk, chunk), 0)
                   - lax.broadcasted_iota(jnp.int32, (chunk, chunk), 1)).astype(F32)
            idx = lax.broadcasted_iota(jnp.int32, (chunk, 1), 0).astype(F32)
            tables[h] = (jnp.where(rel >= 0.0, jnp.exp(jnp.maximum(rel, 0.0) * lg), 0.0),
                         jnp.exp((idx + 1.0) * lg),
                         jnp.exp((chunk - 1.0 - idx) * lg))
        return task

    def normalize(c):
        xs[c] = x_ref[0, c * chunk:(c + 1) * chunk, :]
        xns[c] = _rms(xs[c], gn_ref[0]).astype(BF16)

    def in_piece(c, p):
        def task():
            u[c, p] = _dot(xns[c], win_ref[0, :, p * IN_PIECE_A:(p + 1) * IN_PIECE_A])
        return task

    def cols(c, lo, width):
        p, off = divmod(lo, IN_PIECE_A)
        return u[c, p][:, off:off + width]

    def retention_head(c, h):
        def task():
            rows = slice(c * chunk, (c + 1) * chunk)
            cos_c, sin_c = tables["cos"][rows], tables["sin"][rows]

            def rotary(t):
                return t * cos_c + pltpu.roll(t, half, 1) * sin_c

            decay, cross_decay, state_decay = tables[h]
            q = rotary(cols(c, h * RET_DK, RET_DK)).astype(BF16)
            k = rotary(cols(c, RET_QK_W + h * RET_DK, RET_DK)) * (RET_DK ** -0.5)
            v = cols(c, o_v + h * RET_DV, RET_DV).astype(BF16)
            scores = _dot_nt(q, k.astype(BF16)) * decay
            intra = _dot(scores.astype(BF16), v)
            cross = _dot(q, states[h].astype(BF16)) * cross_decay
            states[h] = (math.exp(chunk * RET_LOG_GAMMA[h]) * states[h]
                         + _dot((k * state_decay).T.astype(BF16), v))
            ret = _rms(intra + cross, gret_ref[0])
            mixed[c, h] = (ret * _silu(cols(c, o_g + h * RET_DV, RET_DV))).astype(BF16)
        return task

    def memory_head(c, h):
        def task():
            mixed[c, RET_HEADS + h] = _mem_attention_head(
                h, cols(c, o_mq + h * MEM_DH, MEM_DH), cols(c, o_mq + MEM_W + h * MEM_DH, MEM_DH),
                kt_ref, mv_ref, gq_ref[0])
        return task

    def out_piece(c, j):
        def task():
            if c not in cats:
                cats[c] = jnp.concatenate(
                    [mixed.pop((c, s)) for s in range(RET_HEADS + MEM_HEADS)], axis=1)
            sl = slice(j * OUT_PIECE_A, (j + 1) * OUT_PIECE_A)
            o_ref[0, c * chunk:(c + 1) * chunk, sl] = xs[c][:, sl] + _dot(cats[c], wout_ref[0, :, sl])
        return task

    setup = [rotary_tables] + [decay_tables(h) for h in range(RET_HEADS)]
    for step in range(n_chunks + 2):
        stages = [setup] if step == 0 else []
        if step < n_chunks:
            normalize(step)
            stages.append([in_piece(step, p) for p in range(n_in)])
        if 0 <= step - 1 < n_chunks:
            stages.append([retention_head(step - 1, h) for h in range(RET_HEADS)]
                          + [memory_head(step - 1, h) for h in range(MEM_HEADS)])
        if 0 <= step - 2 < n_chunks:
            stages.append([out_piece(step - 2, j) for j in range(n_out)])
        for task in _interleaved(*stages):
            task()

    for h in range(RET_HEADS):
        state_ref[h] = states[h]


def _layer_a(x, pos_f32, invf, g_norm, w_in_bf16, g_ret_head, w_out_bf16, kt, mv, g_mem_q,
             la, layer):
    batch, seq, _ = x.shape
    tile = ROW_TILE_A
    n_a = g_norm.shape[0]
    depth = g_mem_q.shape[0]
    a_in_w = w_in_bf16.shape[2]
    a_out_in = w_out_bf16.shape[1]
    mem_len = mv.shape[2]
    return pl.pallas_call(
        _layer_a_kernel,
        grid=(batch, seq // tile),
        in_specs=[
            pl.BlockSpec((1, tile, D_MODEL), lambda b, t: (b, t, 0)),
            pl.BlockSpec((1, 1, tile), lambda b, t: (b, 0, t)),
            _const_spec((RET_DK // 2, 1), lambda b, t: (0, 0)),
            _const_spec((1, 1, D_MODEL), lambda b, t: (la, 0, 0)),
            _const_spec((1, D_MODEL, a_in_w), lambda b, t: (la, 0, 0)),
            _const_spec((1, 1, RET_DV), lambda b, t: (la, 0, 0)),
            _const_spec((1, a_out_in, D_MODEL), lambda b, t: (la, 0, 0)),
            pl.BlockSpec((1, 1, MEM_W, mem_len), lambda b, t: (layer, b, 0, 0)),
            pl.BlockSpec((1, 1, mem_len, MEM_W), lambda b, t: (layer, b, 0, 0)),
            _const_spec((1, 1, MEM_DH), lambda b, t: (layer, 0, 0)),
        ],
        out_specs=pl.BlockSpec((1, tile, D_MODEL), lambda b, t: (b, t, 0)),
        out_shape=jax.ShapeDtypeStruct(x.shape, F32),
        scratch_shapes=[pltpu.VMEM((RET_HEADS, RET_DK, RET_DV), F32)],
        compiler_params=pltpu.CompilerParams(
            dimension_semantics=("arbitrary", "arbitrary"),
            vmem_limit_bytes=VMEM_LIMIT_BYTES),
        name="layer_a",
    )(x, pos_f32, invf, g_norm.reshape(n_a, 1, D_MODEL), w_in_bf16,
      g_ret_head.reshape(n_a, 1, RET_DV), w_out_bf16, kt, mv, g_mem_q.reshape(depth, 1, MEM_DH))


def _proj_b_kernel(x_ref, gkv_ref, gnb_ref, wkv_ref, win_ref, kt_ref, mv_ref, gq_ref,
                   q_ref, k_ref, v_ref, gate_ref, mo_ref):
    x = x_ref[0]
    xr = x * lax.rsqrt(jnp.mean(x * x, axis=-1, keepdims=True) + EPS)
    kv = _dot((xr * gkv_ref[...]).astype(BF16), wkv_ref[...])
    k_ref[0] = kv[:, :SB_W].astype(BF16)
    v_ref[0] = kv[:, SB_W:].astype(BF16)
    u = _dot((xr * gnb_ref[0]).astype(BF16), win_ref[0])
    q_ref[0] = u[:, :SB_W].astype(BF16)
    gate_ref[0] = _silu(u[:, SB_W:2 * SB_W]).astype(BF16)
    o_mq = 2 * SB_W
    mo = _mem_attention_heads(u[:, o_mq:o_mq + MEM_W], u[:, o_mq + MEM_W:], kt_ref, mv_ref, gq_ref[0])
    mo_ref[0] = jnp.concatenate(mo, axis=1)


def _proj_b(x, g_kv, g_norm_b, w_kv_bf16, w_in_bf16, kt, mv, g_mem_q, lb, layer):
    batch, seq, _ = x.shape
    tile = ROW_TILE_B
    n_b = g_norm_b.shape[0]
    depth = g_mem_q.shape[0]
    mem_len = mv.shape[2]
    b_in_w = w_in_bf16.shape[2]
    row_spec = lambda w: pl.BlockSpec((1, tile, w), lambda b, t: (b, t, 0))
    widths = (SB_W, SB_W, SB_W, SB_W, MEM_W)
    return pl.pallas_call(
        _proj_b_kernel,
        grid=(batch, seq // tile),
        in_specs=[
            row_spec(D_MODEL),
            _const_spec((1, D_MODEL), lambda b, t: (0, 0)),
            _const_spec((1, 1, D_MODEL), lambda b, t: (lb, 0, 0)),
            _const_spec((D_MODEL, 2 * SB_W), lambda b, t: (0, 0)),
            _const_spec((1, D_MODEL, b_in_w), lambda b, t: (lb, 0, 0)),
            pl.BlockSpec((1, 1, MEM_W, mem_len), lambda b, t: (layer, b, 0, 0)),
            pl.BlockSpec((1, 1, mem_len, MEM_W), lambda b, t: (layer, b, 0, 0)),
            _const_spec((1, 1, MEM_DH), lambda b, t: (layer, 0, 0)),
        ],
        out_specs=[row_spec(w) for w in widths],
        out_shape=[jax.ShapeDtypeStruct((batch, seq, w), BF16) for w in widths],
        compiler_params=pltpu.CompilerParams(
            dimension_semantics=("arbitrary", "arbitrary"),
            vmem_limit_bytes=VMEM_LIMIT_BYTES),
        name="proj_b",
    )(x, g_kv.reshape(1, D_MODEL), g_norm_b.reshape(n_b, 1, D_MODEL), w_kv_bf16, w_in_bf16,
      kt, mv, g_mem_q.reshape(depth, 1, MEM_DH))


def _sb_out_kernel(x_ref, q_ref, k_ref, v_ref, gate_ref, mo_ref, wout_ref, o_ref, cat_ref, *,
                   n_tiles):
    tile = q_ref.shape[1]
    heads = q_ref.shape[2] // SB_DH
    qi = pl.program_id(1)
    scale = SB_DH ** -0.5

    row = lax.broadcasted_iota(jnp.int32, (tile, tile), 0)
    col = lax.broadcasted_iota(jnp.int32, (tile, tile), 1)
    causal = col < row
    suffix_mat = jnp.where(row > col, 1.0, 0.0).astype(BF16)


    def stage_scores(g, start, nblk, diag):
        hs = slice(g * SB_DH, (g + 1) * SB_DH)
        zr = _dot_nt(q_ref[0, :, hs], k_ref[0, pl.ds(start, nblk * tile), hs])
        z = zr * (scale * LOG2E)
        sp = jnp.maximum(z, 0.0) + jnp.log(1.0 + jnp.exp2(-jnp.abs(z))) * LOG2E
        blocks = [sp[:, j * tile:(j + 1) * tile] for j in range(nblk)]
        if diag:
            blocks[-1] = jnp.where(causal, blocks[-1], 0.0)
        his, los = [], []
        for blk in blocks:
            hi = blk.astype(BF16)
            his.append(hi)
            los.append((blk - hi.astype(F32)).astype(BF16))
        return z, blocks, jnp.concatenate(his, axis=0), jnp.concatenate(los, axis=0)

    def stage_weights(scores, nblk, diag, penalty):
        z, blocks, hi, lo = scores
        suffix = _dot(hi, suffix_mat) + _dot(lo, suffix_mat)
        weights = [None] * nblk
        for j in reversed(range(nblk)):
            sfx = suffix[j * tile:(j + 1) * tile]
            log_a = (z[:, j * tile:(j + 1) * tile] - blocks[j]) - sfx
            if penalty is not None:
                log_a = log_a - penalty
            a = jnp.exp2(log_a)
            if diag and j == nblk - 1:
                a = jnp.where(causal, a, 0.0)
            weights[j] = a.astype(BF16)
            total = sfx[:, 0:1] + blocks[j][:, 0:1]
            penalty = total if penalty is None else penalty + total
        return jnp.concatenate(weights, axis=1), penalty

    def stage_values(g, start, nblk, weights):
        hs = slice(g * SB_DH, (g + 1) * SB_DH)
        return _dot(weights, v_ref[0, pl.ds(start, nblk * tile), hs])

    def windows(start, nblk, diag, penalties, other_tasks=()):
        scores, weights, out = {}, {}, [None] * heads

        def stagger(step):
            def task():
                if step < heads:
                    scores[step] = stage_scores(step, start, nblk, diag)
                g = step - 1
                if 0 <= g < heads:
                    weights[g] = stage_weights(scores.pop(g), nblk, diag,
                                               None if penalties is None else penalties[g])
                g = step - 2
                if 0 <= g < heads:
                    w, penalty = weights.pop(g)
                    out[g] = (stage_values(g, start, nblk, w), penalty)
            return task

        steps = [stagger(step) for step in range(heads + 2)]
        others = list(other_tasks)
        n_front = (len(others) + 1) // 2
        order = steps[:1] + others[:n_front] + steps[1:-1] + others[n_front:] + steps[-1:]
        for task in order:
            task()
        return out

    cur = lax.rem(qi, 2)
    n_out = o_ref.shape[2] // OUT_PIECE_B

    def out_piece(j):
        def task():
            sl = slice(j * OUT_PIECE_B, (j + 1) * OUT_PIECE_B)
            o_ref[0, :, sl] = x_ref[0, :, sl] + _dot(cat_ref[1 - cur], wout_ref[0, :, sl])
        return task

    out_tasks = [out_piece(j) for j in range(n_out)]

    def store(accs):
        for g in range(heads):
            hs = slice(g * SB_DH, (g + 1) * SB_DH)
            cat_ref[cur, :, hs] = (accs[g] * gate_ref[0, :, hs].astype(F32)).astype(BF16)
        cat_ref[cur, :, SB_W:] = mo_ref[0]

    @pl.when(qi == 0)
    def _():
        store([r[0] for r in windows(0, 1, True, None)])

    @pl.when(qi == n_tiles)
    def _():
        for task in out_tasks:
            task()

    @pl.when(jnp.logical_and(qi > 0, qi < n_tiles))
    def _():
        near = windows(pl.multiple_of((qi - 1) * tile, tile), 2, True, None, out_tasks)
        accs = [r[0] for r in near]
        penalties = [r[1] for r in near]
        store(accs)

        def log_weight_bound(pens):
            return -functools.reduce(jnp.minimum, [jnp.min(p) for p in pens])

        def cond(carry):
            return jnp.logical_and(carry[0] <= qi, carry[1] > SB_LOG_WEIGHT_FLOOR)

        def body(carry):
            n, _, accs, pens = carry
            start = pl.multiple_of((qi - n) * tile, tile)
            far = windows(start, 1, False, pens)
            accs = [accs[g] + far[g][0] for g in range(heads)]
            pens = [far[g][1] for g in range(heads)]
            return n + 1, log_weight_bound(pens), accs, pens

        bound = log_weight_bound(penalties)

        @pl.when(jnp.logical_and(qi >= 2, bound > SB_LOG_WEIGHT_FLOOR))
        def _():
            _, _, accs_far, _ = lax.while_loop(cond, body, (jnp.int32(2), bound, accs, penalties))
            store(accs_far)


def _sb_out(x, q, k, v, gate, mo, w_out_bf16, lb):
    batch, seq, _ = x.shape
    tile = SB_TILE
    n_tiles = seq // tile
    b_out_in = w_out_bf16.shape[1]
    attn_spec = lambda w: pl.BlockSpec((1, tile, w), lambda b, i: (b, jnp.minimum(i, n_tiles - 1), 0))
    proj_spec = lambda w: pl.BlockSpec((1, tile, w), lambda b, i: (b, jnp.maximum(i - 1, 0), 0))
    kv_spec = _const_spec((1, seq, SB_W), lambda b, i: (b, 0, 0))
    return pl.pallas_call(
        functools.partial(_sb_out_kernel, n_tiles=n_tiles),
        grid=(batch, n_tiles + 1),
        in_specs=[
            proj_spec(D_MODEL), attn_spec(SB_W), kv_spec, kv_spec, attn_spec(SB_W), attn_spec(MEM_W),
            _const_spec((1, b_out_in, D_MODEL), lambda b, i: (lb, 0, 0)),
        ],
        out_specs=proj_spec(D_MODEL),
        out_shape=jax.ShapeDtypeStruct(x.shape, F32),
        scratch_shapes=[pltpu.VMEM((2, tile, b_out_in), BF16)],
        compiler_params=pltpu.CompilerParams(
            dimension_semantics=("arbitrary", "arbitrary"),
            vmem_limit_bytes=SB_VMEM_LIMIT_BYTES),
        name="sb_out",
    )(x, q, k, v, gate, mo, w_out_bf16)


def kernel(x, mem, positions, g_norm_a, w_in_a, g_ret_head, w_out_a, g_kv, w_kv, g_norm_b, w_in_b,
           w_out_b, g_mem, w_mem_kv, g_mem_q, g_mem_k):
    n_a = g_norm_a.shape[0]
    n_b = g_norm_b.shape[0]
    batch, seq, _ = x.shape

    kt, mv = _memory_kv(mem, g_mem, w_mem_kv.astype(BF16), g_mem_k)

    inv_freq = ROPE_BASE ** (-jnp.arange(0, RET_DK // 2, dtype=F32) * 2.0 / RET_DK)
    invf = inv_freq.reshape(RET_DK // 2, 1)
    pos_f32 = positions.astype(F32).reshape(batch, 1, seq)

    w_in_a_bf16 = w_in_a.astype(BF16)
    w_out_a_bf16 = w_out_a.astype(BF16)
    w_in_b_bf16 = w_in_b.astype(BF16)
    w_out_b_bf16 = w_out_b.astype(BF16)
    w_kv_bf16 = w_kv.astype(BF16)

    for la in range(n_a):
        x = _layer_a(x, pos_f32, invf, g_norm_a, w_in_a_bf16, g_ret_head, w_out_a_bf16,
                     kt, mv, g_mem_q, la, la)

    k_shared = v_shared = None
    for lb in range(n_b):
        layer = n_a + lb
        q, k_new, v_new, gate, mo = _proj_b(x, g_kv, g_norm_b, w_kv_bf16, w_in_b_bf16, kt, mv,
                                            g_mem_q, lb, layer)
        if lb == 0:
            k_shared, v_shared = k_new, v_new
        x = _sb_out(x, q, k_shared, v_shared, gate, mo, w_out_b_bf16, lb)
    return x
```

```python
import functools
import math

import numpy as np
import jax
import jax.numpy as jnp
from jax import lax
from jax.experimental import pallas as pl
from jax.experimental.pallas import tpu as pltpu

F32 = jnp.float32
BF16 = jnp.bfloat16

D_MODEL = 1024
RET_HEADS = 4
RET_DK = 128
RET_DV = 256
RET_QK_W = RET_HEADS * RET_DK
RET_V_W = RET_HEADS * RET_DV
ROPE_BASE = 10000.0
SB_HEADS = 8
SB_DH = 128
SB_W = SB_HEADS * SB_DH
MEM_HEADS = 4
MEM_DH = 128
MEM_W = MEM_HEADS * MEM_DH
EPS = 1e-6

RET_LOG_GAMMA = tuple(
    math.log(float(np.float32(1.0 - 2.0 ** (-5.0 - h)))) for h in range(RET_HEADS))

VMEM_LIMIT_BYTES = 48 * 1024 * 1024

ROW_TILE_A = 1024
RET_CHUNK = 256
IN_PIECE_A = 512
OUT_PIECE_A = 256
ROW_TILE_B = 512
OUT_PIECE_B = 256
SB_TILE = 256
SB_LOG_WEIGHT_FLOOR = -151.0
SB_VMEM_LIMIT_BYTES = 58 * 1024 * 1024
LOG2E = math.log2(math.e)


def _rms(x, g):
    return x * lax.rsqrt(jnp.mean(x * x, axis=-1, keepdims=True) + EPS) * g


def _silu(x):
    return x * (1.0 / (1.0 + jnp.exp(-x)))


def _dot(a, b):
    return jnp.dot(a, b, preferred_element_type=F32)


def _dot_nt(a, b):
    return lax.dot_general(a, b, (((1,), (1,)), ((), ())), preferred_element_type=F32)


def _const_spec(shape, index_map):
    return pl.BlockSpec(shape, index_map, pipeline_mode=pl.Buffered(1))


def _interleaved(*task_lists):
    tagged = [((i + 0.5) / len(tasks), k, task)
              for k, tasks in enumerate(task_lists) for i, task in enumerate(tasks)]
    return [task for _, _, task in sorted(tagged, key=lambda e: e[:2])]


def _memkv_kernel(mem_ref, g_ref, w_ref, gk_ref, kt_ref, v_ref):
    mn = _rms(mem_ref[0], g_ref[0]).astype(BF16)
    kv = _dot(mn, w_ref[0])
    gk = gk_ref[0]
    for h in range(MEM_HEADS):
        kh = _rms(kv[:, h * MEM_DH:(h + 1) * MEM_DH], gk)
        kt_ref[0, 0, h * MEM_DH:(h + 1) * MEM_DH, :] = kh.T.astype(BF16)
    v_ref[0, 0] = kv[:, MEM_W:].astype(BF16)


def _memory_kv(mem, g_mem, w_mem_kv_bf16, g_mem_k):
    depth = g_mem.shape[0]
    batch, mem_len, _ = mem.shape
    return pl.pallas_call(
        _memkv_kernel,
        grid=(depth, batch),
        in_specs=[
            pl.BlockSpec((1, mem_len, D_MODEL), lambda l, b: (b, 0, 0)),
            pl.BlockSpec((1, 1, D_MODEL), lambda l, b: (l, 0, 0)),
            pl.BlockSpec((1, D_MODEL, 2 * MEM_W), lambda l, b: (l, 0, 0)),
            pl.BlockSpec((1, 1, MEM_DH), lambda l, b: (l, 0, 0)),
        ],
        out_specs=[
            pl.BlockSpec((1, 1, MEM_W, mem_len), lambda l, b: (l, b, 0, 0)),
            pl.BlockSpec((1, 1, mem_len, MEM_W), lambda l, b: (l, b, 0, 0)),
        ],
        out_shape=[
            jax.ShapeDtypeStruct((depth, batch, MEM_W, mem_len), BF16),
            jax.ShapeDtypeStruct((depth, batch, mem_len, MEM_W), BF16),
        ],
        compiler_params=pltpu.CompilerParams(
            dimension_semantics=("arbitrary", "arbitrary"),
            vmem_limit_bytes=VMEM_LIMIT_BYTES),
        name="memkv",
    )(mem, g_mem.reshape(depth, 1, D_MODEL), w_mem_kv_bf16, g_mem_k.reshape(depth, 1, MEM_DH))


def _mem_logits(h, memq, kt_ref, gq):
    qn = _rms(memq, gq).astype(BF16)
    return _dot(qn, kt_ref[0, 0, h * MEM_DH:(h + 1) * MEM_DH, :]) * (MEM_DH ** -0.5)


def _mem_values(h, logits, mv_ref):
    e = jnp.exp(logits - jnp.max(logits, axis=-1, keepdims=True))
    return _dot(e.astype(BF16), mv_ref[0, 0, :, h * MEM_DH:(h + 1) * MEM_DH]), \
        jnp.sum(e, axis=-1, keepdims=True)


def _mem_gated(values, memgate):
    o, denom = values
    return (o / denom * _silu(memgate)).astype(BF16)


def _mem_attention_head(h, memq, memgate, kt_ref, mv_ref, gq):
    return _mem_gated(_mem_values(h, _mem_logits(h, memq, kt_ref, gq), mv_ref), memgate)


def _layer_a_kernel(x_ref, pos_ref, invf_ref, gn_ref, win_ref, gret_ref, wout_ref,
                    kt_ref, mv_ref, gq_ref, o_ref, state_ref):
    tile = x_ref.shape[1]
    chunk = RET_CHUNK
    n_chunks = tile // chunk
    half = RET_DK // 2
    n_in = win_ref.shape[2] // IN_PIECE_A
    n_out = wout_ref.shape[2] // OUT_PIECE_A
    o_v = 2 * RET_QK_W
    o_g = o_v + RET_V_W
    o_mq = o_g + RET_V_W

    @pl.when(pl.program_id(1) == 0)
    def _():
        state_ref[...] = jnp.zeros_like(state_ref)

    tables = {}
    states = [state_ref[h] for h in range(RET_HEADS)]
    xs, xns = {}, {}
    u = {}
    mixed = {}
    cats = {}

    def rotary_tables():
        ang_t = invf_ref[...] * pos_ref[0]
        cos_sin = jnp.concatenate([jnp.cos(ang_t), jnp.sin(ang_t)], axis=0).T
        sin_cos = pltpu.roll(cos_sin, half, 1)
        low_lanes = lax.broadcasted_iota(jnp.int32, (tile, RET_DK), 1) < half
        tables["cos"] = jnp.where(low_lanes, cos_sin, sin_cos)
        tables["sin"] = jnp.where(low_lanes, -sin_cos, cos_sin)

    def decay_tables(h):
        def task():
            lg = RET_LOG_GAMMA[h]
            rel = (lax.broadcasted_iota(jnp.int32, (chunk, chunk), 0)
                   - lax.broadcasted_iota(jnp.int32, (chunk, chunk), 1)).astype(F32)
            idx = lax.broadcasted_iota(jnp.int32, (chunk, 1), 0).astype(F32)
            tables[h] = (jnp.where(rel >= 0.0, jnp.exp(jnp.maximum(rel, 0.0) * lg), 0.0),
                         jnp.exp((idx + 1.0) * lg),
                         jnp.exp((chunk - 1.0 - idx) * lg))
        return task

    def normalize(c):
        xs[c] = x_ref[0, c * chunk:(c + 1) * chunk, :]
        xns[c] = _rms(xs[c], gn_ref[0]).astype(BF16)

    def in_piece(c, p):
        def task():
            u[c, p] = _dot(xns[c], win_ref[0, :, p * IN_PIECE_A:(p + 1) * IN_PIECE_A])
        return task

    def cols(c, lo, width):
        p, off = divmod(lo, IN_PIECE_A)
        return u[c, p][:, off:off + width]

    part = {}

    def retention_scores(c, h):
        def task():
            rows = slice(c * chunk, (c + 1) * chunk)
            cos_c, sin_c = tables["cos"][rows], tables["sin"][rows]

            def rotary(t):
                return t * cos_c + pltpu.roll(t, half, 1) * sin_c

            q = rotary(cols(c, h * RET_DK, RET_DK)).astype(BF16)
            k = rotary(cols(c, RET_QK_W + h * RET_DK, RET_DK)) * (RET_DK ** -0.5)
            part[c, h, "q"], part[c, h, "k"] = q, k
            part[c, h, "scores"] = _dot_nt(q, k.astype(BF16))
        return task

    def retention_values(c, h):
        def task():
            decay, cross_decay, state_decay = tables[h]
            q, k = part.pop((c, h, "q")), part.pop((c, h, "k"))
            v = cols(c, o_v + h * RET_DV, RET_DV).astype(BF16)
            intra = _dot((part.pop((c, h, "scores")) * decay).astype(BF16), v)
            cross = _dot(q, states[h].astype(BF16)) * cross_decay
            states[h] = (math.exp(chunk * RET_LOG_GAMMA[h]) * states[h]
                         + _dot((k * state_decay).T.astype(BF16), v))
            part[c, h, "ret"] = intra + cross
        return task

    def retention_gated(c, h):
        def task():
            ret = _rms(part.pop((c, h, "ret")), gret_ref[0])
            mixed[c, h] = (ret * _silu(cols(c, o_g + h * RET_DV, RET_DV))).astype(BF16)
        return task

    def memory_logits(c, h):
        def task():
            part[c, h, "logits"] = _mem_logits(h, cols(c, o_mq + h * MEM_DH, MEM_DH),
                                               kt_ref, gq_ref[0])
        return task

    def memory_values(c, h):
        def task():
            part[c, h, "values"] = _mem_values(h, part.pop((c, h, "logits")), mv_ref)
        return task

    def memory_gated(c, h):
        def task():
            mixed[c, RET_HEADS + h] = _mem_gated(part.pop((c, h, "values")),
                                                 cols(c, o_mq + MEM_W + h * MEM_DH, MEM_DH))
        return task

    def mixer_groups(c):
        groups = [[retention_scores(c, h) for h in range(RET_HEADS)],
                  [memory_logits(c, h) for h in range(MEM_HEADS)],
                  [retention_values(c, h) for h in range(RET_HEADS)],
                  [memory_values(c, h) for h in range(MEM_HEADS)],
                  [retention_gated(c, h) for h in range(RET_HEADS)]
                  + [memory_gated(c, h) for h in range(MEM_HEADS)]]

        def run(group):
            def task():
                for t in group:
                    t()
            return task
        return [run(g) for g in groups]

    def out_piece(c, j):
        def task():
            if c not in cats:
                cats[c] = jnp.concatenate(
                    [mixed.pop((c, s)) for s in range(RET_HEADS + MEM_HEADS)], axis=1)
            sl = slice(j * OUT_PIECE_A, (j + 1) * OUT_PIECE_A)
            o_ref[0, c * chunk:(c + 1) * chunk, sl] = xs[c][:, sl] + _dot(cats[c], wout_ref[0, :, sl])
        return task

    setup = [rotary_tables] + [decay_tables(h) for h in range(RET_HEADS)]
    for step in range(n_chunks + 2):
        stages = [setup] if step == 0 else []
        if step < n_chunks:
            normalize(step)
            stages.append([in_piece(step, p) for p in range(n_in)])
        if 0 <= step - 1 < n_chunks:
            stages.append(mixer_groups(step - 1))
        if 0 <= step - 2 < n_chunks:
            stages.append([out_piece(step - 2, j) for j in range(n_out)])
        for task in _interleaved(*stages):
            task()

    for h in range(RET_HEADS):
        state_ref[h] = states[h]


def _layer_a(x, pos_f32, invf, g_norm, w_in_bf16, g_ret_head, w_out_bf16, kt, mv, g_mem_q,
             la, layer):
    batch, seq, _ = x.shape
    tile = ROW_TILE_A
    n_a = g_norm.shape[0]
    depth = g_mem_q.shape[0]
    a_in_w = w_in_bf16.shape[2]
    a_out_in = w_out_bf16.shape[1]
    mem_len = mv.shape[2]
    return pl.pallas_call(
        _layer_a_kernel,
        grid=(batch, seq // tile),
        in_specs=[
            pl.BlockSpec((1, tile, D_MODEL), lambda b, t: (b, t, 0)),
            pl.BlockSpec((1, 1, tile), lambda b, t: (b, 0, t)),
            _const_spec((RET_DK // 2, 1), lambda b, t: (0, 0)),
            _const_spec((1, 1, D_MODEL), lambda b, t: (la, 0, 0)),
            _const_spec((1, D_MODEL, a_in_w), lambda b, t: (la, 0, 0)),
            _const_spec((1, 1, RET_DV), lambda b, t: (la, 0, 0)),
            _const_spec((1, a_out_in, D_MODEL), lambda b, t: (la, 0, 0)),
            pl.BlockSpec((1, 1, MEM_W, mem_len), lambda b, t: (layer, b, 0, 0)),
            pl.BlockSpec((1, 1, mem_len, MEM_W), lambda b, t: (layer, b, 0, 0)),
            _const_spec((1, 1, MEM_DH), lambda b, t: (layer, 0, 0)),
        ],
        out_specs=pl.BlockSpec((1, tile, D_MODEL), lambda b, t: (b, t, 0)),
        out_shape=jax.ShapeDtypeStruct(x.shape, F32),
        scratch_shapes=[pltpu.VMEM((RET_HEADS, RET_DK, RET_DV), F32)],
        compiler_params=pltpu.CompilerParams(
            dimension_semantics=("arbitrary", "arbitrary"),
            vmem_limit_bytes=VMEM_LIMIT_BYTES),
        name="layer_a",
    )(x, pos_f32, invf, g_norm.reshape(n_a, 1, D_MODEL), w_in_bf16,
      g_ret_head.reshape(n_a, 1, RET_DV), w_out_bf16, kt, mv, g_mem_q.reshape(depth, 1, MEM_DH))


def _proj_b_kernel(x_ref, gkv_ref, gnb_ref, wkv_ref, win_ref, kt_ref, mv_ref, gq_ref,
                   q_ref, k_ref, v_ref, gate_ref, mo_ref):
    x = x_ref[0]
    xr = x * lax.rsqrt(jnp.mean(x * x, axis=-1, keepdims=True) + EPS)
    xkv = (xr * gkv_ref[...]).astype(BF16)
    xb = (xr * gnb_ref[0]).astype(BF16)
    heads = range(MEM_HEADS)
    head = lambda t, h: t[:, h * MEM_DH:(h + 1) * MEM_DH]
    o_mq = 2 * SB_W
    memq = _dot(xb, win_ref[0, :, o_mq:o_mq + MEM_W])
    memgate = _dot(xb, win_ref[0, :, o_mq + MEM_W:])
    logits = [_mem_logits(h, head(memq, h), kt_ref, gq_ref[0]) for h in heads]
    gate_ref[0] = _silu(_dot(xb, win_ref[0, :, SB_W:o_mq])).astype(BF16)
    values = [_mem_values(h, logits[h], mv_ref) for h in heads]
    k_ref[0] = _dot(xkv, wkv_ref[:, :SB_W]).astype(BF16)
    mo_ref[0] = jnp.concatenate([_mem_gated(values[h], head(memgate, h)) for h in heads], axis=1)
    v_ref[0] = _dot(xkv, wkv_ref[:, SB_W:]).astype(BF16)
    q_ref[0] = _dot(xb, win_ref[0, :, :SB_W]).astype(BF16)


def _proj_b(x, g_kv, g_norm_b, w_kv_bf16, w_in_bf16, kt, mv, g_mem_q, lb, layer):
    batch, seq, _ = x.shape
    tile = ROW_TILE_B
    n_b = g_norm_b.shape[0]
    depth = g_mem_q.shape[0]
    mem_len = mv.shape[2]
    b_in_w = w_in_bf16.shape[2]
    row_spec = lambda w: pl.BlockSpec((1, tile, w), lambda b, t: (b, t, 0))
    widths = (SB_W, SB_W, SB_W, SB_W, MEM_W)
    return pl.pallas_call(
        _proj_b_kernel,
        grid=(batch, seq // tile),
        in_specs=[
            row_spec(D_MODEL),
            _const_spec((1, D_MODEL), lambda b, t: (0, 0)),
            _const_spec((1, 1, D_MODEL), lambda b, t: (lb, 0, 0)),
            _const_spec((D_MODEL, 2 * SB_W), lambda b, t: (0, 0)),
            _const_spec((1, D_MODEL, b_in_w), lambda b, t: (lb, 0, 0)),
            pl.BlockSpec((1, 1, MEM_W, mem_len), lambda b, t: (layer, b, 0, 0)),
            pl.BlockSpec((1, 1, mem_len, MEM_W), lambda b, t: (layer, b, 0, 0)),
            _const_spec((1, 1, MEM_DH), lambda b, t: (layer, 0, 0)),
        ],
        out_specs=[row_spec(w) for w in widths],
        out_shape=[jax.ShapeDtypeStruct((batch, seq, w), BF16) for w in widths],
        compiler_params=pltpu.CompilerParams(
            dimension_semantics=("arbitrary", "arbitrary"),
            vmem_limit_bytes=VMEM_LIMIT_BYTES),
        name="proj_b",
    )(x, g_kv.reshape(1, D_MODEL), g_norm_b.reshape(n_b, 1, D_MODEL), w_kv_bf16, w_in_bf16,
      kt, mv, g_mem_q.reshape(depth, 1, MEM_DH))


def _sb_out_kernel(x_ref, q_ref, k_ref, v_ref, gate_ref, mo_ref, wout_ref, o_ref, cat_ref, *,
                   n_tiles):
    tile = q_ref.shape[1]
    heads = q_ref.shape[2] // SB_DH
    qi = pl.program_id(1)
    scale = SB_DH ** -0.5

    row = lax.broadcasted_iota(jnp.int32, (tile, tile), 0)
    col = lax.broadcasted_iota(jnp.int32, (tile, tile), 1)
    causal = col < row
    suffix_mat = jnp.where(row > col, 1.0, 0.0).astype(BF16)


    def stage_scores(g, start, nblk, diag):
        hs = slice(g * SB_DH, (g + 1) * SB_DH)
        zr = _dot_nt(q_ref[0, :, hs], k_ref[0, pl.ds(start, nblk * tile), hs])
        z = zr * (scale * LOG2E)
        sp = jnp.maximum(z, 0.0) + jnp.log(1.0 + jnp.exp2(-jnp.abs(z))) * LOG2E
        blocks = [sp[:, j * tile:(j + 1) * tile] for j in range(nblk)]
        if diag:
            blocks[-1] = jnp.where(causal, blocks[-1], 0.0)
        his, los = [], []
        for blk in blocks:
            hi = blk.astype(BF16)
            his.append(hi)
            los.append((blk - hi.astype(F32)).astype(BF16))
        return z, blocks, jnp.concatenate(his, axis=0), jnp.concatenate(los, axis=0)

    def stage_weights(scores, nblk, diag, penalty):
        z, blocks, hi, lo = scores
        suffix = _dot(hi, suffix_mat) + _dot(lo, suffix_mat)
        weights = [None] * nblk
        for j in reversed(range(nblk)):
            sfx = suffix[j * tile:(j + 1) * tile]
            log_a = (z[:, j * tile:(j + 1) * tile] - blocks[j]) - sfx
            if penalty is not None:
                log_a = log_a - penalty
            a = jnp.exp2(log_a)
            if diag and j == nblk - 1:
                a = jnp.where(causal, a, 0.0)
            weights[j] = a.astype(BF16)
            total = sfx[:, 0:1] + blocks[j][:, 0:1]
            penalty = total if penalty is None else penalty + total
        return jnp.concatenate(weights, axis=1), penalty

    def stage_values(g, start, nblk, weights):
        hs = slice(g * SB_DH, (g + 1) * SB_DH)
        return _dot(weights, v_ref[0, pl.ds(start, nblk * tile), hs])

    def windows(start, nblk, diag, penalties, other_tasks=()):
        scores, weights, out = {}, {}, [None] * heads

        def stagger(step):
            def task():
                if step < heads:
                    scores[step] = stage_scores(step, start, nblk, diag)
                g = step - 1
                if 0 <= g < heads:
                    weights[g] = stage_weights(scores.pop(g), nblk, diag,
                                               None if penalties is None else penalties[g])
                g = step - 2
                if 0 <= g < heads:
                    w, penalty = weights.pop(g)
                    out[g] = (stage_values(g, start, nblk, w), penalty)
            return task

        steps = [stagger(step) for step in range(heads + 2)]
        others = list(other_tasks)
        n_front = (len(others) + 1) // 2
        order = steps[:1] + others[:n_front] + steps[1:-1] + others[n_front:] + steps[-1:]
        for task in order:
            task()
        return out

    cur = lax.rem(qi, 2)
    n_out = o_ref.shape[2] // OUT_PIECE_B

    def out_piece(j):
        def task():
            sl = slice(j * OUT_PIECE_B, (j + 1) * OUT_PIECE_B)
            o_ref[0, :, sl] = x_ref[0, :, sl] + _dot(cat_ref[1 - cur], wout_ref[0, :, sl])
        return task

    out_tasks = [out_piece(j) for j in range(n_out)]

    def store(accs):
        for g in range(heads):
            hs = slice(g * SB_DH, (g + 1) * SB_DH)
            cat_ref[cur, :, hs] = (accs[g] * gate_ref[0, :, hs].astype(F32)).astype(BF16)
        cat_ref[cur, :, SB_W:] = mo_ref[0]

    @pl.when(qi == 0)
    def _():
        store([r[0] for r in windows(0, 1, True, None)])

    @pl.when(qi == n_tiles)
    def _():
        for task in out_tasks:
            task()

    @pl.when(jnp.logical_and(qi > 0, qi < n_tiles))
    def _():
        near = windows(pl.multiple_of((qi - 1) * tile, tile), 2, True, None, out_tasks)
        accs = [r[0] for r in near]
        penalties = [r[1] for r in near]
        store(accs)

        def log_weight_bound(pens):
            return -functools.reduce(jnp.minimum, [jnp.min(p) for p in pens])

        def cond(carry):
            return jnp.logical_and(carry[0] <= qi, carry[1] > SB_LOG_WEIGHT_FLOOR)

        def body(carry):
            n, _, accs, pens = carry
            start = pl.multiple_of((qi - n) * tile, tile)
            far = windows(start, 1, False, pens)
            accs = [accs[g] + far[g][0] for g in range(heads)]
            pens = [far[g][1] for g in range(heads)]
            return n + 1, log_weight_bound(pens), accs, pens

        bound = log_weight_bound(penalties)

        @pl.when(jnp.logical_and(qi >= 2, bound > SB_LOG_WEIGHT_FLOOR))
        def _():
            _, _, accs_far, _ = lax.while_loop(cond, body, (jnp.int32(2), bound, accs, penalties))
            store(accs_far)


def _sb_out(x, q, k, v, gate, mo, w_out_bf16, lb):
    batch, seq, _ = x.shape
    tile = SB_TILE
    n_tiles = seq // tile
    b_out_in = w_out_bf16.shape[1]
    attn_spec = lambda w: pl.BlockSpec((1, tile, w), lambda b, i: (b, jnp.minimum(i, n_tiles - 1), 0))
    proj_spec = lambda w: pl.BlockSpec((1, tile, w), lambda b, i: (b, jnp.maximum(i - 1, 0), 0))
    kv_spec = _const_spec((1, seq, SB_W), lambda b, i: (b, 0, 0))
    return pl.pallas_call(
        functools.partial(_sb_out_kernel, n_tiles=n_tiles),
        grid=(batch, n_tiles + 1),
        in_specs=[
            proj_spec(D_MODEL), attn_spec(SB_W), kv_spec, kv_spec, attn_spec(SB_W), attn_spec(MEM_W),
            _const_spec((1, b_out_in, D_MODEL), lambda b, i: (lb, 0, 0)),
        ],
        out_specs=proj_spec(D_MODEL),
        out_shape=jax.ShapeDtypeStruct(x.shape, F32),
        scratch_shapes=[pltpu.VMEM((2, tile, b_out_in), BF16)],
        compiler_params=pltpu.CompilerParams(
            dimension_semantics=("arbitrary", "arbitrary"),
            vmem_limit_bytes=SB_VMEM_LIMIT_BYTES),
        name="sb_out",
    )(x, q, k, v, gate, mo, w_out_bf16)


def kernel(x, mem, positions, g_norm_a, w_in_a, g_ret_head, w_out_a, g_kv, w_kv, g_norm_b, w_in_b,
           w_out_b, g_mem, w_mem_kv, g_mem_q, g_mem_k):
    n_a = g_norm_a.shape[0]
    n_b = g_norm_b.shape[0]
    batch, seq, _ = x.shape

    kt, mv = _memory_kv(mem, g_mem, w_mem_kv.astype(BF16), g_mem_k)

    inv_freq = ROPE_BASE ** (-jnp.arange(0, RET_DK // 2, dtype=F32) * 2.0 / RET_DK)
    invf = inv_freq.reshape(RET_DK // 2, 1)
    pos_f32 = positions.astype(F32).reshape(batch, 1, seq)

    w_in_a_bf16 = w_in_a.astype(BF16)
    w_out_a_bf16 = w_out_a.astype(BF16)
    w_in_b_bf16 = w_in_b.astype(BF16)
    w_out_b_bf16 = w_out_b.astype(BF16)
    w_kv_bf16 = w_kv.astype(BF16)

    for la in range(n_a):
        x = _layer_a(x, pos_f32, invf, g_norm_a, w_in_a_bf16, g_ret_head, w_out_a_bf16,
                     kt, mv, g_mem_q, la, la)

    k_shared = v_shared = None
    for lb in range(n_b):
        layer = n_a + lb
        q, k_new, v_new, gate, mo = _proj_b(x, g_kv, g_norm_b, w_kv_bf16, w_in_b_bf16, kt, mv,
                                            g_mem_q, lb, layer)
        if lb == 0:
            k_shared, v_shared = k_new, v_new
        x = _sb_out(x, q, k_shared, v_shared, gate, mo, w_out_b_bf16, lb)
    return x
```

```python
import functools
import math

import numpy as np
import jax
import jax.numpy as jnp
from jax import lax
from jax.experimental import pallas as pl
from jax.experimental.pallas import tpu as pltpu

F32 = jnp.float32
BF16 = jnp.bfloat16

D_MODEL = 1024
RET_HEADS = 4
RET_DK = 128
RET_DV = 256
RET_QK_W = RET_HEADS * RET_DK
RET_V_W = RET_HEADS * RET_DV
ROPE_BASE = 10000.0
SB_HEADS = 8
SB_DH = 128
SB_W = SB_HEADS * SB_DH
MEM_HEADS = 4
MEM_DH = 128
MEM_W = MEM_HEADS * MEM_DH
EPS = 1e-6

RET_LOG_GAMMA = tuple(
    math.log(float(np.float32(1.0 - 2.0 ** (-5.0 - h)))) for h in range(RET_HEADS))

VMEM_LIMIT_BYTES = 48 * 1024 * 1024

ROW_TILE_A = 1024
RET_CHUNK = 256
IN_PIECE_A = 512
OUT_PIECE_A = 256
ROW_TILE_B = 512
OUT_PIECE_B = 256
SB_TILE = 256
SB_LOG_WEIGHT_FLOOR = -151.0
SB_EXP2_CLAMP = 64.0
SB_VMEM_LIMIT_BYTES = 58 * 1024 * 1024
LOG2E = math.log2(math.e)


def _rms(x, g):
    return x * lax.rsqrt(jnp.mean(x * x, axis=-1, keepdims=True) + EPS) * g


def _silu(x):
    return x * (1.0 / (1.0 + jnp.exp(-x)))


def _dot(a, b):
    return jnp.dot(a, b, preferred_element_type=F32)


def _dot_nt(a, b):
    return lax.dot_general(a, b, (((1,), (1,)), ((), ())), preferred_element_type=F32)


def _const_spec(shape, index_map):
    return pl.BlockSpec(shape, index_map, pipeline_mode=pl.Buffered(1))


def _interleaved(*task_lists):
    tagged = [((i + 0.5) / len(tasks), k, task)
              for k, tasks in enumerate(task_lists) for i, task in enumerate(tasks)]
    return [task for _, _, task in sorted(tagged, key=lambda e: e[:2])]


def _memkv_kernel(mem_ref, g_ref, w_ref, gk_ref, kt_ref, v_ref):
    mn = _rms(mem_ref[0], g_ref[0]).astype(BF16)
    kv = _dot(mn, w_ref[0])
    gk = gk_ref[0]
    for h in range(MEM_HEADS):
        kh = _rms(kv[:, h * MEM_DH:(h + 1) * MEM_DH], gk)
        kt_ref[0, 0, h * MEM_DH:(h + 1) * MEM_DH, :] = kh.T.astype(BF16)
    v_ref[0, 0] = kv[:, MEM_W:].astype(BF16)


def _memory_kv(mem, g_mem, w_mem_kv_bf16, g_mem_k):
    depth = g_mem.shape[0]
    batch, mem_len, _ = mem.shape
    return pl.pallas_call(
        _memkv_kernel,
        grid=(depth, batch),
        in_specs=[
            pl.BlockSpec((1, mem_len, D_MODEL), lambda l, b: (b, 0, 0)),
            pl.BlockSpec((1, 1, D_MODEL), lambda l, b: (l, 0, 0)),
            pl.BlockSpec((1, D_MODEL, 2 * MEM_W), lambda l, b: (l, 0, 0)),
            pl.BlockSpec((1, 1, MEM_DH), lambda l, b: (l, 0, 0)),
        ],
        out_specs=[
            pl.BlockSpec((1, 1, MEM_W, mem_len), lambda l, b: (l, b, 0, 0)),
            pl.BlockSpec((1, 1, mem_len, MEM_W), lambda l, b: (l, b, 0, 0)),
        ],
        out_shape=[
            jax.ShapeDtypeStruct((depth, batch, MEM_W, mem_len), BF16),
            jax.ShapeDtypeStruct((depth, batch, mem_len, MEM_W), BF16),
        ],
        compiler_params=pltpu.CompilerParams(
            dimension_semantics=("arbitrary", "arbitrary"),
            vmem_limit_bytes=VMEM_LIMIT_BYTES),
        name="memkv",
    )(mem, g_mem.reshape(depth, 1, D_MODEL), w_mem_kv_bf16, g_mem_k.reshape(depth, 1, MEM_DH))


def _mem_logits(h, memq, kt_ref, gq):
    qn = _rms(memq, gq).astype(BF16)
    return _dot(qn, kt_ref[0, 0, h * MEM_DH:(h + 1) * MEM_DH, :]) * (MEM_DH ** -0.5)


def _mem_values(h, logits, mv_ref):
    e = jnp.exp(logits - jnp.max(logits, axis=-1, keepdims=True))
    return _dot(e.astype(BF16), mv_ref[0, 0, :, h * MEM_DH:(h + 1) * MEM_DH]), \
        jnp.sum(e, axis=-1, keepdims=True)


def _mem_gated(values, memgate):
    o, denom = values
    return (o / denom * _silu(memgate)).astype(BF16)


def _mem_attention_head(h, memq, memgate, kt_ref, mv_ref, gq):
    return _mem_gated(_mem_values(h, _mem_logits(h, memq, kt_ref, gq), mv_ref), memgate)


def _layer_a_kernel(x_ref, pos_ref, invf_ref, gn_ref, win_ref, gret_ref, wout_ref,
                    kt_ref, mv_ref, gq_ref, o_ref, state_ref):
    tile = x_ref.shape[1]
    chunk = RET_CHUNK
    n_chunks = tile // chunk
    half = RET_DK // 2
    n_in = win_ref.shape[2] // IN_PIECE_A
    n_out = wout_ref.shape[2] // OUT_PIECE_A
    o_v = 2 * RET_QK_W
    o_g = o_v + RET_V_W
    o_mq = o_g + RET_V_W

    @pl.when(pl.program_id(1) == 0)
    def _():
        state_ref[...] = jnp.zeros_like(state_ref)

    tables = {}
    states = [state_ref[h] for h in range(RET_HEADS)]
    xs, xns = {}, {}
    u = {}
    mixed = {}
    cats = {}

    def rotary_tables():
        ang_t = invf_ref[...] * pos_ref[0]
        cos_sin = jnp.concatenate([jnp.cos(ang_t), jnp.sin(ang_t)], axis=0).T
        sin_cos = pltpu.roll(cos_sin, half, 1)
        low_lanes = lax.broadcasted_iota(jnp.int32, (tile, RET_DK), 1) < half
        tables["cos"] = jnp.where(low_lanes, cos_sin, sin_cos)
        tables["sin"] = jnp.where(low_lanes, -sin_cos, cos_sin)

    def decay_tables(h):
        def task():
            lg = RET_LOG_GAMMA[h]
            rel = (lax.broadcasted_iota(jnp.int32, (chunk, chunk), 0)
                   - lax.broadcasted_iota(jnp.int32, (chunk, chunk), 1)).astype(F32)
            idx = lax.broadcasted_iota(jnp.int32, (chunk, 1), 0).astype(F32)
            tables[h] = (jnp.where(rel >= 0.0, jnp.exp(jnp.maximum(rel, 0.0) * lg), 0.0),
                         jnp.exp((idx + 1.0) * lg),
                         jnp.exp((chunk - 1.0 - idx) * lg))
        return task

    def normalize(c):
        xs[c] = x_ref[0, c * chunk:(c + 1) * chunk, :]
        xns[c] = _rms(xs[c], gn_ref[0]).astype(BF16)

    def in_piece(c, p):
        def task():
            u[c, p] = _dot(xns[c], win_ref[0, :, p * IN_PIECE_A:(p + 1) * IN_PIECE_A])
        return task

    def cols(c, lo, width):
        p, off = divmod(lo, IN_PIECE_A)
        return u[c, p][:, off:off + width]

    part = {}

    def retention_scores(c, h):
        def task():
            rows = slice(c * chunk, (c + 1) * chunk)
            cos_c, sin_c = tables["cos"][rows], tables["sin"][rows]

            def rotary(t):
                return t * cos_c + pltpu.roll(t, half, 1) * sin_c

            q = rotary(cols(c, h * RET_DK, RET_DK)).astype(BF16)
            k = rotary(cols(c, RET_QK_W + h * RET_DK, RET_DK)) * (RET_DK ** -0.5)
            part[c, h, "q"], part[c, h, "k"] = q, k
            part[c, h, "scores"] = _dot_nt(q, k.astype(BF16))
        return task

    def retention_values(c, h):
        def task():
            decay, cross_decay, state_decay = tables[h]
            q, k = part.pop((c, h, "q")), part.pop((c, h, "k"))
            v = cols(c, o_v + h * RET_DV, RET_DV).astype(BF16)
            intra = _dot((part.pop((c, h, "scores")) * decay).astype(BF16), v)
            cross = _dot(q, states[h].astype(BF16)) * cross_decay
            states[h] = (math.exp(chunk * RET_LOG_GAMMA[h]) * states[h]
                         + _dot((k * state_decay).T.astype(BF16), v))
            part[c, h, "ret"] = intra + cross
        return task

    def retention_gated(c, h):
        def task():
            ret = _rms(part.pop((c, h, "ret")), gret_ref[0])
            mixed[c, h] = (ret * _silu(cols(c, o_g + h * RET_DV, RET_DV))).astype(BF16)
        return task

    def memory_logits(c, h):
        def task():
            part[c, h, "logits"] = _mem_logits(h, cols(c, o_mq + h * MEM_DH, MEM_DH),
                                               kt_ref, gq_ref[0])
        return task

    def memory_values(c, h):
        def task():
            part[c, h, "values"] = _mem_values(h, part.pop((c, h, "logits")), mv_ref)
        return task

    def memory_gated(c, h):
        def task():
            mixed[c, RET_HEADS + h] = _mem_gated(part.pop((c, h, "values")),
                                                 cols(c, o_mq + MEM_W + h * MEM_DH, MEM_DH))
        return task

    def mixer_groups(c):
        groups = [[retention_scores(c, h) for h in range(RET_HEADS)],
                  [memory_logits(c, h) for h in range(MEM_HEADS)],
                  [retention_values(c, h) for h in range(RET_HEADS)],
                  [memory_values(c, h) for h in range(MEM_HEADS)],
                  [retention_gated(c, h) for h in range(RET_HEADS)]
                  + [memory_gated(c, h) for h in range(MEM_HEADS)]]

        def run(group):
            def task():
                for t in group:
                    t()
            return task
        return [run(g) for g in groups]

    def out_piece(c, j):
        def task():
            if c not in cats:
                cats[c] = jnp.concatenate(
                    [mixed.pop((c, s)) for s in range(RET_HEADS + MEM_HEADS)], axis=1)
            sl = slice(j * OUT_PIECE_A, (j + 1) * OUT_PIECE_A)
            o_ref[0, c * chunk:(c + 1) * chunk, sl] = xs[c][:, sl] + _dot(cats[c], wout_ref[0, :, sl])
        return task

    setup = [rotary_tables] + [decay_tables(h) for h in range(RET_HEADS)]
    for step in range(n_chunks + 2):
        stages = [setup] if step == 0 else []
        if step < n_chunks:
            normalize(step)
            stages.append([in_piece(step, p) for p in range(n_in)])
        if 0 <= step - 1 < n_chunks:
            stages.append(mixer_groups(step - 1))
        if 0 <= step - 2 < n_chunks:
            stages.append([out_piece(step - 2, j) for j in range(n_out)])
        for task in _interleaved(*stages):
            task()

    for h in range(RET_HEADS):
        state_ref[h] = states[h]


def _layer_a(x, pos_f32, invf, g_norm, w_in_bf16, g_ret_head, w_out_bf16, kt, mv, g_mem_q,
             la, layer):
    batch, seq, _ = x.shape
    tile = ROW_TILE_A
    n_a = g_norm.shape[0]
    depth = g_mem_q.shape[0]
    a_in_w = w_in_bf16.shape[2]
    a_out_in = w_out_bf16.shape[1]
    mem_len = mv.shape[2]
    return pl.pallas_call(
        _layer_a_kernel,
        grid=(batch, seq // tile),
        in_specs=[
            pl.BlockSpec((1, tile, D_MODEL), lambda b, t: (b, t, 0)),
            pl.BlockSpec((1, 1, tile), lambda b, t: (b, 0, t)),
            _const_spec((RET_DK // 2, 1), lambda b, t: (0, 0)),
            _const_spec((1, 1, D_MODEL), lambda b, t: (la, 0, 0)),
            _const_spec((1, D_MODEL, a_in_w), lambda b, t: (la, 0, 0)),
            _const_spec((1, 1, RET_DV), lambda b, t: (la, 0, 0)),
            _const_spec((1, a_out_in, D_MODEL), lambda b, t: (la, 0, 0)),
            pl.BlockSpec((1, 1, MEM_W, mem_len), lambda b, t: (layer, b, 0, 0)),
            pl.BlockSpec((1, 1, mem_len, MEM_W), lambda b, t: (layer, b, 0, 0)),
            _const_spec((1, 1, MEM_DH), lambda b, t: (layer, 0, 0)),
        ],
        out_specs=pl.BlockSpec((1, tile, D_MODEL), lambda b, t: (b, t, 0)),
        out_shape=jax.ShapeDtypeStruct(x.shape, F32),
        scratch_shapes=[pltpu.VMEM((RET_HEADS, RET_DK, RET_DV), F32)],
        compiler_params=pltpu.CompilerParams(
            dimension_semantics=("arbitrary", "arbitrary"),
            vmem_limit_bytes=VMEM_LIMIT_BYTES),
        name="layer_a",
    )(x, pos_f32, invf, g_norm.reshape(n_a, 1, D_MODEL), w_in_bf16,
      g_ret_head.reshape(n_a, 1, RET_DV), w_out_bf16, kt, mv, g_mem_q.reshape(depth, 1, MEM_DH))


def _proj_b_kernel(x_ref, gkv_ref, gnb_ref, wkv_ref, win_ref, kt_ref, mv_ref, gq_ref,
                   q_ref, k_ref, v_ref, gate_ref, mo_ref):
    x = x_ref[0]
    xr = x * lax.rsqrt(jnp.mean(x * x, axis=-1, keepdims=True) + EPS)
    xkv = (xr * gkv_ref[...]).astype(BF16)
    xb = (xr * gnb_ref[0]).astype(BF16)
    heads = range(MEM_HEADS)
    head = lambda t, h: t[:, h * MEM_DH:(h + 1) * MEM_DH]
    o_mq = 2 * SB_W
    memq = _dot(xb, win_ref[0, :, o_mq:o_mq + MEM_W])
    memgate = _dot(xb, win_ref[0, :, o_mq + MEM_W:])
    logits = [_mem_logits(h, head(memq, h), kt_ref, gq_ref[0]) for h in heads]
    gate_ref[0] = _silu(_dot(xb, win_ref[0, :, SB_W:o_mq])).astype(BF16)
    values = [_mem_values(h, logits[h], mv_ref) for h in heads]
    k_ref[0] = _dot(xkv, wkv_ref[:, :SB_W]).astype(BF16)
    mo_ref[0] = jnp.concatenate([_mem_gated(values[h], head(memgate, h)) for h in heads], axis=1)
    v_ref[0] = _dot(xkv, wkv_ref[:, SB_W:]).astype(BF16)
    q_ref[0] = _dot(xb, win_ref[0, :, :SB_W]).astype(BF16)


def _proj_b(x, g_kv, g_norm_b, w_kv_bf16, w_in_bf16, kt, mv, g_mem_q, lb, layer):
    batch, seq, _ = x.shape
    tile = ROW_TILE_B
    n_b = g_norm_b.shape[0]
    depth = g_mem_q.shape[0]
    mem_len = mv.shape[2]
    b_in_w = w_in_bf16.shape[2]
    row_spec = lambda w: pl.BlockSpec((1, tile, w), lambda b, t: (b, t, 0))
    widths = (SB_W, SB_W, SB_W, SB_W, MEM_W)
    return pl.pallas_call(
        _proj_b_kernel,
        grid=(batch, seq // tile),
        in_specs=[
            row_spec(D_MODEL),
            _const_spec((1, D_MODEL), lambda b, t: (0, 0)),
            _const_spec((1, 1, D_MODEL), lambda b, t: (lb, 0, 0)),
            _const_spec((D_MODEL, 2 * SB_W), lambda b, t: (0, 0)),
            _const_spec((1, D_MODEL, b_in_w), lambda b, t: (lb, 0, 0)),
            pl.BlockSpec((1, 1, MEM_W, mem_len), lambda b, t: (layer, b, 0, 0)),
            pl.BlockSpec((1, 1, mem_len, MEM_W), lambda b, t: (layer, b, 0, 0)),
            _const_spec((1, 1, MEM_DH), lambda b, t: (layer, 0, 0)),
        ],
        out_specs=[row_spec(w) for w in widths],
        out_shape=[jax.ShapeDtypeStruct((batch, seq, w), BF16) for w in widths],
        compiler_params=pltpu.CompilerParams(
            dimension_semantics=("arbitrary", "arbitrary"),
            vmem_limit_bytes=VMEM_LIMIT_BYTES),
        name="proj_b",
    )(x, g_kv.reshape(1, D_MODEL), g_norm_b.reshape(n_b, 1, D_MODEL), w_kv_bf16, w_in_bf16,
      kt, mv, g_mem_q.reshape(depth, 1, MEM_DH))


def _sb_out_kernel(x_ref, q_ref, k_ref, v_ref, gate_ref, mo_ref, wout_ref, o_ref, cat_ref, *,
                   n_tiles):
    tile = q_ref.shape[1]
    heads = q_ref.shape[2] // SB_DH
    qi = pl.program_id(1)
    scale = SB_DH ** -0.5

    row = lax.broadcasted_iota(jnp.int32, (tile, tile), 0)
    col = lax.broadcasted_iota(jnp.int32, (tile, tile), 1)
    causal = col < row
    suffix_mat = jnp.where(row > col, 1.0, 0.0).astype(BF16)


    def stage_scores(g, start, nblk, diag):
        hs = slice(g * SB_DH, (g + 1) * SB_DH)
        zr = _dot_nt(q_ref[0, :, hs], k_ref[0, pl.ds(start, nblk * tile), hs])
        z = zr * (scale * LOG2E)
        sp = jnp.maximum(z, jnp.log(1.0 + jnp.exp2(jnp.minimum(z, SB_EXP2_CLAMP))) * LOG2E)
        blocks = [sp[:, j * tile:(j + 1) * tile] for j in range(nblk)]
        if diag:
            blocks[-1] = jnp.where(causal, blocks[-1], 0.0)
        his, los = [], []
        for blk in blocks:
            hi = blk.astype(BF16)
            his.append(hi)
            los.append((blk - hi.astype(F32)).astype(BF16))
        return z, blocks, jnp.concatenate(his, axis=0), jnp.concatenate(los, axis=0)

    def stage_weights(scores, nblk, diag, penalty):
        z, blocks, hi, lo = scores
        suffix = _dot(hi, suffix_mat) + _dot(lo, suffix_mat)
        weights = [None] * nblk
        for j in reversed(range(nblk)):
            sfx = suffix[j * tile:(j + 1) * tile]
            log_a = (z[:, j * tile:(j + 1) * tile] - blocks[j]) - sfx
            if penalty is not None:
                log_a = log_a - penalty
            a = jnp.exp2(log_a)
            if diag and j == nblk - 1:
                a = jnp.where(causal, a, 0.0)
            weights[j] = a.astype(BF16)
            total = sfx[:, 0:1] + blocks[j][:, 0:1]
            penalty = total if penalty is None else penalty + total
        return jnp.concatenate(weights, axis=1), penalty

    def stage_values(g, start, nblk, weights):
        hs = slice(g * SB_DH, (g + 1) * SB_DH)
        return _dot(weights, v_ref[0, pl.ds(start, nblk * tile), hs])

    def windows(start, nblk, diag, penalties, other_tasks=()):
        scores, weights, out = {}, {}, [None] * heads

        def stagger(step):
            def task():
                if step < heads:
                    scores[step] = stage_scores(step, start, nblk, diag)
                g = step - 1
                if 0 <= g < heads:
                    weights[g] = stage_weights(scores.pop(g), nblk, diag,
                                               None if penalties is None else penalties[g])
                g = step - 2
                if 0 <= g < heads:
                    w, penalty = weights.pop(g)
                    out[g] = (stage_values(g, start, nblk, w), penalty)
            return task

        steps = [stagger(step) for step in range(heads + 2)]
        others = list(other_tasks)
        n_front = (len(others) + 1) // 2
        order = steps[:1] + others[:n_front] + steps[1:-1] + others[n_front:] + steps[-1:]
        for task in order:
            task()
        return out

    cur = lax.rem(qi, 2)
    n_out = o_ref.shape[2] // OUT_PIECE_B

    def out_piece(j):
        def task():
            sl = slice(j * OUT_PIECE_B, (j + 1) * OUT_PIECE_B)
            o_ref[0, :, sl] = x_ref[0, :, sl] + _dot(cat_ref[1 - cur], wout_ref[0, :, sl])
        return task

    out_tasks = [out_piece(j) for j in range(n_out)]

    def store(accs):
        for g in range(heads):
            hs = slice(g * SB_DH, (g + 1) * SB_DH)
            cat_ref[cur, :, hs] = (accs[g] * gate_ref[0, :, hs].astype(F32)).astype(BF16)
        cat_ref[cur, :, SB_W:] = mo_ref[0]

    @pl.when(qi == 0)
    def _():
        store([r[0] for r in windows(0, 1, True, None)])

    @pl.when(qi == n_tiles)
    def _():
        for task in out_tasks:
            task()

    @pl.when(jnp.logical_and(qi > 0, qi < n_tiles))
    def _():
        near = windows(pl.multiple_of((qi - 1) * tile, tile), 2, True, None, out_tasks)
        accs = [r[0] for r in near]
        penalties = [r[1] for r in near]
        store(accs)

        def log_weight_bound(pens):
            return -functools.reduce(jnp.minimum, [jnp.min(p) for p in pens])

        def cond(carry):
            return jnp.logical_and(carry[0] <= qi, carry[1] > SB_LOG_WEIGHT_FLOOR)

        def body(carry):
            n, _, accs, pens = carry
            start = pl.multiple_of((qi - n) * tile, tile)
            far = windows(start, 1, False, pens)
            accs = [accs[g] + far[g][0] for g in range(heads)]
            pens = [far[g][1] for g in range(heads)]
            return n + 1, log_weight_bound(pens), accs, pens

        bound = log_weight_bound(penalties)

        @pl.when(jnp.logical_and(qi >= 2, bound > SB_LOG_WEIGHT_FLOOR))
        def _():
            _, _, accs_far, _ = lax.while_loop(cond, body, (jnp.int32(2), bound, accs, penalties))
            store(accs_far)


def _sb_out(x, q, k, v, gate, mo, w_out_bf16, lb):
    batch, seq, _ = x.shape
    tile = SB_TILE
    n_tiles = seq // tile
    b_out_in = w_out_bf16.shape[1]
    attn_spec = lambda w: pl.BlockSpec((1, tile, w), lambda b, i: (b, jnp.minimum(i, n_tiles - 1), 0))
    proj_spec = lambda w: pl.BlockSpec((1, tile, w), lambda b, i: (b, jnp.maximum(i - 1, 0), 0))
    kv_spec = _const_spec((1, seq, SB_W), lambda b, i: (b, 0, 0))
    return pl.pallas_call(
        functools.partial(_sb_out_kernel, n_tiles=n_tiles),
        grid=(batch, n_tiles + 1),
        in_specs=[
            proj_spec(D_MODEL), attn_spec(SB_W), kv_spec, kv_spec, attn_spec(SB_W), attn_spec(MEM_W),
            _const_spec((1, b_out_in, D_MODEL), lambda b, i: (lb, 0, 0)),
        ],
        out_specs=proj_spec(D_MODEL),
        out_shape=jax.ShapeDtypeStruct(x.shape, F32),
        scratch_shapes=[pltpu.VMEM((2, tile, b_out_in), BF16)],
        compiler_params=pltpu.CompilerParams(
            dimension_semantics=("arbitrary", "arbitrary"),
            vmem_limit_bytes=SB_VMEM_LIMIT_BYTES),
        name="sb_out",
    )(x, q, k, v, gate, mo, w_out_bf16)


def kernel(x, mem, positions, g_norm_a, w_in_a, g_ret_head, w_out_a, g_kv, w_kv, g_norm_b, w_in_b,
           w_out_b, g_mem, w_mem_kv, g_mem_q, g_mem_k):
    n_a = g_norm_a.shape[0]
    n_b = g_norm_b.shape[0]
    batch, seq, _ = x.shape

    kt, mv = _memory_kv(mem, g_mem, w_mem_kv.astype(BF16), g_mem_k)

    inv_freq = ROPE_BASE ** (-jnp.arange(0, RET_DK // 2, dtype=F32) * 2.0 / RET_DK)
    invf = inv_freq.reshape(RET_DK // 2, 1)
    pos_f32 = positions.astype(F32).reshape(batch, 1, seq)

    w_in_a_bf16 = w_in_a.astype(BF16)
    w_out_a_bf16 = w_out_a.astype(BF16)
    w_in_b_bf16 = w_in_b.astype(BF16)
    w_out_b_bf16 = w_out_b.astype(BF16)
    w_kv_bf16 = w_kv.astype(BF16)

    for la in range(n_a):
        x = _layer_a(x, pos_f32, invf, g_norm_a, w_in_a_bf16, g_ret_head, w_out_a_bf16,
                     kt, mv, g_mem_q, la, la)

    k_shared = v_shared = None
    for lb in range(n_b):
        layer = n_a + lb
        q, k_new, v_new, gate, mo = _proj_b(x, g_kv, g_norm_b, w_kv_bf16, w_in_b_bf16, kt, mv,
                                            g_mem_q, lb, layer)
        if lb == 0:
            k_shared, v_shared = k_new, v_new
        x = _sb_out(x, q, k_shared, v_shared, gate, mo, w_out_b_bf16, lb)
    return x
```

```python
import functools
import math

import numpy as np
import jax
import jax.numpy as jnp
from jax import lax
from jax.experimental import pallas as pl
from jax.experimental.pallas import tpu as pltpu

F32 = jnp.float32
BF16 = jnp.bfloat16

D_MODEL = 1024
RET_HEADS = 4
RET_DK = 128
RET_DV = 256
RET_QK_W = RET_HEADS * RET_DK
RET_V_W = RET_HEADS * RET_DV
ROPE_BASE = 10000.0
SB_HEADS = 8
SB_DH = 128
SB_W = SB_HEADS * SB_DH
MEM_HEADS = 4
MEM_DH = 128
MEM_W = MEM_HEADS * MEM_DH
EPS = 1e-6

RET_LOG_GAMMA = tuple(
    math.log(float(np.float32(1.0 - 2.0 ** (-5.0 - h)))) for h in range(RET_HEADS))

VMEM_LIMIT_BYTES = 48 * 1024 * 1024

ROW_TILE_A = 1024
RET_CHUNK = 256
IN_PIECE_A = 512
OUT_PIECE_A = 256
ROW_TILE_B = 512
OUT_PIECE_B = 256
SB_TILE = 256
ROW_TILE_SB = 512
SB_LOG_WEIGHT_FLOOR = -151.0
SB_EXP2_CLAMP = 64.0
LOG2E = math.log2(math.e)


def _rms(x, g):
    return x * lax.rsqrt(jnp.mean(x * x, axis=-1, keepdims=True) + EPS) * g


def _silu(x):
    return x * (1.0 / (1.0 + jnp.exp(-x)))


def _dot(a, b):
    return jnp.dot(a, b, preferred_element_type=F32)


def _dot_nt(a, b):
    return lax.dot_general(a, b, (((1,), (1,)), ((), ())), preferred_element_type=F32)


def _const_spec(shape, index_map):
    return pl.BlockSpec(shape, index_map, pipeline_mode=pl.Buffered(1))


def _interleaved(*task_lists):
    tagged = [((i + 0.5) / len(tasks), k, task)
              for k, tasks in enumerate(task_lists) for i, task in enumerate(tasks)]
    return [task for _, _, task in sorted(tagged, key=lambda e: e[:2])]


def _memkv_kernel(mem_ref, g_ref, w_ref, gk_ref, kt_ref, v_ref):
    mn = _rms(mem_ref[0], g_ref[0]).astype(BF16)
    kv = _dot(mn, w_ref[0])
    gk = gk_ref[0]
    for h in range(MEM_HEADS):
        kh = _rms(kv[:, h * MEM_DH:(h + 1) * MEM_DH], gk)
        kt_ref[0, 0, h * MEM_DH:(h + 1) * MEM_DH, :] = kh.T.astype(BF16)
    v_ref[0, 0] = kv[:, MEM_W:].astype(BF16)


def _memory_kv(mem, g_mem, w_mem_kv_bf16, g_mem_k):
    depth = g_mem.shape[0]
    batch, mem_len, _ = mem.shape
    return pl.pallas_call(
        _memkv_kernel,
        grid=(depth, batch),
        in_specs=[
            pl.BlockSpec((1, mem_len, D_MODEL), lambda l, b: (b, 0, 0)),
            pl.BlockSpec((1, 1, D_MODEL), lambda l, b: (l, 0, 0)),
            pl.BlockSpec((1, D_MODEL, 2 * MEM_W), lambda l, b: (l, 0, 0)),
            pl.BlockSpec((1, 1, MEM_DH), lambda l, b: (l, 0, 0)),
        ],
        out_specs=[
            pl.BlockSpec((1, 1, MEM_W, mem_len), lambda l, b: (l, b, 0, 0)),
            pl.BlockSpec((1, 1, mem_len, MEM_W), lambda l, b: (l, b, 0, 0)),
        ],
        out_shape=[
            jax.ShapeDtypeStruct((depth, batch, MEM_W, mem_len), BF16),
            jax.ShapeDtypeStruct((depth, batch, mem_len, MEM_W), BF16),
        ],
        compiler_params=pltpu.CompilerParams(
            dimension_semantics=("arbitrary", "arbitrary"),
            vmem_limit_bytes=VMEM_LIMIT_BYTES),
        name="memkv",
    )(mem, g_mem.reshape(depth, 1, D_MODEL), w_mem_kv_bf16, g_mem_k.reshape(depth, 1, MEM_DH))


def _mem_logits(h, memq, kt_ref, gq):
    qn = _rms(memq, gq).astype(BF16)
    return _dot(qn, kt_ref[0, 0, h * MEM_DH:(h + 1) * MEM_DH, :]) * (MEM_DH ** -0.5)


def _mem_values(h, logits, mv_ref):
    e = jnp.exp(logits - jnp.max(logits, axis=-1, keepdims=True))
    return _dot(e.astype(BF16), mv_ref[0, 0, :, h * MEM_DH:(h + 1) * MEM_DH]), \
        jnp.sum(e, axis=-1, keepdims=True)


def _mem_gated(values, memgate):
    o, denom = values
    return (o / denom * _silu(memgate)).astype(BF16)


def _layer_a_kernel(x_ref, pos_ref, invf_ref, gn_ref, win_ref, gret_ref, wout_ref,
                    kt_ref, mv_ref, gq_ref, o_ref, state_ref):
    tile = x_ref.shape[1]
    chunk = RET_CHUNK
    n_chunks = tile // chunk
    half = RET_DK // 2
    n_in = win_ref.shape[2] // IN_PIECE_A
    n_out = wout_ref.shape[2] // OUT_PIECE_A
    o_v = 2 * RET_QK_W
    o_g = o_v + RET_V_W
    o_mq = o_g + RET_V_W

    @pl.when(pl.program_id(1) == 0)
    def _():
        state_ref[...] = jnp.zeros_like(state_ref)

    tables = {}
    states = [state_ref[h] for h in range(RET_HEADS)]
    xs, xns = {}, {}
    u = {}
    mixed = {}
    cats = {}

    def rotary_tables():
        ang_t = invf_ref[...] * pos_ref[0]
        cos_sin = jnp.concatenate([jnp.cos(ang_t), jnp.sin(ang_t)], axis=0).T
        sin_cos = pltpu.roll(cos_sin, half, 1)
        low_lanes = lax.broadcasted_iota(jnp.int32, (tile, RET_DK), 1) < half
        tables["cos"] = jnp.where(low_lanes, cos_sin, sin_cos)
        tables["sin"] = jnp.where(low_lanes, -sin_cos, cos_sin)

    def decay_tables(h):
        def task():
            lg = RET_LOG_GAMMA[h]
            rel = (lax.broadcasted_iota(jnp.int32, (chunk, chunk), 0)
                   - lax.broadcasted_iota(jnp.int32, (chunk, chunk), 1)).astype(F32)
            idx = lax.broadcasted_iota(jnp.int32, (chunk, 1), 0).astype(F32)
            tables[h] = (jnp.where(rel >= 0.0, jnp.exp(jnp.maximum(rel, 0.0) * lg), 0.0),
                         jnp.exp((idx + 1.0) * lg),
                         jnp.exp((chunk - 1.0 - idx) * lg))
        return task

    def normalize(c):
        xs[c] = x_ref[0, c * chunk:(c + 1) * chunk, :]
        xns[c] = _rms(xs[c], gn_ref[0]).astype(BF16)

    def in_piece(c, p):
        def task():
            u[c, p] = _dot(xns[c], win_ref[0, :, p * IN_PIECE_A:(p + 1) * IN_PIECE_A])
        return task

    def cols(c, lo, width):
        p, off = divmod(lo, IN_PIECE_A)
        return u[c, p][:, off:off + width]

    part = {}

    def retention_scores(c, h):
        def task():
            rows = slice(c * chunk, (c + 1) * chunk)
            cos_c, sin_c = tables["cos"][rows], tables["sin"][rows]

            def rotary(t):
                return t * cos_c + pltpu.roll(t, half, 1) * sin_c

            q = rotary(cols(c, h * RET_DK, RET_DK)).astype(BF16)
            k = rotary(cols(c, RET_QK_W + h * RET_DK, RET_DK)) * (RET_DK ** -0.5)
            part[c, h, "q"], part[c, h, "k"] = q, k
            part[c, h, "scores"] = _dot_nt(q, k.astype(BF16))
        return task

    def retention_values(c, h):
        def task():
            decay, cross_decay, state_decay = tables[h]
            q, k = part.pop((c, h, "q")), part.pop((c, h, "k"))
            v = cols(c, o_v + h * RET_DV, RET_DV).astype(BF16)
            intra = _dot((part.pop((c, h, "scores")) * decay).astype(BF16), v)
            cross = _dot(q, states[h].astype(BF16)) * cross_decay
            states[h] = (math.exp(chunk * RET_LOG_GAMMA[h]) * states[h]
                         + _dot((k * state_decay).T.astype(BF16), v))
            part[c, h, "ret"] = intra + cross
        return task

    def retention_gated(c, h):
        def task():
            ret = _rms(part.pop((c, h, "ret")), gret_ref[0])
            mixed[c, h] = (ret * _silu(cols(c, o_g + h * RET_DV, RET_DV))).astype(BF16)
        return task

    def memory_logits(c, h):
        def task():
            part[c, h, "logits"] = _mem_logits(h, cols(c, o_mq + h * MEM_DH, MEM_DH),
                                               kt_ref, gq_ref[0])
        return task

    def memory_values(c, h):
        def task():
            part[c, h, "values"] = _mem_values(h, part.pop((c, h, "logits")), mv_ref)
        return task

    def memory_gated(c, h):
        def task():
            mixed[c, RET_HEADS + h] = _mem_gated(part.pop((c, h, "values")),
                                                 cols(c, o_mq + MEM_W + h * MEM_DH, MEM_DH))
        return task

    def mixer_groups(c):
        groups = [[retention_scores(c, h) for h in range(RET_HEADS)],
                  [memory_logits(c, h) for h in range(MEM_HEADS)],
                  [retention_values(c, h) for h in range(RET_HEADS)],
                  [memory_values(c, h) for h in range(MEM_HEADS)],
                  [retention_gated(c, h) for h in range(RET_HEADS)]
                  + [memory_gated(c, h) for h in range(MEM_HEADS)]]

        def run(group):
            def task():
                for t in group:
                    t()
            return task
        return [run(g) for g in groups]

    def out_piece(c, j):
        def task():
            if c not in cats:
                cats[c] = jnp.concatenate(
                    [mixed.pop((c, s)) for s in range(RET_HEADS + MEM_HEADS)], axis=1)
            sl = slice(j * OUT_PIECE_A, (j + 1) * OUT_PIECE_A)
            o_ref[0, c * chunk:(c + 1) * chunk, sl] = xs[c][:, sl] + _dot(cats[c], wout_ref[0, :, sl])
        return task

    setup = [rotary_tables] + [decay_tables(h) for h in range(RET_HEADS)]
    for step in range(n_chunks + 2):
        stages = [setup] if step == 0 else []
        if step < n_chunks:
            normalize(step)
            stages.append([in_piece(step, p) for p in range(n_in)])
        if 0 <= step - 1 < n_chunks:
            stages.append(mixer_groups(step - 1))
        if 0 <= step - 2 < n_chunks:
            stages.append([out_piece(step - 2, j) for j in range(n_out)])
        for task in _interleaved(*stages):
            task()

    for h in range(RET_HEADS):
        state_ref[h] = states[h]


def _layer_a(x, pos_f32, invf, g_norm, w_in_bf16, g_ret_head, w_out_bf16, kt, mv, g_mem_q,
             la, layer):
    batch, seq, _ = x.shape
    tile = ROW_TILE_A
    n_a = g_norm.shape[0]
    depth = g_mem_q.shape[0]
    a_in_w = w_in_bf16.shape[2]
    a_out_in = w_out_bf16.shape[1]
    mem_len = mv.shape[2]
    return pl.pallas_call(
        _layer_a_kernel,
        grid=(batch, seq // tile),
        in_specs=[
            pl.BlockSpec((1, tile, D_MODEL), lambda b, t: (b, t, 0)),
            pl.BlockSpec((1, 1, tile), lambda b, t: (b, 0, t)),
            _const_spec((RET_DK // 2, 1), lambda b, t: (0, 0)),
            _const_spec((1, 1, D_MODEL), lambda b, t: (la, 0, 0)),
            _const_spec((1, D_MODEL, a_in_w), lambda b, t: (la, 0, 0)),
            _const_spec((1, 1, RET_DV), lambda b, t: (la, 0, 0)),
            _const_spec((1, a_out_in, D_MODEL), lambda b, t: (la, 0, 0)),
            pl.BlockSpec((1, 1, MEM_W, mem_len), lambda b, t: (layer, b, 0, 0)),
            pl.BlockSpec((1, 1, mem_len, MEM_W), lambda b, t: (layer, b, 0, 0)),
            _const_spec((1, 1, MEM_DH), lambda b, t: (layer, 0, 0)),
        ],
        out_specs=pl.BlockSpec((1, tile, D_MODEL), lambda b, t: (b, t, 0)),
        out_shape=jax.ShapeDtypeStruct(x.shape, F32),
        scratch_shapes=[pltpu.VMEM((RET_HEADS, RET_DK, RET_DV), F32)],
        compiler_params=pltpu.CompilerParams(
            dimension_semantics=("arbitrary", "arbitrary"),
            vmem_limit_bytes=VMEM_LIMIT_BYTES),
        name="layer_a",
    )(x, pos_f32, invf, g_norm.reshape(n_a, 1, D_MODEL), w_in_bf16,
      g_ret_head.reshape(n_a, 1, RET_DV), w_out_bf16, kt, mv, g_mem_q.reshape(depth, 1, MEM_DH))


def _proj_b_kernel(x_ref, gkv_ref, gnb_ref, wkv_ref, win_ref, kt_ref, mv_ref, gq_ref,
                   q_ref, k_ref, v_ref, gate_ref, mo_ref):
    x = x_ref[0]
    xr = x * lax.rsqrt(jnp.mean(x * x, axis=-1, keepdims=True) + EPS)
    xkv = (xr * gkv_ref[...]).astype(BF16)
    xb = (xr * gnb_ref[0]).astype(BF16)
    heads = range(MEM_HEADS)
    head = lambda t, h: t[:, h * MEM_DH:(h + 1) * MEM_DH]
    o_mq = 2 * SB_W
    memq = _dot(xb, win_ref[0, :, o_mq:o_mq + MEM_W])
    memgate = _dot(xb, win_ref[0, :, o_mq + MEM_W:])
    logits = [_mem_logits(h, head(memq, h), kt_ref, gq_ref[0]) for h in heads]
    gate_ref[0] = _silu(_dot(xb, win_ref[0, :, SB_W:o_mq])).astype(BF16)
    values = [_mem_values(h, logits[h], mv_ref) for h in heads]
    k_ref[0] = _dot(xkv, wkv_ref[:, :SB_W]).astype(BF16)
    mo_ref[0] = jnp.concatenate([_mem_gated(values[h], head(memgate, h)) for h in heads], axis=1)
    v_ref[0] = _dot(xkv, wkv_ref[:, SB_W:]).astype(BF16)
    q_ref[0] = _dot(xb, win_ref[0, :, :SB_W]).astype(BF16)


def _proj_b(x, g_kv, g_norm_b, w_kv_bf16, w_in_bf16, kt, mv, g_mem_q, lb, layer):
    batch, seq, _ = x.shape
    tile = ROW_TILE_B
    n_b = g_norm_b.shape[0]
    depth = g_mem_q.shape[0]
    mem_len = mv.shape[2]
    b_in_w = w_in_bf16.shape[2]
    row_spec = lambda w: pl.BlockSpec((1, tile, w), lambda b, t: (b, t, 0))
    widths = (SB_W, SB_W, SB_W, SB_W, MEM_W)
    return pl.pallas_call(
        _proj_b_kernel,
        grid=(batch, seq // tile),
        in_specs=[
            row_spec(D_MODEL),
            _const_spec((1, D_MODEL), lambda b, t: (0, 0)),
            _const_spec((1, 1, D_MODEL), lambda b, t: (lb, 0, 0)),
            _const_spec((D_MODEL, 2 * SB_W), lambda b, t: (0, 0)),
            _const_spec((1, D_MODEL, b_in_w), lambda b, t: (lb, 0, 0)),
            pl.BlockSpec((1, 1, MEM_W, mem_len), lambda b, t: (layer, b, 0, 0)),
            pl.BlockSpec((1, 1, mem_len, MEM_W), lambda b, t: (layer, b, 0, 0)),
            _const_spec((1, 1, MEM_DH), lambda b, t: (layer, 0, 0)),
        ],
        out_specs=[row_spec(w) for w in widths],
        out_shape=[jax.ShapeDtypeStruct((batch, seq, w), BF16) for w in widths],
        compiler_params=pltpu.CompilerParams(
            dimension_semantics=("arbitrary", "arbitrary"),
            vmem_limit_bytes=VMEM_LIMIT_BYTES),
        name="proj_b",
    )(x, g_kv.reshape(1, D_MODEL), g_norm_b.reshape(n_b, 1, D_MODEL), w_kv_bf16, w_in_bf16,
      kt, mv, g_mem_q.reshape(depth, 1, MEM_DH))


def _sb_out_kernel(x_ref, q_ref, kprev_ref, kcur_ref, vprev_ref, vcur_ref, k_hbm, v_hbm,
                   gate_ref, mo_ref, wout_ref, o_ref, cat_ref, kfar_ref, vfar_ref, far_sem, *,
                   n_steps):
    tile = SB_TILE
    tiles_per_step = q_ref.shape[1] // tile
    heads = q_ref.shape[2] // SB_DH
    batch_idx = pl.program_id(0)
    step = pl.program_id(1)
    scale = SB_DH ** -0.5

    row = lax.broadcasted_iota(jnp.int32, (tile, tile), 0)
    col = lax.broadcasted_iota(jnp.int32, (tile, tile), 1)
    causal = col < row
    suffix_mat = jnp.where(row > col, 1.0, 0.0).astype(BF16)


    def window_keys(prev_ref, cur_ref, far_ref, t, g, kind):
        hs = slice(g * SB_DH, (g + 1) * SB_DH)
        if kind == "far":
            return far_ref[:, hs]
        if kind == "diag":
            return cur_ref[0, t * tile:(t + 1) * tile, hs]
        if t == 0:
            return jnp.concatenate([prev_ref[0, :, hs], cur_ref[0, :tile, hs]], axis=0)
        return cur_ref[0, (t - 1) * tile:(t + 1) * tile, hs]

    def stage_scores(t, g, kind):
        nblk = 2 if kind == "near" else 1
        q = q_ref[0, t * tile:(t + 1) * tile, g * SB_DH:(g + 1) * SB_DH]
        zr = _dot_nt(q, window_keys(kprev_ref, kcur_ref, kfar_ref, t, g, kind))
        z = zr * (scale * LOG2E)
        sp = jnp.maximum(z, jnp.log(1.0 + jnp.exp2(jnp.minimum(z, SB_EXP2_CLAMP))) * LOG2E)
        blocks = [sp[:, j * tile:(j + 1) * tile] for j in range(nblk)]
        if kind != "far":
            blocks[-1] = jnp.where(causal, blocks[-1], 0.0)
        his, los = [], []
        for blk in blocks:
            hi = blk.astype(BF16)
            his.append(hi)
            los.append((blk - hi.astype(F32)).astype(BF16))
        return z, blocks, jnp.concatenate(his, axis=0), jnp.concatenate(los, axis=0)

    def stage_weights(scores, kind, penalty):
        z, blocks, hi, lo = scores
        nblk = len(blocks)
        suffix = _dot(hi, suffix_mat) + _dot(lo, suffix_mat)
        weights = [None] * nblk
        for j in reversed(range(nblk)):
            sfx = suffix[j * tile:(j + 1) * tile]
            log_a = (z[:, j * tile:(j + 1) * tile] - blocks[j]) - sfx
            if penalty is not None:
                log_a = log_a - penalty
            a = jnp.exp2(log_a)
            if kind != "far" and j == nblk - 1:
                a = jnp.where(causal, a, 0.0)
            weights[j] = a.astype(BF16)
            total = sfx[:, 0:1] + blocks[j][:, 0:1]
            penalty = total if penalty is None else penalty + total
        return jnp.concatenate(weights, axis=1), penalty

    def stage_values(weights, t, g, kind):
        return _dot(weights, window_keys(vprev_ref, vcur_ref, vfar_ref, t, g, kind))

    def run_windows(wins, penalties=None, other_tasks=()):
        n = len(wins)
        scores, weights, out = {}, {}, [None] * n

        def stagger(s):
            def task():
                if s < n:
                    scores[s] = stage_scores(*wins[s])
                w = s - 1
                if 0 <= w < n:
                    weights[w] = stage_weights(scores.pop(w), wins[w][2],
                                               None if penalties is None else penalties[w])
                w = s - 2
                if 0 <= w < n:
                    a, penalty = weights.pop(w)
                    out[w] = (stage_values(a, *wins[w]), penalty)
            return task

        steps = [stagger(s) for s in range(n + 2)]
        others = list(other_tasks)
        n_front = (len(others) + 1) // 2
        for task in steps[:1] + others[:n_front] + steps[1:-1] + others[n_front:] + steps[-1:]:
            task()
        return out

    cur = lax.rem(step, 2)
    n_out = o_ref.shape[2] // OUT_PIECE_B

    def out_piece(j):
        def task():
            sl = slice(j * OUT_PIECE_B, (j + 1) * OUT_PIECE_B)
            o_ref[0, :, sl] = x_ref[0, :, sl] + _dot(cat_ref[1 - cur], wout_ref[0, :, sl])
        return task

    out_tasks = [out_piece(j) for j in range(n_out)]

    def store(t, accs):
        rows = slice(t * tile, (t + 1) * tile)
        for g in range(heads):
            hs = slice(g * SB_DH, (g + 1) * SB_DH)
            cat_ref[cur, rows, hs] = (accs[g] * gate_ref[0, rows, hs].astype(F32)).astype(BF16)

    def log_weight_bound(pens):
        return -functools.reduce(jnp.minimum, [jnp.min(p) for p in pens])

    def fetch_far(start):
        copies = [pltpu.make_async_copy(src.at[batch_idx, pl.ds(start, tile), :], dst, far_sem.at[n])
                  for n, (src, dst) in enumerate(((k_hbm, kfar_ref), (v_hbm, vfar_ref)))]
        for c in copies:
            c.start()
        for c in copies:
            c.wait()

    def finish_tile(t, results):
        accs = [r[0] for r in results]
        penalties = [r[1] for r in results]
        store(t, accs)
        tile_idx = step * tiles_per_step + t
        bound = log_weight_bound(penalties)

        def cond(carry):
            return jnp.logical_and(carry[0] <= tile_idx, carry[1] > SB_LOG_WEIGHT_FLOOR)

        def body(carry):
            n, _, accs, pens = carry
            fetch_far(pl.multiple_of((tile_idx - n) * tile, tile))
            far = run_windows([(t, g, "far") for g in range(heads)], pens)
            accs = [accs[g] + far[g][0] for g in range(heads)]
            pens = [far[g][1] for g in range(heads)]
            return n + 1, log_weight_bound(pens), accs, pens

        @pl.when(jnp.logical_and(tile_idx >= 2, bound > SB_LOG_WEIGHT_FLOOR))
        def _():
            _, _, accs_far, _ = lax.while_loop(cond, body, (jnp.int32(2), bound, accs, penalties))
            store(t, accs_far)

    def attention(first_kind, other_tasks):
        wins = [(t, g, first_kind if t == 0 else "near")
                for t in range(tiles_per_step) for g in range(heads)]
        results = run_windows(wins, None, other_tasks)
        cat_ref[cur, :, SB_W:] = mo_ref[0]
        for t in range(tiles_per_step):
            finish_tile(t, results[t * heads:(t + 1) * heads])

    @pl.when(step == 0)
    def _():
        attention("diag", ())

    @pl.when(jnp.logical_and(step > 0, step < n_steps))
    def _():
        attention("near", out_tasks)

    @pl.when(step == n_steps)
    def _():
        for task in out_tasks:
            task()


def _sb_out(x, q, k, v, gate, mo, w_out_bf16, lb):
    batch, seq, _ = x.shape
    rows = ROW_TILE_SB
    n_steps = seq // rows
    tiles_per_step = rows // SB_TILE
    b_out_in = w_out_bf16.shape[1]
    attn_idx = lambda i: jnp.minimum(i, n_steps - 1)
    attn_spec = lambda w: pl.BlockSpec((1, rows, w), lambda b, i: (b, attn_idx(i), 0))
    proj_spec = lambda w: pl.BlockSpec((1, rows, w), lambda b, i: (b, jnp.maximum(i - 1, 0), 0))
    prev_spec = pl.BlockSpec(
        (1, SB_TILE, SB_W), lambda b, i: (b, jnp.maximum(attn_idx(i) * tiles_per_step - 1, 0), 0))
    any_spec = pl.BlockSpec(memory_space=pl.ANY)
    return pl.pallas_call(
        functools.partial(_sb_out_kernel, n_steps=n_steps),
        grid=(batch, n_steps + 1),
        in_specs=[
            proj_spec(D_MODEL), attn_spec(SB_W), prev_spec, attn_spec(SB_W), prev_spec,
            attn_spec(SB_W), any_spec, any_spec, attn_spec(SB_W), attn_spec(MEM_W),
            _const_spec((1, b_out_in, D_MODEL), lambda b, i: (lb, 0, 0)),
        ],
        out_specs=proj_spec(D_MODEL),
        out_shape=jax.ShapeDtypeStruct(x.shape, F32),
        scratch_shapes=[pltpu.VMEM((2, rows, b_out_in), BF16),
                        pltpu.VMEM((SB_TILE, SB_W), BF16),
                        pltpu.VMEM((SB_TILE, SB_W), BF16),
                        pltpu.SemaphoreType.DMA((2,))],
        compiler_params=pltpu.CompilerParams(
            dimension_semantics=("arbitrary", "arbitrary"),
            vmem_limit_bytes=VMEM_LIMIT_BYTES),
        name="sb_out",
    )(x, q, k, k, v, v, k, v, gate, mo, w_out_bf16)


def kernel(x, mem, positions, g_norm_a, w_in_a, g_ret_head, w_out_a, g_kv, w_kv, g_norm_b, w_in_b,
           w_out_b, g_mem, w_mem_kv, g_mem_q, g_mem_k):
    n_a = g_norm_a.shape[0]
    n_b = g_norm_b.shape[0]
    batch, seq, _ = x.shape

    kt, mv = _memory_kv(mem, g_mem, w_mem_kv.astype(BF16), g_mem_k)

    inv_freq = ROPE_BASE ** (-jnp.arange(0, RET_DK // 2, dtype=F32) * 2.0 / RET_DK)
    invf = inv_freq.reshape(RET_DK // 2, 1)
    pos_f32 = positions.astype(F32).reshape(batch, 1, seq)

    w_in_a_bf16 = w_in_a.astype(BF16)
    w_out_a_bf16 = w_out_a.astype(BF16)
    w_in_b_bf16 = w_in_b.astype(BF16)
    w_out_b_bf16 = w_out_b.astype(BF16)
    w_kv_bf16 = w_kv.astype(BF16)

    for la in range(n_a):
        x = _layer_a(x, pos_f32, invf, g_norm_a, w_in_a_bf16, g_ret_head, w_out_a_bf16,
                     kt, mv, g_mem_q, la, la)

    k_shared = v_shared = None
    for lb in range(n_b):
        layer = n_a + lb
        q, k_new, v_new, gate, mo = _proj_b(x, g_kv, g_norm_b, w_kv_bf16, w_in_b_bf16, kt, mv,
                                            g_mem_q, lb, layer)
        if lb == 0:
            k_shared, v_shared = k_new, v_new
        x = _sb_out(x, q, k_shared, v_shared, gate, mo, w_out_b_bf16, lb)
    return x
```

```python
import functools
import math

import numpy as np
import jax
import jax.numpy as jnp
from jax import lax
from jax.experimental import pallas as pl
from jax.experimental.pallas import tpu as pltpu

F32 = jnp.float32
BF16 = jnp.bfloat16

D_MODEL = 1024
RET_HEADS = 4
RET_DK = 128
RET_DV = 256
RET_QK_W = RET_HEADS * RET_DK
RET_V_W = RET_HEADS * RET_DV
ROPE_BASE = 10000.0
SB_HEADS = 8
SB_DH = 128
SB_W = SB_HEADS * SB_DH
MEM_HEADS = 4
MEM_DH = 128
MEM_W = MEM_HEADS * MEM_DH
EPS = 1e-6

RET_LOG_GAMMA = tuple(
    math.log(float(np.float32(1.0 - 2.0 ** (-5.0 - h)))) for h in range(RET_HEADS))

VMEM_LIMIT_BYTES = 48 * 1024 * 1024

ROW_TILE_A = 1024
RET_CHUNK = 256
IN_PIECE_A = 512
OUT_PIECE_A = 256
ROW_TILE_B = 512
OUT_PIECE_B = 256
SB_TILE = 256
ROW_TILE_SB = 512
SB_LOG_WEIGHT_FLOOR = -152.0
SB_EXP2_CLAMP = 64.0
LOG2E = math.log2(math.e)


def _rms(x, g):
    return x * lax.rsqrt(jnp.mean(x * x, axis=-1, keepdims=True) + EPS) * g


def _silu(x):
    return x * (1.0 / (1.0 + jnp.exp(-x)))


def _dot(a, b):
    return jnp.dot(a, b, preferred_element_type=F32)


def _dot_nt(a, b):
    return lax.dot_general(a, b, (((1,), (1,)), ((), ())), preferred_element_type=F32)


def _const_spec(shape, index_map):
    return pl.BlockSpec(shape, index_map, pipeline_mode=pl.Buffered(1))


def _interleaved(*task_lists):
    tagged = [((i + 0.5) / len(tasks), k, task)
              for k, tasks in enumerate(task_lists) for i, task in enumerate(tasks)]
    return [task for _, _, task in sorted(tagged, key=lambda e: e[:2])]


def _memkv_kernel(mem_ref, g_ref, w_ref, gk_ref, kt_ref, v_ref):
    mn = _rms(mem_ref[0], g_ref[0]).astype(BF16)
    kv = _dot(mn, w_ref[0])
    gk = gk_ref[0]
    for h in range(MEM_HEADS):
        kh = _rms(kv[:, h * MEM_DH:(h + 1) * MEM_DH], gk)
        kt_ref[0, 0, h * MEM_DH:(h + 1) * MEM_DH, :] = kh.T.astype(BF16)
    v_ref[0, 0] = kv[:, MEM_W:].astype(BF16)


def _memory_kv(mem, g_mem, w_mem_kv_bf16, g_mem_k):
    depth = g_mem.shape[0]
    batch, mem_len, _ = mem.shape
    return pl.pallas_call(
        _memkv_kernel,
        grid=(depth, batch),
        in_specs=[
            pl.BlockSpec((1, mem_len, D_MODEL), lambda l, b: (b, 0, 0)),
            pl.BlockSpec((1, 1, D_MODEL), lambda l, b: (l, 0, 0)),
            pl.BlockSpec((1, D_MODEL, 2 * MEM_W), lambda l, b: (l, 0, 0)),
            pl.BlockSpec((1, 1, MEM_DH), lambda l, b: (l, 0, 0)),
        ],
        out_specs=[
            pl.BlockSpec((1, 1, MEM_W, mem_len), lambda l, b: (l, b, 0, 0)),
            pl.BlockSpec((1, 1, mem_len, MEM_W), lambda l, b: (l, b, 0, 0)),
        ],
        out_shape=[
            jax.ShapeDtypeStruct((depth, batch, MEM_W, mem_len), BF16),
            jax.ShapeDtypeStruct((depth, batch, mem_len, MEM_W), BF16),
        ],
        compiler_params=pltpu.CompilerParams(
            dimension_semantics=("arbitrary", "arbitrary"),
            vmem_limit_bytes=VMEM_LIMIT_BYTES),
        name="memkv",
    )(mem, g_mem.reshape(depth, 1, D_MODEL), w_mem_kv_bf16, g_mem_k.reshape(depth, 1, MEM_DH))


def _mem_logits(h, memq, kt_ref, gq):
    qn = _rms(memq, gq).astype(BF16)
    return _dot(qn, kt_ref[0, 0, h * MEM_DH:(h + 1) * MEM_DH, :]) * (MEM_DH ** -0.5)


def _mem_values(h, logits, mv_ref):
    e = jnp.exp(logits - jnp.max(logits, axis=-1, keepdims=True))
    return _dot(e.astype(BF16), mv_ref[0, 0, :, h * MEM_DH:(h + 1) * MEM_DH]), \
        jnp.sum(e, axis=-1, keepdims=True)


def _mem_gated(values, memgate):
    o, denom = values
    return (o / denom * _silu(memgate)).astype(BF16)


def _layer_a_kernel(x_ref, pos_ref, invf_ref, gn_ref, win_ref, gret_ref, wout_ref,
                    kt_ref, mv_ref, gq_ref, o_ref, state_ref):
    tile = x_ref.shape[1]
    chunk = RET_CHUNK
    n_chunks = tile // chunk
    half = RET_DK // 2
    n_in = win_ref.shape[2] // IN_PIECE_A
    n_out = wout_ref.shape[2] // OUT_PIECE_A
    o_v = 2 * RET_QK_W
    o_g = o_v + RET_V_W
    o_mq = o_g + RET_V_W

    @pl.when(pl.program_id(1) == 0)
    def _():
        state_ref[...] = jnp.zeros_like(state_ref)

    tables = {}
    states = [state_ref[h] for h in range(RET_HEADS)]
    xs, xns = {}, {}
    u = {}
    mixed = {}
    cats = {}

    def rotary_tables():
        ang_t = invf_ref[...] * pos_ref[0]
        cos_sin = jnp.concatenate([jnp.cos(ang_t), jnp.sin(ang_t)], axis=0).T
        sin_cos = pltpu.roll(cos_sin, half, 1)
        low_lanes = lax.broadcasted_iota(jnp.int32, (tile, RET_DK), 1) < half
        tables["cos"] = jnp.where(low_lanes, cos_sin, sin_cos)
        tables["sin"] = jnp.where(low_lanes, -sin_cos, cos_sin)

    def decay_tables(h):
        def task():
            lg = RET_LOG_GAMMA[h]
            rel = (lax.broadcasted_iota(jnp.int32, (chunk, chunk), 0)
                   - lax.broadcasted_iota(jnp.int32, (chunk, chunk), 1)).astype(F32)
            idx = lax.broadcasted_iota(jnp.int32, (chunk, 1), 0).astype(F32)
            tables[h] = (jnp.where(rel >= 0.0, jnp.exp(jnp.maximum(rel, 0.0) * lg), 0.0),
                         jnp.exp((idx + 1.0) * lg),
                         jnp.exp((chunk - 1.0 - idx) * lg))
        return task

    def normalize(c):
        xs[c] = x_ref[0, c * chunk:(c + 1) * chunk, :]
        xns[c] = _rms(xs[c], gn_ref[0]).astype(BF16)

    def in_piece(c, p):
        def task():
            u[c, p] = _dot(xns[c], win_ref[0, :, p * IN_PIECE_A:(p + 1) * IN_PIECE_A])
        return task

    def cols(c, lo, width):
        p, off = divmod(lo, IN_PIECE_A)
        return u[c, p][:, off:off + width]

    part = {}

    def retention_scores(c, h):
        def task():
            rows = slice(c * chunk, (c + 1) * chunk)
            cos_c, sin_c = tables["cos"][rows], tables["sin"][rows]

            def rotary(t):
                return t * cos_c + pltpu.roll(t, half, 1) * sin_c

            q = rotary(cols(c, h * RET_DK, RET_DK)).astype(BF16)
            k = rotary(cols(c, RET_QK_W + h * RET_DK, RET_DK)) * (RET_DK ** -0.5)
            part[c, h, "q"], part[c, h, "k"] = q, k
            part[c, h, "scores"] = _dot_nt(q, k.astype(BF16))
        return task

    def retention_values(c, h):
        def task():
            decay, cross_decay, state_decay = tables[h]
            q, k = part.pop((c, h, "q")), part.pop((c, h, "k"))
            v = cols(c, o_v + h * RET_DV, RET_DV).astype(BF16)
            intra = _dot((part.pop((c, h, "scores")) * decay).astype(BF16), v)
            cross = _dot(q, states[h].astype(BF16)) * cross_decay
            states[h] = (math.exp(chunk * RET_LOG_GAMMA[h]) * states[h]
                         + _dot((k * state_decay).T.astype(BF16), v))
            part[c, h, "ret"] = intra + cross
        return task

    def retention_gated(c, h):
        def task():
            ret = _rms(part.pop((c, h, "ret")), gret_ref[0])
            mixed[c, h] = (ret * _silu(cols(c, o_g + h * RET_DV, RET_DV))).astype(BF16)
        return task

    def memory_logits(c, h):
        def task():
            part[c, h, "logits"] = _mem_logits(h, cols(c, o_mq + h * MEM_DH, MEM_DH),
                                               kt_ref, gq_ref[0])
        return task

    def memory_values(c, h):
        def task():
            part[c, h, "values"] = _mem_values(h, part.pop((c, h, "logits")), mv_ref)
        return task

    def memory_gated(c, h):
        def task():
            mixed[c, RET_HEADS + h] = _mem_gated(part.pop((c, h, "values")),
                                                 cols(c, o_mq + MEM_W + h * MEM_DH, MEM_DH))
        return task

    def mixer_groups(c):
        groups = [[retention_scores(c, h) for h in range(RET_HEADS)],
                  [memory_logits(c, h) for h in range(MEM_HEADS)],
                  [retention_values(c, h) for h in range(RET_HEADS)],
                  [memory_values(c, h) for h in range(MEM_HEADS)],
                  [retention_gated(c, h) for h in range(RET_HEADS)]
                  + [memory_gated(c, h) for h in range(MEM_HEADS)]]

        def run(group):
            def task():
                for t in group:
                    t()
            return task
        return [run(g) for g in groups]

    def out_piece(c, j):
        def task():
            if c not in cats:
                cats[c] = jnp.concatenate(
                    [mixed.pop((c, s)) for s in range(RET_HEADS + MEM_HEADS)], axis=1)
            sl = slice(j * OUT_PIECE_A, (j + 1) * OUT_PIECE_A)
            o_ref[0, c * chunk:(c + 1) * chunk, sl] = xs[c][:, sl] + _dot(cats[c], wout_ref[0, :, sl])
        return task

    setup = [rotary_tables] + [decay_tables(h) for h in range(RET_HEADS)]
    for step in range(n_chunks + 2):
        stages = [setup] if step == 0 else []
        if step < n_chunks:
            normalize(step)
            stages.append([in_piece(step, p) for p in range(n_in)])
        if 0 <= step - 1 < n_chunks:
            stages.append(mixer_groups(step - 1))
        if 0 <= step - 2 < n_chunks:
            stages.append([out_piece(step - 2, j) for j in range(n_out)])
        for task in _interleaved(*stages):
            task()

    for h in range(RET_HEADS):
        state_ref[h] = states[h]


def _layer_a(x, pos_f32, invf, g_norm, w_in_bf16, g_ret_head, w_out_bf16, kt, mv, g_mem_q,
             la, layer):
    batch, seq, _ = x.shape
    tile = ROW_TILE_A
    n_a = g_norm.shape[0]
    depth = g_mem_q.shape[0]
    a_in_w = w_in_bf16.shape[2]
    a_out_in = w_out_bf16.shape[1]
    mem_len = mv.shape[2]
    return pl.pallas_call(
        _layer_a_kernel,
        grid=(batch, seq // tile),
        in_specs=[
            pl.BlockSpec((1, tile, D_MODEL), lambda b, t: (b, t, 0)),
            pl.BlockSpec((1, 1, tile), lambda b, t: (b, 0, t)),
            _const_spec((RET_DK // 2, 1), lambda b, t: (0, 0)),
            _const_spec((1, 1, D_MODEL), lambda b, t: (la, 0, 0)),
            _const_spec((1, D_MODEL, a_in_w), lambda b, t: (la, 0, 0)),
            _const_spec((1, 1, RET_DV), lambda b, t: (la, 0, 0)),
            _const_spec((1, a_out_in, D_MODEL), lambda b, t: (la, 0, 0)),
            pl.BlockSpec((1, 1, MEM_W, mem_len), lambda b, t: (layer, b, 0, 0)),
            pl.BlockSpec((1, 1, mem_len, MEM_W), lambda b, t: (layer, b, 0, 0)),
            _const_spec((1, 1, MEM_DH), lambda b, t: (layer, 0, 0)),
        ],
        out_specs=pl.BlockSpec((1, tile, D_MODEL), lambda b, t: (b, t, 0)),
        out_shape=jax.ShapeDtypeStruct(x.shape, F32),
        scratch_shapes=[pltpu.VMEM((RET_HEADS, RET_DK, RET_DV), F32)],
        compiler_params=pltpu.CompilerParams(
            dimension_semantics=("arbitrary", "arbitrary"),
            vmem_limit_bytes=VMEM_LIMIT_BYTES),
        name="layer_a",
    )(x, pos_f32, invf, g_norm.reshape(n_a, 1, D_MODEL), w_in_bf16,
      g_ret_head.reshape(n_a, 1, RET_DV), w_out_bf16, kt, mv, g_mem_q.reshape(depth, 1, MEM_DH))


def _proj_b_kernel(x_ref, gkv_ref, gnb_ref, wkv_ref, win_ref, kt_ref, mv_ref, gq_ref,
                   q_ref, k_ref, v_ref, gate_ref, mo_ref):
    x = x_ref[0]
    xr = x * lax.rsqrt(jnp.mean(x * x, axis=-1, keepdims=True) + EPS)
    xkv = (xr * gkv_ref[...]).astype(BF16)
    xb = (xr * gnb_ref[0]).astype(BF16)
    heads = range(MEM_HEADS)
    head = lambda t, h: t[:, h * MEM_DH:(h + 1) * MEM_DH]
    o_mq = 2 * SB_W
    memq = _dot(xb, win_ref[0, :, o_mq:o_mq + MEM_W])
    memgate = _dot(xb, win_ref[0, :, o_mq + MEM_W:])
    logits = [_mem_logits(h, head(memq, h), kt_ref, gq_ref[0]) for h in heads]
    gate_ref[0] = _silu(_dot(xb, win_ref[0, :, SB_W:o_mq])).astype(BF16)
    values = [_mem_values(h, logits[h], mv_ref) for h in heads]
    k_ref[0] = _dot(xkv, wkv_ref[:, :SB_W]).astype(BF16)
    mo_ref[0] = jnp.concatenate([_mem_gated(values[h], head(memgate, h)) for h in heads], axis=1)
    v_ref[0] = _dot(xkv, wkv_ref[:, SB_W:]).astype(BF16)
    q_ref[0] = _dot(xb, win_ref[0, :, :SB_W]).astype(BF16)


def _proj_b(x, g_kv, g_norm_b, w_kv_bf16, w_in_bf16, kt, mv, g_mem_q, lb, layer):
    batch, seq, _ = x.shape
    tile = ROW_TILE_B
    n_b = g_norm_b.shape[0]
    depth = g_mem_q.shape[0]
    mem_len = mv.shape[2]
    b_in_w = w_in_bf16.shape[2]
    row_spec = lambda w: pl.BlockSpec((1, tile, w), lambda b, t: (b, t, 0))
    widths = (SB_W, SB_W, SB_W, SB_W, MEM_W)
    return pl.pallas_call(
        _proj_b_kernel,
        grid=(batch, seq // tile),
        in_specs=[
            row_spec(D_MODEL),
            _const_spec((1, D_MODEL), lambda b, t: (0, 0)),
            _const_spec((1, 1, D_MODEL), lambda b, t: (lb, 0, 0)),
            _const_spec((D_MODEL, 2 * SB_W), lambda b, t: (0, 0)),
            _const_spec((1, D_MODEL, b_in_w), lambda b, t: (lb, 0, 0)),
            pl.BlockSpec((1, 1, MEM_W, mem_len), lambda b, t: (layer, b, 0, 0)),
            pl.BlockSpec((1, 1, mem_len, MEM_W), lambda b, t: (layer, b, 0, 0)),
            _const_spec((1, 1, MEM_DH), lambda b, t: (layer, 0, 0)),
        ],
        out_specs=[row_spec(w) for w in widths],
        out_shape=[jax.ShapeDtypeStruct((batch, seq, w), BF16) for w in widths],
        compiler_params=pltpu.CompilerParams(
            dimension_semantics=("arbitrary", "arbitrary"),
            vmem_limit_bytes=VMEM_LIMIT_BYTES),
        name="proj_b",
    )(x, g_kv.reshape(1, D_MODEL), g_norm_b.reshape(n_b, 1, D_MODEL), w_kv_bf16, w_in_bf16,
      kt, mv, g_mem_q.reshape(depth, 1, MEM_DH))


def _sb_out_kernel(x_ref, q_ref, kprev_ref, kcur_ref, vprev_ref, vcur_ref, k_hbm, v_hbm,
                   gate_ref, mo_ref, wout_ref, o_ref, cat_ref, kfar_ref, vfar_ref, far_sem, *,
                   n_steps):
    tile = SB_TILE
    tiles_per_step = q_ref.shape[1] // tile
    heads = q_ref.shape[2] // SB_DH
    batch_idx = pl.program_id(0)
    step = pl.program_id(1)
    scale = SB_DH ** -0.5

    row = lax.broadcasted_iota(jnp.int32, (tile, tile), 0)
    col = lax.broadcasted_iota(jnp.int32, (tile, tile), 1)
    causal = col < row
    suffix_mat = jnp.where(row > col, 1.0, 0.0).astype(BF16)


    def window_keys(prev_ref, cur_ref, far_ref, t, g, kind):
        hs = slice(g * SB_DH, (g + 1) * SB_DH)
        if kind == "far":
            return far_ref[:, hs]
        if kind == "diag":
            return cur_ref[0, t * tile:(t + 1) * tile, hs]
        if t == 0:
            return jnp.concatenate([prev_ref[0, :, hs], cur_ref[0, :tile, hs]], axis=0)
        return cur_ref[0, (t - 1) * tile:(t + 1) * tile, hs]

    def stage_scores(t, g, kind):
        nblk = 2 if kind == "near" else 1
        q = q_ref[0, t * tile:(t + 1) * tile, g * SB_DH:(g + 1) * SB_DH]
        zr = _dot_nt(q, window_keys(kprev_ref, kcur_ref, kfar_ref, t, g, kind))
        z = zr * (scale * LOG2E)
        sp = jnp.maximum(z, jnp.log(1.0 + jnp.exp2(jnp.minimum(z, SB_EXP2_CLAMP))) * LOG2E)
        blocks = [sp[:, j * tile:(j + 1) * tile] for j in range(nblk)]
        if kind != "far":
            blocks[-1] = jnp.where(causal, blocks[-1], 0.0)
        return z, blocks, jnp.concatenate([blk.astype(BF16) for blk in blocks], axis=0)

    def stage_weights(scores, kind, penalty):
        z, blocks, sp_bf16 = scores
        nblk = len(blocks)
        suffix = _dot(sp_bf16, suffix_mat)
        weights = [None] * nblk
        for j in reversed(range(nblk)):
            sfx = suffix[j * tile:(j + 1) * tile]
            log_a = (z[:, j * tile:(j + 1) * tile] - blocks[j]) - sfx
            if penalty is not None:
                log_a = log_a - penalty
            a = jnp.exp2(log_a)
            if kind != "far" and j == nblk - 1:
                a = jnp.where(causal, a, 0.0)
            weights[j] = a.astype(BF16)
            total = sfx[:, 0:1] + blocks[j][:, 0:1]
            penalty = total if penalty is None else penalty + total
        return jnp.concatenate(weights, axis=1), penalty

    def stage_values(weights, t, g, kind):
        return _dot(weights, window_keys(vprev_ref, vcur_ref, vfar_ref, t, g, kind))

    def run_windows(wins, penalties=None, other_tasks=()):
        n = len(wins)
        scores, weights, out = {}, {}, [None] * n

        def stagger(s):
            def task():
                if s < n:
                    scores[s] = stage_scores(*wins[s])
                w = s - 1
                if 0 <= w < n:
                    weights[w] = stage_weights(scores.pop(w), wins[w][2],
                                               None if penalties is None else penalties[w])
                w = s - 2
                if 0 <= w < n:
                    a, penalty = weights.pop(w)
                    out[w] = (stage_values(a, *wins[w]), penalty)
            return task

        steps = [stagger(s) for s in range(n + 2)]
        others = list(other_tasks)
        n_front = (len(others) + 1) // 2
        for task in steps[:1] + others[:n_front] + steps[1:-1] + others[n_front:] + steps[-1:]:
            task()
        return out

    cur = lax.rem(step, 2)
    n_out = o_ref.shape[2] // OUT_PIECE_B

    def out_piece(j):
        def task():
            sl = slice(j * OUT_PIECE_B, (j + 1) * OUT_PIECE_B)
            o_ref[0, :, sl] = x_ref[0, :, sl] + _dot(cat_ref[1 - cur], wout_ref[0, :, sl])
        return task

    out_tasks = [out_piece(j) for j in range(n_out)]

    def store(t, accs):
        rows = slice(t * tile, (t + 1) * tile)
        for g in range(heads):
            hs = slice(g * SB_DH, (g + 1) * SB_DH)
            cat_ref[cur, rows, hs] = (accs[g] * gate_ref[0, rows, hs].astype(F32)).astype(BF16)

    def log_weight_bound(pens):
        return -functools.reduce(jnp.minimum, [jnp.min(p) for p in pens])

    def fetch_far(start):
        copies = [pltpu.make_async_copy(src.at[batch_idx, pl.ds(start, tile), :], dst, far_sem.at[n])
                  for n, (src, dst) in enumerate(((k_hbm, kfar_ref), (v_hbm, vfar_ref)))]
        for c in copies:
            c.start()
        for c in copies:
            c.wait()

    def finish_tile(t, results):
        accs = [r[0] for r in results]
        penalties = [r[1] for r in results]
        store(t, accs)
        tile_idx = step * tiles_per_step + t
        bound = log_weight_bound(penalties)

        def cond(carry):
            return jnp.logical_and(carry[0] <= tile_idx, carry[1] > SB_LOG_WEIGHT_FLOOR)

        def body(carry):
            n, _, accs, pens = carry
            fetch_far(pl.multiple_of((tile_idx - n) * tile, tile))
            far = run_windows([(t, g, "far") for g in range(heads)], pens)
            accs = [accs[g] + far[g][0] for g in range(heads)]
            pens = [far[g][1] for g in range(heads)]
            return n + 1, log_weight_bound(pens), accs, pens

        @pl.when(jnp.logical_and(tile_idx >= 2, bound > SB_LOG_WEIGHT_FLOOR))
        def _():
            _, _, accs_far, _ = lax.while_loop(cond, body, (jnp.int32(2), bound, accs, penalties))
            store(t, accs_far)

    def attention(first_kind, other_tasks):
        wins = [(t, g, first_kind if t == 0 else "near")
                for t in range(tiles_per_step) for g in range(heads)]
        results = run_windows(wins, None, other_tasks)
        cat_ref[cur, :, SB_W:] = mo_ref[0]
        for t in range(tiles_per_step):
            finish_tile(t, results[t * heads:(t + 1) * heads])

    @pl.when(step == 0)
    def _():
        attention("diag", ())

    @pl.when(jnp.logical_and(step > 0, step < n_steps))
    def _():
        attention("near", out_tasks)

    @pl.when(step == n_steps)
    def _():
        for task in out_tasks:
            task()


def _sb_out(x, q, k, v, gate, mo, w_out_bf16, lb):
    batch, seq, _ = x.shape
    rows = ROW_TILE_SB
    n_steps = seq // rows
    tiles_per_step = rows // SB_TILE
    b_out_in = w_out_bf16.shape[1]
    attn_idx = lambda i: jnp.minimum(i, n_steps - 1)
    attn_spec = lambda w: pl.BlockSpec((1, rows, w), lambda b, i: (b, attn_idx(i), 0))
    proj_spec = lambda w: pl.BlockSpec((1, rows, w), lambda b, i: (b, jnp.maximum(i - 1, 0), 0))
    prev_spec = pl.BlockSpec(
        (1, SB_TILE, SB_W), lambda b, i: (b, jnp.maximum(attn_idx(i) * tiles_per_step - 1, 0), 0))
    any_spec = pl.BlockSpec(memory_space=pl.ANY)
    return pl.pallas_call(
        functools.partial(_sb_out_kernel, n_steps=n_steps),
        grid=(batch, n_steps + 1),
        in_specs=[
            proj_spec(D_MODEL), attn_spec(SB_W), prev_spec, attn_spec(SB_W), prev_spec,
            attn_spec(SB_W), any_spec, any_spec, attn_spec(SB_W), attn_spec(MEM_W),
            _const_spec((1, b_out_in, D_MODEL), lambda b, i: (lb, 0, 0)),
        ],
        out_specs=proj_spec(D_MODEL),
        out_shape=jax.ShapeDtypeStruct(x.shape, F32),
        scratch_shapes=[pltpu.VMEM((2, rows, b_out_in), BF16),
                        pltpu.VMEM((SB_TILE, SB_W), BF16),
                        pltpu.VMEM((SB_TILE, SB_W), BF16),
                        pltpu.SemaphoreType.DMA((2,))],
        compiler_params=pltpu.CompilerParams(
            dimension_semantics=("arbitrary", "arbitrary"),
            vmem_limit_bytes=VMEM_LIMIT_BYTES),
        name="sb_out",
    )(x, q, k, k, v, v, k, v, gate, mo, w_out_bf16)


def kernel(x, mem, positions, g_norm_a, w_in_a, g_ret_head, w_out_a, g_kv, w_kv, g_norm_b, w_in_b,
           w_out_b, g_mem, w_mem_kv, g_mem_q, g_mem_k):
    n_a = g_norm_a.shape[0]
    n_b = g_norm_b.shape[0]
    batch, seq, _ = x.shape

    kt, mv = _memory_kv(mem, g_mem, w_mem_kv.astype(BF16), g_mem_k)

    inv_freq = ROPE_BASE ** (-jnp.arange(0, RET_DK // 2, dtype=F32) * 2.0 / RET_DK)
    invf = inv_freq.reshape(RET_DK // 2, 1)
    pos_f32 = positions.astype(F32).reshape(batch, 1, seq)

    w_in_a_bf16 = w_in_a.astype(BF16)
    w_out_a_bf16 = w_out_a.astype(BF16)
    w_in_b_bf16 = w_in_b.astype(BF16)
    w_out_b_bf16 = w_out_b.astype(BF16)
    w_kv_bf16 = w_kv.astype(BF16)

    for la in range(n_a):
        x = _layer_a(x, pos_f32, invf, g_norm_a, w_in_a_bf16, g_ret_head, w_out_a_bf16,
                     kt, mv, g_mem_q, la, la)

    k_shared = v_shared = None
    for lb in range(n_b):
        layer = n_a + lb
        q, k_new, v_new, gate, mo = _proj_b(x, g_kv, g_norm_b, w_kv_bf16, w_in_b_bf16, kt, mv,
                                            g_mem_q, lb, layer)
        if lb == 0:
            k_shared, v_shared = k_new, v_new
        x = _sb_out(x, q, k_shared, v_shared, gate, mo, w_out_b_bf16, lb)
    return x
```

```python
import functools
import math

import numpy as np
import jax
import jax.numpy as jnp
from jax import lax
from jax.experimental import pallas as pl
from jax.experimental.pallas import tpu as pltpu

F32 = jnp.float32
BF16 = jnp.bfloat16

D_MODEL = 1024
RET_HEADS = 4
RET_DK = 128
RET_DV = 256
RET_QK_W = RET_HEADS * RET_DK
RET_V_W = RET_HEADS * RET_DV
ROPE_BASE = 10000.0
SB_HEADS = 8
SB_DH = 128
SB_W = SB_HEADS * SB_DH
MEM_HEADS = 4
MEM_DH = 128
MEM_W = MEM_HEADS * MEM_DH
EPS = 1e-6

RET_LOG_GAMMA = tuple(
    math.log(float(np.float32(1.0 - 2.0 ** (-5.0 - h)))) for h in range(RET_HEADS))

VMEM_LIMIT_BYTES = 48 * 1024 * 1024
WEIGHT_STAGE_BYTES = 1024 * 1024

ROW_TILE_A = 1024
RET_CHUNK = 256
IN_PIECE_A = 512
OUT_PIECE_A = 256
ROW_TILE_B = 512
OUT_PIECE_B = 256
SB_TILE = 256
ROW_TILE_SB = 512
SB_LOG_WEIGHT_FLOOR = -152.0
SB_EXP2_CLAMP = 64.0
LOG2E = math.log2(math.e)


def _rms(x, g):
    return x * lax.rsqrt(jnp.mean(x * x, axis=-1, keepdims=True) + EPS) * g


def _silu(x):
    return x * (1.0 / (1.0 + jnp.exp(-x)))


def _dot(a, b):
    return jnp.dot(a, b, preferred_element_type=F32)


def _dot_nt(a, b):
    return lax.dot_general(a, b, (((1,), (1,)), ((), ())), preferred_element_type=F32)


def _const_spec(shape, index_map):
    return pl.BlockSpec(shape, index_map, pipeline_mode=pl.Buffered(1))


def _stage_weight(w_hbm, layer, dst_ref, stage_ref, sems):
    rows = stage_ref.shape[1]
    n_chunks = dst_ref.shape[0] // rows

    def copy(c):
        return pltpu.make_async_copy(w_hbm.at[layer, pl.ds(c * rows, rows), :],
                                     stage_ref.at[c % 2], sems.at[c % 2])

    copy(0).start()
    for c in range(n_chunks):
        if c + 1 < n_chunks:
            copy(c + 1).start()
        copy(c).wait()
        dst_ref[c * rows:(c + 1) * rows, :] = stage_ref[c % 2].astype(BF16)


def _weight_scratch(shape):
    rows = 1 << int(math.log2(WEIGHT_STAGE_BYTES // (shape[1] * 4)))
    assert shape[0] % rows == 0
    return [pltpu.VMEM(shape, BF16), pltpu.VMEM((2, rows, shape[1]), F32),
            pltpu.SemaphoreType.DMA((2,))]


def _first_grid_step():
    return jnp.logical_and(pl.program_id(0) == 0, pl.program_id(1) == 0)


HBM_SPEC = pl.BlockSpec(memory_space=pl.ANY)


def _interleaved(*task_lists):
    tagged = [((i + 0.5) / len(tasks), k, task)
              for k, tasks in enumerate(task_lists) for i, task in enumerate(tasks)]
    return [task for _, _, task in sorted(tagged, key=lambda e: e[:2])]


def _memkv_kernel(mem_ref, g_ref, w_ref, gk_ref, kt_ref, v_ref):
    mn = _rms(mem_ref[0], g_ref[0]).astype(BF16)
    kv = _dot(mn, w_ref[0])
    gk = gk_ref[0]
    for h in range(MEM_HEADS):
        kh = _rms(kv[:, h * MEM_DH:(h + 1) * MEM_DH], gk)
        kt_ref[0, 0, h * MEM_DH:(h + 1) * MEM_DH, :] = kh.T.astype(BF16)
    v_ref[0, 0] = kv[:, MEM_W:].astype(BF16)


def _memory_kv(mem, g_mem, w_mem_kv_bf16, g_mem_k):
    depth = g_mem.shape[0]
    batch, mem_len, _ = mem.shape
    return pl.pallas_call(
        _memkv_kernel,
        grid=(depth, batch),
        in_specs=[
            pl.BlockSpec((1, mem_len, D_MODEL), lambda l, b: (b, 0, 0)),
            pl.BlockSpec((1, 1, D_MODEL), lambda l, b: (l, 0, 0)),
            pl.BlockSpec((1, D_MODEL, 2 * MEM_W), lambda l, b: (l, 0, 0)),
            pl.BlockSpec((1, 1, MEM_DH), lambda l, b: (l, 0, 0)),
        ],
        out_specs=[
            pl.BlockSpec((1, 1, MEM_W, mem_len), lambda l, b: (l, b, 0, 0)),
            pl.BlockSpec((1, 1, mem_len, MEM_W), lambda l, b: (l, b, 0, 0)),
        ],
        out_shape=[
            jax.ShapeDtypeStruct((depth, batch, MEM_W, mem_len), BF16),
            jax.ShapeDtypeStruct((depth, batch, mem_len, MEM_W), BF16),
        ],
        compiler_params=pltpu.CompilerParams(
            dimension_semantics=("arbitrary", "arbitrary"),
            vmem_limit_bytes=VMEM_LIMIT_BYTES),
        name="memkv",
    )(mem, g_mem.reshape(depth, 1, D_MODEL), w_mem_kv_bf16, g_mem_k.reshape(depth, 1, MEM_DH))


def _mem_logits(h, memq, kt_ref, gq):
    qn = _rms(memq, gq).astype(BF16)
    return _dot(qn, kt_ref[0, 0, h * MEM_DH:(h + 1) * MEM_DH, :]) * (MEM_DH ** -0.5)


def _mem_values(h, logits, mv_ref):
    e = jnp.exp(logits - jnp.max(logits, axis=-1, keepdims=True))
    return _dot(e.astype(BF16), mv_ref[0, 0, :, h * MEM_DH:(h + 1) * MEM_DH]), \
        jnp.sum(e, axis=-1, keepdims=True)


def _mem_gated(values, memgate):
    o, denom = values
    return (o / denom * _silu(memgate)).astype(BF16)


def _layer_a_kernel(x_ref, pos_ref, invf_ref, gn_ref, win_hbm, gret_ref, wout_hbm,
                    kt_ref, mv_ref, gq_ref, o_ref, state_ref,
                    win_ref, win_stage, win_sems, wout_ref, wout_stage, wout_sems, *, la):
    tile = x_ref.shape[1]
    chunk = RET_CHUNK
    n_chunks = tile // chunk
    half = RET_DK // 2
    n_in = win_ref.shape[1] // IN_PIECE_A
    n_out = wout_ref.shape[1] // OUT_PIECE_A
    o_v = 2 * RET_QK_W
    o_g = o_v + RET_V_W
    o_mq = o_g + RET_V_W

    @pl.when(_first_grid_step())
    def _():
        _stage_weight(win_hbm, la, win_ref, win_stage, win_sems)
        _stage_weight(wout_hbm, la, wout_ref, wout_stage, wout_sems)

    @pl.when(pl.program_id(1) == 0)
    def _():
        state_ref[...] = jnp.zeros_like(state_ref)

    tables = {}
    states = [state_ref[h] for h in range(RET_HEADS)]
    xs, xns = {}, {}
    u = {}
    mixed = {}
    cats = {}

    def rotary_tables():
        ang_t = invf_ref[...] * pos_ref[0]
        cos_sin = jnp.concatenate([jnp.cos(ang_t), jnp.sin(ang_t)], axis=0).T
        sin_cos = pltpu.roll(cos_sin, half, 1)
        low_lanes = lax.broadcasted_iota(jnp.int32, (tile, RET_DK), 1) < half
        tables["cos"] = jnp.where(low_lanes, cos_sin, sin_cos)
        tables["sin"] = jnp.where(low_lanes, -sin_cos, cos_sin)

    def decay_tables(h):
        def task():
            lg = RET_LOG_GAMMA[h]
            rel = (lax.broadcasted_iota(jnp.int32, (chunk, chunk), 0)
                   - lax.broadcasted_iota(jnp.int32, (chunk, chunk), 1)).astype(F32)
            idx = lax.broadcasted_iota(jnp.int32, (chunk, 1), 0).astype(F32)
            tables[h] = (jnp.where(rel >= 0.0, jnp.exp(jnp.maximum(rel, 0.0) * lg), 0.0),
                         jnp.exp((idx + 1.0) * lg),
                         jnp.exp((chunk - 1.0 - idx) * lg))
        return task

    def normalize(c):
        xs[c] = x_ref[0, c * chunk:(c + 1) * chunk, :]
        xns[c] = _rms(xs[c], gn_ref[0]).astype(BF16)

    def in_piece(c, p):
        def task():
            u[c, p] = _dot(xns[c], win_ref[:, p * IN_PIECE_A:(p + 1) * IN_PIECE_A])
        return task

    def cols(c, lo, width):
        p, off = divmod(lo, IN_PIECE_A)
        return u[c, p][:, off:off + width]

    part = {}

    def retention_scores(c, h):
        def task():
            rows = slice(c * chunk, (c + 1) * chunk)
            cos_c, sin_c = tables["cos"][rows], tables["sin"][rows]

            def rotary(t):
                return t * cos_c + pltpu.roll(t, half, 1) * sin_c

            q = rotary(cols(c, h * RET_DK, RET_DK)).astype(BF16)
            k = rotary(cols(c, RET_QK_W + h * RET_DK, RET_DK)) * (RET_DK ** -0.5)
            part[c, h, "q"], part[c, h, "k"] = q, k
            part[c, h, "scores"] = _dot_nt(q, k.astype(BF16))
        return task

    def retention_values(c, h):
        def task():
            decay, cross_decay, state_decay = tables[h]
            q, k = part.pop((c, h, "q")), part.pop((c, h, "k"))
            v = cols(c, o_v + h * RET_DV, RET_DV).astype(BF16)
            intra = _dot((part.pop((c, h, "scores")) * decay).astype(BF16), v)
            cross = _dot(q, states[h].astype(BF16)) * cross_decay
            states[h] = (math.exp(chunk * RET_LOG_GAMMA[h]) * states[h]
                         + _dot((k * state_decay).T.astype(BF16), v))
            part[c, h, "ret"] = intra + cross
        return task

    def retention_gated(c, h):
        def task():
            ret = _rms(part.pop((c, h, "ret")), gret_ref[0])
            mixed[c, h] = (ret * _silu(cols(c, o_g + h * RET_DV, RET_DV))).astype(BF16)
        return task

    def memory_logits(c, h):
        def task():
            part[c, h, "logits"] = _mem_logits(h, cols(c, o_mq + h * MEM_DH, MEM_DH),
                                               kt_ref, gq_ref[0])
        return task

    def memory_values(c, h):
        def task():
            part[c, h, "values"] = _mem_values(h, part.pop((c, h, "logits")), mv_ref)
        return task

    def memory_gated(c, h):
        def task():
            mixed[c, RET_HEADS + h] = _mem_gated(part.pop((c, h, "values")),
                                                 cols(c, o_mq + MEM_W + h * MEM_DH, MEM_DH))
        return task

    def mixer_groups(c):
        groups = [[retention_scores(c, h) for h in range(RET_HEADS)],
                  [memory_logits(c, h) for h in range(MEM_HEADS)],
                  [retention_values(c, h) for h in range(RET_HEADS)],
                  [memory_values(c, h) for h in range(MEM_HEADS)],
                  [retention_gated(c, h) for h in range(RET_HEADS)]
                  + [memory_gated(c, h) for h in range(MEM_HEADS)]]

        def run(group):
            def task():
                for t in group:
                    t()
            return task
        return [run(g) for g in groups]

    def out_piece(c, j):
        def task():
            if c not in cats:
                cats[c] = jnp.concatenate(
                    [mixed.pop((c, s)) for s in range(RET_HEADS + MEM_HEADS)], axis=1)
            sl = slice(j * OUT_PIECE_A, (j + 1) * OUT_PIECE_A)
            o_ref[0, c * chunk:(c + 1) * chunk, sl] = xs[c][:, sl] + _dot(cats[c], wout_ref[:, sl])
        return task

    setup = [rotary_tables] + [decay_tables(h) for h in range(RET_HEADS)]
    for step in range(n_chunks + 2):
        stages = [setup] if step == 0 else []
        if step < n_chunks:
            normalize(step)
            stages.append([in_piece(step, p) for p in range(n_in)])
        if 0 <= step - 1 < n_chunks:
            stages.append(mixer_groups(step - 1))
        if 0 <= step - 2 < n_chunks:
            stages.append([out_piece(step - 2, j) for j in range(n_out)])
        for task in _interleaved(*stages):
            task()

    for h in range(RET_HEADS):
        state_ref[h] = states[h]


def _layer_a(x, pos_f32, invf, g_norm, w_in, g_ret_head, w_out, kt, mv, g_mem_q, la, layer):
    batch, seq, _ = x.shape
    tile = ROW_TILE_A
    n_a = g_norm.shape[0]
    depth = g_mem_q.shape[0]
    mem_len = mv.shape[2]
    return pl.pallas_call(
        functools.partial(_layer_a_kernel, la=la),
        grid=(batch, seq // tile),
        in_specs=[
            pl.BlockSpec((1, tile, D_MODEL), lambda b, t: (b, t, 0)),
            pl.BlockSpec((1, 1, tile), lambda b, t: (b, 0, t)),
            _const_spec((RET_DK // 2, 1), lambda b, t: (0, 0)),
            _const_spec((1, 1, D_MODEL), lambda b, t: (la, 0, 0)),
            HBM_SPEC,
            _const_spec((1, 1, RET_DV), lambda b, t: (la, 0, 0)),
            HBM_SPEC,
            pl.BlockSpec((1, 1, MEM_W, mem_len), lambda b, t: (layer, b, 0, 0)),
            pl.BlockSpec((1, 1, mem_len, MEM_W), lambda b, t: (layer, b, 0, 0)),
            _const_spec((1, 1, MEM_DH), lambda b, t: (layer, 0, 0)),
        ],
        out_specs=pl.BlockSpec((1, tile, D_MODEL), lambda b, t: (b, t, 0)),
        out_shape=jax.ShapeDtypeStruct(x.shape, F32),
        scratch_shapes=[pltpu.VMEM((RET_HEADS, RET_DK, RET_DV), F32)]
        + _weight_scratch(w_in.shape[1:]) + _weight_scratch(w_out.shape[1:]),
        compiler_params=pltpu.CompilerParams(
            dimension_semantics=("arbitrary", "arbitrary"),
            vmem_limit_bytes=VMEM_LIMIT_BYTES),
        name="layer_a",
    )(x, pos_f32, invf, g_norm.reshape(n_a, 1, D_MODEL), w_in,
      g_ret_head.reshape(n_a, 1, RET_DV), w_out, kt, mv, g_mem_q.reshape(depth, 1, MEM_DH))


def _proj_b_kernel(x_ref, gkv_ref, gnb_ref, wkv_hbm, win_hbm, kt_ref, mv_ref, gq_ref,
                   q_ref, k_ref, v_ref, gate_ref, mo_ref,
                   wkv_ref, wkv_stage, wkv_sems, win_ref, win_stage, win_sems, *, lb):
    @pl.when(_first_grid_step())
    def _():
        _stage_weight(win_hbm, lb, win_ref, win_stage, win_sems)
        _stage_weight(wkv_hbm, 0, wkv_ref, wkv_stage, wkv_sems)

    x = x_ref[0]
    xr = x * lax.rsqrt(jnp.mean(x * x, axis=-1, keepdims=True) + EPS)
    xkv = (xr * gkv_ref[...]).astype(BF16)
    xb = (xr * gnb_ref[0]).astype(BF16)
    heads = range(MEM_HEADS)
    head = lambda t, h: t[:, h * MEM_DH:(h + 1) * MEM_DH]
    o_mq = 2 * SB_W
    memq = _dot(xb, win_ref[:, o_mq:o_mq + MEM_W])
    memgate = _dot(xb, win_ref[:, o_mq + MEM_W:])
    logits = [_mem_logits(h, head(memq, h), kt_ref, gq_ref[0]) for h in heads]
    gate_ref[0] = _silu(_dot(xb, win_ref[:, SB_W:o_mq])).astype(BF16)
    values = [_mem_values(h, logits[h], mv_ref) for h in heads]
    k_ref[0] = _dot(xkv, wkv_ref[:, :SB_W]).astype(BF16)
    mo_ref[0] = jnp.concatenate([_mem_gated(values[h], head(memgate, h)) for h in heads], axis=1)
    v_ref[0] = _dot(xkv, wkv_ref[:, SB_W:]).astype(BF16)
    q_ref[0] = _dot(xb, win_ref[:, :SB_W]).astype(BF16)


def _proj_b(x, g_kv, g_norm_b, w_kv, w_in, kt, mv, g_mem_q, lb, layer):
    batch, seq, _ = x.shape
    tile = ROW_TILE_B
    n_b = g_norm_b.shape[0]
    depth = g_mem_q.shape[0]
    mem_len = mv.shape[2]
    row_spec = lambda w: pl.BlockSpec((1, tile, w), lambda b, t: (b, t, 0))
    widths = (SB_W, SB_W, SB_W, SB_W, MEM_W)
    return pl.pallas_call(
        functools.partial(_proj_b_kernel, lb=lb),
        grid=(batch, seq // tile),
        in_specs=[
            row_spec(D_MODEL),
            _const_spec((1, D_MODEL), lambda b, t: (0, 0)),
            _const_spec((1, 1, D_MODEL), lambda b, t: (lb, 0, 0)),
            HBM_SPEC,
            HBM_SPEC,
            pl.BlockSpec((1, 1, MEM_W, mem_len), lambda b, t: (layer, b, 0, 0)),
            pl.BlockSpec((1, 1, mem_len, MEM_W), lambda b, t: (layer, b, 0, 0)),
            _const_spec((1, 1, MEM_DH), lambda b, t: (layer, 0, 0)),
        ],
        out_specs=[row_spec(w) for w in widths],
        out_shape=[jax.ShapeDtypeStruct((batch, seq, w), BF16) for w in widths],
        scratch_shapes=_weight_scratch(w_kv.shape) + _weight_scratch(w_in.shape[1:]),
        compiler_params=pltpu.CompilerParams(
            dimension_semantics=("arbitrary", "arbitrary"),
            vmem_limit_bytes=VMEM_LIMIT_BYTES),
        name="proj_b",
    )(x, g_kv.reshape(1, D_MODEL), g_norm_b.reshape(n_b, 1, D_MODEL), w_kv[None], w_in,
      kt, mv, g_mem_q.reshape(depth, 1, MEM_DH))


def _sb_out_kernel(x_ref, q_ref, kprev_ref, kcur_ref, vprev_ref, vcur_ref, k_hbm, v_hbm,
                   gate_ref, mo_ref, wout_hbm, o_ref, cat_ref, kfar_ref, vfar_ref, far_sem,
                   wout_ref, wout_stage, wout_sems, *, n_steps, lb):
    tile = SB_TILE
    tiles_per_step = q_ref.shape[1] // tile
    heads = q_ref.shape[2] // SB_DH
    batch_idx = pl.program_id(0)
    step = pl.program_id(1)
    scale = SB_DH ** -0.5

    @pl.when(_first_grid_step())
    def _():
        _stage_weight(wout_hbm, lb, wout_ref, wout_stage, wout_sems)

    row = lax.broadcasted_iota(jnp.int32, (tile, tile), 0)
    col = lax.broadcasted_iota(jnp.int32, (tile, tile), 1)
    causal = col < row
    suffix_mat = jnp.where(row > col, 1.0, 0.0).astype(BF16)


    def window_keys(prev_ref, cur_ref, far_ref, t, g, kind):
        hs = slice(g * SB_DH, (g + 1) * SB_DH)
        if kind == "far":
            return far_ref[:, hs]
        if kind == "diag":
            return cur_ref[0, t * tile:(t + 1) * tile, hs]
        if t == 0:
            return jnp.concatenate([prev_ref[0, :, hs], cur_ref[0, :tile, hs]], axis=0)
        return cur_ref[0, (t - 1) * tile:(t + 1) * tile, hs]

    def stage_scores(t, g, kind):
        nblk = 2 if kind == "near" else 1
        q = q_ref[0, t * tile:(t + 1) * tile, g * SB_DH:(g + 1) * SB_DH]
        zr = _dot_nt(q, window_keys(kprev_ref, kcur_ref, kfar_ref, t, g, kind))
        z = zr * (scale * LOG2E)
        sp = jnp.maximum(z, jnp.log(1.0 + jnp.exp2(jnp.minimum(z, SB_EXP2_CLAMP))) * LOG2E)
        blocks = [sp[:, j * tile:(j + 1) * tile] for j in range(nblk)]
        if kind != "far":
            blocks[-1] = jnp.where(causal, blocks[-1], 0.0)
        return z, blocks, jnp.concatenate([blk.astype(BF16) for blk in blocks], axis=0)

    def stage_weights(scores, kind, penalty):
        z, blocks, sp_bf16 = scores
        nblk = len(blocks)
        suffix = _dot(sp_bf16, suffix_mat)
        weights = [None] * nblk
        for j in reversed(range(nblk)):
            sfx = suffix[j * tile:(j + 1) * tile]
            log_a = (z[:, j * tile:(j + 1) * tile] - blocks[j]) - sfx
            if penalty is not None:
                log_a = log_a - penalty
            a = jnp.exp2(log_a)
            if kind != "far" and j == nblk - 1:
                a = jnp.where(causal, a, 0.0)
            weights[j] = a.astype(BF16)
            total = sfx[:, 0:1] + blocks[j][:, 0:1]
            penalty = total if penalty is None else penalty + total
        return jnp.concatenate(weights, axis=1), penalty

    def stage_values(weights, t, g, kind):
        return _dot(weights, window_keys(vprev_ref, vcur_ref, vfar_ref, t, g, kind))

    def run_windows(wins, penalties=None, other_tasks=()):
        n = len(wins)
        scores, weights, out = {}, {}, [None] * n

        def stagger(s):
            def task():
                if s < n:
                    scores[s] = stage_scores(*wins[s])
                w = s - 1
                if 0 <= w < n:
                    weights[w] = stage_weights(scores.pop(w), wins[w][2],
                                               None if penalties is None else penalties[w])
                w = s - 2
                if 0 <= w < n:
                    a, penalty = weights.pop(w)
                    out[w] = (stage_values(a, *wins[w]), penalty)
            return task

        steps = [stagger(s) for s in range(n + 2)]
        others = list(other_tasks)
        n_front = (len(others) + 1) // 2
        for task in steps[:1] + others[:n_front] + steps[1:-1] + others[n_front:] + steps[-1:]:
            task()
        return out

    cur = lax.rem(step, 2)
    n_out = o_ref.shape[2] // OUT_PIECE_B

    def out_piece(j):
        def task():
            sl = slice(j * OUT_PIECE_B, (j + 1) * OUT_PIECE_B)
            o_ref[0, :, sl] = x_ref[0, :, sl] + _dot(cat_ref[1 - cur], wout_ref[:, sl])
        return task

    out_tasks = [out_piece(j) for j in range(n_out)]

    def store(t, accs):
        rows = slice(t * tile, (t + 1) * tile)
        for g in range(heads):
            hs = slice(g * SB_DH, (g + 1) * SB_DH)
            cat_ref[cur, rows, hs] = (accs[g] * gate_ref[0, rows, hs].astype(F32)).astype(BF16)

    def log_weight_bound(pens):
        return -functools.reduce(jnp.minimum, [jnp.min(p) for p in pens])

    def fetch_far(start):
        copies = [pltpu.make_async_copy(src.at[batch_idx, pl.ds(start, tile), :], dst, far_sem.at[n])
                  for n, (src, dst) in enumerate(((k_hbm, kfar_ref), (v_hbm, vfar_ref)))]
        for c in copies:
            c.start()
        for c in copies:
            c.wait()

    def finish_tile(t, results):
        accs = [r[0] for r in results]
        penalties = [r[1] for r in results]
        store(t, accs)
        tile_idx = step * tiles_per_step + t
        bound = log_weight_bound(penalties)

        def cond(carry):
            return jnp.logical_and(carry[0] <= tile_idx, carry[1] > SB_LOG_WEIGHT_FLOOR)

        def body(carry):
            n, _, accs, pens = carry
            fetch_far(pl.multiple_of((tile_idx - n) * tile, tile))
            far = run_windows([(t, g, "far") for g in range(heads)], pens)
            accs = [accs[g] + far[g][0] for g in range(heads)]
            pens = [far[g][1] for g in range(heads)]
            return n + 1, log_weight_bound(pens), accs, pens

        @pl.when(jnp.logical_and(tile_idx >= 2, bound > SB_LOG_WEIGHT_FLOOR))
        def _():
            _, _, accs_far, _ = lax.while_loop(cond, body, (jnp.int32(2), bound, accs, penalties))
            store(t, accs_far)

    def attention(first_kind, other_tasks):
        wins = [(t, g, first_kind if t == 0 else "near")
                for t in range(tiles_per_step) for g in range(heads)]
        results = run_windows(wins, None, other_tasks)
        cat_ref[cur, :, SB_W:] = mo_ref[0]
        for t in range(tiles_per_step):
            finish_tile(t, results[t * heads:(t + 1) * heads])

    @pl.when(step == 0)
    def _():
        attention("diag", ())

    @pl.when(jnp.logical_and(step > 0, step < n_steps))
    def _():
        attention("near", out_tasks)

    @pl.when(step == n_steps)
    def _():
        for task in out_tasks:
            task()


def _sb_out(x, q, k, v, gate, mo, w_out, lb):
    batch, seq, _ = x.shape
    rows = ROW_TILE_SB
    n_steps = seq // rows
    tiles_per_step = rows // SB_TILE
    b_out_in = w_out.shape[1]
    attn_idx = lambda i: jnp.minimum(i, n_steps - 1)
    attn_spec = lambda w: pl.BlockSpec((1, rows, w), lambda b, i: (b, attn_idx(i), 0))
    proj_spec = lambda w: pl.BlockSpec((1, rows, w), lambda b, i: (b, jnp.maximum(i - 1, 0), 0))
    prev_spec = pl.BlockSpec(
        (1, SB_TILE, SB_W), lambda b, i: (b, jnp.maximum(attn_idx(i) * tiles_per_step - 1, 0), 0))
    return pl.pallas_call(
        functools.partial(_sb_out_kernel, n_steps=n_steps, lb=lb),
        grid=(batch, n_steps + 1),
        in_specs=[
            proj_spec(D_MODEL), attn_spec(SB_W), prev_spec, attn_spec(SB_W), prev_spec,
            attn_spec(SB_W), HBM_SPEC, HBM_SPEC, attn_spec(SB_W), attn_spec(MEM_W), HBM_SPEC,
        ],
        out_specs=proj_spec(D_MODEL),
        out_shape=jax.ShapeDtypeStruct(x.shape, F32),
        scratch_shapes=[pltpu.VMEM((2, rows, b_out_in), BF16),
                        pltpu.VMEM((SB_TILE, SB_W), BF16),
                        pltpu.VMEM((SB_TILE, SB_W), BF16),
                        pltpu.SemaphoreType.DMA((2,))] + _weight_scratch(w_out.shape[1:]),
        compiler_params=pltpu.CompilerParams(
            dimension_semantics=("arbitrary", "arbitrary"),
            vmem_limit_bytes=VMEM_LIMIT_BYTES),
        name="sb_out",
    )(x, q, k, k, v, v, k, v, gate, mo, w_out)


def kernel(x, mem, positions, g_norm_a, w_in_a, g_ret_head, w_out_a, g_kv, w_kv, g_norm_b, w_in_b,
           w_out_b, g_mem, w_mem_kv, g_mem_q, g_mem_k):
    n_a = g_norm_a.shape[0]
    n_b = g_norm_b.shape[0]
    batch, seq, _ = x.shape

    kt, mv = _memory_kv(mem, g_mem, w_mem_kv.astype(BF16), g_mem_k)

    inv_freq = ROPE_BASE ** (-jnp.arange(0, RET_DK // 2, dtype=F32) * 2.0 / RET_DK)
    invf = inv_freq.reshape(RET_DK // 2, 1)
    pos_f32 = positions.astype(F32).reshape(batch, 1, seq)

    for la in range(n_a):
        x = _layer_a(x, pos_f32, invf, g_norm_a, w_in_a, g_ret_head, w_out_a, kt, mv, g_mem_q,
                     la, la)

    k_shared = v_shared = None
    for lb in range(n_b):
        layer = n_a + lb
        q, k_new, v_new, gate, mo = _proj_b(x, g_kv, g_norm_b, w_kv, w_in_b, kt, mv, g_mem_q,
                                            lb, layer)
        if lb == 0:
            k_shared, v_shared = k_new, v_new
        x = _sb_out(x, q, k_shared, v_shared, gate, mo, w_out_b, lb)
    return x
```

```python
import functools
import math

import numpy as np
import jax
import jax.numpy as jnp
from jax import lax
from jax.experimental import pallas as pl
from jax.experimental.pallas import tpu as pltpu

F32 = jnp.float32
BF16 = jnp.bfloat16

D_MODEL = 1024
RET_HEADS = 4
RET_DK = 128
RET_DV = 256
RET_QK_W = RET_HEADS * RET_DK
RET_V_W = RET_HEADS * RET_DV
ROPE_BASE = 10000.0
SB_HEADS = 8
SB_DH = 128
SB_W = SB_HEADS * SB_DH
MEM_HEADS = 4
MEM_DH = 128
MEM_W = MEM_HEADS * MEM_DH
EPS = 1e-6

RET_LOG_GAMMA = tuple(
    math.log(float(np.float32(1.0 - 2.0 ** (-5.0 - h)))) for h in range(RET_HEADS))

VMEM_LIMIT_BYTES = 48 * 1024 * 1024

ROW_TILE_A = 1024
RET_CHUNK = 256
IN_PIECE_A = 512
OUT_PIECE_A = 256
ROW_TILE_B = 512
OUT_PIECE_B = 256
Q_PIECE_B = 256
SB_TILE = 256
ROW_TILE_SB = 512
SB_LOG_WEIGHT_FLOOR = -152.0
SB_EXP2_CLAMP = 64.0
LOG2E = math.log2(math.e)


def _rms(x, g):
    return x * lax.rsqrt(jnp.mean(x * x, axis=-1, keepdims=True) + EPS) * g


def _silu(x):
    return x * (1.0 / (1.0 + jnp.exp(-x)))


def _dot(a, b):
    return jnp.dot(a, b, preferred_element_type=F32)


def _dot_nt(a, b):
    return lax.dot_general(a, b, (((1,), (1,)), ((), ())), preferred_element_type=F32)


def _const_spec(shape, index_map):
    return pl.BlockSpec(shape, index_map, pipeline_mode=pl.Buffered(1))


def _interleaved(*task_lists):
    tagged = [((i + 0.5) / len(tasks), k, task)
              for k, tasks in enumerate(task_lists) for i, task in enumerate(tasks)]
    return [task for _, _, task in sorted(tagged, key=lambda e: e[:2])]


def _memkv_kernel(mem_ref, g_ref, w_ref, gk_ref, kt_ref, v_ref):
    mn = _rms(mem_ref[0], g_ref[0]).astype(BF16)
    kv = _dot(mn, w_ref[0])
    gk = gk_ref[0]
    for h in range(MEM_HEADS):
        kh = _rms(kv[:, h * MEM_DH:(h + 1) * MEM_DH], gk)
        kt_ref[0, 0, h * MEM_DH:(h + 1) * MEM_DH, :] = kh.T.astype(BF16)
    v_ref[0, 0] = kv[:, MEM_W:].astype(BF16)


def _memory_kv(mem, g_mem, w_mem_kv_bf16, g_mem_k):
    depth = g_mem.shape[0]
    batch, mem_len, _ = mem.shape
    return pl.pallas_call(
        _memkv_kernel,
        grid=(depth, batch),
        in_specs=[
            pl.BlockSpec((1, mem_len, D_MODEL), lambda l, b: (b, 0, 0)),
            pl.BlockSpec((1, 1, D_MODEL), lambda l, b: (l, 0, 0)),
            pl.BlockSpec((1, D_MODEL, 2 * MEM_W), lambda l, b: (l, 0, 0)),
            pl.BlockSpec((1, 1, MEM_DH), lambda l, b: (l, 0, 0)),
        ],
        out_specs=[
            pl.BlockSpec((1, 1, MEM_W, mem_len), lambda l, b: (l, b, 0, 0)),
            pl.BlockSpec((1, 1, mem_len, MEM_W), lambda l, b: (l, b, 0, 0)),
        ],
        out_shape=[
            jax.ShapeDtypeStruct((depth, batch, MEM_W, mem_len), BF16),
            jax.ShapeDtypeStruct((depth, batch, mem_len, MEM_W), BF16),
        ],
        compiler_params=pltpu.CompilerParams(
            dimension_semantics=("arbitrary", "arbitrary"),
            vmem_limit_bytes=VMEM_LIMIT_BYTES),
        name="memkv",
    )(mem, g_mem.reshape(depth, 1, D_MODEL), w_mem_kv_bf16, g_mem_k.reshape(depth, 1, MEM_DH))


def _mem_logits(h, memq, kt_ref, gq):
    qn = _rms(memq, gq).astype(BF16)
    return _dot(qn, kt_ref[0, 0, h * MEM_DH:(h + 1) * MEM_DH, :]) * (MEM_DH ** -0.5)


def _mem_values(h, logits, mv_ref):
    e = jnp.exp(logits - jnp.max(logits, axis=-1, keepdims=True))
    return _dot(e.astype(BF16), mv_ref[0, 0, :, h * MEM_DH:(h + 1) * MEM_DH]), \
        jnp.sum(e, axis=-1, keepdims=True)


def _mem_gated(values, memgate):
    o, denom = values
    return (o / denom * _silu(memgate)).astype(BF16)


def _layer_a_kernel(x_ref, pos_ref, invf_ref, gn_ref, win_ref, gret_ref, wout_ref,
                    kt_ref, mv_ref, gq_ref, o_ref, state_ref):
    tile = x_ref.shape[1]
    chunk = RET_CHUNK
    n_chunks = tile // chunk
    half = RET_DK // 2
    n_in = win_ref.shape[2] // IN_PIECE_A
    n_out = wout_ref.shape[2] // OUT_PIECE_A
    o_v = 2 * RET_QK_W
    o_g = o_v + RET_V_W
    o_mq = o_g + RET_V_W

    @pl.when(pl.program_id(1) == 0)
    def _():
        state_ref[...] = jnp.zeros_like(state_ref)

    tables = {}
    states = [state_ref[h] for h in range(RET_HEADS)]
    xs, xns = {}, {}
    u = {}
    mixed = {}
    cats = {}

    def rotary_tables():
        ang_t = invf_ref[...] * pos_ref[0]
        cos_sin = jnp.concatenate([jnp.cos(ang_t), jnp.sin(ang_t)], axis=0).T
        sin_cos = pltpu.roll(cos_sin, half, 1)
        low_lanes = lax.broadcasted_iota(jnp.int32, (tile, RET_DK), 1) < half
        tables["cos"] = jnp.where(low_lanes, cos_sin, sin_cos)
        tables["sin"] = jnp.where(low_lanes, -sin_cos, cos_sin)

    def decay_tables(h):
        def task():
            lg = RET_LOG_GAMMA[h]
            rel = (lax.broadcasted_iota(jnp.int32, (chunk, chunk), 0)
                   - lax.broadcasted_iota(jnp.int32, (chunk, chunk), 1)).astype(F32)
            idx = lax.broadcasted_iota(jnp.int32, (chunk, 1), 0).astype(F32)
            tables[h] = (jnp.where(rel >= 0.0, jnp.exp(jnp.maximum(rel, 0.0) * lg), 0.0),
                         jnp.exp((idx + 1.0) * lg),
                         jnp.exp((chunk - 1.0 - idx) * lg))
        return task

    def normalize(c):
        xs[c] = x_ref[0, c * chunk:(c + 1) * chunk, :]
        xns[c] = _rms(xs[c], gn_ref[0]).astype(BF16)

    def in_piece(c, p):
        def task():
            u[c, p] = _dot(xns[c], win_ref[0, :, p * IN_PIECE_A:(p + 1) * IN_PIECE_A])
        return task

    def cols(c, lo, width):
        p, off = divmod(lo, IN_PIECE_A)
        return u[c, p][:, off:off + width]

    part = {}

    def retention_scores(c, h):
        def task():
            rows = slice(c * chunk, (c + 1) * chunk)
            cos_c, sin_c = tables["cos"][rows], tables["sin"][rows]

            def rotary(t):
                return t * cos_c + pltpu.roll(t, half, 1) * sin_c

            q = rotary(cols(c, h * RET_DK, RET_DK)).astype(BF16)
            k = rotary(cols(c, RET_QK_W + h * RET_DK, RET_DK)) * (RET_DK ** -0.5)
            part[c, h, "q"], part[c, h, "k"] = q, k
            part[c, h, "scores"] = _dot_nt(q, k.astype(BF16))
        return task

    def retention_values(c, h):
        def task():
            decay, cross_decay, state_decay = tables[h]
            q, k = part.pop((c, h, "q")), part.pop((c, h, "k"))
            v = cols(c, o_v + h * RET_DV, RET_DV).astype(BF16)
            intra = _dot((part.pop((c, h, "scores")) * decay).astype(BF16), v)
            cross = _dot(q, states[h].astype(BF16)) * cross_decay
            states[h] = (math.exp(chunk * RET_LOG_GAMMA[h]) * states[h]
                         + _dot((k * state_decay).T.astype(BF16), v))
            part[c, h, "ret"] = intra + cross
        return task

    def retention_gated(c, h):
        def task():
            ret = _rms(part.pop((c, h, "ret")), gret_ref[0])
            mixed[c, h] = (ret * _silu(cols(c, o_g + h * RET_DV, RET_DV))).astype(BF16)
        return task

    def memory_logits(c, h):
        def task():
            part[c, h, "logits"] = _mem_logits(h, cols(c, o_mq + h * MEM_DH, MEM_DH),
                                               kt_ref, gq_ref[0])
        return task

    def memory_values(c, h):
        def task():
            part[c, h, "values"] = _mem_values(h, part.pop((c, h, "logits")), mv_ref)
        return task

    def memory_gated(c, h):
        def task():
            mixed[c, RET_HEADS + h] = _mem_gated(part.pop((c, h, "values")),
                                                 cols(c, o_mq + MEM_W + h * MEM_DH, MEM_DH))
        return task

    def mixer_groups(c):
        groups = [[retention_scores(c, h) for h in range(RET_HEADS)],
                  [memory_logits(c, h) for h in range(MEM_HEADS)],
                  [retention_values(c, h) for h in range(RET_HEADS)],
                  [memory_values(c, h) for h in range(MEM_HEADS)],
                  [retention_gated(c, h) for h in range(RET_HEADS)]
                  + [memory_gated(c, h) for h in range(MEM_HEADS)]]

        def run(group):
            def task():
                for t in group:
                    t()
            return task
        return [run(g) for g in groups]

    def out_piece(c, j):
        def task():
            if c not in cats:
                cats[c] = jnp.concatenate(
                    [mixed.pop((c, s)) for s in range(RET_HEADS + MEM_HEADS)], axis=1)
            sl = slice(j * OUT_PIECE_A, (j + 1) * OUT_PIECE_A)
            o_ref[0, c * chunk:(c + 1) * chunk, sl] = xs[c][:, sl] + _dot(cats[c], wout_ref[0, :, sl])
        return task

    setup = [rotary_tables] + [decay_tables(h) for h in range(RET_HEADS)]
    for step in range(n_chunks + 2):
        stages = [setup] if step == 0 else []
        if step < n_chunks:
            normalize(step)
            stages.append([in_piece(step, p) for p in range(n_in)])
        if 0 <= step - 1 < n_chunks:
            stages.append(mixer_groups(step - 1))
        if 0 <= step - 2 < n_chunks:
            stages.append([out_piece(step - 2, j) for j in range(n_out)])
        for task in _interleaved(*stages):
            task()

    for h in range(RET_HEADS):
        state_ref[h] = states[h]


def _layer_a(x, pos_f32, invf, g_norm, w_in_bf16, g_ret_head, w_out_bf16, kt, mv, g_mem_q,
             la, layer):
    batch, seq, _ = x.shape
    tile = ROW_TILE_A
    n_a = g_norm.shape[0]
    depth = g_mem_q.shape[0]
    a_in_w = w_in_bf16.shape[2]
    a_out_in = w_out_bf16.shape[1]
    mem_len = mv.shape[2]
    return pl.pallas_call(
        _layer_a_kernel,
        grid=(batch, seq // tile),
        in_specs=[
            pl.BlockSpec((1, tile, D_MODEL), lambda b, t: (b, t, 0)),
            pl.BlockSpec((1, 1, tile), lambda b, t: (b, 0, t)),
            _const_spec((RET_DK // 2, 1), lambda b, t: (0, 0)),
            _const_spec((1, 1, D_MODEL), lambda b, t: (la, 0, 0)),
            _const_spec((1, D_MODEL, a_in_w), lambda b, t: (la, 0, 0)),
            _const_spec((1, 1, RET_DV), lambda b, t: (la, 0, 0)),
            _const_spec((1, a_out_in, D_MODEL), lambda b, t: (la, 0, 0)),
            pl.BlockSpec((1, 1, MEM_W, mem_len), lambda b, t: (layer, b, 0, 0)),
            pl.BlockSpec((1, 1, mem_len, MEM_W), lambda b, t: (layer, b, 0, 0)),
            _const_spec((1, 1, MEM_DH), lambda b, t: (layer, 0, 0)),
        ],
        out_specs=pl.BlockSpec((1, tile, D_MODEL), lambda b, t: (b, t, 0)),
        out_shape=jax.ShapeDtypeStruct(x.shape, F32),
        scratch_shapes=[pltpu.VMEM((RET_HEADS, RET_DK, RET_DV), F32)],
        compiler_params=pltpu.CompilerParams(
            dimension_semantics=("arbitrary", "arbitrary"),
            vmem_limit_bytes=VMEM_LIMIT_BYTES),
        name="layer_a",
    )(x, pos_f32, invf, g_norm.reshape(n_a, 1, D_MODEL), w_in_bf16,
      g_ret_head.reshape(n_a, 1, RET_DV), w_out_bf16, kt, mv, g_mem_q.reshape(depth, 1, MEM_DH))


def _proj_b_kernel(x_ref, gkv_ref, gnb_ref, wkv_ref, win_ref, kt_ref, mv_ref, gq_ref,
                   xb_ref, k_ref, v_ref, gate_ref, mo_ref):
    x = x_ref[0]
    xr = x * lax.rsqrt(jnp.mean(x * x, axis=-1, keepdims=True) + EPS)
    xkv = (xr * gkv_ref[...]).astype(BF16)
    xb = (xr * gnb_ref[0]).astype(BF16)
    heads = range(MEM_HEADS)
    head = lambda t, h: t[:, h * MEM_DH:(h + 1) * MEM_DH]
    o_mq = 2 * SB_W
    memq = _dot(xb, win_ref[0, :, o_mq:o_mq + MEM_W])
    memgate = _dot(xb, win_ref[0, :, o_mq + MEM_W:])
    logits = [_mem_logits(h, head(memq, h), kt_ref, gq_ref[0]) for h in heads]
    gate_ref[0] = _silu(_dot(xb, win_ref[0, :, SB_W:o_mq])).astype(BF16)
    values = [_mem_values(h, logits[h], mv_ref) for h in heads]
    k_ref[0] = _dot(xkv, wkv_ref[:, :SB_W]).astype(BF16)
    mo_ref[0] = jnp.concatenate([_mem_gated(values[h], head(memgate, h)) for h in heads], axis=1)
    v_ref[0] = _dot(xkv, wkv_ref[:, SB_W:]).astype(BF16)
    xb_ref[0] = xb


def _proj_b(x, g_kv, g_norm_b, w_kv_bf16, w_in_bf16, kt, mv, g_mem_q, lb, layer):
    batch, seq, _ = x.shape
    tile = ROW_TILE_B
    n_b = g_norm_b.shape[0]
    depth = g_mem_q.shape[0]
    mem_len = mv.shape[2]
    b_in_w = w_in_bf16.shape[2]
    row_spec = lambda w: pl.BlockSpec((1, tile, w), lambda b, t: (b, t, 0))
    widths = (SB_W, SB_W, SB_W, SB_W, MEM_W)
    return pl.pallas_call(
        _proj_b_kernel,
        grid=(batch, seq // tile),
        in_specs=[
            row_spec(D_MODEL),
            _const_spec((1, D_MODEL), lambda b, t: (0, 0)),
            _const_spec((1, 1, D_MODEL), lambda b, t: (lb, 0, 0)),
            _const_spec((D_MODEL, 2 * SB_W), lambda b, t: (0, 0)),
            _const_spec((1, D_MODEL, b_in_w), lambda b, t: (lb, 0, 0)),
            pl.BlockSpec((1, 1, MEM_W, mem_len), lambda b, t: (layer, b, 0, 0)),
            pl.BlockSpec((1, 1, mem_len, MEM_W), lambda b, t: (layer, b, 0, 0)),
            _const_spec((1, 1, MEM_DH), lambda b, t: (layer, 0, 0)),
        ],
        out_specs=[row_spec(w) for w in widths],
        out_shape=[jax.ShapeDtypeStruct((batch, seq, w), BF16) for w in widths],
        compiler_params=pltpu.CompilerParams(
            dimension_semantics=("arbitrary", "arbitrary"),
            vmem_limit_bytes=VMEM_LIMIT_BYTES),
        name="proj_b",
    )(x, g_kv.reshape(1, D_MODEL), g_norm_b.reshape(n_b, 1, D_MODEL), w_kv_bf16, w_in_bf16,
      kt, mv, g_mem_q.reshape(depth, 1, MEM_DH))


def _sb_out_kernel(x_ref, xb_ref, xbnext_ref, kprev_ref, kcur_ref, vprev_ref, vcur_ref, k_hbm,
                   v_hbm, gate_ref, mo_ref, wq_ref, wout_ref, o_ref, cat_ref, q_ref, kfar_ref,
                   vfar_ref, far_sem, *, n_steps):
    tile = SB_TILE
    tiles_per_step = xb_ref.shape[1] // tile
    heads = wq_ref.shape[2] // SB_DH
    batch_idx = pl.program_id(0)
    step = pl.program_id(1)
    scale = SB_DH ** -0.5

    row = lax.broadcasted_iota(jnp.int32, (tile, tile), 0)
    col = lax.broadcasted_iota(jnp.int32, (tile, tile), 1)
    causal = col < row
    suffix_mat = jnp.where(row > col, 1.0, 0.0).astype(BF16)


    def window_keys(prev_ref, cur_ref, far_ref, t, g, kind):
        hs = slice(g * SB_DH, (g + 1) * SB_DH)
        if kind == "far":
            return far_ref[:, hs]
        if kind == "diag":
            return cur_ref[0, t * tile:(t + 1) * tile, hs]
        if t == 0:
            return jnp.concatenate([prev_ref[0, :, hs], cur_ref[0, :tile, hs]], axis=0)
        return cur_ref[0, (t - 1) * tile:(t + 1) * tile, hs]

    cur = lax.rem(step, 2)
    n_q = wq_ref.shape[2] // Q_PIECE_B

    def q_piece(src_ref, slot, p):
        def task():
            sl = slice(p * Q_PIECE_B, (p + 1) * Q_PIECE_B)
            q_ref[slot, :, sl] = _dot(src_ref[0], wq_ref[0, :, sl]).astype(BF16)
        return task

    def stage_scores(t, g, kind):
        nblk = 2 if kind == "near" else 1
        q = q_ref[cur, t * tile:(t + 1) * tile, g * SB_DH:(g + 1) * SB_DH]
        zr = _dot_nt(q, window_keys(kprev_ref, kcur_ref, kfar_ref, t, g, kind))
        z = zr * (scale * LOG2E)
        sp = jnp.maximum(z, jnp.log(1.0 + jnp.exp2(jnp.minimum(z, SB_EXP2_CLAMP))) * LOG2E)
        blocks = [sp[:, j * tile:(j + 1) * tile] for j in range(nblk)]
        if kind != "far":
            blocks[-1] = jnp.where(causal, blocks[-1], 0.0)
        return z, blocks, jnp.concatenate([blk.astype(BF16) for blk in blocks], axis=0)

    def stage_weights(scores, kind, penalty):
        z, blocks, sp_bf16 = scores
        nblk = len(blocks)
        suffix = _dot(sp_bf16, suffix_mat)
        weights = [None] * nblk
        for j in reversed(range(nblk)):
            sfx = suffix[j * tile:(j + 1) * tile]
            log_a = (z[:, j * tile:(j + 1) * tile] - blocks[j]) - sfx
            if penalty is not None:
                log_a = log_a - penalty
            a = jnp.exp2(log_a)
            if kind != "far" and j == nblk - 1:
                a = jnp.where(causal, a, 0.0)
            weights[j] = a.astype(BF16)
            total = sfx[:, 0:1] + blocks[j][:, 0:1]
            penalty = total if penalty is None else penalty + total
        return jnp.concatenate(weights, axis=1), penalty

    def stage_values(weights, t, g, kind):
        return _dot(weights, window_keys(vprev_ref, vcur_ref, vfar_ref, t, g, kind))

    def run_windows(wins, penalties=None, tasks_before=None):
        n = len(wins)
        scores, weights, out = {}, {}, [None] * n

        def stagger(s):
            def task():
                if s < n:
                    scores[s] = stage_scores(*wins[s])
                w = s - 1
                if 0 <= w < n:
                    weights[w] = stage_weights(scores.pop(w), wins[w][2],
                                               None if penalties is None else penalties[w])
                w = s - 2
                if 0 <= w < n:
                    a, penalty = weights.pop(w)
                    out[w] = (stage_values(a, *wins[w]), penalty)
            return task

        for s in range(n + 2):
            for task in (tasks_before or {}).get(s, ()):
                task()
            stagger(s)()
        return out

    n_out = o_ref.shape[2] // OUT_PIECE_B

    def out_piece(j):
        def task():
            sl = slice(j * OUT_PIECE_B, (j + 1) * OUT_PIECE_B)
            o_ref[0, :, sl] = x_ref[0, :, sl] + _dot(cat_ref[1 - cur], wout_ref[0, :, sl])
        return task

    out_tasks = [out_piece(j) for j in range(n_out)]

    def store(t, accs):
        rows = slice(t * tile, (t + 1) * tile)
        for g in range(heads):
            hs = slice(g * SB_DH, (g + 1) * SB_DH)
            cat_ref[cur, rows, hs] = (accs[g] * gate_ref[0, rows, hs].astype(F32)).astype(BF16)

    def log_weight_bound(pens):
        return -functools.reduce(jnp.minimum, [jnp.min(p) for p in pens])

    def fetch_far(start):
        copies = [pltpu.make_async_copy(src.at[batch_idx, pl.ds(start, tile), :], dst, far_sem.at[n])
                  for n, (src, dst) in enumerate(((k_hbm, kfar_ref), (v_hbm, vfar_ref)))]
        for c in copies:
            c.start()
        for c in copies:
            c.wait()

    def finish_tile(t, results):
        accs = [r[0] for r in results]
        penalties = [r[1] for r in results]
        store(t, accs)
        tile_idx = step * tiles_per_step + t
        bound = log_weight_bound(penalties)

        def cond(carry):
            return jnp.logical_and(carry[0] <= tile_idx, carry[1] > SB_LOG_WEIGHT_FLOOR)

        def body(carry):
            n, _, accs, pens = carry
            fetch_far(pl.multiple_of((tile_idx - n) * tile, tile))
            far = run_windows([(t, g, "far") for g in range(heads)], pens)
            accs = [accs[g] + far[g][0] for g in range(heads)]
            pens = [far[g][1] for g in range(heads)]
            return n + 1, log_weight_bound(pens), accs, pens

        @pl.when(jnp.logical_and(tile_idx >= 2, bound > SB_LOG_WEIGHT_FLOOR))
        def _():
            _, _, accs_far, _ = lax.while_loop(cond, body, (jnp.int32(2), bound, accs, penalties))
            store(t, accs_far)

    def attention(first_kind, other_tasks):
        wins = [(t, g, first_kind if t == 0 else "near")
                for t in range(tiles_per_step) for g in range(heads)]
        tasks_before = {(2 * p + 1) * len(wins) // (2 * n_q): [q_piece(xbnext_ref, 1 - cur, p)]
                        for p in range(n_q)}
        n_front = (len(other_tasks) + 1) // 2
        tasks_before.setdefault(1, []).extend(other_tasks[:n_front])
        tasks_before.setdefault(len(wins) + 1, []).extend(other_tasks[n_front:])
        if first_kind == "diag":
            for p in range(n_q):
                q_piece(xb_ref, cur, p)()
        results = run_windows(wins, None, tasks_before)
        cat_ref[cur, :, SB_W:] = mo_ref[0]
        for t in range(tiles_per_step):
            finish_tile(t, results[t * heads:(t + 1) * heads])

    @pl.when(step == 0)
    def _():
        attention("diag", ())

    @pl.when(jnp.logical_and(step > 0, step < n_steps))
    def _():
        attention("near", out_tasks)

    @pl.when(step == n_steps)
    def _():
        for task in out_tasks:
            task()


def _sb_out(x, xb, k, v, gate, mo, w_in_bf16, w_out_bf16, lb):
    batch, seq, _ = x.shape
    rows = ROW_TILE_SB
    n_steps = seq // rows
    tiles_per_step = rows // SB_TILE
    b_out_in = w_out_bf16.shape[1]
    attn_idx = lambda i: jnp.minimum(i, n_steps - 1)
    attn_spec = lambda w: pl.BlockSpec((1, rows, w), lambda b, i: (b, attn_idx(i), 0))
    proj_spec = lambda w: pl.BlockSpec((1, rows, w), lambda b, i: (b, jnp.maximum(i - 1, 0), 0))
    prev_spec = pl.BlockSpec(
        (1, SB_TILE, SB_W), lambda b, i: (b, jnp.maximum(attn_idx(i) * tiles_per_step - 1, 0), 0))
    any_spec = pl.BlockSpec(memory_space=pl.ANY)
    next_spec = pl.BlockSpec((1, rows, D_MODEL),
                             lambda b, i: (b, jnp.minimum(i + 1, n_steps - 1), 0))
    return pl.pallas_call(
        functools.partial(_sb_out_kernel, n_steps=n_steps),
        grid=(batch, n_steps + 1),
        in_specs=[
            proj_spec(D_MODEL), attn_spec(D_MODEL), next_spec, prev_spec, attn_spec(SB_W), prev_spec,
            attn_spec(SB_W), any_spec, any_spec, attn_spec(SB_W), attn_spec(MEM_W),
            _const_spec((1, D_MODEL, SB_W), lambda b, i: (lb, 0, 0)),
            _const_spec((1, b_out_in, D_MODEL), lambda b, i: (lb, 0, 0)),
        ],
        out_specs=proj_spec(D_MODEL),
        out_shape=jax.ShapeDtypeStruct(x.shape, F32),
        scratch_shapes=[pltpu.VMEM((2, rows, b_out_in), BF16),
                        pltpu.VMEM((2, rows, SB_W), BF16),
                        pltpu.VMEM((SB_TILE, SB_W), BF16),
                        pltpu.VMEM((SB_TILE, SB_W), BF16),
                        pltpu.SemaphoreType.DMA((2,))],
        compiler_params=pltpu.CompilerParams(
            dimension_semantics=("arbitrary", "arbitrary"),
            vmem_limit_bytes=VMEM_LIMIT_BYTES),
        name="sb_out",
    )(x, xb, xb, k, k, v, v, k, v, gate, mo, w_in_bf16, w_out_bf16)


def kernel(x, mem, positions, g_norm_a, w_in_a, g_ret_head, w_out_a, g_kv, w_kv, g_norm_b, w_in_b,
           w_out_b, g_mem, w_mem_kv, g_mem_q, g_mem_k):
    n_a = g_norm_a.shape[0]
    n_b = g_norm_b.shape[0]
    batch, seq, _ = x.shape

    kt, mv = _memory_kv(mem, g_mem, w_mem_kv.astype(BF16), g_mem_k)

    inv_freq = ROPE_BASE ** (-jnp.arange(0, RET_DK // 2, dtype=F32) * 2.0 / RET_DK)
    invf = inv_freq.reshape(RET_DK // 2, 1)
    pos_f32 = positions.astype(F32).reshape(batch, 1, seq)

    w_in_a_bf16 = w_in_a.astype(BF16)
    w_out_a_bf16 = w_out_a.astype(BF16)
    w_in_b_bf16 = w_in_b.astype(BF16)
    w_out_b_bf16 = w_out_b.astype(BF16)
    w_kv_bf16 = w_kv.astype(BF16)

    for la in range(n_a):
        x = _layer_a(x, pos_f32, invf, g_norm_a, w_in_a_bf16, g_ret_head, w_out_a_bf16,
                     kt, mv, g_mem_q, la, la)

    k_shared = v_shared = None
    for lb in range(n_b):
        layer = n_a + lb
        xb, k_new, v_new, gate, mo = _proj_b(x, g_kv, g_norm_b, w_kv_bf16, w_in_b_bf16, kt, mv,
                                             g_mem_q, lb, layer)
        if lb == 0:
            k_shared, v_shared = k_new, v_new
        x = _sb_out(x, xb, k_shared, v_shared, gate, mo, w_in_b_bf16, w_out_b_bf16, lb)
    return x
```

```python
import functools
import math

import numpy as np
import jax
import jax.numpy as jnp
from jax import lax
from jax.experimental import pallas as pl
from jax.experimental.pallas import tpu as pltpu

F32 = jnp.float32
BF16 = jnp.bfloat16

D_MODEL = 1024
RET_HEADS = 4
RET_DK = 128
RET_DV = 256
RET_QK_W = RET_HEADS * RET_DK
RET_V_W = RET_HEADS * RET_DV
ROPE_BASE = 10000.0
SB_HEADS = 8
SB_DH = 128
SB_W = SB_HEADS * SB_DH
MEM_HEADS = 4
MEM_DH = 128
MEM_W = MEM_HEADS * MEM_DH
EPS = 1e-6

RET_LOG_GAMMA = tuple(
    math.log(float(np.float32(1.0 - 2.0 ** (-5.0 - h)))) for h in range(RET_HEADS))

VMEM_LIMIT_BYTES = 48 * 1024 * 1024

ROW_TILE_A = 1024
RET_CHUNK = 256
IN_PIECE_A = 512
OUT_PIECE_A = 256
ROW_TILE_B = 512
OUT_PIECE_B = 256
SB_TILE = 256
ROW_TILE_SB = 512
SB_LOG_WEIGHT_FLOOR = -152.0
SB_EXP2_CLAMP = 64.0
SB_MASKED_LOGIT = -1e30
LOG2E = math.log2(math.e)
SB_Q_SCALE = LOG2E * SB_DH ** -0.5


def _rms(x, g):
    return x * lax.rsqrt(jnp.mean(x * x, axis=-1, keepdims=True) + EPS) * g


def _silu(x):
    return x * (1.0 / (1.0 + jnp.exp(-x)))


def _dot(a, b):
    return jnp.dot(a, b, preferred_element_type=F32)


def _dot_nt(a, b):
    return lax.dot_general(a, b, (((1,), (1,)), ((), ())), preferred_element_type=F32)


def _const_spec(shape, index_map):
    return pl.BlockSpec(shape, index_map, pipeline_mode=pl.Buffered(1))


def _interleaved(*task_lists):
    tagged = [((i + 0.5) / len(tasks), k, task)
              for k, tasks in enumerate(task_lists) for i, task in enumerate(tasks)]
    return [task for _, _, task in sorted(tagged, key=lambda e: e[:2])]


def _memkv_kernel(mem_ref, g_ref, w_ref, gk_ref, kt_ref, v_ref):
    mn = _rms(mem_ref[0], g_ref[0]).astype(BF16)
    kv = _dot(mn, w_ref[0])
    gk = gk_ref[0]
    for h in range(MEM_HEADS):
        kh = _rms(kv[:, h * MEM_DH:(h + 1) * MEM_DH], gk)
        kt_ref[0, 0, h * MEM_DH:(h + 1) * MEM_DH, :] = kh.T.astype(BF16)
    v_ref[0, 0] = kv[:, MEM_W:].astype(BF16)


def _memory_kv(mem, g_mem, w_mem_kv_bf16, g_mem_k):
    depth = g_mem.shape[0]
    batch, mem_len, _ = mem.shape
    return pl.pallas_call(
        _memkv_kernel,
        grid=(depth, batch),
        in_specs=[
            pl.BlockSpec((1, mem_len, D_MODEL), lambda l, b: (b, 0, 0)),
            pl.BlockSpec((1, 1, D_MODEL), lambda l, b: (l, 0, 0)),
            pl.BlockSpec((1, D_MODEL, 2 * MEM_W), lambda l, b: (l, 0, 0)),
            pl.BlockSpec((1, 1, MEM_DH), lambda l, b: (l, 0, 0)),
        ],
        out_specs=[
            pl.BlockSpec((1, 1, MEM_W, mem_len), lambda l, b: (l, b, 0, 0)),
            pl.BlockSpec((1, 1, mem_len, MEM_W), lambda l, b: (l, b, 0, 0)),
        ],
        out_shape=[
            jax.ShapeDtypeStruct((depth, batch, MEM_W, mem_len), BF16),
            jax.ShapeDtypeStruct((depth, batch, mem_len, MEM_W), BF16),
        ],
        compiler_params=pltpu.CompilerParams(
            dimension_semantics=("arbitrary", "arbitrary"),
            vmem_limit_bytes=VMEM_LIMIT_BYTES),
        name="memkv",
    )(mem, g_mem.reshape(depth, 1, D_MODEL), w_mem_kv_bf16, g_mem_k.reshape(depth, 1, MEM_DH))


def _mem_logits(h, memq, kt_ref, gq):
    qn = _rms(memq, gq).astype(BF16)
    return _dot(qn, kt_ref[0, 0, h * MEM_DH:(h + 1) * MEM_DH, :]) * (MEM_DH ** -0.5)


def _mem_values(h, logits, mv_ref):
    e = jnp.exp(logits - jnp.max(logits, axis=-1, keepdims=True))
    return _dot(e.astype(BF16), mv_ref[0, 0, :, h * MEM_DH:(h + 1) * MEM_DH]), \
        jnp.sum(e, axis=-1, keepdims=True)


def _mem_gated(values, memgate):
    o, denom = values
    return (o / denom * _silu(memgate)).astype(BF16)


def _layer_a_kernel(x_ref, pos_ref, invf_ref, gn_ref, win_ref, gret_ref, wout_ref,
                    kt_ref, mv_ref, gq_ref, o_ref, state_ref):
    tile = x_ref.shape[1]
    chunk = RET_CHUNK
    n_chunks = tile // chunk
    half = RET_DK // 2
    n_in = win_ref.shape[2] // IN_PIECE_A
    n_out = wout_ref.shape[2] // OUT_PIECE_A
    o_v = 2 * RET_QK_W
    o_g = o_v + RET_V_W
    o_mq = o_g + RET_V_W

    @pl.when(pl.program_id(1) == 0)
    def _():
        state_ref[...] = jnp.zeros_like(state_ref)

    tables = {}
    states = [state_ref[h] for h in range(RET_HEADS)]
    xs, xns = {}, {}
    u = {}
    mixed = {}
    cats = {}

    def rotary_tables():
        ang_t = invf_ref[...] * pos_ref[0]
        cos_sin = jnp.concatenate([jnp.cos(ang_t), jnp.sin(ang_t)], axis=0).T
        sin_cos = pltpu.roll(cos_sin, half, 1)
        low_lanes = lax.broadcasted_iota(jnp.int32, (tile, RET_DK), 1) < half
        tables["cos"] = jnp.where(low_lanes, cos_sin, sin_cos)
        tables["sin"] = jnp.where(low_lanes, -sin_cos, cos_sin)

    def decay_tables(h):
        def task():
            lg = RET_LOG_GAMMA[h]
            rel = (lax.broadcasted_iota(jnp.int32, (chunk, chunk), 0)
                   - lax.broadcasted_iota(jnp.int32, (chunk, chunk), 1)).astype(F32)
            idx = lax.broadcasted_iota(jnp.int32, (chunk, 1), 0).astype(F32)
            tables[h] = (jnp.where(rel >= 0.0, jnp.exp(jnp.maximum(rel, 0.0) * lg), 0.0),
                         jnp.exp((idx + 1.0) * lg),
                         jnp.exp((chunk - 1.0 - idx) * lg))
        return task

    def normalize(c):
        xs[c] = x_ref[0, c * chunk:(c + 1) * chunk, :]
        xns[c] = _rms(xs[c], gn_ref[0]).astype(BF16)

    def in_piece(c, p):
        def task():
            u[c, p] = _dot(xns[c], win_ref[0, :, p * IN_PIECE_A:(p + 1) * IN_PIECE_A])
        return task

    def cols(c, lo, width):
        p, off = divmod(lo, IN_PIECE_A)
        return u[c, p][:, off:off + width]

    part = {}

    def retention_scores(c, h):
        def task():
            rows = slice(c * chunk, (c + 1) * chunk)
            cos_c, sin_c = tables["cos"][rows], tables["sin"][rows]

            def rotary(t):
                return t * cos_c + pltpu.roll(t, half, 1) * sin_c

            q = rotary(cols(c, h * RET_DK, RET_DK)).astype(BF16)
            k = rotary(cols(c, RET_QK_W + h * RET_DK, RET_DK)) * (RET_DK ** -0.5)
            part[c, h, "q"], part[c, h, "k"] = q, k
            part[c, h, "scores"] = _dot_nt(q, k.astype(BF16))
        return task

    def retention_values(c, h):
        def task():
            decay, cross_decay, state_decay = tables[h]
            q, k = part.pop((c, h, "q")), part.pop((c, h, "k"))
            v = cols(c, o_v + h * RET_DV, RET_DV).astype(BF16)
            intra = _dot((part.pop((c, h, "scores")) * decay).astype(BF16), v)
            cross = _dot(q, states[h].astype(BF16)) * cross_decay
            states[h] = (math.exp(chunk * RET_LOG_GAMMA[h]) * states[h]
                         + _dot((k * state_decay).T.astype(BF16), v))
            part[c, h, "ret"] = intra + cross
        return task

    def retention_gated(c, h):
        def task():
            ret = _rms(part.pop((c, h, "ret")), gret_ref[0])
            mixed[c, h] = (ret * _silu(cols(c, o_g + h * RET_DV, RET_DV))).astype(BF16)
        return task

    def memory_logits(c, h):
        def task():
            part[c, h, "logits"] = _mem_logits(h, cols(c, o_mq + h * MEM_DH, MEM_DH),
                                               kt_ref, gq_ref[0])
        return task

    def memory_values(c, h):
        def task():
            part[c, h, "values"] = _mem_values(h, part.pop((c, h, "logits")), mv_ref)
        return task

    def memory_gated(c, h):
        def task():
            mixed[c, RET_HEADS + h] = _mem_gated(part.pop((c, h, "values")),
                                                 cols(c, o_mq + MEM_W + h * MEM_DH, MEM_DH))
        return task

    def mixer_groups(c):
        groups = [[retention_scores(c, h) for h in range(RET_HEADS)],
                  [memory_logits(c, h) for h in range(MEM_HEADS)],
                  [retention_values(c, h) for h in range(RET_HEADS)],
                  [memory_values(c, h) for h in range(MEM_HEADS)],
                  [retention_gated(c, h) for h in range(RET_HEADS)]
                  + [memory_gated(c, h) for h in range(MEM_HEADS)]]

        def run(group):
            def task():
                for t in group:
                    t()
            return task
        return [run(g) for g in groups]

    def out_piece(c, j):
        def task():
            if c not in cats:
                cats[c] = jnp.concatenate(
                    [mixed.pop((c, s)) for s in range(RET_HEADS + MEM_HEADS)], axis=1)
            sl = slice(j * OUT_PIECE_A, (j + 1) * OUT_PIECE_A)
            o_ref[0, c * chunk:(c + 1) * chunk, sl] = xs[c][:, sl] + _dot(cats[c], wout_ref[0, :, sl])
        return task

    setup = [rotary_tables] + [decay_tables(h) for h in range(RET_HEADS)]
    for step in range(n_chunks + 2):
        stages = [setup] if step == 0 else []
        if step < n_chunks:
            normalize(step)
            stages.append([in_piece(step, p) for p in range(n_in)])
        if 0 <= step - 1 < n_chunks:
            stages.append(mixer_groups(step - 1))
        if 0 <= step - 2 < n_chunks:
            stages.append([out_piece(step - 2, j) for j in range(n_out)])
        for task in _interleaved(*stages):
            task()

    for h in range(RET_HEADS):
        state_ref[h] = states[h]


def _layer_a(x, pos_f32, invf, g_norm, w_in_bf16, g_ret_head, w_out_bf16, kt, mv, g_mem_q,
             la, layer):
    batch, seq, _ = x.shape
    tile = ROW_TILE_A
    n_a = g_norm.shape[0]
    depth = g_mem_q.shape[0]
    a_in_w = w_in_bf16.shape[2]
    a_out_in = w_out_bf16.shape[1]
    mem_len = mv.shape[2]
    return pl.pallas_call(
        _layer_a_kernel,
        grid=(batch, seq // tile),
        in_specs=[
            pl.BlockSpec((1, tile, D_MODEL), lambda b, t: (b, t, 0)),
            pl.BlockSpec((1, 1, tile), lambda b, t: (b, 0, t)),
            _const_spec((RET_DK // 2, 1), lambda b, t: (0, 0)),
            _const_spec((1, 1, D_MODEL), lambda b, t: (la, 0, 0)),
            _const_spec((1, D_MODEL, a_in_w), lambda b, t: (la, 0, 0)),
            _const_spec((1, 1, RET_DV), lambda b, t: (la, 0, 0)),
            _const_spec((1, a_out_in, D_MODEL), lambda b, t: (la, 0, 0)),
            pl.BlockSpec((1, 1, MEM_W, mem_len), lambda b, t: (layer, b, 0, 0)),
            pl.BlockSpec((1, 1, mem_len, MEM_W), lambda b, t: (layer, b, 0, 0)),
            _const_spec((1, 1, MEM_DH), lambda b, t: (layer, 0, 0)),
        ],
        out_specs=pl.BlockSpec((1, tile, D_MODEL), lambda b, t: (b, t, 0)),
        out_shape=jax.ShapeDtypeStruct(x.shape, F32),
        scratch_shapes=[pltpu.VMEM((RET_HEADS, RET_DK, RET_DV), F32)],
        compiler_params=pltpu.CompilerParams(
            dimension_semantics=("arbitrary", "arbitrary"),
            vmem_limit_bytes=VMEM_LIMIT_BYTES),
        name="layer_a",
    )(x, pos_f32, invf, g_norm.reshape(n_a, 1, D_MODEL), w_in_bf16,
      g_ret_head.reshape(n_a, 1, RET_DV), w_out_bf16, kt, mv, g_mem_q.reshape(depth, 1, MEM_DH))


def _proj_b_kernel(x_ref, gkv_ref, gnb_ref, wkv_ref, win_ref, kt_ref, mv_ref, gq_ref,
                   q_ref, k_ref, v_ref, gate_ref, mo_ref):
    x = x_ref[0]
    xr = x * lax.rsqrt(jnp.mean(x * x, axis=-1, keepdims=True) + EPS)
    xkv = (xr * gkv_ref[...]).astype(BF16)
    xb = (xr * gnb_ref[0]).astype(BF16)
    heads = range(MEM_HEADS)
    head = lambda t, h: t[:, h * MEM_DH:(h + 1) * MEM_DH]
    o_mq = 2 * SB_W
    memq = _dot(xb, win_ref[0, :, o_mq:o_mq + MEM_W])
    memgate = _dot(xb, win_ref[0, :, o_mq + MEM_W:])
    logits = [_mem_logits(h, head(memq, h), kt_ref, gq_ref[0]) for h in heads]
    gate_ref[0] = _silu(_dot(xb, win_ref[0, :, SB_W:o_mq])).astype(BF16)
    values = [_mem_values(h, logits[h], mv_ref) for h in heads]
    k_ref[0] = _dot(xkv, wkv_ref[:, :SB_W]).astype(BF16)
    mo_ref[0] = jnp.concatenate([_mem_gated(values[h], head(memgate, h)) for h in heads], axis=1)
    v_ref[0] = _dot(xkv, wkv_ref[:, SB_W:]).astype(BF16)
    q_ref[0] = (_dot(xb, win_ref[0, :, :SB_W]) * SB_Q_SCALE).astype(BF16)


def _proj_b(x, g_kv, g_norm_b, w_kv_bf16, w_in_bf16, kt, mv, g_mem_q, lb, layer):
    batch, seq, _ = x.shape
    tile = ROW_TILE_B
    n_b = g_norm_b.shape[0]
    depth = g_mem_q.shape[0]
    mem_len = mv.shape[2]
    b_in_w = w_in_bf16.shape[2]
    row_spec = lambda w: pl.BlockSpec((1, tile, w), lambda b, t: (b, t, 0))
    widths = (SB_W, SB_W, SB_W, SB_W, MEM_W)
    return pl.pallas_call(
        _proj_b_kernel,
        grid=(batch, seq // tile),
        in_specs=[
            row_spec(D_MODEL),
            _const_spec((1, D_MODEL), lambda b, t: (0, 0)),
            _const_spec((1, 1, D_MODEL), lambda b, t: (lb, 0, 0)),
            _const_spec((D_MODEL, 2 * SB_W), lambda b, t: (0, 0)),
            _const_spec((1, D_MODEL, b_in_w), lambda b, t: (lb, 0, 0)),
            pl.BlockSpec((1, 1, MEM_W, mem_len), lambda b, t: (layer, b, 0, 0)),
            pl.BlockSpec((1, 1, mem_len, MEM_W), lambda b, t: (layer, b, 0, 0)),
            _const_spec((1, 1, MEM_DH), lambda b, t: (layer, 0, 0)),
        ],
        out_specs=[row_spec(w) for w in widths],
        out_shape=[jax.ShapeDtypeStruct((batch, seq, w), BF16) for w in widths],
        compiler_params=pltpu.CompilerParams(
            dimension_semantics=("arbitrary", "arbitrary"),
            vmem_limit_bytes=VMEM_LIMIT_BYTES),
        name="proj_b",
    )(x, g_kv.reshape(1, D_MODEL), g_norm_b.reshape(n_b, 1, D_MODEL), w_kv_bf16, w_in_bf16,
      kt, mv, g_mem_q.reshape(depth, 1, MEM_DH))


def _sb_out_kernel(x_ref, q_ref, kprev_ref, kcur_ref, vprev_ref, vcur_ref, k_hbm, v_hbm,
                   gate_ref, mo_ref, wout_ref, o_ref, cat_ref, kfar_ref, vfar_ref, far_sem, *,
                   n_steps):
    tile = SB_TILE
    tiles_per_step = q_ref.shape[1] // tile
    heads = q_ref.shape[2] // SB_DH
    batch_idx = pl.program_id(0)
    step = pl.program_id(1)

    row = lax.broadcasted_iota(jnp.int32, (tile, tile), 0)
    col = lax.broadcasted_iota(jnp.int32, (tile, tile), 1)
    causal = col < row
    suffix_mat = jnp.where(row > col, 1.0, 0.0).astype(BF16)


    def window_keys(prev_ref, cur_ref, far_ref, t, g, kind):
        hs = slice(g * SB_DH, (g + 1) * SB_DH)
        if kind == "far":
            return far_ref[:, hs]
        if kind == "diag":
            return cur_ref[0, t * tile:(t + 1) * tile, hs]
        if t == 0:
            return jnp.concatenate([prev_ref[0, :, hs], cur_ref[0, :tile, hs]], axis=0)
        return cur_ref[0, (t - 1) * tile:(t + 1) * tile, hs]

    def stage_scores(t, g, kind):
        nblk = 2 if kind == "near" else 1
        q = q_ref[0, t * tile:(t + 1) * tile, g * SB_DH:(g + 1) * SB_DH]
        z = _dot_nt(q, window_keys(kprev_ref, kcur_ref, kfar_ref, t, g, kind))
        if kind != "far":
            diag = jnp.where(causal, z[:, -tile:], SB_MASKED_LOGIT)
            z = diag if nblk == 1 else jnp.concatenate([z[:, :-tile], diag], axis=1)
        sp = jnp.maximum(z, jnp.log(1.0 + jnp.exp2(jnp.minimum(z, SB_EXP2_CLAMP))) * LOG2E)
        blocks = [sp[:, j * tile:(j + 1) * tile] for j in range(nblk)]
        return z, blocks, jnp.concatenate([blk.astype(BF16) for blk in blocks], axis=0)

    def stage_weights(scores, kind, penalty):
        z, blocks, sp_bf16 = scores
        nblk = len(blocks)
        suffix = _dot(sp_bf16, suffix_mat)
        weights = [None] * nblk
        for j in reversed(range(nblk)):
            sfx = suffix[j * tile:(j + 1) * tile]
            log_a = (z[:, j * tile:(j + 1) * tile] - blocks[j]) - sfx
            if penalty is not None:
                log_a = log_a - penalty
            weights[j] = jnp.exp2(log_a).astype(BF16)
            total = sfx[:, 0:1] + blocks[j][:, 0:1]
            penalty = total if penalty is None else penalty + total
        return jnp.concatenate(weights, axis=1), penalty

    def stage_values(weights, t, g, kind):
        return _dot(weights, window_keys(vprev_ref, vcur_ref, vfar_ref, t, g, kind))

    def run_windows(wins, penalties=None, other_tasks=()):
        n = len(wins)
        scores, weights, out = {}, {}, [None] * n

        def stagger(s):
            def task():
                if s < n:
                    scores[s] = stage_scores(*wins[s])
                w = s - 1
                if 0 <= w < n:
                    weights[w] = stage_weights(scores.pop(w), wins[w][2],
                                               None if penalties is None else penalties[w])
                w = s - 2
                if 0 <= w < n:
                    a, penalty = weights.pop(w)
                    out[w] = (stage_values(a, *wins[w]), penalty)
            return task

        steps = [stagger(s) for s in range(n + 2)]
        others = list(other_tasks)
        n_front = (len(others) + 1) // 2
        for task in steps[:1] + others[:n_front] + steps[1:-1] + others[n_front:] + steps[-1:]:
            task()
        return out

    cur = lax.rem(step, 2)
    n_out = o_ref.shape[2] // OUT_PIECE_B

    def out_piece(j):
        def task():
            sl = slice(j * OUT_PIECE_B, (j + 1) * OUT_PIECE_B)
            o_ref[0, :, sl] = x_ref[0, :, sl] + _dot(cat_ref[1 - cur], wout_ref[0, :, sl])
        return task

    out_tasks = [out_piece(j) for j in range(n_out)]

    def store(t, accs):
        rows = slice(t * tile, (t + 1) * tile)
        for g in range(heads):
            hs = slice(g * SB_DH, (g + 1) * SB_DH)
            cat_ref[cur, rows, hs] = (accs[g] * gate_ref[0, rows, hs].astype(F32)).astype(BF16)

    def log_weight_bound(pens):
        return -functools.reduce(jnp.minimum, [jnp.min(p) for p in pens])

    def fetch_far(start):
        copies = [pltpu.make_async_copy(src.at[batch_idx, pl.ds(start, tile), :], dst, far_sem.at[n])
                  for n, (src, dst) in enumerate(((k_hbm, kfar_ref), (v_hbm, vfar_ref)))]
        for c in copies:
            c.start()
        for c in copies:
            c.wait()

    def finish_tile(t, results):
        accs = [r[0] for r in results]
        penalties = [r[1] for r in results]
        store(t, accs)
        tile_idx = step * tiles_per_step + t
        bound = log_weight_bound(penalties)

        def cond(carry):
            return jnp.logical_and(carry[0] <= tile_idx, carry[1] > SB_LOG_WEIGHT_FLOOR)

        def body(carry):
            n, _, accs, pens = carry
            fetch_far(pl.multiple_of((tile_idx - n) * tile, tile))
            far = run_windows([(t, g, "far") for g in range(heads)], pens)
            accs = [accs[g] + far[g][0] for g in range(heads)]
            pens = [far[g][1] for g in range(heads)]
            return n + 1, log_weight_bound(pens), accs, pens

        @pl.when(jnp.logical_and(tile_idx >= 2, bound > SB_LOG_WEIGHT_FLOOR))
        def _():
            _, _, accs_far, _ = lax.while_loop(cond, body, (jnp.int32(2), bound, accs, penalties))
            store(t, accs_far)

    def attention(first_kind, other_tasks):
        wins = [(t, g, first_kind if t == 0 else "near")
                for t in range(tiles_per_step) for g in range(heads)]
        results = run_windows(wins, None, other_tasks)
        cat_ref[cur, :, SB_W:] = mo_ref[0]
        for t in range(tiles_per_step):
            finish_tile(t, results[t * heads:(t + 1) * heads])

    @pl.when(step == 0)
    def _():
        attention("diag", ())

    @pl.when(jnp.logical_and(step > 0, step < n_steps))
    def _():
        attention("near", out_tasks)

    @pl.when(step == n_steps)
    def _():
        for task in out_tasks:
            task()


def _sb_out(x, q, k, v, gate, mo, w_out_bf16, lb):
    batch, seq, _ = x.shape
    rows = ROW_TILE_SB
    n_steps = seq // rows
    tiles_per_step = rows // SB_TILE
    b_out_in = w_out_bf16.shape[1]
    attn_idx = lambda i: jnp.minimum(i, n_steps - 1)
    attn_spec = lambda w: pl.BlockSpec((1, rows, w), lambda b, i: (b, attn_idx(i), 0))
    proj_spec = lambda w: pl.BlockSpec((1, rows, w), lambda b, i: (b, jnp.maximum(i - 1, 0), 0))
    prev_spec = pl.BlockSpec(
        (1, SB_TILE, SB_W), lambda b, i: (b, jnp.maximum(attn_idx(i) * tiles_per_step - 1, 0), 0))
    any_spec = pl.BlockSpec(memory_space=pl.ANY)
    return pl.pallas_call(
        functools.partial(_sb_out_kernel, n_steps=n_steps),
        grid=(batch, n_steps + 1),
        in_specs=[
            proj_spec(D_MODEL), attn_spec(SB_W), prev_spec, attn_spec(SB_W), prev_spec,
            attn_spec(SB_W), any_spec, any_spec, attn_spec(SB_W), attn_spec(MEM_W),
            _const_spec((1, b_out_in, D_MODEL), lambda b, i: (lb, 0, 0)),
        ],
        out_specs=proj_spec(D_MODEL),
        out_shape=jax.ShapeDtypeStruct(x.shape, F32),
        scratch_shapes=[pltpu.VMEM((2, rows, b_out_in), BF16),
                        pltpu.VMEM((SB_TILE, SB_W), BF16),
                        pltpu.VMEM((SB_TILE, SB_W), BF16),
                        pltpu.SemaphoreType.DMA((2,))],
        compiler_params=pltpu.CompilerParams(
            dimension_semantics=("arbitrary", "arbitrary"),
            vmem_limit_bytes=VMEM_LIMIT_BYTES),
        name="sb_out",
    )(x, q, k, k, v, v, k, v, gate, mo, w_out_bf16)


def kernel(x, mem, positions, g_norm_a, w_in_a, g_ret_head, w_out_a, g_kv, w_kv, g_norm_b, w_in_b,
           w_out_b, g_mem, w_mem_kv, g_mem_q, g_mem_k):
    n_a = g_norm_a.shape[0]
    n_b = g_norm_b.shape[0]
    batch, seq, _ = x.shape

    kt, mv = _memory_kv(mem, g_mem, w_mem_kv.astype(BF16), g_mem_k)

    inv_freq = ROPE_BASE ** (-jnp.arange(0, RET_DK // 2, dtype=F32) * 2.0 / RET_DK)
    invf = inv_freq.reshape(RET_DK // 2, 1)
    pos_f32 = positions.astype(F32).reshape(batch, 1, seq)

    w_in_a_bf16 = w_in_a.astype(BF16)
    w_out_a_bf16 = w_out_a.astype(BF16)
    w_in_b_bf16 = w_in_b.astype(BF16)
    w_out_b_bf16 = w_out_b.astype(BF16)
    w_kv_bf16 = w_kv.astype(BF16)

    for la in range(n_a):
        x = _layer_a(x, pos_f32, invf, g_norm_a, w_in_a_bf16, g_ret_head, w_out_a_bf16,
                     kt, mv, g_mem_q, la, la)

    k_shared = v_shared = None
    for lb in range(n_b):
        layer = n_a + lb
        q, k_new, v_new, gate, mo = _proj_b(x, g_kv, g_norm_b, w_kv_bf16, w_in_b_bf16, kt, mv,
                                            g_mem_q, lb, layer)
        if lb == 0:
            k_shared, v_shared = k_new, v_new
        x = _sb_out(x, q, k_shared, v_shared, gate, mo, w_out_b_bf16, lb)
    return x
```

```python
import functools
import math

import numpy as np
import jax
import jax.numpy as jnp
from jax import lax
from jax.experimental import pallas as pl
from jax.experimental.pallas import tpu as pltpu

F32 = jnp.float32
BF16 = jnp.bfloat16

D_MODEL = 1024
RET_HEADS = 4
RET_DK = 128
RET_DV = 256
RET_QK_W = RET_HEADS * RET_DK
RET_V_W = RET_HEADS * RET_DV
ROPE_BASE = 10000.0
SB_HEADS = 8
SB_DH = 128
SB_W = SB_HEADS * SB_DH
MEM_HEADS = 4
MEM_DH = 128
MEM_W = MEM_HEADS * MEM_DH
EPS = 1e-6

RET_LOG_GAMMA = tuple(
    math.log(float(np.float32(1.0 - 2.0 ** (-5.0 - h)))) for h in range(RET_HEADS))

VMEM_LIMIT_BYTES = 48 * 1024 * 1024

ROW_TILE_A = 1024
RET_CHUNK = 256
IN_PIECE_A = 512
OUT_PIECE_A = 256
ROW_TILE_B = 512
OUT_PIECE_B = 256
SB_TILE = 256
ROW_TILE_SB = 512
SB_LOG_WEIGHT_FLOOR = -152.0
SB_EXP2_CLAMP = 64.0
SB_MASKED_LOGIT = -1e30
LOG2E = math.log2(math.e)
SB_Q_SCALE = LOG2E * SB_DH ** -0.5


def _rms(x, g):
    return x * lax.rsqrt(jnp.mean(x * x, axis=-1, keepdims=True) + EPS) * g


def _silu(x):
    return x * (1.0 / (1.0 + jnp.exp(-x)))


def _dot(a, b):
    return jnp.dot(a, b, preferred_element_type=F32)


def _dot_nt(a, b):
    return lax.dot_general(a, b, (((1,), (1,)), ((), ())), preferred_element_type=F32)


def _const_spec(shape, index_map):
    return pl.BlockSpec(shape, index_map, pipeline_mode=pl.Buffered(1))


def _interleaved(*task_lists):
    tagged = [((i + 0.5) / len(tasks), k, task)
              for k, tasks in enumerate(task_lists) for i, task in enumerate(tasks)]
    return [task for _, _, task in sorted(tagged, key=lambda e: e[:2])]


def _memkv_kernel(mem_ref, g_ref, w_ref, gk_ref, kt_ref, v_ref):
    mn = _rms(mem_ref[0], g_ref[0]).astype(BF16)
    kv = _dot(mn, w_ref[0])
    gk = gk_ref[0]
    for h in range(MEM_HEADS):
        kh = _rms(kv[:, h * MEM_DH:(h + 1) * MEM_DH], gk)
        kt_ref[0, 0, h * MEM_DH:(h + 1) * MEM_DH, :] = kh.T.astype(BF16)
    v_ref[0, 0] = kv[:, MEM_W:].astype(BF16)


def _memory_kv(mem, g_mem, w_mem_kv_bf16, g_mem_k):
    depth = g_mem.shape[0]
    batch, mem_len, _ = mem.shape
    return pl.pallas_call(
        _memkv_kernel,
        grid=(depth, batch),
        in_specs=[
            pl.BlockSpec((1, mem_len, D_MODEL), lambda l, b: (b, 0, 0)),
            pl.BlockSpec((1, 1, D_MODEL), lambda l, b: (l, 0, 0)),
            pl.BlockSpec((1, D_MODEL, 2 * MEM_W), lambda l, b: (l, 0, 0)),
            pl.BlockSpec((1, 1, MEM_DH), lambda l, b: (l, 0, 0)),
        ],
        out_specs=[
            pl.BlockSpec((1, 1, MEM_W, mem_len), lambda l, b: (l, b, 0, 0)),
            pl.BlockSpec((1, 1, mem_len, MEM_W), lambda l, b: (l, b, 0, 0)),
        ],
        out_shape=[
            jax.ShapeDtypeStruct((depth, batch, MEM_W, mem_len), BF16),
            jax.ShapeDtypeStruct((depth, batch, mem_len, MEM_W), BF16),
        ],
        compiler_params=pltpu.CompilerParams(
            dimension_semantics=("arbitrary", "arbitrary"),
            vmem_limit_bytes=VMEM_LIMIT_BYTES),
        name="memkv",
    )(mem, g_mem.reshape(depth, 1, D_MODEL), w_mem_kv_bf16, g_mem_k.reshape(depth, 1, MEM_DH))


def _mem_logits(h, memq, kt_ref, gq):
    qn = _rms(memq, gq).astype(BF16)
    return _dot(qn, kt_ref[0, 0, h * MEM_DH:(h + 1) * MEM_DH, :]) * (MEM_DH ** -0.5)


def _mem_values(h, logits, mv_ref):
    e = jnp.exp(logits - jnp.max(logits, axis=-1, keepdims=True))
    return _dot(e.astype(BF16), mv_ref[0, 0, :, h * MEM_DH:(h + 1) * MEM_DH]), \
        jnp.sum(e, axis=-1, keepdims=True)


def _mem_gated(values, memgate):
    o, denom = values
    return (o / denom * _silu(memgate)).astype(BF16)


def _layer_a_kernel(*refs, n_cast):
    n_in_refs = 10
    cast_in = refs[:n_cast]
    (x_ref, pos_ref, invf_ref, gn_ref, win_ref, gret_ref, wout_ref,
     kt_ref, mv_ref, gq_ref) = refs[n_cast:n_cast + n_in_refs]
    cast_out = refs[n_cast + n_in_refs:2 * n_cast + n_in_refs]
    o_ref, state_ref = refs[2 * n_cast + n_in_refs:]
    for src, dst in zip(cast_in, cast_out):
        dst[...] = src[...].astype(BF16)

    tile = x_ref.shape[1]
    chunk = RET_CHUNK
    n_chunks = tile // chunk
    half = RET_DK // 2
    n_in = win_ref.shape[2] // IN_PIECE_A
    n_out = wout_ref.shape[2] // OUT_PIECE_A
    o_v = 2 * RET_QK_W
    o_g = o_v + RET_V_W
    o_mq = o_g + RET_V_W

    @pl.when(pl.program_id(1) == 0)
    def _():
        state_ref[...] = jnp.zeros_like(state_ref)

    tables = {}
    states = [state_ref[h] for h in range(RET_HEADS)]
    xs, xns = {}, {}
    u = {}
    mixed = {}
    cats = {}

    def rotary_tables():
        ang_t = invf_ref[...] * pos_ref[0]
        cos_sin = jnp.concatenate([jnp.cos(ang_t), jnp.sin(ang_t)], axis=0).T
        sin_cos = pltpu.roll(cos_sin, half, 1)
        low_lanes = lax.broadcasted_iota(jnp.int32, (tile, RET_DK), 1) < half
        tables["cos"] = jnp.where(low_lanes, cos_sin, sin_cos)
        tables["sin"] = jnp.where(low_lanes, -sin_cos, cos_sin)

    def decay_tables(h):
        def task():
            lg = RET_LOG_GAMMA[h]
            rel = (lax.broadcasted_iota(jnp.int32, (chunk, chunk), 0)
                   - lax.broadcasted_iota(jnp.int32, (chunk, chunk), 1)).astype(F32)
            idx = lax.broadcasted_iota(jnp.int32, (chunk, 1), 0).astype(F32)
            tables[h] = (jnp.where(rel >= 0.0, jnp.exp(jnp.maximum(rel, 0.0) * lg), 0.0),
                         jnp.exp((idx + 1.0) * lg),
                         jnp.exp((chunk - 1.0 - idx) * lg))
        return task

    def normalize(c):
        xs[c] = x_ref[0, c * chunk:(c + 1) * chunk, :]
        xns[c] = _rms(xs[c], gn_ref[0]).astype(BF16)

    def in_piece(c, p):
        def task():
            u[c, p] = _dot(xns[c], win_ref[0, :, p * IN_PIECE_A:(p + 1) * IN_PIECE_A])
        return task

    def cols(c, lo, width):
        p, off = divmod(lo, IN_PIECE_A)
        return u[c, p][:, off:off + width]

    part = {}

    def retention_scores(c, h):
        def task():
            rows = slice(c * chunk, (c + 1) * chunk)
            cos_c, sin_c = tables["cos"][rows], tables["sin"][rows]

            def rotary(t):
                return t * cos_c + pltpu.roll(t, half, 1) * sin_c

            q = rotary(cols(c, h * RET_DK, RET_DK)).astype(BF16)
            k = rotary(cols(c, RET_QK_W + h * RET_DK, RET_DK)) * (RET_DK ** -0.5)
            part[c, h, "q"], part[c, h, "k"] = q, k
            part[c, h, "scores"] = _dot_nt(q, k.astype(BF16))
        return task

    def retention_values(c, h):
        def task():
            decay, cross_decay, state_decay = tables[h]
            q, k = part.pop((c, h, "q")), part.pop((c, h, "k"))
            v = cols(c, o_v + h * RET_DV, RET_DV).astype(BF16)
            intra = _dot((part.pop((c, h, "scores")) * decay).astype(BF16), v)
            cross = _dot(q, states[h].astype(BF16)) * cross_decay
            states[h] = (math.exp(chunk * RET_LOG_GAMMA[h]) * states[h]
                         + _dot((k * state_decay).T.astype(BF16), v))
            part[c, h, "ret"] = intra + cross
        return task

    def retention_gated(c, h):
        def task():
            ret = _rms(part.pop((c, h, "ret")), gret_ref[0])
            mixed[c, h] = (ret * _silu(cols(c, o_g + h * RET_DV, RET_DV))).astype(BF16)
        return task

    def memory_logits(c, h):
        def task():
            part[c, h, "logits"] = _mem_logits(h, cols(c, o_mq + h * MEM_DH, MEM_DH),
                                               kt_ref, gq_ref[0])
        return task

    def memory_values(c, h):
        def task():
            part[c, h, "values"] = _mem_values(h, part.pop((c, h, "logits")), mv_ref)
        return task

    def memory_gated(c, h):
        def task():
            mixed[c, RET_HEADS + h] = _mem_gated(part.pop((c, h, "values")),
                                                 cols(c, o_mq + MEM_W + h * MEM_DH, MEM_DH))
        return task

    def mixer_groups(c):
        groups = [[retention_scores(c, h) for h in range(RET_HEADS)],
                  [memory_logits(c, h) for h in range(MEM_HEADS)],
                  [retention_values(c, h) for h in range(RET_HEADS)],
                  [memory_values(c, h) for h in range(MEM_HEADS)],
                  [retention_gated(c, h) for h in range(RET_HEADS)]
                  + [memory_gated(c, h) for h in range(MEM_HEADS)]]

        def run(group):
            def task():
                for t in group:
                    t()
            return task
        return [run(g) for g in groups]

    def out_piece(c, j):
        def task():
            if c not in cats:
                cats[c] = jnp.concatenate(
                    [mixed.pop((c, s)) for s in range(RET_HEADS + MEM_HEADS)], axis=1)
            sl = slice(j * OUT_PIECE_A, (j + 1) * OUT_PIECE_A)
            o_ref[0, c * chunk:(c + 1) * chunk, sl] = xs[c][:, sl] + _dot(cats[c], wout_ref[0, :, sl])
        return task

    setup = [rotary_tables] + [decay_tables(h) for h in range(RET_HEADS)]
    for step in range(n_chunks + 2):
        stages = [setup] if step == 0 else []
        if step < n_chunks:
            normalize(step)
            stages.append([in_piece(step, p) for p in range(n_in)])
        if 0 <= step - 1 < n_chunks:
            stages.append(mixer_groups(step - 1))
        if 0 <= step - 2 < n_chunks:
            stages.append([out_piece(step - 2, j) for j in range(n_out)])
        for task in _interleaved(*stages):
            task()

    for h in range(RET_HEADS):
        state_ref[h] = states[h]


def _layer_a(x, pos_f32, invf, g_norm, w_in_bf16, g_ret_head, w_out_bf16, kt, mv, g_mem_q,
             la, layer, later_weights=()):
    batch, seq, _ = x.shape
    tile = ROW_TILE_A
    n_a = g_norm.shape[0]
    depth = g_mem_q.shape[0]
    a_in_w = w_in_bf16.shape[2]
    a_out_in = w_out_bf16.shape[1]
    mem_len = mv.shape[2]
    steps_per_batch = seq // tile
    n_steps = batch * steps_per_batch
    cast_specs = [pl.BlockSpec((w.shape[0] // n_steps, w.shape[1]),
                               lambda b, t: (b * steps_per_batch + t, 0)) for w in later_weights]
    outs = pl.pallas_call(
        functools.partial(_layer_a_kernel, n_cast=len(later_weights)),
        grid=(batch, steps_per_batch),
        in_specs=cast_specs + [
            pl.BlockSpec((1, tile, D_MODEL), lambda b, t: (b, t, 0)),
            pl.BlockSpec((1, 1, tile), lambda b, t: (b, 0, t)),
            _const_spec((RET_DK // 2, 1), lambda b, t: (0, 0)),
            _const_spec((1, 1, D_MODEL), lambda b, t: (la, 0, 0)),
            _const_spec((1, D_MODEL, a_in_w), lambda b, t: (la, 0, 0)),
            _const_spec((1, 1, RET_DV), lambda b, t: (la, 0, 0)),
            _const_spec((1, a_out_in, D_MODEL), lambda b, t: (la, 0, 0)),
            pl.BlockSpec((1, 1, MEM_W, mem_len), lambda b, t: (layer, b, 0, 0)),
            pl.BlockSpec((1, 1, mem_len, MEM_W), lambda b, t: (layer, b, 0, 0)),
            _const_spec((1, 1, MEM_DH), lambda b, t: (layer, 0, 0)),
        ],
        out_specs=cast_specs + [pl.BlockSpec((1, tile, D_MODEL), lambda b, t: (b, t, 0))],
        out_shape=[jax.ShapeDtypeStruct(w.shape, BF16) for w in later_weights]
        + [jax.ShapeDtypeStruct(x.shape, F32)],
        scratch_shapes=[pltpu.VMEM((RET_HEADS, RET_DK, RET_DV), F32)],
        compiler_params=pltpu.CompilerParams(
            dimension_semantics=("arbitrary", "arbitrary"),
            vmem_limit_bytes=VMEM_LIMIT_BYTES),
        name="layer_a",
    )(*later_weights, x, pos_f32, invf, g_norm.reshape(n_a, 1, D_MODEL), w_in_bf16,
      g_ret_head.reshape(n_a, 1, RET_DV), w_out_bf16, kt, mv, g_mem_q.reshape(depth, 1, MEM_DH))
    return outs[-1], outs[:-1]


def _proj_b_kernel(x_ref, gkv_ref, gnb_ref, wkv_ref, win_ref, kt_ref, mv_ref, gq_ref,
                   q_ref, k_ref, v_ref, gate_ref, mo_ref):
    x = x_ref[0]
    xr = x * lax.rsqrt(jnp.mean(x * x, axis=-1, keepdims=True) + EPS)
    xkv = (xr * gkv_ref[...]).astype(BF16)
    xb = (xr * gnb_ref[0]).astype(BF16)
    heads = range(MEM_HEADS)
    head = lambda t, h: t[:, h * MEM_DH:(h + 1) * MEM_DH]
    o_mq = 2 * SB_W
    memq = _dot(xb, win_ref[0, :, o_mq:o_mq + MEM_W])
    memgate = _dot(xb, win_ref[0, :, o_mq + MEM_W:])
    logits = [_mem_logits(h, head(memq, h), kt_ref, gq_ref[0]) for h in heads]
    gate_ref[0] = _silu(_dot(xb, win_ref[0, :, SB_W:o_mq])).astype(BF16)
    values = [_mem_values(h, logits[h], mv_ref) for h in heads]
    k_ref[0] = _dot(xkv, wkv_ref[:, :SB_W]).astype(BF16)
    mo_ref[0] = jnp.concatenate([_mem_gated(values[h], head(memgate, h)) for h in heads], axis=1)
    v_ref[0] = _dot(xkv, wkv_ref[:, SB_W:]).astype(BF16)
    q_ref[0] = (_dot(xb, win_ref[0, :, :SB_W]) * SB_Q_SCALE).astype(BF16)


def _proj_b(x, g_kv, g_norm_b, w_kv_bf16, w_in_bf16, kt, mv, g_mem_q, lb, layer):
    batch, seq, _ = x.shape
    tile = ROW_TILE_B
    n_b = g_norm_b.shape[0]
    depth = g_mem_q.shape[0]
    mem_len = mv.shape[2]
    b_in_w = w_in_bf16.shape[2]
    row_spec = lambda w: pl.BlockSpec((1, tile, w), lambda b, t: (b, t, 0))
    widths = (SB_W, SB_W, SB_W, SB_W, MEM_W)
    return pl.pallas_call(
        _proj_b_kernel,
        grid=(batch, seq // tile),
        in_specs=[
            row_spec(D_MODEL),
            _const_spec((1, D_MODEL), lambda b, t: (0, 0)),
            _const_spec((1, 1, D_MODEL), lambda b, t: (lb, 0, 0)),
            _const_spec((D_MODEL, 2 * SB_W), lambda b, t: (0, 0)),
            _const_spec((1, D_MODEL, b_in_w), lambda b, t: (lb, 0, 0)),
            pl.BlockSpec((1, 1, MEM_W, mem_len), lambda b, t: (layer, b, 0, 0)),
            pl.BlockSpec((1, 1, mem_len, MEM_W), lambda b, t: (layer, b, 0, 0)),
            _const_spec((1, 1, MEM_DH), lambda b, t: (layer, 0, 0)),
        ],
        out_specs=[row_spec(w) for w in widths],
        out_shape=[jax.ShapeDtypeStruct((batch, seq, w), BF16) for w in widths],
        compiler_params=pltpu.CompilerParams(
            dimension_semantics=("arbitrary", "arbitrary"),
            vmem_limit_bytes=VMEM_LIMIT_BYTES),
        name="proj_b",
    )(x, g_kv.reshape(1, D_MODEL), g_norm_b.reshape(n_b, 1, D_MODEL), w_kv_bf16, w_in_bf16,
      kt, mv, g_mem_q.reshape(depth, 1, MEM_DH))


def _sb_out_kernel(x_ref, q_ref, kprev_ref, kcur_ref, vprev_ref, vcur_ref, k_hbm, v_hbm,
                   gate_ref, mo_ref, wout_ref, o_ref, cat_ref, kfar_ref, vfar_ref, far_sem, *,
                   n_steps):
    tile = SB_TILE
    tiles_per_step = q_ref.shape[1] // tile
    heads = q_ref.shape[2] // SB_DH
    batch_idx = pl.program_id(0)
    step = pl.program_id(1)

    row = lax.broadcasted_iota(jnp.int32, (tile, tile), 0)
    col = lax.broadcasted_iota(jnp.int32, (tile, tile), 1)
    causal = col < row
    suffix_mat = jnp.where(row > col, 1.0, 0.0).astype(BF16)


    def window_keys(prev_ref, cur_ref, far_ref, t, g, kind):
        hs = slice(g * SB_DH, (g + 1) * SB_DH)
        if kind == "far":
            return far_ref[:, hs]
        if kind == "diag":
            return cur_ref[0, t * tile:(t + 1) * tile, hs]
        if t == 0:
            return jnp.concatenate([prev_ref[0, :, hs], cur_ref[0, :tile, hs]], axis=0)
        return cur_ref[0, (t - 1) * tile:(t + 1) * tile, hs]

    def stage_scores(t, g, kind):
        nblk = 2 if kind == "near" else 1
        q = q_ref[0, t * tile:(t + 1) * tile, g * SB_DH:(g + 1) * SB_DH]
        z = _dot_nt(q, window_keys(kprev_ref, kcur_ref, kfar_ref, t, g, kind))
        if kind != "far":
            diag = jnp.where(causal, z[:, -tile:], SB_MASKED_LOGIT)
            z = diag if nblk == 1 else jnp.concatenate([z[:, :-tile], diag], axis=1)
        sp = jnp.maximum(z, jnp.log(1.0 + jnp.exp2(jnp.minimum(z, SB_EXP2_CLAMP))) * LOG2E)
        blocks = [sp[:, j * tile:(j + 1) * tile] for j in range(nblk)]
        return z, blocks, jnp.concatenate([blk.astype(BF16) for blk in blocks], axis=0)

    def stage_weights(scores, kind, penalty):
        z, blocks, sp_bf16 = scores
        nblk = len(blocks)
        suffix = _dot(sp_bf16, suffix_mat)
        weights = [None] * nblk
        for j in reversed(range(nblk)):
            sfx = suffix[j * tile:(j + 1) * tile]
            log_a = (z[:, j * tile:(j + 1) * tile] - blocks[j]) - sfx
            if penalty is not None:
                log_a = log_a - penalty
            weights[j] = jnp.exp2(log_a).astype(BF16)
            total = sfx[:, 0:1] + blocks[j][:, 0:1]
            penalty = total if penalty is None else penalty + total
        return jnp.concatenate(weights, axis=1), penalty

    def stage_values(weights, t, g, kind):
        return _dot(weights, window_keys(vprev_ref, vcur_ref, vfar_ref, t, g, kind))

    def run_windows(wins, penalties=None, other_tasks=()):
        n = len(wins)
        scores, weights, out = {}, {}, [None] * n

        def stagger(s):
            def task():
                if s < n:
                    scores[s] = stage_scores(*wins[s])
                w = s - 1
                if 0 <= w < n:
                    weights[w] = stage_weights(scores.pop(w), wins[w][2],
                                               None if penalties is None else penalties[w])
                w = s - 2
                if 0 <= w < n:
                    a, penalty = weights.pop(w)
                    out[w] = (stage_values(a, *wins[w]), penalty)
            return task

        steps = [stagger(s) for s in range(n + 2)]
        others = list(other_tasks)
        n_front = (len(others) + 1) // 2
        for task in steps[:1] + others[:n_front] + steps[1:-1] + others[n_front:] + steps[-1:]:
            task()
        return out

    cur = lax.rem(step, 2)
    n_out = o_ref.shape[2] // OUT_PIECE_B

    def out_piece(j):
        def task():
            sl = slice(j * OUT_PIECE_B, (j + 1) * OUT_PIECE_B)
            o_ref[0, :, sl] = x_ref[0, :, sl] + _dot(cat_ref[1 - cur], wout_ref[0, :, sl])
        return task

    out_tasks = [out_piece(j) for j in range(n_out)]

    def store(t, accs):
        rows = slice(t * tile, (t + 1) * tile)
        for g in range(heads):
            hs = slice(g * SB_DH, (g + 1) * SB_DH)
            cat_ref[cur, rows, hs] = (accs[g] * gate_ref[0, rows, hs].astype(F32)).astype(BF16)

    def log_weight_bound(pens):
        return -functools.reduce(jnp.minimum, [jnp.min(p) for p in pens])

    def fetch_far(start):
        copies = [pltpu.make_async_copy(src.at[batch_idx, pl.ds(start, tile), :], dst, far_sem.at[n])
                  for n, (src, dst) in enumerate(((k_hbm, kfar_ref), (v_hbm, vfar_ref)))]
        for c in copies:
            c.start()
        for c in copies:
            c.wait()

    def finish_tile(t, results):
        accs = [r[0] for r in results]
        penalties = [r[1] for r in results]
        store(t, accs)
        tile_idx = step * tiles_per_step + t
        bound = log_weight_bound(penalties)

        def cond(carry):
            return jnp.logical_and(carry[0] <= tile_idx, carry[1] > SB_LOG_WEIGHT_FLOOR)

        def body(carry):
            n, _, accs, pens = carry
            fetch_far(pl.multiple_of((tile_idx - n) * tile, tile))
            far = run_windows([(t, g, "far") for g in range(heads)], pens)
            accs = [accs[g] + far[g][0] for g in range(heads)]
            pens = [far[g][1] for g in range(heads)]
            return n + 1, log_weight_bound(pens), accs, pens

        @pl.when(jnp.logical_and(tile_idx >= 2, bound > SB_LOG_WEIGHT_FLOOR))
        def _():
            _, _, accs_far, _ = lax.while_loop(cond, body, (jnp.int32(2), bound, accs, penalties))
            store(t, accs_far)

    def attention(first_kind, other_tasks):
        wins = [(t, g, first_kind if t == 0 else "near")
                for t in range(tiles_per_step) for g in range(heads)]
        results = run_windows(wins, None, other_tasks)
        cat_ref[cur, :, SB_W:] = mo_ref[0]
        for t in range(tiles_per_step):
            finish_tile(t, results[t * heads:(t + 1) * heads])

    @pl.when(step == 0)
    def _():
        attention("diag", ())

    @pl.when(jnp.logical_and(step > 0, step < n_steps))
    def _():
        attention("near", out_tasks)

    @pl.when(step == n_steps)
    def _():
        for task in out_tasks:
            task()


def _sb_out(x, q, k, v, gate, mo, w_out_bf16, lb):
    batch, seq, _ = x.shape
    rows = ROW_TILE_SB
    n_steps = seq // rows
    tiles_per_step = rows // SB_TILE
    b_out_in = w_out_bf16.shape[1]
    attn_idx = lambda i: jnp.minimum(i, n_steps - 1)
    attn_spec = lambda w: pl.BlockSpec((1, rows, w), lambda b, i: (b, attn_idx(i), 0))
    proj_spec = lambda w: pl.BlockSpec((1, rows, w), lambda b, i: (b, jnp.maximum(i - 1, 0), 0))
    prev_spec = pl.BlockSpec(
        (1, SB_TILE, SB_W), lambda b, i: (b, jnp.maximum(attn_idx(i) * tiles_per_step - 1, 0), 0))
    any_spec = pl.BlockSpec(memory_space=pl.ANY)
    return pl.pallas_call(
        functools.partial(_sb_out_kernel, n_steps=n_steps),
        grid=(batch, n_steps + 1),
        in_specs=[
            proj_spec(D_MODEL), attn_spec(SB_W), prev_spec, attn_spec(SB_W), prev_spec,
            attn_spec(SB_W), any_spec, any_spec, attn_spec(SB_W), attn_spec(MEM_W),
            _const_spec((1, b_out_in, D_MODEL), lambda b, i: (lb, 0, 0)),
        ],
        out_specs=proj_spec(D_MODEL),
        out_shape=jax.ShapeDtypeStruct(x.shape, F32),
        scratch_shapes=[pltpu.VMEM((2, rows, b_out_in), BF16),
                        pltpu.VMEM((SB_TILE, SB_W), BF16),
                        pltpu.VMEM((SB_TILE, SB_W), BF16),
                        pltpu.SemaphoreType.DMA((2,))],
        compiler_params=pltpu.CompilerParams(
            dimension_semantics=("arbitrary", "arbitrary"),
            vmem_limit_bytes=VMEM_LIMIT_BYTES),
        name="sb_out",
    )(x, q, k, k, v, v, k, v, gate, mo, w_out_bf16)


def kernel(x, mem, positions, g_norm_a, w_in_a, g_ret_head, w_out_a, g_kv, w_kv, g_norm_b, w_in_b,
           w_out_b, g_mem, w_mem_kv, g_mem_q, g_mem_k):
    n_a = g_norm_a.shape[0]
    n_b = g_norm_b.shape[0]
    batch, seq, _ = x.shape

    kt, mv = _memory_kv(mem, g_mem, w_mem_kv.astype(BF16), g_mem_k)

    inv_freq = ROPE_BASE ** (-jnp.arange(0, RET_DK // 2, dtype=F32) * 2.0 / RET_DK)
    invf = inv_freq.reshape(RET_DK // 2, 1)
    pos_f32 = positions.astype(F32).reshape(batch, 1, seq)

    w_in_a_bf16 = w_in_a.astype(BF16)
    w_out_a_bf16 = w_out_a.astype(BF16)

    later = [w_in_b, w_out_b, w_kv]
    later_bf16 = None
    for la in range(n_a):
        x, casted = _layer_a(x, pos_f32, invf, g_norm_a, w_in_a_bf16, g_ret_head, w_out_a_bf16,
                             kt, mv, g_mem_q, la, la,
                             [w.reshape(-1, w.shape[-1]) for w in later] if la == 0 else ())
        later_bf16 = later_bf16 or casted
    w_in_b_bf16, w_out_b_bf16, w_kv_bf16 = (c.reshape(w.shape) for c, w in zip(later_bf16, later))

    k_shared = v_shared = None
    for lb in range(n_b):
        layer = n_a + lb
        q, k_new, v_new, gate, mo = _proj_b(x, g_kv, g_norm_b, w_kv_bf16, w_in_b_bf16, kt, mv,
                                            g_mem_q, lb, layer)
        if lb == 0:
            k_shared, v_shared = k_new, v_new
        x = _sb_out(x, q, k_shared, v_shared, gate, mo, w_out_b_bf16, lb)
    return x
```

```python
import functools
import math

import numpy as np
import jax
import jax.numpy as jnp
from jax import lax
from jax.experimental import pallas as pl
from jax.experimental.pallas import tpu as pltpu

F32 = jnp.float32
BF16 = jnp.bfloat16

D_MODEL = 1024
RET_HEADS = 4
RET_DK = 128
RET_DV = 256
RET_QK_W = RET_HEADS * RET_DK
RET_V_W = RET_HEADS * RET_DV
ROPE_BASE = 10000.0
SB_HEADS = 8
SB_DH = 128
SB_W = SB_HEADS * SB_DH
MEM_HEADS = 4
MEM_DH = 128
MEM_W = MEM_HEADS * MEM_DH
EPS = 1e-6

RET_LOG_GAMMA = tuple(
    math.log(float(np.float32(1.0 - 2.0 ** (-5.0 - h)))) for h in range(RET_HEADS))

VMEM_LIMIT_BYTES = 48 * 1024 * 1024

ROW_TILE_A = 1024
RET_CHUNK = 256
IN_PIECE_A = 512
OUT_PIECE_A = 256
ROW_TILE_B = 512
OUT_PIECE_B = 256
SB_TILE = 256
ROW_TILE_SB = 512
SB_LOG_WEIGHT_FLOOR = -152.0
SB_EXP2_CLAMP = 64.0
SB_MASKED_LOGIT = -1e30
LOG2E = math.log2(math.e)
SB_Q_SCALE = LOG2E * SB_DH ** -0.5


def _rms(x, g):
    return x * lax.rsqrt(jnp.mean(x * x, axis=-1, keepdims=True) + EPS) * g


def _silu(x):
    return x * (1.0 / (1.0 + jnp.exp(-x)))


def _dot(a, b):
    return jnp.dot(a, b, preferred_element_type=F32)


def _dot_nt(a, b):
    return lax.dot_general(a, b, (((1,), (1,)), ((), ())), preferred_element_type=F32)


def _const_spec(shape, index_map):
    return pl.BlockSpec(shape, index_map, pipeline_mode=pl.Buffered(1))


def _interleaved(*task_lists):
    tagged = [((i + 0.5) / len(tasks), k, task)
              for k, tasks in enumerate(task_lists) for i, task in enumerate(tasks)]
    return [task for _, _, task in sorted(tagged, key=lambda e: e[:2])]


def _memkv_kernel(*refs, n_cast):
    cast_in = refs[:n_cast]
    mem_ref, g_ref, w_ref, gk_ref = refs[n_cast:n_cast + 4]
    cast_out = refs[n_cast + 4:2 * n_cast + 4]
    kt_ref, v_ref = refs[2 * n_cast + 4:]
    for src, dst in zip(cast_in, cast_out):
        dst[...] = src[...].astype(BF16)

    mn = _rms(mem_ref[0], g_ref[0]).astype(BF16)
    kv = _dot(mn, w_ref[0].astype(BF16))
    gk = gk_ref[0]
    for h in range(MEM_HEADS):
        kh = _rms(kv[:, h * MEM_DH:(h + 1) * MEM_DH], gk)
        kt_ref[0, 0, h * MEM_DH:(h + 1) * MEM_DH, :] = kh.T.astype(BF16)
    v_ref[0, 0] = kv[:, MEM_W:].astype(BF16)


def _memory_kv(mem, g_mem, w_mem_kv, g_mem_k, later_weights=()):
    depth = g_mem.shape[0]
    batch, mem_len, _ = mem.shape
    n_steps = depth * batch
    cast_specs = [pl.BlockSpec((w.shape[0] // n_steps, w.shape[1]), lambda l, b: (l * batch + b, 0))
                  for w in later_weights]
    outs = pl.pallas_call(
        functools.partial(_memkv_kernel, n_cast=len(later_weights)),
        grid=(depth, batch),
        in_specs=cast_specs + [
            pl.BlockSpec((1, mem_len, D_MODEL), lambda l, b: (b, 0, 0)),
            pl.BlockSpec((1, 1, D_MODEL), lambda l, b: (l, 0, 0)),
            pl.BlockSpec((1, D_MODEL, 2 * MEM_W), lambda l, b: (l, 0, 0)),
            pl.BlockSpec((1, 1, MEM_DH), lambda l, b: (l, 0, 0)),
        ],
        out_specs=cast_specs + [
            pl.BlockSpec((1, 1, MEM_W, mem_len), lambda l, b: (l, b, 0, 0)),
            pl.BlockSpec((1, 1, mem_len, MEM_W), lambda l, b: (l, b, 0, 0)),
        ],
        out_shape=[jax.ShapeDtypeStruct(w.shape, BF16) for w in later_weights] + [
            jax.ShapeDtypeStruct((depth, batch, MEM_W, mem_len), BF16),
            jax.ShapeDtypeStruct((depth, batch, mem_len, MEM_W), BF16),
        ],
        compiler_params=pltpu.CompilerParams(
            dimension_semantics=("arbitrary", "arbitrary"),
            vmem_limit_bytes=VMEM_LIMIT_BYTES),
        name="memkv",
    )(*later_weights, mem, g_mem.reshape(depth, 1, D_MODEL), w_mem_kv,
      g_mem_k.reshape(depth, 1, MEM_DH))
    return outs[-2], outs[-1], outs[:-2]


def _mem_logits(h, memq, kt_ref, gq):
    qn = _rms(memq, gq).astype(BF16)
    return _dot(qn, kt_ref[0, 0, h * MEM_DH:(h + 1) * MEM_DH, :]) * (MEM_DH ** -0.5)


def _mem_values(h, logits, mv_ref):
    e = jnp.exp(logits - jnp.max(logits, axis=-1, keepdims=True))
    return _dot(e.astype(BF16), mv_ref[0, 0, :, h * MEM_DH:(h + 1) * MEM_DH]), \
        jnp.sum(e, axis=-1, keepdims=True)


def _mem_gated(values, memgate):
    o, denom = values
    return (o / denom * _silu(memgate)).astype(BF16)


def _layer_a_kernel(*refs, n_cast):
    n_in_refs = 10
    cast_in = refs[:n_cast]
    (x_ref, pos_ref, invf_ref, gn_ref, win_ref, gret_ref, wout_ref,
     kt_ref, mv_ref, gq_ref) = refs[n_cast:n_cast + n_in_refs]
    cast_out = refs[n_cast + n_in_refs:2 * n_cast + n_in_refs]
    o_ref, state_ref = refs[2 * n_cast + n_in_refs:]
    for src, dst in zip(cast_in, cast_out):
        dst[...] = src[...].astype(BF16)

    tile = x_ref.shape[1]
    chunk = RET_CHUNK
    n_chunks = tile // chunk
    half = RET_DK // 2
    n_in = win_ref.shape[2] // IN_PIECE_A
    n_out = wout_ref.shape[2] // OUT_PIECE_A
    o_v = 2 * RET_QK_W
    o_g = o_v + RET_V_W
    o_mq = o_g + RET_V_W

    @pl.when(pl.program_id(1) == 0)
    def _():
        state_ref[...] = jnp.zeros_like(state_ref)

    tables = {}
    states = [state_ref[h] for h in range(RET_HEADS)]
    xs, xns = {}, {}
    u = {}
    mixed = {}
    cats = {}

    def rotary_tables():
        ang_t = invf_ref[...] * pos_ref[0]
        cos_sin = jnp.concatenate([jnp.cos(ang_t), jnp.sin(ang_t)], axis=0).T
        sin_cos = pltpu.roll(cos_sin, half, 1)
        low_lanes = lax.broadcasted_iota(jnp.int32, (tile, RET_DK), 1) < half
        tables["cos"] = jnp.where(low_lanes, cos_sin, sin_cos)
        tables["sin"] = jnp.where(low_lanes, -sin_cos, cos_sin)

    def decay_tables(h):
        def task():
            lg = RET_LOG_GAMMA[h]
            rel = (lax.broadcasted_iota(jnp.int32, (chunk, chunk), 0)
                   - lax.broadcasted_iota(jnp.int32, (chunk, chunk), 1)).astype(F32)
            idx = lax.broadcasted_iota(jnp.int32, (chunk, 1), 0).astype(F32)
            tables[h] = (jnp.where(rel >= 0.0, jnp.exp(jnp.maximum(rel, 0.0) * lg), 0.0),
                         jnp.exp((idx + 1.0) * lg),
                         jnp.exp((chunk - 1.0 - idx) * lg))
        return task

    def normalize(c):
        xs[c] = x_ref[0, c * chunk:(c + 1) * chunk, :]
        xns[c] = _rms(xs[c], gn_ref[0]).astype(BF16)

    def in_piece(c, p):
        def task():
            u[c, p] = _dot(xns[c], win_ref[0, :, p * IN_PIECE_A:(p + 1) * IN_PIECE_A])
        return task

    def cols(c, lo, width):
        p, off = divmod(lo, IN_PIECE_A)
        return u[c, p][:, off:off + width]

    part = {}

    def retention_scores(c, h):
        def task():
            rows = slice(c * chunk, (c + 1) * chunk)
            cos_c, sin_c = tables["cos"][rows], tables["sin"][rows]

            def rotary(t):
                return t * cos_c + pltpu.roll(t, half, 1) * sin_c

            q = rotary(cols(c, h * RET_DK, RET_DK)).astype(BF16)
            k = rotary(cols(c, RET_QK_W + h * RET_DK, RET_DK)) * (RET_DK ** -0.5)
            part[c, h, "q"], part[c, h, "k"] = q, k
            part[c, h, "scores"] = _dot_nt(q, k.astype(BF16))
        return task

    def retention_values(c, h):
        def task():
            decay, cross_decay, state_decay = tables[h]
            q, k = part.pop((c, h, "q")), part.pop((c, h, "k"))
            v = cols(c, o_v + h * RET_DV, RET_DV).astype(BF16)
            intra = _dot((part.pop((c, h, "scores")) * decay).astype(BF16), v)
            cross = _dot(q, states[h].astype(BF16)) * cross_decay
            states[h] = (math.exp(chunk * RET_LOG_GAMMA[h]) * states[h]
                         + _dot((k * state_decay).T.astype(BF16), v))
            part[c, h, "ret"] = intra + cross
        return task

    def retention_gated(c, h):
        def task():
            ret = _rms(part.pop((c, h, "ret")), gret_ref[0])
            mixed[c, h] = (ret * _silu(cols(c, o_g + h * RET_DV, RET_DV))).astype(BF16)
        return task

    def memory_logits(c, h):
        def task():
            part[c, h, "logits"] = _mem_logits(h, cols(c, o_mq + h * MEM_DH, MEM_DH),
                                               kt_ref, gq_ref[0])
        return task

    def memory_values(c, h):
        def task():
            part[c, h, "values"] = _mem_values(h, part.pop((c, h, "logits")), mv_ref)
        return task

    def memory_gated(c, h):
        def task():
            mixed[c, RET_HEADS + h] = _mem_gated(part.pop((c, h, "values")),
                                                 cols(c, o_mq + MEM_W + h * MEM_DH, MEM_DH))
        return task

    def mixer_groups(c):
        groups = [[retention_scores(c, h) for h in range(RET_HEADS)],
                  [memory_logits(c, h) for h in range(MEM_HEADS)],
                  [retention_values(c, h) for h in range(RET_HEADS)],
                  [memory_values(c, h) for h in range(MEM_HEADS)],
                  [retention_gated(c, h) for h in range(RET_HEADS)]
                  + [memory_gated(c, h) for h in range(MEM_HEADS)]]

        def run(group):
            def task():
                for t in group:
                    t()
            return task
        return [run(g) for g in groups]

    def out_piece(c, j):
        def task():
            if c not in cats:
                cats[c] = jnp.concatenate(
                    [mixed.pop((c, s)) for s in range(RET_HEADS + MEM_HEADS)], axis=1)
            sl = slice(j * OUT_PIECE_A, (j + 1) * OUT_PIECE_A)
            o_ref[0, c * chunk:(c + 1) * chunk, sl] = xs[c][:, sl] + _dot(cats[c], wout_ref[0, :, sl])
        return task

    setup = [rotary_tables] + [decay_tables(h) for h in range(RET_HEADS)]
    for step in range(n_chunks + 2):
        stages = [setup] if step == 0 else []
        if step < n_chunks:
            normalize(step)
            stages.append([in_piece(step, p) for p in range(n_in)])
        if 0 <= step - 1 < n_chunks:
            stages.append(mixer_groups(step - 1))
        if 0 <= step - 2 < n_chunks:
            stages.append([out_piece(step - 2, j) for j in range(n_out)])
        for task in _interleaved(*stages):
            task()

    for h in range(RET_HEADS):
        state_ref[h] = states[h]


def _layer_a(x, pos_f32, invf, g_norm, w_in_bf16, g_ret_head, w_out_bf16, kt, mv, g_mem_q,
             la, layer, later_weights=()):
    batch, seq, _ = x.shape
    tile = ROW_TILE_A
    n_a = g_norm.shape[0]
    depth = g_mem_q.shape[0]
    a_in_w = w_in_bf16.shape[2]
    a_out_in = w_out_bf16.shape[1]
    mem_len = mv.shape[2]
    steps_per_batch = seq // tile
    n_steps = batch * steps_per_batch
    cast_specs = [pl.BlockSpec((w.shape[0] // n_steps, w.shape[1]),
                               lambda b, t: (b * steps_per_batch + t, 0)) for w in later_weights]
    outs = pl.pallas_call(
        functools.partial(_layer_a_kernel, n_cast=len(later_weights)),
        grid=(batch, steps_per_batch),
        in_specs=cast_specs + [
            pl.BlockSpec((1, tile, D_MODEL), lambda b, t: (b, t, 0)),
            pl.BlockSpec((1, 1, tile), lambda b, t: (b, 0, t)),
            _const_spec((RET_DK // 2, 1), lambda b, t: (0, 0)),
            _const_spec((1, 1, D_MODEL), lambda b, t: (la, 0, 0)),
            _const_spec((1, D_MODEL, a_in_w), lambda b, t: (la, 0, 0)),
            _const_spec((1, 1, RET_DV), lambda b, t: (la, 0, 0)),
            _const_spec((1, a_out_in, D_MODEL), lambda b, t: (la, 0, 0)),
            pl.BlockSpec((1, 1, MEM_W, mem_len), lambda b, t: (layer, b, 0, 0)),
            pl.BlockSpec((1, 1, mem_len, MEM_W), lambda b, t: (layer, b, 0, 0)),
            _const_spec((1, 1, MEM_DH), lambda b, t: (layer, 0, 0)),
        ],
        out_specs=cast_specs + [pl.BlockSpec((1, tile, D_MODEL), lambda b, t: (b, t, 0))],
        out_shape=[jax.ShapeDtypeStruct(w.shape, BF16) for w in later_weights]
        + [jax.ShapeDtypeStruct(x.shape, F32)],
        scratch_shapes=[pltpu.VMEM((RET_HEADS, RET_DK, RET_DV), F32)],
        compiler_params=pltpu.CompilerParams(
            dimension_semantics=("arbitrary", "arbitrary"),
            vmem_limit_bytes=VMEM_LIMIT_BYTES),
        name="layer_a",
    )(*later_weights, x, pos_f32, invf, g_norm.reshape(n_a, 1, D_MODEL), w_in_bf16,
      g_ret_head.reshape(n_a, 1, RET_DV), w_out_bf16, kt, mv, g_mem_q.reshape(depth, 1, MEM_DH))
    return outs[-1], outs[:-1]


def _proj_b_kernel(x_ref, gkv_ref, gnb_ref, wkv_ref, win_ref, kt_ref, mv_ref, gq_ref,
                   q_ref, k_ref, v_ref, gate_ref, mo_ref):
    x = x_ref[0]
    xr = x * lax.rsqrt(jnp.mean(x * x, axis=-1, keepdims=True) + EPS)
    xkv = (xr * gkv_ref[...]).astype(BF16)
    xb = (xr * gnb_ref[0]).astype(BF16)
    heads = range(MEM_HEADS)
    head = lambda t, h: t[:, h * MEM_DH:(h + 1) * MEM_DH]
    o_mq = 2 * SB_W
    memq = _dot(xb, win_ref[0, :, o_mq:o_mq + MEM_W])
    memgate = _dot(xb, win_ref[0, :, o_mq + MEM_W:])
    logits = [_mem_logits(h, head(memq, h), kt_ref, gq_ref[0]) for h in heads]
    gate_ref[0] = _silu(_dot(xb, win_ref[0, :, SB_W:o_mq])).astype(BF16)
    values = [_mem_values(h, logits[h], mv_ref) for h in heads]
    k_ref[0] = _dot(xkv, wkv_ref[:, :SB_W]).astype(BF16)
    mo_ref[0] = jnp.concatenate([_mem_gated(values[h], head(memgate, h)) for h in heads], axis=1)
    v_ref[0] = _dot(xkv, wkv_ref[:, SB_W:]).astype(BF16)
    q_ref[0] = (_dot(xb, win_ref[0, :, :SB_W]) * SB_Q_SCALE).astype(BF16)


def _proj_b(x, g_kv, g_norm_b, w_kv_bf16, w_in_bf16, kt, mv, g_mem_q, lb, layer):
    batch, seq, _ = x.shape
    tile = ROW_TILE_B
    n_b = g_norm_b.shape[0]
    depth = g_mem_q.shape[0]
    mem_len = mv.shape[2]
    b_in_w = w_in_bf16.shape[2]
    row_spec = lambda w: pl.BlockSpec((1, tile, w), lambda b, t: (b, t, 0))
    widths = (SB_W, SB_W, SB_W, SB_W, MEM_W)
    return pl.pallas_call(
        _proj_b_kernel,
        grid=(batch, seq // tile),
        in_specs=[
            row_spec(D_MODEL),
            _const_spec((1, D_MODEL), lambda b, t: (0, 0)),
            _const_spec((1, 1, D_MODEL), lambda b, t: (lb, 0, 0)),
            _const_spec((D_MODEL, 2 * SB_W), lambda b, t: (0, 0)),
            _const_spec((1, D_MODEL, b_in_w), lambda b, t: (lb, 0, 0)),
            pl.BlockSpec((1, 1, MEM_W, mem_len), lambda b, t: (layer, b, 0, 0)),
            pl.BlockSpec((1, 1, mem_len, MEM_W), lambda b, t: (layer, b, 0, 0)),
            _const_spec((1, 1, MEM_DH), lambda b, t: (layer, 0, 0)),
        ],
        out_specs=[row_spec(w) for w in widths],
        out_shape=[jax.ShapeDtypeStruct((batch, seq, w), BF16) for w in widths],
        compiler_params=pltpu.CompilerParams(
            dimension_semantics=("arbitrary", "arbitrary"),
            vmem_limit_bytes=VMEM_LIMIT_BYTES),
        name="proj_b",
    )(x, g_kv.reshape(1, D_MODEL), g_norm_b.reshape(n_b, 1, D_MODEL), w_kv_bf16, w_in_bf16,
      kt, mv, g_mem_q.reshape(depth, 1, MEM_DH))


def _sb_out_kernel(x_ref, q_ref, kprev_ref, kcur_ref, vprev_ref, vcur_ref, k_hbm, v_hbm,
                   gate_ref, mo_ref, wout_ref, o_ref, cat_ref, kfar_ref, vfar_ref, far_sem, *,
                   n_steps):
    tile = SB_TILE
    tiles_per_step = q_ref.shape[1] // tile
    heads = q_ref.shape[2] // SB_DH
    batch_idx = pl.program_id(0)
    step = pl.program_id(1)

    row = lax.broadcasted_iota(jnp.int32, (tile, tile), 0)
    col = lax.broadcasted_iota(jnp.int32, (tile, tile), 1)
    causal = col < row
    suffix_mat = jnp.where(row > col, 1.0, 0.0).astype(BF16)


    def window_keys(prev_ref, cur_ref, far_ref, t, g, kind):
        hs = slice(g * SB_DH, (g + 1) * SB_DH)
        if kind == "far":
            return far_ref[:, hs]
        if kind == "diag":
            return cur_ref[0, t * tile:(t + 1) * tile, hs]
        if t == 0:
            return jnp.concatenate([prev_ref[0, :, hs], cur_ref[0, :tile, hs]], axis=0)
        return cur_ref[0, (t - 1) * tile:(t + 1) * tile, hs]

    def stage_scores(t, g, kind):
        nblk = 2 if kind == "near" else 1
        q = q_ref[0, t * tile:(t + 1) * tile, g * SB_DH:(g + 1) * SB_DH]
        z = _dot_nt(q, window_keys(kprev_ref, kcur_ref, kfar_ref, t, g, kind))
        if kind != "far":
            diag = jnp.where(causal, z[:, -tile:], SB_MASKED_LOGIT)
            z = diag if nblk == 1 else jnp.concatenate([z[:, :-tile], diag], axis=1)
        sp = jnp.maximum(z, jnp.log(1.0 + jnp.exp2(jnp.minimum(z, SB_EXP2_CLAMP))) * LOG2E)
        blocks = [sp[:, j * tile:(j + 1) * tile] for j in range(nblk)]
        return z, blocks, jnp.concatenate([blk.astype(BF16) for blk in blocks], axis=0)

    def stage_weights(scores, kind, penalty):
        z, blocks, sp_bf16 = scores
        nblk = len(blocks)
        suffix = _dot(sp_bf16, suffix_mat)
        weights = [None] * nblk
        for j in reversed(range(nblk)):
            sfx = suffix[j * tile:(j + 1) * tile]
            log_a = (z[:, j * tile:(j + 1) * tile] - blocks[j]) - sfx
            if penalty is not None:
                log_a = log_a - penalty
            weights[j] = jnp.exp2(log_a).astype(BF16)
            total = sfx[:, 0:1] + blocks[j][:, 0:1]
            penalty = total if penalty is None else penalty + total
        return jnp.concatenate(weights, axis=1), penalty

    def stage_values(weights, t, g, kind):
        return _dot(weights, window_keys(vprev_ref, vcur_ref, vfar_ref, t, g, kind))

    def run_windows(wins, penalties=None, other_tasks=()):
        n = len(wins)
        scores, weights, out = {}, {}, [None] * n

        def stagger(s):
            def task():
                if s < n:
                    scores[s] = stage_scores(*wins[s])
                w = s - 1
                if 0 <= w < n:
                    weights[w] = stage_weights(scores.pop(w), wins[w][2],
                                               None if penalties is None else penalties[w])
                w = s - 2
                if 0 <= w < n:
                    a, penalty = weights.pop(w)
                    out[w] = (stage_values(a, *wins[w]), penalty)
            return task

        steps = [stagger(s) for s in range(n + 2)]
        others = list(other_tasks)
        n_front = (len(others) + 1) // 2
        for task in steps[:1] + others[:n_front] + steps[1:-1] + others[n_front:] + steps[-1:]:
            task()
        return out

    cur = lax.rem(step, 2)
    n_out = o_ref.shape[2] // OUT_PIECE_B

    def out_piece(j):
        def task():
            sl = slice(j * OUT_PIECE_B, (j + 1) * OUT_PIECE_B)
            o_ref[0, :, sl] = x_ref[0, :, sl] + _dot(cat_ref[1 - cur], wout_ref[0, :, sl])
        return task

    out_tasks = [out_piece(j) for j in range(n_out)]

    def store(t, accs):
        rows = slice(t * tile, (t + 1) * tile)
        for g in range(heads):
            hs = slice(g * SB_DH, (g + 1) * SB_DH)
            cat_ref[cur, rows, hs] = (accs[g] * gate_ref[0, rows, hs].astype(F32)).astype(BF16)

    def log_weight_bound(pens):
        return -functools.reduce(jnp.minimum, [jnp.min(p) for p in pens])

    def fetch_far(start):
        copies = [pltpu.make_async_copy(src.at[batch_idx, pl.ds(start, tile), :], dst, far_sem.at[n])
                  for n, (src, dst) in enumerate(((k_hbm, kfar_ref), (v_hbm, vfar_ref)))]
        for c in copies:
            c.start()
        for c in copies:
            c.wait()

    def finish_tile(t, results):
        accs = [r[0] for r in results]
        penalties = [r[1] for r in results]
        store(t, accs)
        tile_idx = step * tiles_per_step + t
        bound = log_weight_bound(penalties)

        def cond(carry):
            return jnp.logical_and(carry[0] <= tile_idx, carry[1] > SB_LOG_WEIGHT_FLOOR)

        def body(carry):
            n, _, accs, pens = carry
            fetch_far(pl.multiple_of((tile_idx - n) * tile, tile))
            far = run_windows([(t, g, "far") for g in range(heads)], pens)
            accs = [accs[g] + far[g][0] for g in range(heads)]
            pens = [far[g][1] for g in range(heads)]
            return n + 1, log_weight_bound(pens), accs, pens

        @pl.when(jnp.logical_and(tile_idx >= 2, bound > SB_LOG_WEIGHT_FLOOR))
        def _():
            _, _, accs_far, _ = lax.while_loop(cond, body, (jnp.int32(2), bound, accs, penalties))
            store(t, accs_far)

    def attention(first_kind, other_tasks):
        wins = [(t, g, first_kind if t == 0 else "near")
                for t in range(tiles_per_step) for g in range(heads)]
        results = run_windows(wins, None, other_tasks)
        cat_ref[cur, :, SB_W:] = mo_ref[0]
        for t in range(tiles_per_step):
            finish_tile(t, results[t * heads:(t + 1) * heads])

    @pl.when(step == 0)
    def _():
        attention("diag", ())

    @pl.when(jnp.logical_and(step > 0, step < n_steps))
    def _():
        attention("near", out_tasks)

    @pl.when(step == n_steps)
    def _():
        for task in out_tasks:
            task()


def _sb_out(x, q, k, v, gate, mo, w_out_bf16, lb):
    batch, seq, _ = x.shape
    rows = ROW_TILE_SB
    n_steps = seq // rows
    tiles_per_step = rows // SB_TILE
    b_out_in = w_out_bf16.shape[1]
    attn_idx = lambda i: jnp.minimum(i, n_steps - 1)
    attn_spec = lambda w: pl.BlockSpec((1, rows, w), lambda b, i: (b, attn_idx(i), 0))
    proj_spec = lambda w: pl.BlockSpec((1, rows, w), lambda b, i: (b, jnp.maximum(i - 1, 0), 0))
    prev_spec = pl.BlockSpec(
        (1, SB_TILE, SB_W), lambda b, i: (b, jnp.maximum(attn_idx(i) * tiles_per_step - 1, 0), 0))
    any_spec = pl.BlockSpec(memory_space=pl.ANY)
    return pl.pallas_call(
        functools.partial(_sb_out_kernel, n_steps=n_steps),
        grid=(batch, n_steps + 1),
        in_specs=[
            proj_spec(D_MODEL), attn_spec(SB_W), prev_spec, attn_spec(SB_W), prev_spec,
            attn_spec(SB_W), any_spec, any_spec, attn_spec(SB_W), attn_spec(MEM_W),
            _const_spec((1, b_out_in, D_MODEL), lambda b, i: (lb, 0, 0)),
        ],
        out_specs=proj_spec(D_MODEL),
        out_shape=jax.ShapeDtypeStruct(x.shape, F32),
        scratch_shapes=[pltpu.VMEM((2, rows, b_out_in), BF16),
                        pltpu.VMEM((SB_TILE, SB_W), BF16),
                        pltpu.VMEM((SB_TILE, SB_W), BF16),
                        pltpu.SemaphoreType.DMA((2,))],
        compiler_params=pltpu.CompilerParams(
            dimension_semantics=("arbitrary", "arbitrary"),
            vmem_limit_bytes=VMEM_LIMIT_BYTES),
        name="sb_out",
    )(x, q, k, k, v, v, k, v, gate, mo, w_out_bf16)


def kernel(x, mem, positions, g_norm_a, w_in_a, g_ret_head, w_out_a, g_kv, w_kv, g_norm_b, w_in_b,
           w_out_b, g_mem, w_mem_kv, g_mem_q, g_mem_k):
    n_a = g_norm_a.shape[0]
    n_b = g_norm_b.shape[0]
    batch, seq, _ = x.shape

    flat = lambda w: w.reshape(-1, w.shape[-1])
    kt, mv, casted = _memory_kv(mem, g_mem, w_mem_kv, g_mem_k, [flat(w_in_a), flat(w_out_a)])
    w_in_a_bf16, w_out_a_bf16 = (c.reshape(w.shape) for c, w in zip(casted, (w_in_a, w_out_a)))

    inv_freq = ROPE_BASE ** (-jnp.arange(0, RET_DK // 2, dtype=F32) * 2.0 / RET_DK)
    invf = inv_freq.reshape(RET_DK // 2, 1)
    pos_f32 = positions.astype(F32).reshape(batch, 1, seq)

    later = [w_in_b, w_out_b, w_kv]
    later_bf16 = None
    for la in range(n_a):
        x, casted = _layer_a(x, pos_f32, invf, g_norm_a, w_in_a_bf16, g_ret_head, w_out_a_bf16,
                             kt, mv, g_mem_q, la, la,
                             [flat(w) for w in later] if la == 0 else ())
        later_bf16 = later_bf16 or casted
    w_in_b_bf16, w_out_b_bf16, w_kv_bf16 = (c.reshape(w.shape) for c, w in zip(later_bf16, later))

    k_shared = v_shared = None
    for lb in range(n_b):
        layer = n_a + lb
        q, k_new, v_new, gate, mo = _proj_b(x, g_kv, g_norm_b, w_kv_bf16, w_in_b_bf16, kt, mv,
                                            g_mem_q, lb, layer)
        if lb == 0:
            k_shared, v_shared = k_new, v_new
        x = _sb_out(x, q, k_shared, v_shared, gate, mo, w_out_b_bf16, lb)
    return x
```

```python
import functools
import math

import numpy as np
import jax
import jax.numpy as jnp
from jax import lax
from jax.experimental import pallas as pl
from jax.experimental.pallas import tpu as pltpu

F32 = jnp.float32
BF16 = jnp.bfloat16

D_MODEL = 1024
RET_HEADS = 4
RET_DK = 128
RET_DV = 256
RET_QK_W = RET_HEADS * RET_DK
RET_V_W = RET_HEADS * RET_DV
ROPE_BASE = 10000.0
SB_HEADS = 8
SB_DH = 128
SB_W = SB_HEADS * SB_DH
MEM_HEADS = 4
MEM_DH = 128
MEM_W = MEM_HEADS * MEM_DH
EPS = 1e-6

RET_LOG_GAMMA = tuple(
    math.log(float(np.float32(1.0 - 2.0 ** (-5.0 - h)))) for h in range(RET_HEADS))

VMEM_LIMIT_BYTES = 48 * 1024 * 1024

ROW_TILE_A = 1024
RET_CHUNK = 256
IN_PIECE_A = 512
OUT_PIECE_A = 256
ROW_TILE_B = 512
OUT_PIECE_B = 256
SB_TILE = 256
ROW_TILE_SB = 512
SB_LOG_WEIGHT_FLOOR = -152.0
SB_EXP2_CLAMP = 64.0
SB_MASKED_LOGIT = -1e30
LOG2E = math.log2(math.e)
SB_Q_SCALE = LOG2E * SB_DH ** -0.5


def _rms(x, g):
    return x * lax.rsqrt(jnp.mean(x * x, axis=-1, keepdims=True) + EPS) * g


def _silu(x):
    return x * (1.0 / (1.0 + jnp.exp(-x)))


def _dot(a, b):
    return jnp.dot(a, b, preferred_element_type=F32)


def _dot_nt(a, b):
    return lax.dot_general(a, b, (((1,), (1,)), ((), ())), preferred_element_type=F32)


def _const_spec(shape, index_map):
    return pl.BlockSpec(shape, index_map, pipeline_mode=pl.Buffered(1))


def _interleaved(*task_lists):
    tagged = [((i + 0.5) / len(tasks), k, task)
              for k, tasks in enumerate(task_lists) for i, task in enumerate(tasks)]
    return [task for _, _, task in sorted(tagged, key=lambda e: e[:2])]


def _memkv_kernel(*refs, n_cast):
    cast_in = refs[:n_cast]
    mem_ref, g_ref, w_ref, gk_ref = refs[n_cast:n_cast + 4]
    cast_out = refs[n_cast + 4:2 * n_cast + 4]
    kt_ref, v_ref = refs[2 * n_cast + 4:]
    for src, dst in zip(cast_in, cast_out):
        dst[...] = src[...].astype(BF16)

    mn = _rms(mem_ref[0], g_ref[0]).astype(BF16)
    kv = _dot(mn, w_ref[0].astype(BF16))
    gk = gk_ref[0]
    for h in range(MEM_HEADS):
        kh = _rms(kv[:, h * MEM_DH:(h + 1) * MEM_DH], gk)
        kt_ref[0, 0, h * MEM_DH:(h + 1) * MEM_DH, :] = kh.T.astype(BF16)
    v_ref[0, 0] = kv[:, MEM_W:].astype(BF16)


def _memory_kv(mem, g_mem, w_mem_kv, g_mem_k, later_weights=()):
    depth = g_mem.shape[0]
    batch, mem_len, _ = mem.shape
    n_steps = depth * batch
    cast_specs = [pl.BlockSpec((w.shape[0] // n_steps, w.shape[1]), lambda l, b: (l * batch + b, 0))
                  for w in later_weights]
    outs = pl.pallas_call(
        functools.partial(_memkv_kernel, n_cast=len(later_weights)),
        grid=(depth, batch),
        in_specs=cast_specs + [
            pl.BlockSpec((1, mem_len, D_MODEL), lambda l, b: (b, 0, 0)),
            pl.BlockSpec((1, 1, D_MODEL), lambda l, b: (l, 0, 0)),
            pl.BlockSpec((1, D_MODEL, 2 * MEM_W), lambda l, b: (l, 0, 0)),
            pl.BlockSpec((1, 1, MEM_DH), lambda l, b: (l, 0, 0)),
        ],
        out_specs=cast_specs + [
            pl.BlockSpec((1, 1, MEM_W, mem_len), lambda l, b: (l, b, 0, 0)),
            pl.BlockSpec((1, 1, mem_len, MEM_W), lambda l, b: (l, b, 0, 0)),
        ],
        out_shape=[jax.ShapeDtypeStruct(w.shape, BF16) for w in later_weights] + [
            jax.ShapeDtypeStruct((depth, batch, MEM_W, mem_len), BF16),
            jax.ShapeDtypeStruct((depth, batch, mem_len, MEM_W), BF16),
        ],
        compiler_params=pltpu.CompilerParams(
            dimension_semantics=("arbitrary", "arbitrary"),
            vmem_limit_bytes=VMEM_LIMIT_BYTES),
        name="memkv",
    )(*later_weights, mem, g_mem.reshape(depth, 1, D_MODEL), w_mem_kv,
      g_mem_k.reshape(depth, 1, MEM_DH))
    return outs[-2], outs[-1], outs[:-2]


def _mem_logits(h, memq, kt_ref, gq):
    qn = _rms(memq, gq).astype(BF16)
    return _dot(qn, kt_ref[0, 0, h * MEM_DH:(h + 1) * MEM_DH, :]) * (MEM_DH ** -0.5)


def _mem_values(h, logits, mv_ref):
    e = jnp.exp(logits - jnp.max(logits, axis=-1, keepdims=True))
    return _dot(e.astype(BF16), mv_ref[0, 0, :, h * MEM_DH:(h + 1) * MEM_DH]), \
        jnp.sum(e, axis=-1, keepdims=True)


def _mem_gated(values, memgate):
    o, denom = values
    return (o / denom * _silu(memgate)).astype(BF16)


def _layer_a_kernel(*refs, n_cast):
    n_in_refs = 10
    cast_in = refs[:n_cast]
    (x_ref, pos_ref, invf_ref, gn_ref, win_ref, gret_ref, wout_ref,
     kt_ref, mv_ref, gq_ref) = refs[n_cast:n_cast + n_in_refs]
    cast_out = refs[n_cast + n_in_refs:2 * n_cast + n_in_refs]
    o_ref, state_ref = refs[2 * n_cast + n_in_refs:]
    for src, dst in zip(cast_in, cast_out):
        dst[...] = src[...].astype(BF16)

    tile = x_ref.shape[1]
    chunk = RET_CHUNK
    n_chunks = tile // chunk
    half = RET_DK // 2
    n_in = win_ref.shape[2] // IN_PIECE_A
    n_out = wout_ref.shape[2] // OUT_PIECE_A
    o_v = 2 * RET_QK_W
    o_g = o_v + RET_V_W
    o_mq = o_g + RET_V_W

    @pl.when(pl.program_id(1) == 0)
    def _():
        state_ref[...] = jnp.zeros_like(state_ref)

    tables = {}
    states = [state_ref[h] for h in range(RET_HEADS)]
    xs, xns = {}, {}
    u = {}
    mixed = {}
    cats = {}

    def rotary_tables():
        ang_t = invf_ref[...] * pos_ref[0]
        cos_sin = jnp.concatenate([jnp.cos(ang_t), jnp.sin(ang_t)], axis=0).T
        sin_cos = pltpu.roll(cos_sin, half, 1)
        low_lanes = lax.broadcasted_iota(jnp.int32, (tile, RET_DK), 1) < half
        tables["cos"] = jnp.where(low_lanes, cos_sin, sin_cos)
        tables["sin"] = jnp.where(low_lanes, -sin_cos, cos_sin)

    def decay_tables(h):
        def task():
            lg = RET_LOG_GAMMA[h]
            rel = (lax.broadcasted_iota(jnp.int32, (chunk, chunk), 0)
                   - lax.broadcasted_iota(jnp.int32, (chunk, chunk), 1)).astype(F32)
            idx = lax.broadcasted_iota(jnp.int32, (chunk, 1), 0).astype(F32)
            tables[h] = (jnp.where(rel >= 0.0, jnp.exp(jnp.maximum(rel, 0.0) * lg), 0.0),
                         jnp.exp((idx + 1.0) * lg),
                         jnp.exp((chunk - 1.0 - idx) * lg))
        return task

    def normalize(c):
        xs[c] = x_ref[0, c * chunk:(c + 1) * chunk, :]
        xns[c] = _rms(xs[c], gn_ref[0]).astype(BF16)

    def in_piece(c, p):
        def task():
            u[c, p] = _dot(xns[c], win_ref[0, :, p * IN_PIECE_A:(p + 1) * IN_PIECE_A])
        return task

    def cols(c, lo, width):
        p, off = divmod(lo, IN_PIECE_A)
        return u[c, p][:, off:off + width]

    part = {}

    def retention_scores(c, h):
        def task():
            rows = slice(c * chunk, (c + 1) * chunk)
            cos_c, sin_c = tables["cos"][rows], tables["sin"][rows]

            def rotary(t):
                return t * cos_c + pltpu.roll(t, half, 1) * sin_c

            q = rotary(cols(c, h * RET_DK, RET_DK)).astype(BF16)
            k = rotary(cols(c, RET_QK_W + h * RET_DK, RET_DK)) * (RET_DK ** -0.5)
            part[c, h, "q"], part[c, h, "k"] = q, k
            part[c, h, "scores"] = _dot_nt(q, k.astype(BF16))
        return task

    def retention_values(c, h):
        def task():
            decay, cross_decay, state_decay = tables[h]
            q, k = part.pop((c, h, "q")), part.pop((c, h, "k"))
            v = cols(c, o_v + h * RET_DV, RET_DV).astype(BF16)
            intra = _dot((part.pop((c, h, "scores")) * decay).astype(BF16), v)
            cross = _dot(q, states[h].astype(BF16)) * cross_decay
            states[h] = (math.exp(chunk * RET_LOG_GAMMA[h]) * states[h]
                         + _dot((k * state_decay).T.astype(BF16), v))
            part[c, h, "ret"] = intra + cross
        return task

    def retention_gated(c, h):
        def task():
            ret = _rms(part.pop((c, h, "ret")), gret_ref[0])
            mixed[c, h] = (ret * _silu(cols(c, o_g + h * RET_DV, RET_DV))).astype(BF16)
        return task

    def memory_logits(c, h):
        def task():
            part[c, h, "logits"] = _mem_logits(h, cols(c, o_mq + h * MEM_DH, MEM_DH),
                                               kt_ref, gq_ref[0])
        return task

    def memory_values(c, h):
        def task():
            part[c, h, "values"] = _mem_values(h, part.pop((c, h, "logits")), mv_ref)
        return task

    def memory_gated(c, h):
        def task():
            mixed[c, RET_HEADS + h] = _mem_gated(part.pop((c, h, "values")),
                                                 cols(c, o_mq + MEM_W + h * MEM_DH, MEM_DH))
        return task

    def mixer_groups(c):
        groups = [[retention_scores(c, h) for h in range(RET_HEADS)],
                  [memory_logits(c, h) for h in range(MEM_HEADS)],
                  [retention_values(c, h) for h in range(RET_HEADS)],
                  [memory_values(c, h) for h in range(MEM_HEADS)],
                  [retention_gated(c, h) for h in range(RET_HEADS)]
                  + [memory_gated(c, h) for h in range(MEM_HEADS)]]

        def run(group):
            def task():
                for t in group:
                    t()
            return task
        return [run(g) for g in groups]

    def out_piece(c, j):
        def task():
            if c not in cats:
                cats[c] = jnp.concatenate(
                    [mixed.pop((c, s)) for s in range(RET_HEADS + MEM_HEADS)], axis=1)
            sl = slice(j * OUT_PIECE_A, (j + 1) * OUT_PIECE_A)
            o_ref[0, c * chunk:(c + 1) * chunk, sl] = xs[c][:, sl] + _dot(cats[c], wout_ref[0, :, sl])
        return task

    setup = [rotary_tables] + [decay_tables(h) for h in range(RET_HEADS)]
    for step in range(n_chunks + 2):
        stages = [setup] if step == 0 else []
        if step < n_chunks:
            normalize(step)
            stages.append([in_piece(step, p) for p in range(n_in)])
        if 0 <= step - 1 < n_chunks:
            stages.append(mixer_groups(step - 1))
        if 0 <= step - 2 < n_chunks:
            stages.append([out_piece(step - 2, j) for j in range(n_out)])
        for task in _interleaved(*stages):
            task()

    for h in range(RET_HEADS):
        state_ref[h] = states[h]


def _layer_a(x, pos_f32, invf, g_norm, w_in_bf16, g_ret_head, w_out_bf16, kt, mv, g_mem_q,
             la, layer, later_weights=()):
    batch, seq, _ = x.shape
    tile = ROW_TILE_A
    n_a = g_norm.shape[0]
    depth = g_mem_q.shape[0]
    a_in_w = w_in_bf16.shape[2]
    a_out_in = w_out_bf16.shape[1]
    mem_len = mv.shape[2]
    steps_per_batch = seq // tile
    n_steps = batch * steps_per_batch
    cast_specs = [pl.BlockSpec((w.shape[0] // n_steps, w.shape[1]),
                               lambda b, t: (b * steps_per_batch + t, 0)) for w in later_weights]
    outs = pl.pallas_call(
        functools.partial(_layer_a_kernel, n_cast=len(later_weights)),
        grid=(batch, steps_per_batch),
        in_specs=cast_specs + [
            pl.BlockSpec((1, tile, D_MODEL), lambda b, t: (b, t, 0)),
            pl.BlockSpec((1, 1, tile), lambda b, t: (b, 0, t)),
            _const_spec((RET_DK // 2, 1), lambda b, t: (0, 0)),
            _const_spec((1, 1, D_MODEL), lambda b, t: (la, 0, 0)),
            _const_spec((1, D_MODEL, a_in_w), lambda b, t: (la, 0, 0)),
            _const_spec((1, 1, RET_DV), lambda b, t: (la, 0, 0)),
            _const_spec((1, a_out_in, D_MODEL), lambda b, t: (la, 0, 0)),
            pl.BlockSpec((1, 1, MEM_W, mem_len), lambda b, t: (layer, b, 0, 0)),
            pl.BlockSpec((1, 1, mem_len, MEM_W), lambda b, t: (layer, b, 0, 0)),
            _const_spec((1, 1, MEM_DH), lambda b, t: (layer, 0, 0)),
        ],
        out_specs=cast_specs + [pl.BlockSpec((1, tile, D_MODEL), lambda b, t: (b, t, 0))],
        out_shape=[jax.ShapeDtypeStruct(w.shape, BF16) for w in later_weights]
        + [jax.ShapeDtypeStruct(x.shape, F32)],
        scratch_shapes=[pltpu.VMEM((RET_HEADS, RET_DK, RET_DV), F32)],
        compiler_params=pltpu.CompilerParams(
            dimension_semantics=("arbitrary", "arbitrary"),
            vmem_limit_bytes=VMEM_LIMIT_BYTES),
        name="layer_a",
    )(*later_weights, x, pos_f32, invf, g_norm.reshape(n_a, 1, D_MODEL), w_in_bf16,
      g_ret_head.reshape(n_a, 1, RET_DV), w_out_bf16, kt, mv, g_mem_q.reshape(depth, 1, MEM_DH))
    return outs[-1], outs[:-1]


def _proj_b_kernel(x_ref, gkv_ref, gnb_ref, wkv_ref, win_ref, kt_ref, mv_ref, gq_ref,
                   q_ref, k_ref, v_ref, gate_ref, mo_ref):
    x = x_ref[0]
    xr = x * lax.rsqrt(jnp.mean(x * x, axis=-1, keepdims=True) + EPS)
    xkv = (xr * gkv_ref[...]).astype(BF16)
    xb = (xr * gnb_ref[0]).astype(BF16)
    heads = range(MEM_HEADS)
    head = lambda t, h: t[:, h * MEM_DH:(h + 1) * MEM_DH]
    o_mq = 2 * SB_W
    memq = _dot(xb, win_ref[0, :, o_mq:o_mq + MEM_W])
    memgate = _dot(xb, win_ref[0, :, o_mq + MEM_W:])
    logits = [_mem_logits(h, head(memq, h), kt_ref, gq_ref[0]) for h in heads]
    gate_ref[0] = _silu(_dot(xb, win_ref[0, :, SB_W:o_mq])).astype(BF16)
    values = [_mem_values(h, logits[h], mv_ref) for h in heads]
    k_ref[0] = _dot(xkv, wkv_ref[:, :SB_W]).astype(BF16)
    mo_ref[0] = jnp.concatenate([_mem_gated(values[h], head(memgate, h)) for h in heads], axis=1)
    v_ref[0] = _dot(xkv, wkv_ref[:, SB_W:]).astype(BF16)
    q_ref[0] = (_dot(xb, win_ref[0, :, :SB_W]) * SB_Q_SCALE).astype(BF16)


def _proj_b(x, g_kv, g_norm_b, w_kv_bf16, w_in_bf16, kt, mv, g_mem_q, lb, layer):
    batch, seq, _ = x.shape
    tile = ROW_TILE_B
    n_b = g_norm_b.shape[0]
    depth = g_mem_q.shape[0]
    mem_len = mv.shape[2]
    b_in_w = w_in_bf16.shape[2]
    row_spec = lambda w: pl.BlockSpec((1, tile, w), lambda b, t: (b, t, 0))
    widths = (SB_W, SB_W, SB_W, SB_W, MEM_W)
    return pl.pallas_call(
        _proj_b_kernel,
        grid=(batch, seq // tile),
        in_specs=[
            row_spec(D_MODEL),
            _const_spec((1, D_MODEL), lambda b, t: (0, 0)),
            _const_spec((1, 1, D_MODEL), lambda b, t: (lb, 0, 0)),
            _const_spec((D_MODEL, 2 * SB_W), lambda b, t: (0, 0)),
            _const_spec((1, D_MODEL, b_in_w), lambda b, t: (lb, 0, 0)),
            pl.BlockSpec((1, 1, MEM_W, mem_len), lambda b, t: (layer, b, 0, 0)),
            pl.BlockSpec((1, 1, mem_len, MEM_W), lambda b, t: (layer, b, 0, 0)),
            _const_spec((1, 1, MEM_DH), lambda b, t: (layer, 0, 0)),
        ],
        out_specs=[row_spec(w) for w in widths],
        out_shape=[jax.ShapeDtypeStruct((batch, seq, w), BF16) for w in widths],
        compiler_params=pltpu.CompilerParams(
            dimension_semantics=("arbitrary", "arbitrary"),
            vmem_limit_bytes=VMEM_LIMIT_BYTES),
        name="proj_b",
    )(x, g_kv.reshape(1, D_MODEL), g_norm_b.reshape(n_b, 1, D_MODEL), w_kv_bf16, w_in_bf16,
      kt, mv, g_mem_q.reshape(depth, 1, MEM_DH))


def _sb_out_kernel(x_ref, q_ref, kprev_ref, kcur_ref, vprev_ref, vcur_ref, k_hbm, v_hbm,
                   gate_ref, mo_ref, wout_ref, o_ref, cat_ref, kfar_ref, vfar_ref, far_sem, *,
                   n_steps):
    tile = SB_TILE
    tiles_per_step = q_ref.shape[1] // tile
    heads = q_ref.shape[2] // SB_DH
    batch_idx = pl.program_id(0)
    step = pl.program_id(1)

    row = lax.broadcasted_iota(jnp.int32, (tile, tile), 0)
    col = lax.broadcasted_iota(jnp.int32, (tile, tile), 1)
    causal = col < row
    suffix_mat = jnp.where(row > col, 1.0, 0.0).astype(BF16)


    def window_keys(prev_ref, cur_ref, far_ref, t, g, kind):
        hs = slice(g * SB_DH, (g + 1) * SB_DH)
        if kind == "far":
            return far_ref[:, hs]
        if kind == "diag":
            return cur_ref[0, t * tile:(t + 1) * tile, hs]
        if t == 0:
            return jnp.concatenate([prev_ref[0, :, hs], cur_ref[0, :tile, hs]], axis=0)
        return cur_ref[0, (t - 1) * tile:(t + 1) * tile, hs]

    def stage_scores(t, g, kind):
        nblk = 2 if kind == "near" else 1
        q = q_ref[0, t * tile:(t + 1) * tile, g * SB_DH:(g + 1) * SB_DH]
        z = _dot_nt(q, window_keys(kprev_ref, kcur_ref, kfar_ref, t, g, kind))
        if kind != "far":
            diag = jnp.where(causal, z[:, -tile:], SB_MASKED_LOGIT)
            z = diag if nblk == 1 else jnp.concatenate([z[:, :-tile], diag], axis=1)
        sp = jnp.maximum(z, jnp.log(1.0 + jnp.exp2(jnp.minimum(z, SB_EXP2_CLAMP))) * LOG2E)
        blocks = [sp[:, j * tile:(j + 1) * tile] for j in range(nblk)]
        return z, blocks, jnp.concatenate([blk.astype(BF16) for blk in blocks], axis=0)

    def stage_weights(scores, kind, penalty):
        z, blocks, sp_bf16 = scores
        nblk = len(blocks)
        suffix = _dot(sp_bf16, suffix_mat)
        weights = [None] * nblk
        for j in reversed(range(nblk)):
            sfx = suffix[j * tile:(j + 1) * tile]
            log_a = (z[:, j * tile:(j + 1) * tile] - blocks[j]) - sfx
            if penalty is not None:
                log_a = log_a - penalty
            weights[j] = jnp.exp2(log_a).astype(BF16)
            total = sfx[:, 0:1] + blocks[j][:, 0:1]
            penalty = total if penalty is None else penalty + total
        return jnp.concatenate(weights, axis=1), penalty

    def stage_values(weights, t, g, kind):
        return _dot(weights, window_keys(vprev_ref, vcur_ref, vfar_ref, t, g, kind))

    def run_windows(wins, penalties=None, other_tasks=()):
        n = len(wins)
        scores, weights, out = {}, {}, [None] * n

        def stagger(s):
            def task():
                if s < n:
                    scores[s] = stage_scores(*wins[s])
                w = s - 1
                if 0 <= w < n:
                    weights[w] = stage_weights(scores.pop(w), wins[w][2],
                                               None if penalties is None else penalties[w])
                w = s - 2
                if 0 <= w < n:
                    a, penalty = weights.pop(w)
                    out[w] = (stage_values(a, *wins[w]), penalty)
            return task

        steps = [stagger(s) for s in range(n + 2)]
        others = list(other_tasks)
        n_front = (len(others) + 1) // 2
        for task in steps[:1] + others[:n_front] + steps[1:-1] + others[n_front:] + steps[-1:]:
            task()
        return out

    cur = lax.rem(step, 2)
    n_out = o_ref.shape[2] // OUT_PIECE_B

    def out_piece(j):
        def task():
            sl = slice(j * OUT_PIECE_B, (j + 1) * OUT_PIECE_B)
            o_ref[0, :, sl] = x_ref[0, :, sl] + _dot(cat_ref[1 - cur], wout_ref[0, :, sl])
        return task

    out_tasks = [out_piece(j) for j in range(n_out)]

    def store(t, accs):
        rows = slice(t * tile, (t + 1) * tile)
        for g in range(heads):
            hs = slice(g * SB_DH, (g + 1) * SB_DH)
            cat_ref[cur, rows, hs] = (accs[g] * gate_ref[0, rows, hs].astype(F32)).astype(BF16)

    def log_weight_bound(pens):
        return -functools.reduce(jnp.minimum, [jnp.min(p) for p in pens])

    def fetch_far(start):
        copies = [pltpu.make_async_copy(src.at[batch_idx, pl.ds(start, tile), :], dst, far_sem.at[n])
                  for n, (src, dst) in enumerate(((k_hbm, kfar_ref), (v_hbm, vfar_ref)))]
        for c in copies:
            c.start()
        for c in copies:
            c.wait()

    def finish_tile(t, results):
        accs = [r[0] for r in results]
        penalties = [r[1] for r in results]
        store(t, accs)
        tile_idx = step * tiles_per_step + t
        bound = log_weight_bound(penalties)

        def cond(carry):
            return jnp.logical_and(carry[0] <= tile_idx, carry[1] > SB_LOG_WEIGHT_FLOOR)

        def body(carry):
            n, _, accs, pens = carry
            fetch_far(pl.multiple_of((tile_idx - n) * tile, tile))
            far = run_windows([(t, g, "far") for g in range(heads)], pens)
            accs = [accs[g] + far[g][0] for g in range(heads)]
            pens = [far[g][1] for g in range(heads)]
            return n + 1, log_weight_bound(pens), accs, pens

        @pl.when(jnp.logical_and(tile_idx >= 2, bound > SB_LOG_WEIGHT_FLOOR))
        def _():
            _, _, accs_far, _ = lax.while_loop(cond, body, (jnp.int32(2), bound, accs, penalties))
            store(t, accs_far)

    def attention(first_kind, other_tasks):
        wins = [(t, g, first_kind if t == 0 else "near")
                for t in range(tiles_per_step) for g in range(heads)]
        results = run_windows(wins, None, other_tasks)
        cat_ref[cur, :, SB_W:] = mo_ref[0]
        for t in range(tiles_per_step):
            finish_tile(t, results[t * heads:(t + 1) * heads])

    @pl.when(step == 0)
    def _():
        attention("diag", ())

    @pl.when(jnp.logical_and(step > 0, step < n_steps))
    def _():
        attention("near", out_tasks)

    @pl.when(step == n_steps)
    def _():
        for task in out_tasks:
            task()


def _sb_out(x, q, k, v, gate, mo, w_out_bf16, lb):
    batch, seq, _ = x.shape
    rows = ROW_TILE_SB
    n_steps = seq // rows
    tiles_per_step = rows // SB_TILE
    b_out_in = w_out_bf16.shape[1]
    attn_idx = lambda i: jnp.minimum(i, n_steps - 1)
    attn_spec = lambda w: pl.BlockSpec((1, rows, w), lambda b, i: (b, attn_idx(i), 0))
    proj_spec = lambda w: pl.BlockSpec((1, rows, w), lambda b, i: (b, jnp.maximum(i - 1, 0), 0))
    prev_spec = pl.BlockSpec(
        (1, SB_TILE, SB_W), lambda b, i: (b, jnp.maximum(attn_idx(i) * tiles_per_step - 1, 0), 0))
    any_spec = pl.BlockSpec(memory_space=pl.ANY)
    return pl.pallas_call(
        functools.partial(_sb_out_kernel, n_steps=n_steps),
        grid=(batch, n_steps + 1),
        in_specs=[
            proj_spec(D_MODEL), attn_spec(SB_W), prev_spec, attn_spec(SB_W), prev_spec,
            attn_spec(SB_W), any_spec, any_spec, attn_spec(SB_W), attn_spec(MEM_W),
            _const_spec((1, b_out_in, D_MODEL), lambda b, i: (lb, 0, 0)),
        ],
        out_specs=proj_spec(D_MODEL),
        out_shape=jax.ShapeDtypeStruct(x.shape, F32),
        scratch_shapes=[pltpu.VMEM((2, rows, b_out_in), BF16),
                        pltpu.VMEM((SB_TILE, SB_W), BF16),
                        pltpu.VMEM((SB_TILE, SB_W), BF16),
                        pltpu.SemaphoreType.DMA((2,))],
        compiler_params=pltpu.CompilerParams(
            dimension_semantics=("arbitrary", "arbitrary"),
            vmem_limit_bytes=VMEM_LIMIT_BYTES),
        name="sb_out",
    )(x, q, k, k, v, v, k, v, gate, mo, w_out_bf16)


def kernel(x, mem, positions, g_norm_a, w_in_a, g_ret_head, w_out_a, g_kv, w_kv, g_norm_b, w_in_b,
           w_out_b, g_mem, w_mem_kv, g_mem_q, g_mem_k):
    n_a = g_norm_a.shape[0]
    n_b = g_norm_b.shape[0]
    batch, seq, _ = x.shape

    flat = lambda w: w.reshape(-1, w.shape[-1])
    kt, mv, casted = _memory_kv(mem, g_mem, w_mem_kv, g_mem_k, [flat(w_in_a), flat(w_out_a)])
    w_in_a_bf16, w_out_a_bf16 = (c.reshape(w.shape) for c, w in zip(casted, (w_in_a, w_out_a)))

    inv_freq = ROPE_BASE ** (-jnp.arange(0, RET_DK // 2, dtype=F32) * 2.0 / RET_DK)
    invf = inv_freq.reshape(RET_DK // 2, 1)
    pos_f32 = positions.astype(F32).reshape(batch, 1, seq)

    later = [w_in_b, w_kv]
    w_out_b_bf16 = w_out_b.astype(BF16)
    later_bf16 = None
    for la in range(n_a):
        x, casted = _layer_a(x, pos_f32, invf, g_norm_a, w_in_a_bf16, g_ret_head, w_out_a_bf16,
                             kt, mv, g_mem_q, la, la,
                             [flat(w) for w in later] if la == 0 else ())
        later_bf16 = later_bf16 or casted
    w_in_b_bf16, w_kv_bf16 = (c.reshape(w.shape) for c, w in zip(later_bf16, later))

    k_shared = v_shared = None
    for lb in range(n_b):
        layer = n_a + lb
        q, k_new, v_new, gate, mo = _proj_b(x, g_kv, g_norm_b, w_kv_bf16, w_in_b_bf16, kt, mv,
                                            g_mem_q, lb, layer)
        if lb == 0:
            k_shared, v_shared = k_new, v_new
        x = _sb_out(x, q, k_shared, v_shared, gate, mo, w_out_b_bf16, lb)
    return x
```

```python
import functools
import math

import numpy as np
import jax
import jax.numpy as jnp
from jax import lax
from jax.experimental import pallas as pl
from jax.experimental.pallas import tpu as pltpu

F32 = jnp.float32
BF16 = jnp.bfloat16

D_MODEL = 1024
RET_HEADS = 4
RET_DK = 128
RET_DV = 256
RET_QK_W = RET_HEADS * RET_DK
RET_V_W = RET_HEADS * RET_DV
ROPE_BASE = 10000.0
SB_HEADS = 8
SB_DH = 128
SB_W = SB_HEADS * SB_DH
MEM_HEADS = 4
MEM_DH = 128
MEM_W = MEM_HEADS * MEM_DH
EPS = 1e-6

RET_LOG_GAMMA = tuple(
    math.log(float(np.float32(1.0 - 2.0 ** (-5.0 - h)))) for h in range(RET_HEADS))

VMEM_LIMIT_BYTES = 48 * 1024 * 1024

ROW_TILE_A = 1024
RET_CHUNK = 256
IN_PIECE_A = 512
OUT_PIECE_A = 256
ROW_TILE_B = 512
OUT_PIECE_B = 256
SB_TILE = 256
ROW_TILE_SB = 512
SB_LOG_WEIGHT_FLOOR = -152.0
SB_EXP2_CLAMP = 64.0
SB_MASKED_LOGIT = -1e30
LOG2E = math.log2(math.e)
SB_Q_SCALE = LOG2E * SB_DH ** -0.5


def _rms(x, g):
    return x * lax.rsqrt(jnp.mean(x * x, axis=-1, keepdims=True) + EPS) * g


def _silu(x):
    return x * (1.0 / (1.0 + jnp.exp(-x)))


def _dot(a, b):
    return jnp.dot(a, b, preferred_element_type=F32)


def _dot_nt(a, b):
    return lax.dot_general(a, b, (((1,), (1,)), ((), ())), preferred_element_type=F32)


def _const_spec(shape, index_map):
    return pl.BlockSpec(shape, index_map, pipeline_mode=pl.Buffered(1))


def _interleaved(*task_lists):
    tagged = [((i + 0.5) / len(tasks), k, task)
              for k, tasks in enumerate(task_lists) for i, task in enumerate(tasks)]
    return [task for _, _, task in sorted(tagged, key=lambda e: e[:2])]


def _memkv_kernel(*refs, n_cast):
    cast_in = refs[:n_cast]
    mem_ref, g_ref, w_ref, gk_ref = refs[n_cast:n_cast + 4]
    cast_out = refs[n_cast + 4:2 * n_cast + 4]
    kt_ref, v_ref = refs[2 * n_cast + 4:]
    for src, dst in zip(cast_in, cast_out):
        dst[...] = src[...].astype(BF16)

    mn = _rms(mem_ref[0], g_ref[0]).astype(BF16)
    kv = _dot(mn, w_ref[0].astype(BF16))
    gk = gk_ref[0]
    for h in range(MEM_HEADS):
        kh = _rms(kv[:, h * MEM_DH:(h + 1) * MEM_DH], gk)
        kt_ref[0, 0, h * MEM_DH:(h + 1) * MEM_DH, :] = kh.T.astype(BF16)
    v_ref[0, 0] = kv[:, MEM_W:].astype(BF16)


def _memory_kv(mem, g_mem, w_mem_kv, g_mem_k, later_weights=()):
    depth = g_mem.shape[0]
    batch, mem_len, _ = mem.shape
    n_steps = depth * batch
    cast_specs = [pl.BlockSpec((w.shape[0] // n_steps, w.shape[1]), lambda l, b: (l * batch + b, 0))
                  for w in later_weights]
    outs = pl.pallas_call(
        functools.partial(_memkv_kernel, n_cast=len(later_weights)),
        grid=(depth, batch),
        in_specs=cast_specs + [
            pl.BlockSpec((1, mem_len, D_MODEL), lambda l, b: (b, 0, 0)),
            pl.BlockSpec((1, 1, D_MODEL), lambda l, b: (l, 0, 0)),
            pl.BlockSpec((1, D_MODEL, 2 * MEM_W), lambda l, b: (l, 0, 0)),
            pl.BlockSpec((1, 1, MEM_DH), lambda l, b: (l, 0, 0)),
        ],
        out_specs=cast_specs + [
            pl.BlockSpec((1, 1, MEM_W, mem_len), lambda l, b: (l, b, 0, 0)),
            pl.BlockSpec((1, 1, mem_len, MEM_W), lambda l, b: (l, b, 0, 0)),
        ],
        out_shape=[jax.ShapeDtypeStruct(w.shape, BF16) for w in later_weights] + [
            jax.ShapeDtypeStruct((depth, batch, MEM_W, mem_len), BF16),
            jax.ShapeDtypeStruct((depth, batch, mem_len, MEM_W), BF16),
        ],
        compiler_params=pltpu.CompilerParams(
            dimension_semantics=("arbitrary", "arbitrary"),
            vmem_limit_bytes=VMEM_LIMIT_BYTES),
        name="memkv",
    )(*later_weights, mem, g_mem.reshape(depth, 1, D_MODEL), w_mem_kv,
      g_mem_k.reshape(depth, 1, MEM_DH))
    return outs[-2], outs[-1], outs[:-2]


def _mem_logits(h, memq, kt_ref, gq):
    qn = _rms(memq, gq).astype(BF16)
    return _dot(qn, kt_ref[0, 0, h * MEM_DH:(h + 1) * MEM_DH, :]) * (MEM_DH ** -0.5)


def _mem_values(h, logits, mv_ref):
    e = jnp.exp(logits - jnp.max(logits, axis=-1, keepdims=True))
    return _dot(e.astype(BF16), mv_ref[0, 0, :, h * MEM_DH:(h + 1) * MEM_DH]), \
        jnp.sum(e, axis=-1, keepdims=True)


def _mem_gated(values, memgate):
    o, denom = values
    return (o / denom * _silu(memgate)).astype(BF16)


def _layer_a_kernel(*refs, n_cast):
    n_in_refs = 10
    cast_in = refs[:n_cast]
    (x_ref, pos_ref, invf_ref, gn_ref, win_ref, gret_ref, wout_ref,
     kt_ref, mv_ref, gq_ref) = refs[n_cast:n_cast + n_in_refs]
    cast_out = refs[n_cast + n_in_refs:2 * n_cast + n_in_refs]
    o_ref, state_ref = refs[2 * n_cast + n_in_refs:]
    for src, dst in zip(cast_in, cast_out):
        dst[...] = src[...].astype(BF16)

    tile = x_ref.shape[1]
    chunk = RET_CHUNK
    n_chunks = tile // chunk
    half = RET_DK // 2
    n_in = win_ref.shape[2] // IN_PIECE_A
    n_out = wout_ref.shape[2] // OUT_PIECE_A
    o_v = 2 * RET_QK_W
    o_g = o_v + RET_V_W
    o_mq = o_g + RET_V_W

    @pl.when(pl.program_id(1) == 0)
    def _():
        state_ref[...] = jnp.zeros_like(state_ref)

    tables = {}
    states = [state_ref[h] for h in range(RET_HEADS)]
    xs, xns = {}, {}
    u = {}
    mixed = {}
    cats = {}

    def rotary_tables():
        ang_t = invf_ref[...] * pos_ref[0]
        cos_sin = jnp.concatenate([jnp.cos(ang_t), jnp.sin(ang_t)], axis=0).T
        sin_cos = pltpu.roll(cos_sin, half, 1)
        low_lanes = lax.broadcasted_iota(jnp.int32, (tile, RET_DK), 1) < half
        tables["cos"] = jnp.where(low_lanes, cos_sin, sin_cos)
        tables["sin"] = jnp.where(low_lanes, -sin_cos, cos_sin)

    def decay_tables(h):
        def task():
            lg = RET_LOG_GAMMA[h]
            rel = (lax.broadcasted_iota(jnp.int32, (chunk, chunk), 0)
                   - lax.broadcasted_iota(jnp.int32, (chunk, chunk), 1)).astype(F32)
            idx = lax.broadcasted_iota(jnp.int32, (chunk, 1), 0).astype(F32)
            tables[h] = (jnp.where(rel >= 0.0, jnp.exp(jnp.maximum(rel, 0.0) * lg), 0.0),
                         jnp.exp((idx + 1.0) * lg),
                         jnp.exp((chunk - 1.0 - idx) * lg))
        return task

    def normalize(c):
        xs[c] = x_ref[0, c * chunk:(c + 1) * chunk, :]
        xns[c] = _rms(xs[c], gn_ref[0]).astype(BF16)

    def in_piece(c, p):
        def task():
            u[c, p] = _dot(xns[c], win_ref[0, :, p * IN_PIECE_A:(p + 1) * IN_PIECE_A])
        return task

    def cols(c, lo, width):
        p, off = divmod(lo, IN_PIECE_A)
        return u[c, p][:, off:off + width]

    part = {}

    def retention_scores(c, h):
        def task():
            rows = slice(c * chunk, (c + 1) * chunk)
            cos_c, sin_c = tables["cos"][rows], tables["sin"][rows]

            def rotary(t):
                return t * cos_c + pltpu.roll(t, half, 1) * sin_c

            q = rotary(cols(c, h * RET_DK, RET_DK)).astype(BF16)
            k = rotary(cols(c, RET_QK_W + h * RET_DK, RET_DK)) * (RET_DK ** -0.5)
            part[c, h, "q"], part[c, h, "k"] = q, k
            part[c, h, "scores"] = _dot_nt(q, k.astype(BF16))
        return task

    def retention_values(c, h):
        def task():
            decay, cross_decay, state_decay = tables[h]
            q, k = part.pop((c, h, "q")), part.pop((c, h, "k"))
            v = cols(c, o_v + h * RET_DV, RET_DV).astype(BF16)
            intra = _dot((part.pop((c, h, "scores")) * decay).astype(BF16), v)
            cross = _dot(q, states[h].astype(BF16)) * cross_decay
            states[h] = (math.exp(chunk * RET_LOG_GAMMA[h]) * states[h]
                         + _dot((k * state_decay).T.astype(BF16), v))
            part[c, h, "ret"] = intra + cross
        return task

    def retention_gated(c, h):
        def task():
            ret = _rms(part.pop((c, h, "ret")), gret_ref[0])
            mixed[c, h] = (ret * _silu(cols(c, o_g + h * RET_DV, RET_DV))).astype(BF16)
        return task

    def memory_logits(c, h):
        def task():
            part[c, h, "logits"] = _mem_logits(h, cols(c, o_mq + h * MEM_DH, MEM_DH),
                                               kt_ref, gq_ref[0])
        return task

    def memory_values(c, h):
        def task():
            part[c, h, "values"] = _mem_values(h, part.pop((c, h, "logits")), mv_ref)
        return task

    def memory_gated(c, h):
        def task():
            mixed[c, RET_HEADS + h] = _mem_gated(part.pop((c, h, "values")),
                                                 cols(c, o_mq + MEM_W + h * MEM_DH, MEM_DH))
        return task

    def mixer_groups(c):
        groups = [[retention_scores(c, h) for h in range(RET_HEADS)],
                  [memory_logits(c, h) for h in range(MEM_HEADS)],
                  [retention_values(c, h) for h in range(RET_HEADS)],
                  [memory_values(c, h) for h in range(MEM_HEADS)],
                  [retention_gated(c, h) for h in range(RET_HEADS)]
                  + [memory_gated(c, h) for h in range(MEM_HEADS)]]

        def run(group):
            def task():
                for t in group:
                    t()
            return task
        return [run(g) for g in groups]

    def out_piece(c, j):
        def task():
            if c not in cats:
                cats[c] = jnp.concatenate(
                    [mixed.pop((c, s)) for s in range(RET_HEADS + MEM_HEADS)], axis=1)
            sl = slice(j * OUT_PIECE_A, (j + 1) * OUT_PIECE_A)
            o_ref[0, c * chunk:(c + 1) * chunk, sl] = xs[c][:, sl] + _dot(cats[c], wout_ref[0, :, sl])
        return task

    setup = [rotary_tables] + [decay_tables(h) for h in range(RET_HEADS)]
    for step in range(n_chunks + 2):
        stages = [setup] if step == 0 else []
        if step < n_chunks:
            normalize(step)
            stages.append([in_piece(step, p) for p in range(n_in)])
        if 0 <= step - 1 < n_chunks:
            stages.append(mixer_groups(step - 1))
        if 0 <= step - 2 < n_chunks:
            stages.append([out_piece(step - 2, j) for j in range(n_out)])
        for task in _interleaved(*stages):
            task()

    for h in range(RET_HEADS):
        state_ref[h] = states[h]


def _layer_a(x, pos_f32, invf, g_norm, w_in_bf16, g_ret_head, w_out_bf16, kt, mv, g_mem_q,
             la, layer, later_weights=()):
    batch, seq, _ = x.shape
    tile = ROW_TILE_A
    n_a = g_norm.shape[0]
    depth = g_mem_q.shape[0]
    a_in_w = w_in_bf16.shape[2]
    a_out_in = w_out_bf16.shape[1]
    mem_len = mv.shape[2]
    steps_per_batch = seq // tile
    n_steps = batch * steps_per_batch
    cast_specs = [pl.BlockSpec((w.shape[0] // n_steps, w.shape[1]),
                               lambda b, t: (b * steps_per_batch + t, 0)) for w in later_weights]
    outs = pl.pallas_call(
        functools.partial(_layer_a_kernel, n_cast=len(later_weights)),
        grid=(batch, steps_per_batch),
        in_specs=cast_specs + [
            pl.BlockSpec((1, tile, D_MODEL), lambda b, t: (b, t, 0)),
            pl.BlockSpec((1, 1, tile), lambda b, t: (b, 0, t)),
            _const_spec((RET_DK // 2, 1), lambda b, t: (0, 0)),
            _const_spec((1, 1, D_MODEL), lambda b, t: (la, 0, 0)),
            _const_spec((1, D_MODEL, a_in_w), lambda b, t: (la, 0, 0)),
            _const_spec((1, 1, RET_DV), lambda b, t: (la, 0, 0)),
            _const_spec((1, a_out_in, D_MODEL), lambda b, t: (la, 0, 0)),
            pl.BlockSpec((1, 1, MEM_W, mem_len), lambda b, t: (layer, b, 0, 0)),
            pl.BlockSpec((1, 1, mem_len, MEM_W), lambda b, t: (layer, b, 0, 0)),
            _const_spec((1, 1, MEM_DH), lambda b, t: (layer, 0, 0)),
        ],
        out_specs=cast_specs + [pl.BlockSpec((1, tile, D_MODEL), lambda b, t: (b, t, 0))],
        out_shape=[jax.ShapeDtypeStruct(w.shape, BF16) for w in later_weights]
        + [jax.ShapeDtypeStruct(x.shape, F32)],
        scratch_shapes=[pltpu.VMEM((RET_HEADS, RET_DK, RET_DV), F32)],
        compiler_params=pltpu.CompilerParams(
            dimension_semantics=("arbitrary", "arbitrary"),
            vmem_limit_bytes=VMEM_LIMIT_BYTES),
        name="layer_a",
    )(*later_weights, x, pos_f32, invf, g_norm.reshape(n_a, 1, D_MODEL), w_in_bf16,
      g_ret_head.reshape(n_a, 1, RET_DV), w_out_bf16, kt, mv, g_mem_q.reshape(depth, 1, MEM_DH))
    return outs[-1], outs[:-1]


def _proj_b_kernel(x_ref, gkv_ref, gnb_ref, wkv_ref, win_ref, kt_ref, mv_ref, gq_ref,
                   q_ref, k_ref, v_ref, gate_ref, mo_ref):
    x = x_ref[0]
    xr = x * lax.rsqrt(jnp.mean(x * x, axis=-1, keepdims=True) + EPS)
    xkv = (xr * gkv_ref[...]).astype(BF16)
    xb = (xr * gnb_ref[0]).astype(BF16)
    heads = range(MEM_HEADS)
    head = lambda t, h: t[:, h * MEM_DH:(h + 1) * MEM_DH]
    o_mq = 2 * SB_W
    memq = _dot(xb, win_ref[0, :, o_mq:o_mq + MEM_W])
    memgate = _dot(xb, win_ref[0, :, o_mq + MEM_W:])
    logits = [_mem_logits(h, head(memq, h), kt_ref, gq_ref[0]) for h in heads]
    gate_ref[0] = _silu(_dot(xb, win_ref[0, :, SB_W:o_mq])).astype(BF16)
    values = [_mem_values(h, logits[h], mv_ref) for h in heads]
    k_ref[0] = _dot(xkv, wkv_ref[:, :SB_W]).astype(BF16)
    mo_ref[0] = jnp.concatenate([_mem_gated(values[h], head(memgate, h)) for h in heads], axis=1)
    v_ref[0] = _dot(xkv, wkv_ref[:, SB_W:]).astype(BF16)
    q_ref[0] = (_dot(xb, win_ref[0, :, :SB_W]) * SB_Q_SCALE).astype(BF16)


def _proj_b(x, g_kv, g_norm_b, w_kv_bf16, w_in_bf16, kt, mv, g_mem_q, lb, layer):
    batch, seq, _ = x.shape
    tile = ROW_TILE_B
    n_b = g_norm_b.shape[0]
    depth = g_mem_q.shape[0]
    mem_len = mv.shape[2]
    b_in_w = w_in_bf16.shape[2]
    row_spec = lambda w: pl.BlockSpec((1, tile, w), lambda b, t: (b, t, 0))
    widths = (SB_W, SB_W, SB_W, SB_W, MEM_W)
    return pl.pallas_call(
        _proj_b_kernel,
        grid=(batch, seq // tile),
        in_specs=[
            row_spec(D_MODEL),
            _const_spec((1, D_MODEL), lambda b, t: (0, 0)),
            _const_spec((1, 1, D_MODEL), lambda b, t: (lb, 0, 0)),
            _const_spec((D_MODEL, 2 * SB_W), lambda b, t: (0, 0)),
            _const_spec((1, D_MODEL, b_in_w), lambda b, t: (lb, 0, 0)),
            pl.BlockSpec((1, 1, MEM_W, mem_len), lambda b, t: (layer, b, 0, 0)),
            pl.BlockSpec((1, 1, mem_len, MEM_W), lambda b, t: (layer, b, 0, 0)),
            _const_spec((1, 1, MEM_DH), lambda b, t: (layer, 0, 0)),
        ],
        out_specs=[row_spec(w) for w in widths],
        out_shape=[jax.ShapeDtypeStruct((batch, seq, w), BF16) for w in widths],
        compiler_params=pltpu.CompilerParams(
            dimension_semantics=("arbitrary", "arbitrary"),
            vmem_limit_bytes=VMEM_LIMIT_BYTES),
        name="proj_b",
    )(x, g_kv.reshape(1, D_MODEL), g_norm_b.reshape(n_b, 1, D_MODEL), w_kv_bf16, w_in_bf16,
      kt, mv, g_mem_q.reshape(depth, 1, MEM_DH))


def _sb_out_kernel(x_ref, q_ref, kprev_ref, kcur_ref, vprev_ref, vcur_ref, k_hbm, v_hbm,
                   gate_ref, mo_ref, wout_ref, o_ref, cat_ref, kfar_ref, vfar_ref, far_sem, *,
                   n_steps):
    tile = SB_TILE
    tiles_per_step = q_ref.shape[1] // tile
    heads = q_ref.shape[2] // SB_DH
    batch_idx = pl.program_id(0)
    step = pl.program_id(1)

    row = lax.broadcasted_iota(jnp.int32, (tile, tile), 0)
    col = lax.broadcasted_iota(jnp.int32, (tile, tile), 1)
    causal = col < row
    suffix_mat = jnp.where(row > col, 1.0, 0.0).astype(BF16)


    def window_keys(prev_ref, cur_ref, far_ref, t, g, kind):
        hs = slice(g * SB_DH, (g + 1) * SB_DH)
        if kind == "far":
            return far_ref[:, hs]
        if kind == "diag":
            return cur_ref[0, t * tile:(t + 1) * tile, hs]
        if t == 0:
            return jnp.concatenate([prev_ref[0, :, hs], cur_ref[0, :tile, hs]], axis=0)
        return cur_ref[0, (t - 1) * tile:(t + 1) * tile, hs]

    def stage_scores(t, g, kind):
        nblk = 2 if kind == "near" else 1
        q = q_ref[0, t * tile:(t + 1) * tile, g * SB_DH:(g + 1) * SB_DH]
        z = _dot_nt(q, window_keys(kprev_ref, kcur_ref, kfar_ref, t, g, kind))
        if kind != "far":
            diag = jnp.where(causal, z[:, -tile:], SB_MASKED_LOGIT)
            z = diag if nblk == 1 else jnp.concatenate([z[:, :-tile], diag], axis=1)
        sp = jnp.maximum(z, jnp.log(1.0 + jnp.exp2(jnp.minimum(z, SB_EXP2_CLAMP))) * LOG2E)
        blocks = [sp[:, j * tile:(j + 1) * tile] for j in range(nblk)]
        return (z - sp, [blk[:, 0:1] for blk in blocks],
                jnp.concatenate([blk.astype(BF16) for blk in blocks], axis=0))

    def stage_weights(scores, kind, penalty):
        log_beta, first_sp, sp_bf16 = scores
        nblk = len(first_sp)
        suffix = _dot(sp_bf16, suffix_mat)
        weights = [None] * nblk
        for j in reversed(range(nblk)):
            sfx = suffix[j * tile:(j + 1) * tile]
            log_a = log_beta[:, j * tile:(j + 1) * tile] - sfx
            if penalty is not None:
                log_a = log_a - penalty
            weights[j] = jnp.exp2(log_a).astype(BF16)
            total = sfx[:, 0:1] + first_sp[j]
            penalty = total if penalty is None else penalty + total
        return jnp.concatenate(weights, axis=1), penalty

    def stage_values(weights, t, g, kind):
        return _dot(weights, window_keys(vprev_ref, vcur_ref, vfar_ref, t, g, kind))

    def run_windows(wins, penalties=None, other_tasks=()):
        n = len(wins)
        scores, weights, out = {}, {}, [None] * n

        def stagger(s):
            def task():
                if s < n:
                    scores[s] = stage_scores(*wins[s])
                w = s - 1
                if 0 <= w < n:
                    weights[w] = stage_weights(scores.pop(w), wins[w][2],
                                               None if penalties is None else penalties[w])
                w = s - 2
                if 0 <= w < n:
                    a, penalty = weights.pop(w)
                    out[w] = (stage_values(a, *wins[w]), penalty)
            return task

        steps = [stagger(s) for s in range(n + 2)]
        others = list(other_tasks)
        n_front = (len(others) + 1) // 2
        for task in steps[:1] + others[:n_front] + steps[1:-1] + others[n_front:] + steps[-1:]:
            task()
        return out

    cur = lax.rem(step, 2)
    n_out = o_ref.shape[2] // OUT_PIECE_B

    def out_piece(j):
        def task():
            sl = slice(j * OUT_PIECE_B, (j + 1) * OUT_PIECE_B)
            o_ref[0, :, sl] = x_ref[0, :, sl] + _dot(cat_ref[1 - cur], wout_ref[0, :, sl])
        return task

    out_tasks = [out_piece(j) for j in range(n_out)]

    def store(t, accs):
        rows = slice(t * tile, (t + 1) * tile)
        for g in range(heads):
            hs = slice(g * SB_DH, (g + 1) * SB_DH)
            cat_ref[cur, rows, hs] = (accs[g] * gate_ref[0, rows, hs].astype(F32)).astype(BF16)

    def log_weight_bound(pens):
        return -functools.reduce(jnp.minimum, [jnp.min(p) for p in pens])

    def fetch_far(start):
        copies = [pltpu.make_async_copy(src.at[batch_idx, pl.ds(start, tile), :], dst, far_sem.at[n])
                  for n, (src, dst) in enumerate(((k_hbm, kfar_ref), (v_hbm, vfar_ref)))]
        for c in copies:
            c.start()
        for c in copies:
            c.wait()

    def finish_tile(t, results):
        accs = [r[0] for r in results]
        penalties = [r[1] for r in results]
        store(t, accs)
        tile_idx = step * tiles_per_step + t
        bound = log_weight_bound(penalties)

        def cond(carry):
            return jnp.logical_and(carry[0] <= tile_idx, carry[1] > SB_LOG_WEIGHT_FLOOR)

        def body(carry):
            n, _, accs, pens = carry
            fetch_far(pl.multiple_of((tile_idx - n) * tile, tile))
            far = run_windows([(t, g, "far") for g in range(heads)], pens)
            accs = [accs[g] + far[g][0] for g in range(heads)]
            pens = [far[g][1] for g in range(heads)]
            return n + 1, log_weight_bound(pens), accs, pens

        @pl.when(jnp.logical_and(tile_idx >= 2, bound > SB_LOG_WEIGHT_FLOOR))
        def _():
            _, _, accs_far, _ = lax.while_loop(cond, body, (jnp.int32(2), bound, accs, penalties))
            store(t, accs_far)

    def attention(first_kind, other_tasks):
        wins = [(t, g, first_kind if t == 0 else "near")
                for t in range(tiles_per_step) for g in range(heads)]
        results = run_windows(wins, None, other_tasks)
        cat_ref[cur, :, SB_W:] = mo_ref[0]
        for t in range(tiles_per_step):
            finish_tile(t, results[t * heads:(t + 1) * heads])

    @pl.when(step == 0)
    def _():
        attention("diag", ())

    @pl.when(jnp.logical_and(step > 0, step < n_steps))
    def _():
        attention("near", out_tasks)

    @pl.when(step == n_steps)
    def _():
        for task in out_tasks:
            task()


def _sb_out(x, q, k, v, gate, mo, w_out_bf16, lb):
    batch, seq, _ = x.shape
    rows = ROW_TILE_SB
    n_steps = seq // rows
    tiles_per_step = rows // SB_TILE
    b_out_in = w_out_bf16.shape[1]
    attn_idx = lambda i: jnp.minimum(i, n_steps - 1)
    attn_spec = lambda w: pl.BlockSpec((1, rows, w), lambda b, i: (b, attn_idx(i), 0))
    proj_spec = lambda w: pl.BlockSpec((1, rows, w), lambda b, i: (b, jnp.maximum(i - 1, 0), 0))
    prev_spec = pl.BlockSpec(
        (1, SB_TILE, SB_W), lambda b, i: (b, jnp.maximum(attn_idx(i) * tiles_per_step - 1, 0), 0))
    any_spec = pl.BlockSpec(memory_space=pl.ANY)
    return pl.pallas_call(
        functools.partial(_sb_out_kernel, n_steps=n_steps),
        grid=(batch, n_steps + 1),
        in_specs=[
            proj_spec(D_MODEL), attn_spec(SB_W), prev_spec, attn_spec(SB_W), prev_spec,
            attn_spec(SB_W), any_spec, any_spec, attn_spec(SB_W), attn_spec(MEM_W),
            _const_spec((1, b_out_in, D_MODEL), lambda b, i: (lb, 0, 0)),
        ],
        out_specs=proj_spec(D_MODEL),
        out_shape=jax.ShapeDtypeStruct(x.shape, F32),
        scratch_shapes=[pltpu.VMEM((2, rows, b_out_in), BF16),
                        pltpu.VMEM((SB_TILE, SB_W), BF16),
                        pltpu.VMEM((SB_TILE, SB_W), BF16),
                        pltpu.SemaphoreType.DMA((2,))],
        compiler_params=pltpu.CompilerParams(
            dimension_semantics=("arbitrary", "arbitrary"),
            vmem_limit_bytes=VMEM_LIMIT_BYTES),
        name="sb_out",
    )(x, q, k, k, v, v, k, v, gate, mo, w_out_bf16)


def kernel(x, mem, positions, g_norm_a, w_in_a, g_ret_head, w_out_a, g_kv, w_kv, g_norm_b, w_in_b,
           w_out_b, g_mem, w_mem_kv, g_mem_q, g_mem_k):
    n_a = g_norm_a.shape[0]
    n_b = g_norm_b.shape[0]
    batch, seq, _ = x.shape

    flat = lambda w: w.reshape(-1, w.shape[-1])
    kt, mv, casted = _memory_kv(mem, g_mem, w_mem_kv, g_mem_k, [flat(w_in_a), flat(w_out_a)])
    w_in_a_bf16, w_out_a_bf16 = (c.reshape(w.shape) for c, w in zip(casted, (w_in_a, w_out_a)))

    inv_freq = ROPE_BASE ** (-jnp.arange(0, RET_DK // 2, dtype=F32) * 2.0 / RET_DK)
    invf = inv_freq.reshape(RET_DK // 2, 1)
    pos_f32 = positions.astype(F32).reshape(batch, 1, seq)

    later = [w_in_b, w_out_b, w_kv]
    later_bf16 = None
    for la in range(n_a):
        x, casted = _layer_a(x, pos_f32, invf, g_norm_a, w_in_a_bf16, g_ret_head, w_out_a_bf16,
                             kt, mv, g_mem_q, la, la,
                             [flat(w) for w in later] if la == 0 else ())
        later_bf16 = later_bf16 or casted
    w_in_b_bf16, w_out_b_bf16, w_kv_bf16 = (c.reshape(w.shape) for c, w in zip(later_bf16, later))

    k_shared = v_shared = None
    for lb in range(n_b):
        layer = n_a + lb
        q, k_new, v_new, gate, mo = _proj_b(x, g_kv, g_norm_b, w_kv_bf16, w_in_b_bf16, kt, mv,
                                            g_mem_q, lb, layer)
        if lb == 0:
            k_shared, v_shared = k_new, v_new
        x = _sb_out(x, q, k_shared, v_shared, gate, mo, w_out_b_bf16, lb)
    return x
```

```python
import functools
import math

import numpy as np
import jax
import jax.numpy as jnp
from jax import lax
from jax.experimental import pallas as pl
from jax.experimental.pallas import tpu as pltpu

F32 = jnp.float32
BF16 = jnp.bfloat16

D_MODEL = 1024
RET_HEADS = 4
RET_DK = 128
RET_DV = 256
RET_QK_W = RET_HEADS * RET_DK
RET_V_W = RET_HEADS * RET_DV
ROPE_BASE = 10000.0
SB_HEADS = 8
SB_DH = 128
SB_W = SB_HEADS * SB_DH
MEM_HEADS = 4
MEM_DH = 128
MEM_W = MEM_HEADS * MEM_DH
EPS = 1e-6

RET_LOG_GAMMA = tuple(
    math.log(float(np.float32(1.0 - 2.0 ** (-5.0 - h)))) for h in range(RET_HEADS))

VMEM_LIMIT_BYTES = 48 * 1024 * 1024

ROW_TILE_A = 1024
RET_CHUNK = 256
IN_PIECE_A = 512
OUT_PIECE_A = 256
ROW_TILE_B = 512
OUT_PIECE_B = 256
SB_TILE = 256
ROW_TILE_SB = 512
SB_LOG_WEIGHT_FLOOR = -152.0
SB_EXP2_CLAMP = 64.0
SB_MASKED_LOGIT = -1e30
LOG2E = math.log2(math.e)
SB_Q_SCALE = LOG2E * SB_DH ** -0.5


def _rms(x, g):
    return x * lax.rsqrt(jnp.mean(x * x, axis=-1, keepdims=True) + EPS) * g


def _silu(x):
    return x * (1.0 / (1.0 + jnp.exp(-x)))


def _dot(a, b):
    return jnp.dot(a, b, preferred_element_type=F32)


def _dot_nt(a, b):
    return lax.dot_general(a, b, (((1,), (1,)), ((), ())), preferred_element_type=F32)


def _const_spec(shape, index_map):
    return pl.BlockSpec(shape, index_map, pipeline_mode=pl.Buffered(1))


def _interleaved(*task_lists):
    tagged = [((i + 0.5) / len(tasks), k, task)
              for k, tasks in enumerate(task_lists) for i, task in enumerate(tasks)]
    return [task for _, _, task in sorted(tagged, key=lambda e: e[:2])]


def _memkv_kernel(*refs, n_cast):
    cast_in = refs[:n_cast]
    mem_ref, g_ref, w_ref, gk_ref = refs[n_cast:n_cast + 4]
    cast_out = refs[n_cast + 4:2 * n_cast + 4]
    kt_ref, v_ref = refs[2 * n_cast + 4:]
    for src, dst in zip(cast_in, cast_out):
        dst[...] = src[...].astype(BF16)

    mn = _rms(mem_ref[0], g_ref[0]).astype(BF16)
    kv = _dot(mn, w_ref[0].astype(BF16))
    gk = gk_ref[0]
    for h in range(MEM_HEADS):
        kh = _rms(kv[:, h * MEM_DH:(h + 1) * MEM_DH], gk)
        kt_ref[0, 0, h * MEM_DH:(h + 1) * MEM_DH, :] = kh.T.astype(BF16)
    v_ref[0, 0] = kv[:, MEM_W:].astype(BF16)


def _memory_kv(mem, g_mem, w_mem_kv, g_mem_k, later_weights=()):
    depth = g_mem.shape[0]
    batch, mem_len, _ = mem.shape
    n_steps = depth * batch
    cast_specs = [pl.BlockSpec((w.shape[0] // n_steps, w.shape[1]), lambda l, b: (l * batch + b, 0))
                  for w in later_weights]
    outs = pl.pallas_call(
        functools.partial(_memkv_kernel, n_cast=len(later_weights)),
        grid=(depth, batch),
        in_specs=cast_specs + [
            pl.BlockSpec((1, mem_len, D_MODEL), lambda l, b: (b, 0, 0)),
            pl.BlockSpec((1, 1, D_MODEL), lambda l, b: (l, 0, 0)),
            pl.BlockSpec((1, D_MODEL, 2 * MEM_W), lambda l, b: (l, 0, 0)),
            pl.BlockSpec((1, 1, MEM_DH), lambda l, b: (l, 0, 0)),
        ],
        out_specs=cast_specs + [
            pl.BlockSpec((1, 1, MEM_W, mem_len), lambda l, b: (l, b, 0, 0)),
            pl.BlockSpec((1, 1, mem_len, MEM_W), lambda l, b: (l, b, 0, 0)),
        ],
        out_shape=[jax.ShapeDtypeStruct(w.shape, BF16) for w in later_weights] + [
            jax.ShapeDtypeStruct((depth, batch, MEM_W, mem_len), BF16),
            jax.ShapeDtypeStruct((depth, batch, mem_len, MEM_W), BF16),
        ],
        compiler_params=pltpu.CompilerParams(
            dimension_semantics=("arbitrary", "arbitrary"),
            vmem_limit_bytes=VMEM_LIMIT_BYTES),
        name="memkv",
    )(*later_weights, mem, g_mem.reshape(depth, 1, D_MODEL), w_mem_kv,
      g_mem_k.reshape(depth, 1, MEM_DH))
    return outs[-2], outs[-1], outs[:-2]


def _mem_logits(h, memq, kt_ref, gq):
    qn = _rms(memq, gq).astype(BF16)
    return _dot(qn, kt_ref[0, 0, h * MEM_DH:(h + 1) * MEM_DH, :]) * (MEM_DH ** -0.5)


def _mem_values(h, logits, mv_ref):
    e = jnp.exp(logits - jnp.max(logits, axis=-1, keepdims=True))
    return _dot(e.astype(BF16), mv_ref[0, 0, :, h * MEM_DH:(h + 1) * MEM_DH]), \
        jnp.sum(e, axis=-1, keepdims=True)


def _mem_gated(values, memgate):
    o, denom = values
    return (o / denom * _silu(memgate)).astype(BF16)


def _layer_a_kernel(*refs, n_cast):
    n_in_refs = 10
    cast_in = refs[:n_cast]
    (x_ref, pos_ref, invf_ref, gn_ref, win_ref, gret_ref, wout_ref,
     kt_ref, mv_ref, gq_ref) = refs[n_cast:n_cast + n_in_refs]
    cast_out = refs[n_cast + n_in_refs:2 * n_cast + n_in_refs]
    o_ref, state_ref = refs[2 * n_cast + n_in_refs:]
    for src, dst in zip(cast_in, cast_out):
        dst[...] = src[...].astype(BF16)

    tile = x_ref.shape[1]
    chunk = RET_CHUNK
    n_chunks = tile // chunk
    half = RET_DK // 2
    n_in = win_ref.shape[2] // IN_PIECE_A
    n_out = wout_ref.shape[2] // OUT_PIECE_A
    o_v = 2 * RET_QK_W
    o_g = o_v + RET_V_W
    o_mq = o_g + RET_V_W

    @pl.when(pl.program_id(1) == 0)
    def _():
        state_ref[...] = jnp.zeros_like(state_ref)

    tables = {}
    states = [state_ref[h] for h in range(RET_HEADS)]
    xs, xns = {}, {}
    u = {}
    mixed = {}
    cats = {}

    def rotary_tables():
        ang_t = invf_ref[...] * pos_ref[0]
        cos_sin = jnp.concatenate([jnp.cos(ang_t), jnp.sin(ang_t)], axis=0).T
        sin_cos = pltpu.roll(cos_sin, half, 1)
        low_lanes = lax.broadcasted_iota(jnp.int32, (tile, RET_DK), 1) < half
        tables["cos"] = jnp.where(low_lanes, cos_sin, sin_cos)
        tables["sin"] = jnp.where(low_lanes, -sin_cos, cos_sin)

    def decay_tables(h):
        def task():
            lg = RET_LOG_GAMMA[h]
            rel = (lax.broadcasted_iota(jnp.int32, (chunk, chunk), 0)
                   - lax.broadcasted_iota(jnp.int32, (chunk, chunk), 1)).astype(F32)
            idx = lax.broadcasted_iota(jnp.int32, (chunk, 1), 0).astype(F32)
            tables[h] = (jnp.where(rel >= 0.0, jnp.exp(jnp.maximum(rel, 0.0) * lg), 0.0),
                         jnp.exp((idx + 1.0) * lg),
                         jnp.exp((chunk - 1.0 - idx) * lg))
        return task

    def normalize(c):
        xs[c] = x_ref[0, c * chunk:(c + 1) * chunk, :]
        xns[c] = _rms(xs[c], gn_ref[0]).astype(BF16)

    def in_piece(c, p):
        def task():
            u[c, p] = _dot(xns[c], win_ref[0, :, p * IN_PIECE_A:(p + 1) * IN_PIECE_A])
        return task

    def cols(c, lo, width):
        p, off = divmod(lo, IN_PIECE_A)
        return u[c, p][:, off:off + width]

    part = {}

    def retention_scores(c, h):
        def task():
            rows = slice(c * chunk, (c + 1) * chunk)
            cos_c, sin_c = tables["cos"][rows], tables["sin"][rows]

            def rotary(t):
                return t * cos_c + pltpu.roll(t, half, 1) * sin_c

            q = rotary(cols(c, h * RET_DK, RET_DK)).astype(BF16)
            k = rotary(cols(c, RET_QK_W + h * RET_DK, RET_DK)) * (RET_DK ** -0.5)
            part[c, h, "q"], part[c, h, "k"] = q, k
            part[c, h, "scores"] = _dot_nt(q, k.astype(BF16))
        return task

    def retention_values(c, h):
        def task():
            decay, cross_decay, state_decay = tables[h]
            q, k = part.pop((c, h, "q")), part.pop((c, h, "k"))
            v = cols(c, o_v + h * RET_DV, RET_DV).astype(BF16)
            intra = _dot((part.pop((c, h, "scores")) * decay).astype(BF16), v)
            cross = _dot(q, states[h].astype(BF16)) * cross_decay
            states[h] = (math.exp(chunk * RET_LOG_GAMMA[h]) * states[h]
                         + _dot((k * state_decay).T.astype(BF16), v))
            part[c, h, "ret"] = intra + cross
        return task

    def retention_gated(c, h):
        def task():
            ret = _rms(part.pop((c, h, "ret")), gret_ref[0])
            mixed[c, h] = (ret * _silu(cols(c, o_g + h * RET_DV, RET_DV))).astype(BF16)
        return task

    def memory_logits(c, h):
        def task():
            part[c, h, "logits"] = _mem_logits(h, cols(c, o_mq + h * MEM_DH, MEM_DH),
                                               kt_ref, gq_ref[0])
        return task

    def memory_values(c, h):
        def task():
            part[c, h, "values"] = _mem_values(h, part.pop((c, h, "logits")), mv_ref)
        return task

    def memory_gated(c, h):
        def task():
            mixed[c, RET_HEADS + h] = _mem_gated(part.pop((c, h, "values")),
                                                 cols(c, o_mq + MEM_W + h * MEM_DH, MEM_DH))
        return task

    def mixer_groups(c):
        groups = [[retention_scores(c, h) for h in range(RET_HEADS)],
                  [memory_logits(c, h) for h in range(MEM_HEADS)],
                  [retention_values(c, h) for h in range(RET_HEADS)],
                  [memory_values(c, h) for h in range(MEM_HEADS)],
                  [retention_gated(c, h) for h in range(RET_HEADS)]
                  + [memory_gated(c, h) for h in range(MEM_HEADS)]]

        def run(group):
            def task():
                for t in group:
                    t()
            return task
        return [run(g) for g in groups]

    def out_piece(c, j):
        def task():
            if c not in cats:
                cats[c] = jnp.concatenate(
                    [mixed.pop((c, s)) for s in range(RET_HEADS + MEM_HEADS)], axis=1)
            sl = slice(j * OUT_PIECE_A, (j + 1) * OUT_PIECE_A)
            o_ref[0, c * chunk:(c + 1) * chunk, sl] = xs[c][:, sl] + _dot(cats[c], wout_ref[0, :, sl])
        return task

    setup = [rotary_tables] + [decay_tables(h) for h in range(RET_HEADS)]
    for step in range(n_chunks + 2):
        stages = [setup] if step == 0 else []
        if step < n_chunks:
            normalize(step)
            stages.append([in_piece(step, p) for p in range(n_in)])
        if 0 <= step - 1 < n_chunks:
            stages.append(mixer_groups(step - 1))
        if 0 <= step - 2 < n_chunks:
            stages.append([out_piece(step - 2, j) for j in range(n_out)])
        for task in _interleaved(*stages):
            task()

    for h in range(RET_HEADS):
        state_ref[h] = states[h]


def _layer_a(x, pos_f32, invf, g_norm, w_in_bf16, g_ret_head, w_out_bf16, kt, mv, g_mem_q,
             la, layer, later_weights=()):
    batch, seq, _ = x.shape
    tile = ROW_TILE_A
    n_a = g_norm.shape[0]
    depth = g_mem_q.shape[0]
    a_in_w = w_in_bf16.shape[2]
    a_out_in = w_out_bf16.shape[1]
    mem_len = mv.shape[2]
    steps_per_batch = seq // tile
    n_steps = batch * steps_per_batch
    cast_specs = [pl.BlockSpec((w.shape[0] // n_steps, w.shape[1]),
                               lambda b, t: (b * steps_per_batch + t, 0)) for w in later_weights]
    outs = pl.pallas_call(
        functools.partial(_layer_a_kernel, n_cast=len(later_weights)),
        grid=(batch, steps_per_batch),
        in_specs=cast_specs + [
            pl.BlockSpec((1, tile, D_MODEL), lambda b, t: (b, t, 0)),
            pl.BlockSpec((1, 1, tile), lambda b, t: (b, 0, t)),
            _const_spec((RET_DK // 2, 1), lambda b, t: (0, 0)),
            _const_spec((1, 1, D_MODEL), lambda b, t: (la, 0, 0)),
            _const_spec((1, D_MODEL, a_in_w), lambda b, t: (la, 0, 0)),
            _const_spec((1, 1, RET_DV), lambda b, t: (la, 0, 0)),
            _const_spec((1, a_out_in, D_MODEL), lambda b, t: (la, 0, 0)),
            pl.BlockSpec((1, 1, MEM_W, mem_len), lambda b, t: (layer, b, 0, 0)),
            pl.BlockSpec((1, 1, mem_len, MEM_W), lambda b, t: (layer, b, 0, 0)),
            _const_spec((1, 1, MEM_DH), lambda b, t: (layer, 0, 0)),
        ],
        out_specs=cast_specs + [pl.BlockSpec((1, tile, D_MODEL), lambda b, t: (b, t, 0))],
        out_shape=[jax.ShapeDtypeStruct(w.shape, BF16) for w in later_weights]
        + [jax.ShapeDtypeStruct(x.shape, F32)],
        scratch_shapes=[pltpu.VMEM((RET_HEADS, RET_DK, RET_DV), F32)],
        compiler_params=pltpu.CompilerParams(
            dimension_semantics=("arbitrary", "arbitrary"),
            vmem_limit_bytes=VMEM_LIMIT_BYTES),
        name="layer_a",
    )(*later_weights, x, pos_f32, invf, g_norm.reshape(n_a, 1, D_MODEL), w_in_bf16,
      g_ret_head.reshape(n_a, 1, RET_DV), w_out_bf16, kt, mv, g_mem_q.reshape(depth, 1, MEM_DH))
    return outs[-1], outs[:-1]


def _proj_b_kernel(x_ref, gkv_ref, gnb_ref, wkv_ref, win_ref, kt_ref, mv_ref, gq_ref,
                   q_ref, k_ref, v_ref, gate_ref, mo_ref):
    x = x_ref[0]
    xr = x * lax.rsqrt(jnp.mean(x * x, axis=-1, keepdims=True) + EPS)
    xkv = (xr * gkv_ref[...]).astype(BF16)
    xb = (xr * gnb_ref[0]).astype(BF16)
    heads = range(MEM_HEADS)
    head = lambda t, h: t[:, h * MEM_DH:(h + 1) * MEM_DH]
    o_mq = 2 * SB_W
    memq = _dot(xb, win_ref[0, :, o_mq:o_mq + MEM_W])
    memgate = _dot(xb, win_ref[0, :, o_mq + MEM_W:])
    logits = [_mem_logits(h, head(memq, h), kt_ref, gq_ref[0]) for h in heads]
    gate_ref[0] = _silu(_dot(xb, win_ref[0, :, SB_W:o_mq])).astype(BF16)
    values = [_mem_values(h, logits[h], mv_ref) for h in heads]
    k_ref[0] = _dot(xkv, wkv_ref[:, :SB_W]).astype(BF16)
    mo_ref[0] = jnp.concatenate([_mem_gated(values[h], head(memgate, h)) for h in heads], axis=1)
    v_ref[0] = _dot(xkv, wkv_ref[:, SB_W:]).astype(BF16)
    q_ref[0] = (_dot(xb, win_ref[0, :, :SB_W]) * SB_Q_SCALE).astype(BF16)


def _proj_b(x, g_kv, g_norm_b, w_kv_bf16, w_in_bf16, kt, mv, g_mem_q, lb, layer):
    batch, seq, _ = x.shape
    tile = ROW_TILE_B
    n_b = g_norm_b.shape[0]
    depth = g_mem_q.shape[0]
    mem_len = mv.shape[2]
    b_in_w = w_in_bf16.shape[2]
    row_spec = lambda w: pl.BlockSpec((1, tile, w), lambda b, t: (b, t, 0))
    widths = (SB_W, SB_W, SB_W, SB_W, MEM_W)
    return pl.pallas_call(
        _proj_b_kernel,
        grid=(batch, seq // tile),
        in_specs=[
            row_spec(D_MODEL),
            _const_spec((1, D_MODEL), lambda b, t: (0, 0)),
            _const_spec((1, 1, D_MODEL), lambda b, t: (lb, 0, 0)),
            _const_spec((D_MODEL, 2 * SB_W), lambda b, t: (0, 0)),
            _const_spec((1, D_MODEL, b_in_w), lambda b, t: (lb, 0, 0)),
            pl.BlockSpec((1, 1, MEM_W, mem_len), lambda b, t: (layer, b, 0, 0)),
            pl.BlockSpec((1, 1, mem_len, MEM_W), lambda b, t: (layer, b, 0, 0)),
            _const_spec((1, 1, MEM_DH), lambda b, t: (layer, 0, 0)),
        ],
        out_specs=[row_spec(w) for w in widths],
        out_shape=[jax.ShapeDtypeStruct((batch, seq, w), BF16) for w in widths],
        compiler_params=pltpu.CompilerParams(
            dimension_semantics=("arbitrary", "arbitrary"),
            vmem_limit_bytes=VMEM_LIMIT_BYTES),
        name="proj_b",
    )(x, g_kv.reshape(1, D_MODEL), g_norm_b.reshape(n_b, 1, D_MODEL), w_kv_bf16, w_in_bf16,
      kt, mv, g_mem_q.reshape(depth, 1, MEM_DH))


def _sb_out_kernel(x_ref, q_ref, kprev_ref, kcur_ref, vprev_ref, vcur_ref, k_hbm, v_hbm,
                   gate_ref, mo_ref, wout_ref, o_ref, cat_ref, kfar_ref, vfar_ref, far_sem, *,
                   n_steps):
    tile = SB_TILE
    tiles_per_step = q_ref.shape[1] // tile
    heads = q_ref.shape[2] // SB_DH
    batch_idx = pl.program_id(0)
    step = pl.program_id(1)

    row = lax.broadcasted_iota(jnp.int32, (tile, tile), 0)
    col = lax.broadcasted_iota(jnp.int32, (tile, tile), 1)
    causal = col < row
    suffix_mat = jnp.where(row > col, 1.0, 0.0).astype(BF16)


    def window_keys(prev_ref, cur_ref, far_ref, t, g, kind):
        hs = slice(g * SB_DH, (g + 1) * SB_DH)
        if kind == "far":
            return far_ref[:, hs]
        if kind == "diag":
            return cur_ref[0, t * tile:(t + 1) * tile, hs]
        if t == 0:
            return jnp.concatenate([prev_ref[0, :, hs], cur_ref[0, :tile, hs]], axis=0)
        return cur_ref[0, (t - 1) * tile:(t + 1) * tile, hs]

    def stage_scores(t, g, kind):
        nblk = 2 if kind == "near" else 1
        q = q_ref[0, t * tile:(t + 1) * tile, g * SB_DH:(g + 1) * SB_DH]
        z = _dot_nt(q, window_keys(kprev_ref, kcur_ref, kfar_ref, t, g, kind))
        if kind != "far":
            diag = jnp.where(causal, z[:, -tile:], SB_MASKED_LOGIT)
            z = diag if nblk == 1 else jnp.concatenate([z[:, :-tile], diag], axis=1)
        sp = jnp.maximum(z, jnp.log(1.0 + jnp.exp2(jnp.minimum(z, SB_EXP2_CLAMP))) * LOG2E)
        blocks = [sp[:, j * tile:(j + 1) * tile] for j in range(nblk)]
        return z, blocks, jnp.concatenate([blk.astype(BF16) for blk in blocks], axis=0)

    def stage_weights(scores, kind, penalty):
        z, blocks, sp_bf16 = scores
        nblk = len(blocks)
        suffix = _dot(sp_bf16, suffix_mat)
        weights = [None] * nblk
        for j in reversed(range(nblk)):
            sfx = suffix[j * tile:(j + 1) * tile]
            log_a = (z[:, j * tile:(j + 1) * tile] - blocks[j]) - sfx
            if penalty is not None:
                log_a = log_a - penalty
            weights[j] = jnp.exp2(log_a).astype(BF16)
            total = sfx[:, 0:1] + blocks[j][:, 0:1]
            penalty = total if penalty is None else penalty + total
        return jnp.concatenate(weights, axis=1), penalty

    def stage_values(weights, t, g, kind):
        return _dot(weights, window_keys(vprev_ref, vcur_ref, vfar_ref, t, g, kind))

    def run_windows(wins, penalties=None, other_tasks=()):
        n = len(wins)
        scores, weights, out = {}, {}, [None] * n

        def stagger(s):
            def task():
                if s < n:
                    scores[s] = stage_scores(*wins[s])
                w = s - 1
                if 0 <= w < n:
                    weights[w] = stage_weights(scores.pop(w), wins[w][2],
                                               None if penalties is None else penalties[w])
                w = s - 2
                if 0 <= w < n:
                    a, penalty = weights.pop(w)
                    out[w] = (stage_values(a, *wins[w]), penalty)
            return task

        steps = [stagger(s) for s in range(n + 2)]
        others = list(other_tasks)
        n_front = (len(others) + 1) // 2
        for task in steps[:1] + others[:n_front] + steps[1:-1] + others[n_front:] + steps[-1:]:
            task()
        return out

    cur = lax.rem(step, 2)
    n_out = o_ref.shape[2] // OUT_PIECE_B

    def out_piece(j):
        def task():
            sl = slice(j * OUT_PIECE_B, (j + 1) * OUT_PIECE_B)
            o_ref[0, :, sl] = x_ref[0, :, sl] + _dot(cat_ref[1 - cur], wout_ref[0, :, sl])
        return task

    out_tasks = [out_piece(j) for j in range(n_out)]

    def store(t, accs):
        rows = slice(t * tile, (t + 1) * tile)
        for g in range(heads):
            hs = slice(g * SB_DH, (g + 1) * SB_DH)
            cat_ref[cur, rows, hs] = (accs[g] * gate_ref[0, rows, hs].astype(F32)).astype(BF16)

    def log_weight_bound(pens):
        return -functools.reduce(jnp.minimum, [jnp.min(p) for p in pens])

    def fetch_far(start):
        copies = [pltpu.make_async_copy(src.at[batch_idx, pl.ds(start, tile), :], dst, far_sem.at[n])
                  for n, (src, dst) in enumerate(((k_hbm, kfar_ref), (v_hbm, vfar_ref)))]
        for c in copies:
            c.start()
        for c in copies:
            c.wait()

    def finish_tile(t, results):
        accs = [r[0] for r in results]
        penalties = [r[1] for r in results]
        store(t, accs)
        tile_idx = step * tiles_per_step + t
        bound = log_weight_bound(penalties)

        def cond(carry):
            return jnp.logical_and(carry[0] <= tile_idx, carry[1] > SB_LOG_WEIGHT_FLOOR)

        def body(carry):
            n, _, accs, pens = carry
            fetch_far(pl.multiple_of((tile_idx - n) * tile, tile))
            far = run_windows([(t, g, "far") for g in range(heads)], pens)
            accs = [accs[g] + far[g][0] for g in range(heads)]
            pens = [far[g][1] for g in range(heads)]
            return n + 1, log_weight_bound(pens), accs, pens

        @pl.when(jnp.logical_and(tile_idx >= 2, bound > SB_LOG_WEIGHT_FLOOR))
        def _():
            _, _, accs_far, _ = lax.while_loop(cond, body, (jnp.int32(2), bound, accs, penalties))
            store(t, accs_far)

    def attention(first_kind, other_tasks):
        wins = [(t, g, first_kind if t == 0 else "near")
                for t in range(tiles_per_step) for g in range(heads)]
        results = run_windows(wins, None, other_tasks)
        cat_ref[cur, :, SB_W:] = mo_ref[0]
        for t in range(tiles_per_step):
            finish_tile(t, results[t * heads:(t + 1) * heads])

    @pl.when(step == 0)
    def _():
        attention("diag", ())

    @pl.when(jnp.logical_and(step > 0, step < n_steps))
    def _():
        attention("near", out_tasks)

    @pl.when(step == n_steps)
    def _():
        for task in out_tasks:
            task()


def _sb_out(x, q, k, v, gate, mo, w_out_bf16, lb):
    batch, seq, _ = x.shape
    rows = ROW_TILE_SB
    n_steps = seq // rows
    tiles_per_step = rows // SB_TILE
    b_out_in = w_out_bf16.shape[1]
    attn_idx = lambda i: jnp.minimum(i, n_steps - 1)
    attn_spec = lambda w: pl.BlockSpec((1, rows, w), lambda b, i: (b, attn_idx(i), 0))
    proj_spec = lambda w: pl.BlockSpec((1, rows, w), lambda b, i: (b, jnp.maximum(i - 1, 0), 0))
    prev_spec = pl.BlockSpec(
        (1, SB_TILE, SB_W), lambda b, i: (b, jnp.maximum(attn_idx(i) * tiles_per_step - 1, 0), 0))
    any_spec = pl.BlockSpec(memory_space=pl.ANY)
    return pl.pallas_call(
        functools.partial(_sb_out_kernel, n_steps=n_steps),
        grid=(batch, n_steps + 1),
        in_specs=[
            proj_spec(D_MODEL), attn_spec(SB_W), prev_spec, attn_spec(SB_W), prev_spec,
            attn_spec(SB_W), any_spec, any_spec, attn_spec(SB_W), attn_spec(MEM_W),
            _const_spec((1, b_out_in, D_MODEL), lambda b, i: (lb, 0, 0)),
        ],
        out_specs=proj_spec(D_MODEL),
        out_shape=jax.ShapeDtypeStruct(x.shape, F32),
        scratch_shapes=[pltpu.VMEM((2, rows, b_out_in), BF16),
                        pltpu.VMEM((SB_TILE, SB_W), BF16),
                        pltpu.VMEM((SB_TILE, SB_W), BF16),
                        pltpu.SemaphoreType.DMA((2,))],
        compiler_params=pltpu.CompilerParams(
            dimension_semantics=("arbitrary", "arbitrary"),
            vmem_limit_bytes=VMEM_LIMIT_BYTES),
        name="sb_out",
    )(x, q, k, k, v, v, k, v, gate, mo, w_out_bf16)


def kernel(x, mem, positions, g_norm_a, w_in_a, g_ret_head, w_out_a, g_kv, w_kv, g_norm_b, w_in_b,
           w_out_b, g_mem, w_mem_kv, g_mem_q, g_mem_k):
    n_a = g_norm_a.shape[0]
    n_b = g_norm_b.shape[0]
    batch, seq, d_model = x.shape
    assert d_model == D_MODEL and n_a >= 1 and n_b >= 1
    assert seq % ROW_TILE_A == 0 and seq % ROW_TILE_B == 0 and seq % ROW_TILE_SB == 0
    assert ROW_TILE_A % RET_CHUNK == 0 and ROW_TILE_SB % SB_TILE == 0

    flat = lambda w: w.reshape(-1, w.shape[-1])
    kt, mv, casted = _memory_kv(mem, g_mem, w_mem_kv, g_mem_k, [flat(w_in_a), flat(w_out_a)])
    w_in_a_bf16, w_out_a_bf16 = (c.reshape(w.shape) for c, w in zip(casted, (w_in_a, w_out_a)))

    inv_freq = ROPE_BASE ** (-jnp.arange(0, RET_DK // 2, dtype=F32) * 2.0 / RET_DK)
    invf = inv_freq.reshape(RET_DK // 2, 1)
    pos_f32 = positions.astype(F32).reshape(batch, 1, seq)

    later = [w_in_b, w_out_b, w_kv]
    later_bf16 = None
    for la in range(n_a):
        x, casted = _layer_a(x, pos_f32, invf, g_norm_a, w_in_a_bf16, g_ret_head, w_out_a_bf16,
                             kt, mv, g_mem_q, la, la,
                             [flat(w) for w in later] if la == 0 else ())
        later_bf16 = later_bf16 or casted
    w_in_b_bf16, w_out_b_bf16, w_kv_bf16 = (c.reshape(w.shape) for c, w in zip(later_bf16, later))

    k_shared = v_shared = None
    for lb in range(n_b):
        layer = n_a + lb
        q, k_new, v_new, gate, mo = _proj_b(x, g_kv, g_norm_b, w_kv_bf16, w_in_b_bf16, kt, mv,
                                            g_mem_q, lb, layer)
        if lb == 0:
            k_shared, v_shared = k_new, v_new
        x = _sb_out(x, q, k_shared, v_shared, gate, mo, w_out_b_bf16, lb)
    return x
```

```python
import functools
import math

import numpy as np
import jax
import jax.numpy as jnp
from jax import lax
from jax.experimental import pallas as pl
from jax.experimental.pallas import tpu as pltpu

F32 = jnp.float32
BF16 = jnp.bfloat16

D_MODEL = 1024
RET_HEADS = 4
RET_DK = 128
RET_DV = 256
RET_QK_W = RET_HEADS * RET_DK
RET_V_W = RET_HEADS * RET_DV
ROPE_BASE = 10000.0
SB_HEADS = 8
SB_DH = 128
SB_W = SB_HEADS * SB_DH
MEM_HEADS = 4
MEM_DH = 128
MEM_W = MEM_HEADS * MEM_DH
EPS = 1e-6

RET_LOG_GAMMA = tuple(
    math.log(float(np.float32(1.0 - 2.0 ** (-5.0 - h)))) for h in range(RET_HEADS))

VMEM_LIMIT_BYTES = 48 * 1024 * 1024

MEMKV_CAST_STEPS = 4
ROW_TILE_A = 1024
RET_CHUNK = 256
IN_PIECE_A = 512
OUT_PIECE_A = 256
ROW_TILE_B = 512
OUT_PIECE_B = 256
SB_TILE = 256
ROW_TILE_SB = 512
SB_LOG_WEIGHT_FLOOR = -152.0
SB_EXP2_CLAMP = 64.0
SB_MASKED_LOGIT = -1e30
LOG2E = math.log2(math.e)
SB_Q_SCALE = LOG2E * SB_DH ** -0.5


def _rms(x, g):
    return x * lax.rsqrt(jnp.mean(x * x, axis=-1, keepdims=True) + EPS) * g


def _silu(x):
    return x * (1.0 / (1.0 + jnp.exp(-x)))


def _dot(a, b):
    return jnp.dot(a, b, preferred_element_type=F32)


def _dot_nt(a, b):
    return lax.dot_general(a, b, (((1,), (1,)), ((), ())), preferred_element_type=F32)


def _const_spec(shape, index_map):
    return pl.BlockSpec(shape, index_map, pipeline_mode=pl.Buffered(1))


def _interleaved(*task_lists):
    tagged = [((i + 0.5) / len(tasks), k, task)
              for k, tasks in enumerate(task_lists) for i, task in enumerate(tasks)]
    return [task for _, _, task in sorted(tagged, key=lambda e: e[:2])]


def _memkv_kernel(*refs, n_cast):
    cast_in = refs[:n_cast]
    mem_ref, g_ref, w_ref, gk_ref = refs[n_cast:n_cast + 4]
    cast_out = refs[n_cast + 4:2 * n_cast + 4]
    kt_ref, v_ref = refs[2 * n_cast + 4:]
    for src, dst in zip(cast_in, cast_out):
        dst[...] = src[...].astype(BF16)

    @pl.when(pl.program_id(2) == 0)
    def _():
        mn = _rms(mem_ref[0], g_ref[0]).astype(BF16)
        kv = _dot(mn, w_ref[0].astype(BF16))
        gk = gk_ref[0]
        for h in range(MEM_HEADS):
            kh = _rms(kv[:, h * MEM_DH:(h + 1) * MEM_DH], gk)
            kt_ref[0, 0, h * MEM_DH:(h + 1) * MEM_DH, :] = kh.T.astype(BF16)
        v_ref[0, 0] = kv[:, MEM_W:].astype(BF16)


def _memory_kv(mem, g_mem, w_mem_kv, g_mem_k, later_weights=()):
    depth = g_mem.shape[0]
    batch, mem_len, _ = mem.shape
    sub = MEMKV_CAST_STEPS
    n_steps = depth * batch * sub
    cast_specs = [pl.BlockSpec((w.shape[0] // n_steps, w.shape[1]),
                               lambda l, b, s: ((l * batch + b) * sub + s, 0))
                  for w in later_weights]
    outs = pl.pallas_call(
        functools.partial(_memkv_kernel, n_cast=len(later_weights)),
        grid=(depth, batch, sub),
        in_specs=cast_specs + [
            pl.BlockSpec((1, mem_len, D_MODEL), lambda l, b, s: (b, 0, 0)),
            pl.BlockSpec((1, 1, D_MODEL), lambda l, b, s: (l, 0, 0)),
            pl.BlockSpec((1, D_MODEL, 2 * MEM_W), lambda l, b, s: (l, 0, 0)),
            pl.BlockSpec((1, 1, MEM_DH), lambda l, b, s: (l, 0, 0)),
        ],
        out_specs=cast_specs + [
            pl.BlockSpec((1, 1, MEM_W, mem_len), lambda l, b, s: (l, b, 0, 0)),
            pl.BlockSpec((1, 1, mem_len, MEM_W), lambda l, b, s: (l, b, 0, 0)),
        ],
        out_shape=[jax.ShapeDtypeStruct(w.shape, BF16) for w in later_weights] + [
            jax.ShapeDtypeStruct((depth, batch, MEM_W, mem_len), BF16),
            jax.ShapeDtypeStruct((depth, batch, mem_len, MEM_W), BF16),
        ],
        compiler_params=pltpu.CompilerParams(
            dimension_semantics=("arbitrary", "arbitrary", "arbitrary"),
            vmem_limit_bytes=VMEM_LIMIT_BYTES),
        name="memkv",
    )(*later_weights, mem, g_mem.reshape(depth, 1, D_MODEL), w_mem_kv,
      g_mem_k.reshape(depth, 1, MEM_DH))
    return outs[-2], outs[-1], outs[:-2]


def _mem_logits(h, memq, kt_ref, gq):
    qn = _rms(memq, gq).astype(BF16)
    return _dot(qn, kt_ref[0, 0, h * MEM_DH:(h + 1) * MEM_DH, :]) * (MEM_DH ** -0.5)


def _mem_values(h, logits, mv_ref):
    e = jnp.exp(logits - jnp.max(logits, axis=-1, keepdims=True))
    return _dot(e.astype(BF16), mv_ref[0, 0, :, h * MEM_DH:(h + 1) * MEM_DH]), \
        jnp.sum(e, axis=-1, keepdims=True)


def _mem_gated(values, memgate):
    o, denom = values
    return (o / denom * _silu(memgate)).astype(BF16)


def _layer_a_kernel(*refs, n_cast):
    n_in_refs = 10
    cast_in = refs[:n_cast]
    (x_ref, pos_ref, invf_ref, gn_ref, win_ref, gret_ref, wout_ref,
     kt_ref, mv_ref, gq_ref) = refs[n_cast:n_cast + n_in_refs]
    cast_out = refs[n_cast + n_in_refs:2 * n_cast + n_in_refs]
    o_ref, state_ref = refs[2 * n_cast + n_in_refs:]
    for src, dst in zip(cast_in, cast_out):
        dst[...] = src[...].astype(BF16)

    tile = x_ref.shape[1]
    chunk = RET_CHUNK
    n_chunks = tile // chunk
    half = RET_DK // 2
    n_in = win_ref.shape[2] // IN_PIECE_A
    n_out = wout_ref.shape[2] // OUT_PIECE_A
    o_v = 2 * RET_QK_W
    o_g = o_v + RET_V_W
    o_mq = o_g + RET_V_W

    @pl.when(pl.program_id(1) == 0)
    def _():
        state_ref[...] = jnp.zeros_like(state_ref)

    tables = {}
    states = [state_ref[h] for h in range(RET_HEADS)]
    xs, xns = {}, {}
    u = {}
    mixed = {}
    cats = {}

    def rotary_tables():
        ang_t = invf_ref[...] * pos_ref[0]
        cos_sin = jnp.concatenate([jnp.cos(ang_t), jnp.sin(ang_t)], axis=0).T
        sin_cos = pltpu.roll(cos_sin, half, 1)
        low_lanes = lax.broadcasted_iota(jnp.int32, (tile, RET_DK), 1) < half
        tables["cos"] = jnp.where(low_lanes, cos_sin, sin_cos)
        tables["sin"] = jnp.where(low_lanes, -sin_cos, cos_sin)

    def decay_tables(h):
        def task():
            lg = RET_LOG_GAMMA[h]
            rel = (lax.broadcasted_iota(jnp.int32, (chunk, chunk), 0)
                   - lax.broadcasted_iota(jnp.int32, (chunk, chunk), 1)).astype(F32)
            idx = lax.broadcasted_iota(jnp.int32, (chunk, 1), 0).astype(F32)
            tables[h] = (jnp.where(rel >= 0.0, jnp.exp(jnp.maximum(rel, 0.0) * lg), 0.0),
                         jnp.exp((idx + 1.0) * lg),
                         jnp.exp((chunk - 1.0 - idx) * lg))
        return task

    def normalize(c):
        xs[c] = x_ref[0, c * chunk:(c + 1) * chunk, :]
        xns[c] = _rms(xs[c], gn_ref[0]).astype(BF16)

    def in_piece(c, p):
        def task():
            u[c, p] = _dot(xns[c], win_ref[0, :, p * IN_PIECE_A:(p + 1) * IN_PIECE_A])
        return task

    def cols(c, lo, width):
        p, off = divmod(lo, IN_PIECE_A)
        return u[c, p][:, off:off + width]

    part = {}

    def retention_scores(c, h):
        def task():
            rows = slice(c * chunk, (c + 1) * chunk)
            cos_c, sin_c = tables["cos"][rows], tables["sin"][rows]

            def rotary(t):
                return t * cos_c + pltpu.roll(t, half, 1) * sin_c

            q = rotary(cols(c, h * RET_DK, RET_DK)).astype(BF16)
            k = rotary(cols(c, RET_QK_W + h * RET_DK, RET_DK)) * (RET_DK ** -0.5)
            part[c, h, "q"], part[c, h, "k"] = q, k
            part[c, h, "scores"] = _dot_nt(q, k.astype(BF16))
        return task

    def retention_values(c, h):
        def task():
            decay, cross_decay, state_decay = tables[h]
            q, k = part.pop((c, h, "q")), part.pop((c, h, "k"))
            v = cols(c, o_v + h * RET_DV, RET_DV).astype(BF16)
            intra = _dot((part.pop((c, h, "scores")) * decay).astype(BF16), v)
            cross = _dot(q, states[h].astype(BF16)) * cross_decay
            states[h] = (math.exp(chunk * RET_LOG_GAMMA[h]) * states[h]
                         + _dot((k * state_decay).T.astype(BF16), v))
            part[c, h, "ret"] = intra + cross
        return task

    def retention_gated(c, h):
        def task():
            ret = _rms(part.pop((c, h, "ret")), gret_ref[0])
            mixed[c, h] = (ret * _silu(cols(c, o_g + h * RET_DV, RET_DV))).astype(BF16)
        return task

    def memory_logits(c, h):
        def task():
            part[c, h, "logits"] = _mem_logits(h, cols(c, o_mq + h * MEM_DH, MEM_DH),
                                               kt_ref, gq_ref[0])
        return task

    def memory_values(c, h):
        def task():
            part[c, h, "values"] = _mem_values(h, part.pop((c, h, "logits")), mv_ref)
        return task

    def memory_gated(c, h):
        def task():
            mixed[c, RET_HEADS + h] = _mem_gated(part.pop((c, h, "values")),
                                                 cols(c, o_mq + MEM_W + h * MEM_DH, MEM_DH))
        return task

    def mixer_groups(c):
        groups = [[retention_scores(c, h) for h in range(RET_HEADS)],
                  [memory_logits(c, h) for h in range(MEM_HEADS)],
                  [retention_values(c, h) for h in range(RET_HEADS)],
                  [memory_values(c, h) for h in range(MEM_HEADS)],
                  [retention_gated(c, h) for h in range(RET_HEADS)]
                  + [memory_gated(c, h) for h in range(MEM_HEADS)]]

        def run(group):
            def task():
                for t in group:
                    t()
            return task
        return [run(g) for g in groups]

    def out_piece(c, j):
        def task():
            if c not in cats:
                cats[c] = jnp.concatenate(
                    [mixed.pop((c, s)) for s in range(RET_HEADS + MEM_HEADS)], axis=1)
            sl = slice(j * OUT_PIECE_A, (j + 1) * OUT_PIECE_A)
            o_ref[0, c * chunk:(c + 1) * chunk, sl] = xs[c][:, sl] + _dot(cats[c], wout_ref[0, :, sl])
        return task

    setup = [rotary_tables] + [decay_tables(h) for h in range(RET_HEADS)]
    for step in range(n_chunks + 2):
        stages = [setup] if step == 0 else []
        if step < n_chunks:
            normalize(step)
            stages.append([in_piece(step, p) for p in range(n_in)])
        if 0 <= step - 1 < n_chunks:
            stages.append(mixer_groups(step - 1))
        if 0 <= step - 2 < n_chunks:
            stages.append([out_piece(step - 2, j) for j in range(n_out)])
        for task in _interleaved(*stages):
            task()

    for h in range(RET_HEADS):
        state_ref[h] = states[h]


def _layer_a(x, pos_f32, invf, g_norm, w_in_bf16, g_ret_head, w_out_bf16, kt, mv, g_mem_q,
             la, layer, later_weights=()):
    batch, seq, _ = x.shape
    tile = ROW_TILE_A
    n_a = g_norm.shape[0]
    depth = g_mem_q.shape[0]
    a_in_w = w_in_bf16.shape[2]
    a_out_in = w_out_bf16.shape[1]
    mem_len = mv.shape[2]
    steps_per_batch = seq // tile
    n_steps = batch * steps_per_batch
    cast_specs = [pl.BlockSpec((w.shape[0] // n_steps, w.shape[1]),
                               lambda b, t: (b * steps_per_batch + t, 0)) for w in later_weights]
    outs = pl.pallas_call(
        functools.partial(_layer_a_kernel, n_cast=len(later_weights)),
        grid=(batch, steps_per_batch),
        in_specs=cast_specs + [
            pl.BlockSpec((1, tile, D_MODEL), lambda b, t: (b, t, 0)),
            pl.BlockSpec((1, 1, tile), lambda b, t: (b, 0, t)),
            _const_spec((RET_DK // 2, 1), lambda b, t: (0, 0)),
            _const_spec((1, 1, D_MODEL), lambda b, t: (la, 0, 0)),
            _const_spec((1, D_MODEL, a_in_w), lambda b, t: (la, 0, 0)),
            _const_spec((1, 1, RET_DV), lambda b, t: (la, 0, 0)),
            _const_spec((1, a_out_in, D_MODEL), lambda b, t: (la, 0, 0)),
            pl.BlockSpec((1, 1, MEM_W, mem_len), lambda b, t: (layer, b, 0, 0)),
            pl.BlockSpec((1, 1, mem_len, MEM_W), lambda b, t: (layer, b, 0, 0)),
            _const_spec((1, 1, MEM_DH), lambda b, t: (layer, 0, 0)),
        ],
        out_specs=cast_specs + [pl.BlockSpec((1, tile, D_MODEL), lambda b, t: (b, t, 0))],
        out_shape=[jax.ShapeDtypeStruct(w.shape, BF16) for w in later_weights]
        + [jax.ShapeDtypeStruct(x.shape, F32)],
        scratch_shapes=[pltpu.VMEM((RET_HEADS, RET_DK, RET_DV), F32)],
        compiler_params=pltpu.CompilerParams(
            dimension_semantics=("arbitrary", "arbitrary"),
            vmem_limit_bytes=VMEM_LIMIT_BYTES),
        name="layer_a",
    )(*later_weights, x, pos_f32, invf, g_norm.reshape(n_a, 1, D_MODEL), w_in_bf16,
      g_ret_head.reshape(n_a, 1, RET_DV), w_out_bf16, kt, mv, g_mem_q.reshape(depth, 1, MEM_DH))
    return outs[-1], outs[:-1]


def _proj_b_kernel(x_ref, gkv_ref, gnb_ref, wkv_ref, win_ref, kt_ref, mv_ref, gq_ref,
                   q_ref, k_ref, v_ref, gate_ref, mo_ref):
    x = x_ref[0]
    xr = x * lax.rsqrt(jnp.mean(x * x, axis=-1, keepdims=True) + EPS)
    xkv = (xr * gkv_ref[...]).astype(BF16)
    xb = (xr * gnb_ref[0]).astype(BF16)
    heads = range(MEM_HEADS)
    head = lambda t, h: t[:, h * MEM_DH:(h + 1) * MEM_DH]
    o_mq = 2 * SB_W
    memq = _dot(xb, win_ref[0, :, o_mq:o_mq + MEM_W])
    memgate = _dot(xb, win_ref[0, :, o_mq + MEM_W:])
    logits = [_mem_logits(h, head(memq, h), kt_ref, gq_ref[0]) for h in heads]
    gate_ref[0] = _silu(_dot(xb, win_ref[0, :, SB_W:o_mq])).astype(BF16)
    values = [_mem_values(h, logits[h], mv_ref) for h in heads]
    k_ref[0] = _dot(xkv, wkv_ref[:, :SB_W]).astype(BF16)
    mo_ref[0] = jnp.concatenate([_mem_gated(values[h], head(memgate, h)) for h in heads], axis=1)
    v_ref[0] = _dot(xkv, wkv_ref[:, SB_W:]).astype(BF16)
    q_ref[0] = (_dot(xb, win_ref[0, :, :SB_W]) * SB_Q_SCALE).astype(BF16)


def _proj_b(x, g_kv, g_norm_b, w_kv_bf16, w_in_bf16, kt, mv, g_mem_q, lb, layer):
    batch, seq, _ = x.shape
    tile = ROW_TILE_B
    n_b = g_norm_b.shape[0]
    depth = g_mem_q.shape[0]
    mem_len = mv.shape[2]
    b_in_w = w_in_bf16.shape[2]
    row_spec = lambda w: pl.BlockSpec((1, tile, w), lambda b, t: (b, t, 0))
    widths = (SB_W, SB_W, SB_W, SB_W, MEM_W)
    return pl.pallas_call(
        _proj_b_kernel,
        grid=(batch, seq // tile),
        in_specs=[
            row_spec(D_MODEL),
            _const_spec((1, D_MODEL), lambda b, t: (0, 0)),
            _const_spec((1, 1, D_MODEL), lambda b, t: (lb, 0, 0)),
            _const_spec((D_MODEL, 2 * SB_W), lambda b, t: (0, 0)),
            _const_spec((1, D_MODEL, b_in_w), lambda b, t: (lb, 0, 0)),
            pl.BlockSpec((1, 1, MEM_W, mem_len), lambda b, t: (layer, b, 0, 0)),
            pl.BlockSpec((1, 1, mem_len, MEM_W), lambda b, t: (layer, b, 0, 0)),
            _const_spec((1, 1, MEM_DH), lambda b, t: (layer, 0, 0)),
        ],
        out_specs=[row_spec(w) for w in widths],
        out_shape=[jax.ShapeDtypeStruct((batch, seq, w), BF16) for w in widths],
        compiler_params=pltpu.CompilerParams(
            dimension_semantics=("arbitrary", "arbitrary"),
            vmem_limit_bytes=VMEM_LIMIT_BYTES),
        name="proj_b",
    )(x, g_kv.reshape(1, D_MODEL), g_norm_b.reshape(n_b, 1, D_MODEL), w_kv_bf16, w_in_bf16,
      kt, mv, g_mem_q.reshape(depth, 1, MEM_DH))


def _sb_out_kernel(x_ref, q_ref, kprev_ref, kcur_ref, vprev_ref, vcur_ref, k_hbm, v_hbm,
                   gate_ref, mo_ref, wout_ref, o_ref, cat_ref, kfar_ref, vfar_ref, far_sem, *,
                   n_steps):
    tile = SB_TILE
    tiles_per_step = q_ref.shape[1] // tile
    heads = q_ref.shape[2] // SB_DH
    batch_idx = pl.program_id(0)
    step = pl.program_id(1)

    row = lax.broadcasted_iota(jnp.int32, (tile, tile), 0)
    col = lax.broadcasted_iota(jnp.int32, (tile, tile), 1)
    causal = col < row
    suffix_mat = jnp.where(row > col, 1.0, 0.0).astype(BF16)


    def window_keys(prev_ref, cur_ref, far_ref, t, g, kind):
        hs = slice(g * SB_DH, (g + 1) * SB_DH)
        if kind == "far":
            return far_ref[:, hs]
        if kind == "diag":
            return cur_ref[0, t * tile:(t + 1) * tile, hs]
        if t == 0:
            return jnp.concatenate([prev_ref[0, :, hs], cur_ref[0, :tile, hs]], axis=0)
        return cur_ref[0, (t - 1) * tile:(t + 1) * tile, hs]

    def stage_scores(t, g, kind):
        nblk = 2 if kind == "near" else 1
        q = q_ref[0, t * tile:(t + 1) * tile, g * SB_DH:(g + 1) * SB_DH]
        z = _dot_nt(q, window_keys(kprev_ref, kcur_ref, kfar_ref, t, g, kind))
        if kind != "far":
            diag = jnp.where(causal, z[:, -tile:], SB_MASKED_LOGIT)
            z = diag if nblk == 1 else jnp.concatenate([z[:, :-tile], diag], axis=1)
        sp = jnp.maximum(z, jnp.log(1.0 + jnp.exp2(jnp.minimum(z, SB_EXP2_CLAMP))) * LOG2E)
        blocks = [sp[:, j * tile:(j + 1) * tile] for j in range(nblk)]
        return z, blocks, jnp.concatenate([blk.astype(BF16) for blk in blocks], axis=0)

    def stage_weights(scores, kind, penalty):
        z, blocks, sp_bf16 = scores
        nblk = len(blocks)
        suffix = _dot(sp_bf16, suffix_mat)
        weights = [None] * nblk
        for j in reversed(range(nblk)):
            sfx = suffix[j * tile:(j + 1) * tile]
            log_a = (z[:, j * tile:(j + 1) * tile] - blocks[j]) - sfx
            if penalty is not None:
                log_a = log_a - penalty
            weights[j] = jnp.exp2(log_a).astype(BF16)
            total = sfx[:, 0:1] + blocks[j][:, 0:1]
            penalty = total if penalty is None else penalty + total
        return jnp.concatenate(weights, axis=1), penalty

    def stage_values(weights, t, g, kind):
        return _dot(weights, window_keys(vprev_ref, vcur_ref, vfar_ref, t, g, kind))

    def run_windows(wins, penalties=None, other_tasks=()):
        n = len(wins)
        scores, weights, out = {}, {}, [None] * n

        def stagger(s):
            def task():
                if s < n:
                    scores[s] = stage_scores(*wins[s])
                w = s - 1
                if 0 <= w < n:
                    weights[w] = stage_weights(scores.pop(w), wins[w][2],
                                               None if penalties is None else penalties[w])
                w = s - 2
                if 0 <= w < n:
                    a, penalty = weights.pop(w)
                    out[w] = (stage_values(a, *wins[w]), penalty)
            return task

        steps = [stagger(s) for s in range(n + 2)]
        others = list(other_tasks)
        n_front = (len(others) + 1) // 2
        for task in steps[:1] + others[:n_front] + steps[1:-1] + others[n_front:] + steps[-1:]:
            task()
        return out

    cur = lax.rem(step, 2)
    n_out = o_ref.shape[2] // OUT_PIECE_B

    def out_piece(j):
        def task():
            sl = slice(j * OUT_PIECE_B, (j + 1) * OUT_PIECE_B)
            o_ref[0, :, sl] = x_ref[0, :, sl] + _dot(cat_ref[1 - cur], wout_ref[0, :, sl])
        return task

    out_tasks = [out_piece(j) for j in range(n_out)]

    def store(t, accs):
        rows = slice(t * tile, (t + 1) * tile)
        for g in range(heads):
            hs = slice(g * SB_DH, (g + 1) * SB_DH)
            cat_ref[cur, rows, hs] = (accs[g] * gate_ref[0, rows, hs].astype(F32)).astype(BF16)

    def log_weight_bound(pens):
        return -functools.reduce(jnp.minimum, [jnp.min(p) for p in pens])

    def fetch_far(start):
        copies = [pltpu.make_async_copy(src.at[batch_idx, pl.ds(start, tile), :], dst, far_sem.at[n])
                  for n, (src, dst) in enumerate(((k_hbm, kfar_ref), (v_hbm, vfar_ref)))]
        for c in copies:
            c.start()
        for c in copies:
            c.wait()

    def finish_tile(t, results):
        accs = [r[0] for r in results]
        penalties = [r[1] for r in results]
        store(t, accs)
        tile_idx = step * tiles_per_step + t
        bound = log_weight_bound(penalties)

        def cond(carry):
            return jnp.logical_and(carry[0] <= tile_idx, carry[1] > SB_LOG_WEIGHT_FLOOR)

        def body(carry):
            n, _, accs, pens = carry
            fetch_far(pl.multiple_of((tile_idx - n) * tile, tile))
            far = run_windows([(t, g, "far") for g in range(heads)], pens)
            accs = [accs[g] + far[g][0] for g in range(heads)]
            pens = [far[g][1] for g in range(heads)]
            return n + 1, log_weight_bound(pens), accs, pens

        @pl.when(jnp.logical_and(tile_idx >= 2, bound > SB_LOG_WEIGHT_FLOOR))
        def _():
            _, _, accs_far, _ = lax.while_loop(cond, body, (jnp.int32(2), bound, accs, penalties))
            store(t, accs_far)

    def attention(first_kind, other_tasks):
        wins = [(t, g, first_kind if t == 0 else "near")
                for t in range(tiles_per_step) for g in range(heads)]
        results = run_windows(wins, None, other_tasks)
        cat_ref[cur, :, SB_W:] = mo_ref[0]
        for t in range(tiles_per_step):
            finish_tile(t, results[t * heads:(t + 1) * heads])

    @pl.when(step == 0)
    def _():
        attention("diag", ())

    @pl.when(jnp.logical_and(step > 0, step < n_steps))
    def _():
        attention("near", out_tasks)

    @pl.when(step == n_steps)
    def _():
        for task in out_tasks:
            task()


def _sb_out(x, q, k, v, gate, mo, w_out_bf16, lb):
    batch, seq, _ = x.shape
    rows = ROW_TILE_SB
    n_steps = seq // rows
    tiles_per_step = rows // SB_TILE
    b_out_in = w_out_bf16.shape[1]
    attn_idx = lambda i: jnp.minimum(i, n_steps - 1)
    attn_spec = lambda w: pl.BlockSpec((1, rows, w), lambda b, i: (b, attn_idx(i), 0))
    proj_spec = lambda w: pl.BlockSpec((1, rows, w), lambda b, i: (b, jnp.maximum(i - 1, 0), 0))
    prev_spec = pl.BlockSpec(
        (1, SB_TILE, SB_W), lambda b, i: (b, jnp.maximum(attn_idx(i) * tiles_per_step - 1, 0), 0))
    any_spec = pl.BlockSpec(memory_space=pl.ANY)
    return pl.pallas_call(
        functools.partial(_sb_out_kernel, n_steps=n_steps),
        grid=(batch, n_steps + 1),
        in_specs=[
            proj_spec(D_MODEL), attn_spec(SB_W), prev_spec, attn_spec(SB_W), prev_spec,
            attn_spec(SB_W), any_spec, any_spec, attn_spec(SB_W), attn_spec(MEM_W),
            _const_spec((1, b_out_in, D_MODEL), lambda b, i: (lb, 0, 0)),
        ],
        out_specs=proj_spec(D_MODEL),
        out_shape=jax.ShapeDtypeStruct(x.shape, F32),
        scratch_shapes=[pltpu.VMEM((2, rows, b_out_in), BF16),
                        pltpu.VMEM((SB_TILE, SB_W), BF16),
                        pltpu.VMEM((SB_TILE, SB_W), BF16),
                        pltpu.SemaphoreType.DMA((2,))],
        compiler_params=pltpu.CompilerParams(
            dimension_semantics=("arbitrary", "arbitrary"),
            vmem_limit_bytes=VMEM_LIMIT_BYTES),
        name="sb_out",
    )(x, q, k, k, v, v, k, v, gate, mo, w_out_bf16)


def kernel(x, mem, positions, g_norm_a, w_in_a, g_ret_head, w_out_a, g_kv, w_kv, g_norm_b, w_in_b,
           w_out_b, g_mem, w_mem_kv, g_mem_q, g_mem_k):
    n_a = g_norm_a.shape[0]
    n_b = g_norm_b.shape[0]
    batch, seq, d_model = x.shape
    assert d_model == D_MODEL and n_a >= 1 and n_b >= 1
    assert seq % ROW_TILE_A == 0 and seq % ROW_TILE_B == 0 and seq % ROW_TILE_SB == 0
    assert ROW_TILE_A % RET_CHUNK == 0 and ROW_TILE_SB % SB_TILE == 0

    flat = lambda w: w.reshape(-1, w.shape[-1])
    kt, mv, casted = _memory_kv(mem, g_mem, w_mem_kv, g_mem_k, [flat(w_in_a), flat(w_out_a)])
    w_in_a_bf16, w_out_a_bf16 = (c.reshape(w.shape) for c, w in zip(casted, (w_in_a, w_out_a)))

    inv_freq = ROPE_BASE ** (-jnp.arange(0, RET_DK // 2, dtype=F32) * 2.0 / RET_DK)
    invf = inv_freq.reshape(RET_DK // 2, 1)
    pos_f32 = positions.astype(F32).reshape(batch, 1, seq)

    later = [w_in_b, w_out_b, w_kv]
    later_bf16 = None
    for la in range(n_a):
        x, casted = _layer_a(x, pos_f32, invf, g_norm_a, w_in_a_bf16, g_ret_head, w_out_a_bf16,
                             kt, mv, g_mem_q, la, la,
                             [flat(w) for w in later] if la == 0 else ())
        later_bf16 = later_bf16 or casted
    w_in_b_bf16, w_out_b_bf16, w_kv_bf16 = (c.reshape(w.shape) for c, w in zip(later_bf16, later))

    k_shared = v_shared = None
    for lb in range(n_b):
        layer = n_a + lb
        q, k_new, v_new, gate, mo = _proj_b(x, g_kv, g_norm_b, w_kv_bf16, w_in_b_bf16, kt, mv,
                                            g_mem_q, lb, layer)
        if lb == 0:
            k_shared, v_shared = k_new, v_new
        x = _sb_out(x, q, k_shared, v_shared, gate, mo, w_out_b_bf16, lb)
    return x
```

```python
import functools
import math

import numpy as np
import jax
import jax.numpy as jnp
from jax import lax
from jax.experimental import pallas as pl
from jax.experimental.pallas import tpu as pltpu

F32 = jnp.float32
BF16 = jnp.bfloat16

D_MODEL = 1024
RET_HEADS = 4
RET_DK = 128
RET_DV = 256
RET_QK_W = RET_HEADS * RET_DK
RET_V_W = RET_HEADS * RET_DV
ROPE_BASE = 10000.0
SB_HEADS = 8
SB_DH = 128
SB_W = SB_HEADS * SB_DH
MEM_HEADS = 4
MEM_DH = 128
MEM_W = MEM_HEADS * MEM_DH
EPS = 1e-6

RET_LOG_GAMMA = tuple(
    math.log(float(np.float32(1.0 - 2.0 ** (-5.0 - h)))) for h in range(RET_HEADS))

VMEM_LIMIT_BYTES = 48 * 1024 * 1024

ROW_TILE_A = 1024
RET_CHUNK = 256
IN_PIECE_A = 512
OUT_PIECE_A = 256
ROW_TILE_B = 1024
OUT_PIECE_B = 256
SB_TILE = 256
ROW_TILE_SB = 512
SB_LOG_WEIGHT_FLOOR = -152.0
SB_EXP2_CLAMP = 64.0
SB_MASKED_LOGIT = -1e30
LOG2E = math.log2(math.e)
SB_Q_SCALE = LOG2E * SB_DH ** -0.5


def _rms(x, g):
    return x * lax.rsqrt(jnp.mean(x * x, axis=-1, keepdims=True) + EPS) * g


def _silu(x):
    return x * (1.0 / (1.0 + jnp.exp(-x)))


def _dot(a, b):
    return jnp.dot(a, b, preferred_element_type=F32)


def _dot_nt(a, b):
    return lax.dot_general(a, b, (((1,), (1,)), ((), ())), preferred_element_type=F32)


def _const_spec(shape, index_map):
    return pl.BlockSpec(shape, index_map, pipeline_mode=pl.Buffered(1))


def _interleaved(*task_lists):
    tagged = [((i + 0.5) / len(tasks), k, task)
              for k, tasks in enumerate(task_lists) for i, task in enumerate(tasks)]
    return [task for _, _, task in sorted(tagged, key=lambda e: e[:2])]


def _memkv_kernel(*refs, n_cast):
    cast_in = refs[:n_cast]
    mem_ref, g_ref, w_ref, gk_ref = refs[n_cast:n_cast + 4]
    cast_out = refs[n_cast + 4:2 * n_cast + 4]
    kt_ref, v_ref = refs[2 * n_cast + 4:]
    for src, dst in zip(cast_in, cast_out):
        dst[...] = src[...].astype(BF16)

    mn = _rms(mem_ref[0], g_ref[0]).astype(BF16)
    kv = _dot(mn, w_ref[0].astype(BF16))
    gk = gk_ref[0]
    for h in range(MEM_HEADS):
        kh = _rms(kv[:, h * MEM_DH:(h + 1) * MEM_DH], gk)
        kt_ref[0, 0, h * MEM_DH:(h + 1) * MEM_DH, :] = kh.T.astype(BF16)
    v_ref[0, 0] = kv[:, MEM_W:].astype(BF16)


def _memory_kv(mem, g_mem, w_mem_kv, g_mem_k, later_weights=()):
    depth = g_mem.shape[0]
    batch, mem_len, _ = mem.shape
    n_steps = depth * batch
    cast_specs = [pl.BlockSpec((w.shape[0] // n_steps, w.shape[1]), lambda l, b: (l * batch + b, 0))
                  for w in later_weights]
    outs = pl.pallas_call(
        functools.partial(_memkv_kernel, n_cast=len(later_weights)),
        grid=(depth, batch),
        in_specs=cast_specs + [
            pl.BlockSpec((1, mem_len, D_MODEL), lambda l, b: (b, 0, 0)),
            pl.BlockSpec((1, 1, D_MODEL), lambda l, b: (l, 0, 0)),
            pl.BlockSpec((1, D_MODEL, 2 * MEM_W), lambda l, b: (l, 0, 0)),
            pl.BlockSpec((1, 1, MEM_DH), lambda l, b: (l, 0, 0)),
        ],
        out_specs=cast_specs + [
            pl.BlockSpec((1, 1, MEM_W, mem_len), lambda l, b: (l, b, 0, 0)),
            pl.BlockSpec((1, 1, mem_len, MEM_W), lambda l, b: (l, b, 0, 0)),
        ],
        out_shape=[jax.ShapeDtypeStruct(w.shape, BF16) for w in later_weights] + [
            jax.ShapeDtypeStruct((depth, batch, MEM_W, mem_len), BF16),
            jax.ShapeDtypeStruct((depth, batch, mem_len, MEM_W), BF16),
        ],
        compiler_params=pltpu.CompilerParams(
            dimension_semantics=("arbitrary", "arbitrary"),
            vmem_limit_bytes=VMEM_LIMIT_BYTES),
        name="memkv",
    )(*later_weights, mem, g_mem.reshape(depth, 1, D_MODEL), w_mem_kv,
      g_mem_k.reshape(depth, 1, MEM_DH))
    return outs[-2], outs[-1], outs[:-2]


def _mem_logits(h, memq, kt_ref, gq):
    qn = _rms(memq, gq).astype(BF16)
    return _dot(qn, kt_ref[0, 0, h * MEM_DH:(h + 1) * MEM_DH, :]) * (MEM_DH ** -0.5)


def _mem_values(h, logits, mv_ref):
    e = jnp.exp(logits - jnp.max(logits, axis=-1, keepdims=True))
    return _dot(e.astype(BF16), mv_ref[0, 0, :, h * MEM_DH:(h + 1) * MEM_DH]), \
        jnp.sum(e, axis=-1, keepdims=True)


def _mem_gated(values, memgate):
    o, denom = values
    return (o / denom * _silu(memgate)).astype(BF16)


def _layer_a_kernel(*refs, n_cast):
    n_in_refs = 10
    cast_in = refs[:n_cast]
    (x_ref, pos_ref, invf_ref, gn_ref, win_ref, gret_ref, wout_ref,
     kt_ref, mv_ref, gq_ref) = refs[n_cast:n_cast + n_in_refs]
    cast_out = refs[n_cast + n_in_refs:2 * n_cast + n_in_refs]
    o_ref, state_ref = refs[2 * n_cast + n_in_refs:]
    for src, dst in zip(cast_in, cast_out):
        dst[...] = src[...].astype(BF16)

    tile = x_ref.shape[1]
    chunk = RET_CHUNK
    n_chunks = tile // chunk
    half = RET_DK // 2
    n_in = win_ref.shape[2] // IN_PIECE_A
    n_out = wout_ref.shape[2] // OUT_PIECE_A
    o_v = 2 * RET_QK_W
    o_g = o_v + RET_V_W
    o_mq = o_g + RET_V_W

    @pl.when(pl.program_id(1) == 0)
    def _():
        state_ref[...] = jnp.zeros_like(state_ref)

    tables = {}
    states = [state_ref[h] for h in range(RET_HEADS)]
    xs, xns = {}, {}
    u = {}
    mixed = {}
    cats = {}

    def rotary_tables():
        ang_t = invf_ref[...] * pos_ref[0]
        cos_sin = jnp.concatenate([jnp.cos(ang_t), jnp.sin(ang_t)], axis=0).T
        sin_cos = pltpu.roll(cos_sin, half, 1)
        low_lanes = lax.broadcasted_iota(jnp.int32, (tile, RET_DK), 1) < half
        tables["cos"] = jnp.where(low_lanes, cos_sin, sin_cos)
        tables["sin"] = jnp.where(low_lanes, -sin_cos, cos_sin)

    def decay_tables(h):
        def task():
            lg = RET_LOG_GAMMA[h]
            rel = (lax.broadcasted_iota(jnp.int32, (chunk, chunk), 0)
                   - lax.broadcasted_iota(jnp.int32, (chunk, chunk), 1)).astype(F32)
            idx = lax.broadcasted_iota(jnp.int32, (chunk, 1), 0).astype(F32)
            tables[h] = (jnp.where(rel >= 0.0, jnp.exp(jnp.maximum(rel, 0.0) * lg), 0.0),
                         jnp.exp((idx + 1.0) * lg),
                         jnp.exp((chunk - 1.0 - idx) * lg))
        return task

    def normalize(c):
        xs[c] = x_ref[0, c * chunk:(c + 1) * chunk, :]
        xns[c] = _rms(xs[c], gn_ref[0]).astype(BF16)

    def in_piece(c, p):
        def task():
            u[c, p] = _dot(xns[c], win_ref[0, :, p * IN_PIECE_A:(p + 1) * IN_PIECE_A])
        return task

    def cols(c, lo, width):
        p, off = divmod(lo, IN_PIECE_A)
        return u[c, p][:, off:off + width]

    part = {}

    def retention_scores(c, h):
        def task():
            rows = slice(c * chunk, (c + 1) * chunk)
            cos_c, sin_c = tables["cos"][rows], tables["sin"][rows]

            def rotary(t):
                return t * cos_c + pltpu.roll(t, half, 1) * sin_c

            q = rotary(cols(c, h * RET_DK, RET_DK)).astype(BF16)
            k = rotary(cols(c, RET_QK_W + h * RET_DK, RET_DK)) * (RET_DK ** -0.5)
            part[c, h, "q"], part[c, h, "k"] = q, k
            part[c, h, "scores"] = _dot_nt(q, k.astype(BF16))
        return task

    def retention_values(c, h):
        def task():
            decay, cross_decay, state_decay = tables[h]
            q, k = part.pop((c, h, "q")), part.pop((c, h, "k"))
            v = cols(c, o_v + h * RET_DV, RET_DV).astype(BF16)
            intra = _dot((part.pop((c, h, "scores")) * decay).astype(BF16), v)
            cross = _dot(q, states[h].astype(BF16)) * cross_decay
            states[h] = (math.exp(chunk * RET_LOG_GAMMA[h]) * states[h]
                         + _dot((k * state_decay).T.astype(BF16), v))
            part[c, h, "ret"] = intra + cross
        return task

    def retention_gated(c, h):
        def task():
            ret = _rms(part.pop((c, h, "ret")), gret_ref[0])
            mixed[c, h] = (ret * _silu(cols(c, o_g + h * RET_DV, RET_DV))).astype(BF16)
        return task

    def memory_logits(c, h):
        def task():
            part[c, h, "logits"] = _mem_logits(h, cols(c, o_mq + h * MEM_DH, MEM_DH),
                                               kt_ref, gq_ref[0])
        return task

    def memory_values(c, h):
        def task():
            part[c, h, "values"] = _mem_values(h, part.pop((c, h, "logits")), mv_ref)
        return task

    def memory_gated(c, h):
        def task():
            mixed[c, RET_HEADS + h] = _mem_gated(part.pop((c, h, "values")),
                                                 cols(c, o_mq + MEM_W + h * MEM_DH, MEM_DH))
        return task

    def mixer_groups(c):
        groups = [[retention_scores(c, h) for h in range(RET_HEADS)],
                  [memory_logits(c, h) for h in range(MEM_HEADS)],
                  [retention_values(c, h) for h in range(RET_HEADS)],
                  [memory_values(c, h) for h in range(MEM_HEADS)],
                  [retention_gated(c, h) for h in range(RET_HEADS)]
                  + [memory_gated(c, h) for h in range(MEM_HEADS)]]

        def run(group):
            def task():
                for t in group:
                    t()
            return task
        return [run(g) for g in groups]

    def out_piece(c, j):
        def task():
            if c not in cats:
                cats[c] = jnp.concatenate(
                    [mixed.pop((c, s)) for s in range(RET_HEADS + MEM_HEADS)], axis=1)
            sl = slice(j * OUT_PIECE_A, (j + 1) * OUT_PIECE_A)
            o_ref[0, c * chunk:(c + 1) * chunk, sl] = xs[c][:, sl] + _dot(cats[c], wout_ref[0, :, sl])
        return task

    setup = [rotary_tables] + [decay_tables(h) for h in range(RET_HEADS)]
    for step in range(n_chunks + 2):
        stages = [setup] if step == 0 else []
        if step < n_chunks:
            normalize(step)
            stages.append([in_piece(step, p) for p in range(n_in)])
        if 0 <= step - 1 < n_chunks:
            stages.append(mixer_groups(step - 1))
        if 0 <= step - 2 < n_chunks:
            stages.append([out_piece(step - 2, j) for j in range(n_out)])
        for task in _interleaved(*stages):
            task()

    for h in range(RET_HEADS):
        state_ref[h] = states[h]


def _layer_a(x, pos_f32, invf, g_norm, w_in_bf16, g_ret_head, w_out_bf16, kt, mv, g_mem_q,
             la, layer, later_weights=()):
    batch, seq, _ = x.shape
    tile = ROW_TILE_A
    n_a = g_norm.shape[0]
    depth = g_mem_q.shape[0]
    a_in_w = w_in_bf16.shape[2]
    a_out_in = w_out_bf16.shape[1]
    mem_len = mv.shape[2]
    steps_per_batch = seq // tile
    n_steps = batch * steps_per_batch
    cast_specs = [pl.BlockSpec((w.shape[0] // n_steps, w.shape[1]),
                               lambda b, t: (b * steps_per_batch + t, 0)) for w in later_weights]
    outs = pl.pallas_call(
        functools.partial(_layer_a_kernel, n_cast=len(later_weights)),
        grid=(batch, steps_per_batch),
        in_specs=cast_specs + [
            pl.BlockSpec((1, tile, D_MODEL), lambda b, t: (b, t, 0)),
            pl.BlockSpec((1, 1, tile), lambda b, t: (b, 0, t)),
            _const_spec((RET_DK // 2, 1), lambda b, t: (0, 0)),
            _const_spec((1, 1, D_MODEL), lambda b, t: (la, 0, 0)),
            _const_spec((1, D_MODEL, a_in_w), lambda b, t: (la, 0, 0)),
            _const_spec((1, 1, RET_DV), lambda b, t: (la, 0, 0)),
            _const_spec((1, a_out_in, D_MODEL), lambda b, t: (la, 0, 0)),
            pl.BlockSpec((1, 1, MEM_W, mem_len), lambda b, t: (layer, b, 0, 0)),
            pl.BlockSpec((1, 1, mem_len, MEM_W), lambda b, t: (layer, b, 0, 0)),
            _const_spec((1, 1, MEM_DH), lambda b, t: (layer, 0, 0)),
        ],
        out_specs=cast_specs + [pl.BlockSpec((1, tile, D_MODEL), lambda b, t: (b, t, 0))],
        out_shape=[jax.ShapeDtypeStruct(w.shape, BF16) for w in later_weights]
        + [jax.ShapeDtypeStruct(x.shape, F32)],
        scratch_shapes=[pltpu.VMEM((RET_HEADS, RET_DK, RET_DV), F32)],
        compiler_params=pltpu.CompilerParams(
            dimension_semantics=("arbitrary", "arbitrary"),
            vmem_limit_bytes=VMEM_LIMIT_BYTES),
        name="layer_a",
    )(*later_weights, x, pos_f32, invf, g_norm.reshape(n_a, 1, D_MODEL), w_in_bf16,
      g_ret_head.reshape(n_a, 1, RET_DV), w_out_bf16, kt, mv, g_mem_q.reshape(depth, 1, MEM_DH))
    return outs[-1], outs[:-1]


def _proj_b_kernel(x_ref, gkv_ref, gnb_ref, wkv_ref, win_ref, kt_ref, mv_ref, gq_ref,
                   q_ref, k_ref, v_ref, gate_ref, mo_ref):
    x = x_ref[0]
    xr = x * lax.rsqrt(jnp.mean(x * x, axis=-1, keepdims=True) + EPS)
    xkv = (xr * gkv_ref[...]).astype(BF16)
    xb = (xr * gnb_ref[0]).astype(BF16)
    heads = range(MEM_HEADS)
    head = lambda t, h: t[:, h * MEM_DH:(h + 1) * MEM_DH]
    o_mq = 2 * SB_W
    memq = _dot(xb, win_ref[0, :, o_mq:o_mq + MEM_W])
    memgate = _dot(xb, win_ref[0, :, o_mq + MEM_W:])
    logits = [_mem_logits(h, head(memq, h), kt_ref, gq_ref[0]) for h in heads]
    gate_ref[0] = _silu(_dot(xb, win_ref[0, :, SB_W:o_mq])).astype(BF16)
    values = [_mem_values(h, logits[h], mv_ref) for h in heads]
    k_ref[0] = _dot(xkv, wkv_ref[:, :SB_W]).astype(BF16)
    mo_ref[0] = jnp.concatenate([_mem_gated(values[h], head(memgate, h)) for h in heads], axis=1)
    v_ref[0] = _dot(xkv, wkv_ref[:, SB_W:]).astype(BF16)
    q_ref[0] = (_dot(xb, win_ref[0, :, :SB_W]) * SB_Q_SCALE).astype(BF16)


def _proj_b(x, g_kv, g_norm_b, w_kv_bf16, w_in_bf16, kt, mv, g_mem_q, lb, layer):
    batch, seq, _ = x.shape
    tile = ROW_TILE_B
    n_b = g_norm_b.shape[0]
    depth = g_mem_q.shape[0]
    mem_len = mv.shape[2]
    b_in_w = w_in_bf16.shape[2]
    row_spec = lambda w: pl.BlockSpec((1, tile, w), lambda b, t: (b, t, 0))
    widths = (SB_W, SB_W, SB_W, SB_W, MEM_W)
    return pl.pallas_call(
        _proj_b_kernel,
        grid=(batch, seq // tile),
        in_specs=[
            row_spec(D_MODEL),
            _const_spec((1, D_MODEL), lambda b, t: (0, 0)),
            _const_spec((1, 1, D_MODEL), lambda b, t: (lb, 0, 0)),
            _const_spec((D_MODEL, 2 * SB_W), lambda b, t: (0, 0)),
            _const_spec((1, D_MODEL, b_in_w), lambda b, t: (lb, 0, 0)),
            pl.BlockSpec((1, 1, MEM_W, mem_len), lambda b, t: (layer, b, 0, 0)),
            pl.BlockSpec((1, 1, mem_len, MEM_W), lambda b, t: (layer, b, 0, 0)),
            _const_spec((1, 1, MEM_DH), lambda b, t: (layer, 0, 0)),
        ],
        out_specs=[row_spec(w) for w in widths],
        out_shape=[jax.ShapeDtypeStruct((batch, seq, w), BF16) for w in widths],
        compiler_params=pltpu.CompilerParams(
            dimension_semantics=("arbitrary", "arbitrary"),
            vmem_limit_bytes=58 * 1024 * 1024),
        name="proj_b",
    )(x, g_kv.reshape(1, D_MODEL), g_norm_b.reshape(n_b, 1, D_MODEL), w_kv_bf16, w_in_bf16,
      kt, mv, g_mem_q.reshape(depth, 1, MEM_DH))


def _sb_out_kernel(x_ref, q_ref, kprev_ref, kcur_ref, vprev_ref, vcur_ref, k_hbm, v_hbm,
                   gate_ref, mo_ref, wout_ref, o_ref, cat_ref, kfar_ref, vfar_ref, far_sem, *,
                   n_steps):
    tile = SB_TILE
    tiles_per_step = q_ref.shape[1] // tile
    heads = q_ref.shape[2] // SB_DH
    batch_idx = pl.program_id(0)
    step = pl.program_id(1)

    row = lax.broadcasted_iota(jnp.int32, (tile, tile), 0)
    col = lax.broadcasted_iota(jnp.int32, (tile, tile), 1)
    causal = col < row
    suffix_mat = jnp.where(row > col, 1.0, 0.0).astype(BF16)


    def window_keys(prev_ref, cur_ref, far_ref, t, g, kind):
        hs = slice(g * SB_DH, (g + 1) * SB_DH)
        if kind == "far":
            return far_ref[:, hs]
        if kind == "diag":
            return cur_ref[0, t * tile:(t + 1) * tile, hs]
        if t == 0:
            return jnp.concatenate([prev_ref[0, :, hs], cur_ref[0, :tile, hs]], axis=0)
        return cur_ref[0, (t - 1) * tile:(t + 1) * tile, hs]

    def stage_scores(t, g, kind):
        nblk = 2 if kind == "near" else 1
        q = q_ref[0, t * tile:(t + 1) * tile, g * SB_DH:(g + 1) * SB_DH]
        z = _dot_nt(q, window_keys(kprev_ref, kcur_ref, kfar_ref, t, g, kind))
        if kind != "far":
            diag = jnp.where(causal, z[:, -tile:], SB_MASKED_LOGIT)
            z = diag if nblk == 1 else jnp.concatenate([z[:, :-tile], diag], axis=1)
        sp = jnp.maximum(z, jnp.log(1.0 + jnp.exp2(jnp.minimum(z, SB_EXP2_CLAMP))) * LOG2E)
        blocks = [sp[:, j * tile:(j + 1) * tile] for j in range(nblk)]
        return z, blocks, jnp.concatenate([blk.astype(BF16) for blk in blocks], axis=0)

    def stage_weights(scores, kind, penalty):
        z, blocks, sp_bf16 = scores
        nblk = len(blocks)
        suffix = _dot(sp_bf16, suffix_mat)
        weights = [None] * nblk
        for j in reversed(range(nblk)):
            sfx = suffix[j * tile:(j + 1) * tile]
            log_a = (z[:, j * tile:(j + 1) * tile] - blocks[j]) - sfx
            if penalty is not None:
                log_a = log_a - penalty
            weights[j] = jnp.exp2(log_a).astype(BF16)
            total = sfx[:, 0:1] + blocks[j][:, 0:1]
            penalty = total if penalty is None else penalty + total
        return jnp.concatenate(weights, axis=1), penalty

    def stage_values(weights, t, g, kind):
        return _dot(weights, window_keys(vprev_ref, vcur_ref, vfar_ref, t, g, kind))

    def run_windows(wins, penalties=None, other_tasks=()):
        n = len(wins)
        scores, weights, out = {}, {}, [None] * n

        def stagger(s):
            def task():
                if s < n:
                    scores[s] = stage_scores(*wins[s])
                w = s - 1
                if 0 <= w < n:
                    weights[w] = stage_weights(scores.pop(w), wins[w][2],
                                               None if penalties is None else penalties[w])
                w = s - 2
                if 0 <= w < n:
                    a, penalty = weights.pop(w)
                    out[w] = (stage_values(a, *wins[w]), penalty)
            return task

        steps = [stagger(s) for s in range(n + 2)]
        others = list(other_tasks)
        n_front = (len(others) + 1) // 2
        for task in steps[:1] + others[:n_front] + steps[1:-1] + others[n_front:] + steps[-1:]:
            task()
        return out

    cur = lax.rem(step, 2)
    n_out = o_ref.shape[2] // OUT_PIECE_B

    def out_piece(j):
        def task():
            sl = slice(j * OUT_PIECE_B, (j + 1) * OUT_PIECE_B)
            o_ref[0, :, sl] = x_ref[0, :, sl] + _dot(cat_ref[1 - cur], wout_ref[0, :, sl])
        return task

    out_tasks = [out_piece(j) for j in range(n_out)]

    def store(t, accs):
        rows = slice(t * tile, (t + 1) * tile)
        for g in range(heads):
            hs = slice(g * SB_DH, (g + 1) * SB_DH)
            cat_ref[cur, rows, hs] = (accs[g] * gate_ref[0, rows, hs].astype(F32)).astype(BF16)

    def log_weight_bound(pens):
        return -functools.reduce(jnp.minimum, [jnp.min(p) for p in pens])

    def fetch_far(start):
        copies = [pltpu.make_async_copy(src.at[batch_idx, pl.ds(start, tile), :], dst, far_sem.at[n])
                  for n, (src, dst) in enumerate(((k_hbm, kfar_ref), (v_hbm, vfar_ref)))]
        for c in copies:
            c.start()
        for c in copies:
            c.wait()

    def finish_tile(t, results):
        accs = [r[0] for r in results]
        penalties = [r[1] for r in results]
        store(t, accs)
        tile_idx = step * tiles_per_step + t
        bound = log_weight_bound(penalties)

        def cond(carry):
            return jnp.logical_and(carry[0] <= tile_idx, carry[1] > SB_LOG_WEIGHT_FLOOR)

        def body(carry):
            n, _, accs, pens = carry
            fetch_far(pl.multiple_of((tile_idx - n) * tile, tile))
            far = run_windows([(t, g, "far") for g in range(heads)], pens)
            accs = [accs[g] + far[g][0] for g in range(heads)]
            pens = [far[g][1] for g in range(heads)]
            return n + 1, log_weight_bound(pens), accs, pens

        @pl.when(jnp.logical_and(tile_idx >= 2, bound > SB_LOG_WEIGHT_FLOOR))
        def _():
            _, _, accs_far, _ = lax.while_loop(cond, body, (jnp.int32(2), bound, accs, penalties))
            store(t, accs_far)

    def attention(first_kind, other_tasks):
        wins = [(t, g, first_kind if t == 0 else "near")
                for t in range(tiles_per_step) for g in range(heads)]
        results = run_windows(wins, None, other_tasks)
        cat_ref[cur, :, SB_W:] = mo_ref[0]
        for t in range(tiles_per_step):
            finish_tile(t, results[t * heads:(t + 1) * heads])

    @pl.when(step == 0)
    def _():
        attention("diag", ())

    @pl.when(jnp.logical_and(step > 0, step < n_steps))
    def _():
        attention("near", out_tasks)

    @pl.when(step == n_steps)
    def _():
        for task in out_tasks:
            task()


def _sb_out(x, q, k, v, gate, mo, w_out_bf16, lb):
    batch, seq, _ = x.shape
    rows = ROW_TILE_SB
    n_steps = seq // rows
    tiles_per_step = rows // SB_TILE
    b_out_in = w_out_bf16.shape[1]
    attn_idx = lambda i: jnp.minimum(i, n_steps - 1)
    attn_spec = lambda w: pl.BlockSpec((1, rows, w), lambda b, i: (b, attn_idx(i), 0))
    proj_spec = lambda w: pl.BlockSpec((1, rows, w), lambda b, i: (b, jnp.maximum(i - 1, 0), 0))
    prev_spec = pl.BlockSpec(
        (1, SB_TILE, SB_W), lambda b, i: (b, jnp.maximum(attn_idx(i) * tiles_per_step - 1, 0), 0))
    any_spec = pl.BlockSpec(memory_space=pl.ANY)
    return pl.pallas_call(
        functools.partial(_sb_out_kernel, n_steps=n_steps),
        grid=(batch, n_steps + 1),
        in_specs=[
            proj_spec(D_MODEL), attn_spec(SB_W), prev_spec, attn_spec(SB_W), prev_spec,
            attn_spec(SB_W), any_spec, any_spec, attn_spec(SB_W), attn_spec(MEM_W),
            _const_spec((1, b_out_in, D_MODEL), lambda b, i: (lb, 0, 0)),
        ],
        out_specs=proj_spec(D_MODEL),
        out_shape=jax.ShapeDtypeStruct(x.shape, F32),
        scratch_shapes=[pltpu.VMEM((2, rows, b_out_in), BF16),
                        pltpu.VMEM((SB_TILE, SB_W), BF16),
                        pltpu.VMEM((SB_TILE, SB_W), BF16),
                        pltpu.SemaphoreType.DMA((2,))],
        compiler_params=pltpu.CompilerParams(
            dimension_semantics=("arbitrary", "arbitrary"),
            vmem_limit_bytes=VMEM_LIMIT_BYTES),
        name="sb_out",
    )(x, q, k, k, v, v, k, v, gate, mo, w_out_bf16)


def kernel(x, mem, positions, g_norm_a, w_in_a, g_ret_head, w_out_a, g_kv, w_kv, g_norm_b, w_in_b,
           w_out_b, g_mem, w_mem_kv, g_mem_q, g_mem_k):
    n_a = g_norm_a.shape[0]
    n_b = g_norm_b.shape[0]
    batch, seq, d_model = x.shape
    assert d_model == D_MODEL and n_a >= 1 and n_b >= 1
    assert seq % ROW_TILE_A == 0 and seq % ROW_TILE_B == 0 and seq % ROW_TILE_SB == 0
    assert ROW_TILE_A % RET_CHUNK == 0 and ROW_TILE_SB % SB_TILE == 0

    flat = lambda w: w.reshape(-1, w.shape[-1])
    kt, mv, casted = _memory_kv(mem, g_mem, w_mem_kv, g_mem_k, [flat(w_in_a), flat(w_out_a)])
    w_in_a_bf16, w_out_a_bf16 = (c.reshape(w.shape) for c, w in zip(casted, (w_in_a, w_out_a)))

    inv_freq = ROPE_BASE ** (-jnp.arange(0, RET_DK // 2, dtype=F32) * 2.0 / RET_DK)
    invf = inv_freq.reshape(RET_DK // 2, 1)
    pos_f32 = positions.astype(F32).reshape(batch, 1, seq)

    later = [w_in_b, w_out_b, w_kv]
    later_bf16 = None
    for la in range(n_a):
        x, casted = _layer_a(x, pos_f32, invf, g_norm_a, w_in_a_bf16, g_ret_head, w_out_a_bf16,
                             kt, mv, g_mem_q, la, la,
                             [flat(w) for w in later] if la == 0 else ())
        later_bf16 = later_bf16 or casted
    w_in_b_bf16, w_out_b_bf16, w_kv_bf16 = (c.reshape(w.shape) for c, w in zip(later_bf16, later))

    k_shared = v_shared = None
    for lb in range(n_b):
        layer = n_a + lb
        q, k_new, v_new, gate, mo = _proj_b(x, g_kv, g_norm_b, w_kv_bf16, w_in_b_bf16, kt, mv,
                                            g_mem_q, lb, layer)
        if lb == 0:
            k_shared, v_shared = k_new, v_new
        x = _sb_out(x, q, k_shared, v_shared, gate, mo, w_out_b_bf16, lb)
    return x
```

```python
import functools
import math

import numpy as np
import jax
import jax.numpy as jnp
from jax import lax
from jax.experimental import pallas as pl
from jax.experimental.pallas import tpu as pltpu

F32 = jnp.float32
BF16 = jnp.bfloat16

D_MODEL = 1024
RET_HEADS = 4
RET_DK = 128
RET_DV = 256
RET_QK_W = RET_HEADS * RET_DK
RET_V_W = RET_HEADS * RET_DV
ROPE_BASE = 10000.0
SB_HEADS = 8
SB_DH = 128
SB_W = SB_HEADS * SB_DH
MEM_HEADS = 4
MEM_DH = 128
MEM_W = MEM_HEADS * MEM_DH
EPS = 1e-6

RET_LOG_GAMMA = tuple(
    math.log(float(np.float32(1.0 - 2.0 ** (-5.0 - h)))) for h in range(RET_HEADS))

VMEM_LIMIT_BYTES = 48 * 1024 * 1024
VMEM_LIMIT_LARGE_BYTES = 58 * 1024 * 1024

ROW_TILE_A = 1024
RET_CHUNK = 256
IN_PIECE_A = 512
OUT_PIECE_A = 256
ROW_TILE_B = 1024
OUT_PIECE_B = 256
SB_TILE = 256
ROW_TILE_SB = 512
SB_LOG_WEIGHT_FLOOR = -152.0
SB_EXP2_CLAMP = 64.0
SB_MASKED_LOGIT = -1e30
LOG2E = math.log2(math.e)
SB_Q_SCALE = LOG2E * SB_DH ** -0.5


def _rms(x, g):
    return x * lax.rsqrt(jnp.mean(x * x, axis=-1, keepdims=True) + EPS) * g


def _silu(x):
    return x * (1.0 / (1.0 + jnp.exp(-x)))


def _dot(a, b):
    return jnp.dot(a, b, preferred_element_type=F32)


def _dot_nt(a, b):
    return lax.dot_general(a, b, (((1,), (1,)), ((), ())), preferred_element_type=F32)


def _const_spec(shape, index_map):
    return pl.BlockSpec(shape, index_map, pipeline_mode=pl.Buffered(1))


def _interleaved(*task_lists):
    tagged = [((i + 0.5) / len(tasks), k, task)
              for k, tasks in enumerate(task_lists) for i, task in enumerate(tasks)]
    return [task for _, _, task in sorted(tagged, key=lambda e: e[:2])]


def _memkv_kernel(*refs, n_cast):
    cast_in = refs[:n_cast]
    mem_ref, g_ref, w_ref, gk_ref = refs[n_cast:n_cast + 4]
    cast_out = refs[n_cast + 4:2 * n_cast + 4]
    kt_ref, v_ref = refs[2 * n_cast + 4:]
    for src, dst in zip(cast_in, cast_out):
        dst[...] = src[...].astype(BF16)

    mn = _rms(mem_ref[0], g_ref[0]).astype(BF16)
    kv = _dot(mn, w_ref[0].astype(BF16))
    gk = gk_ref[0]
    for h in range(MEM_HEADS):
        kh = _rms(kv[:, h * MEM_DH:(h + 1) * MEM_DH], gk)
        kt_ref[0, 0, h * MEM_DH:(h + 1) * MEM_DH, :] = kh.T.astype(BF16)
    v_ref[0, 0] = kv[:, MEM_W:].astype(BF16)


def _memory_kv(mem, g_mem, w_mem_kv, g_mem_k, later_weights=()):
    depth = g_mem.shape[0]
    batch, mem_len, _ = mem.shape
    n_steps = depth * batch
    cast_specs = [pl.BlockSpec((w.shape[0] // n_steps, w.shape[1]), lambda l, b: (l * batch + b, 0))
                  for w in later_weights]
    outs = pl.pallas_call(
        functools.partial(_memkv_kernel, n_cast=len(later_weights)),
        grid=(depth, batch),
        in_specs=cast_specs + [
            pl.BlockSpec((1, mem_len, D_MODEL), lambda l, b: (b, 0, 0)),
            pl.BlockSpec((1, 1, D_MODEL), lambda l, b: (l, 0, 0)),
            pl.BlockSpec((1, D_MODEL, 2 * MEM_W), lambda l, b: (l, 0, 0)),
            pl.BlockSpec((1, 1, MEM_DH), lambda l, b: (l, 0, 0)),
        ],
        out_specs=cast_specs + [
            pl.BlockSpec((1, 1, MEM_W, mem_len), lambda l, b: (l, b, 0, 0)),
            pl.BlockSpec((1, 1, mem_len, MEM_W), lambda l, b: (l, b, 0, 0)),
        ],
        out_shape=[jax.ShapeDtypeStruct(w.shape, BF16) for w in later_weights] + [
            jax.ShapeDtypeStruct((depth, batch, MEM_W, mem_len), BF16),
            jax.ShapeDtypeStruct((depth, batch, mem_len, MEM_W), BF16),
        ],
        compiler_params=pltpu.CompilerParams(
            dimension_semantics=("arbitrary", "arbitrary"),
            vmem_limit_bytes=VMEM_LIMIT_BYTES),
        name="memkv",
    )(*later_weights, mem, g_mem.reshape(depth, 1, D_MODEL), w_mem_kv,
      g_mem_k.reshape(depth, 1, MEM_DH))
    return outs[-2], outs[-1], outs[:-2]


def _mem_logits(h, memq, kt_ref, gq):
    qn = _rms(memq, gq).astype(BF16)
    return _dot(qn, kt_ref[0, 0, h * MEM_DH:(h + 1) * MEM_DH, :]) * (MEM_DH ** -0.5)


def _mem_values(h, logits, mv_ref):
    e = jnp.exp(logits - jnp.max(logits, axis=-1, keepdims=True))
    return _dot(e.astype(BF16), mv_ref[0, 0, :, h * MEM_DH:(h + 1) * MEM_DH]), \
        jnp.sum(e, axis=-1, keepdims=True)


def _mem_gated(values, memgate):
    o, denom = values
    return (o / denom * _silu(memgate)).astype(BF16)


def _layer_a_kernel(*refs, n_cast):
    n_in_refs = 10
    cast_in = refs[:n_cast]
    (x_ref, pos_ref, invf_ref, gn_ref, win_ref, gret_ref, wout_ref,
     kt_ref, mv_ref, gq_ref) = refs[n_cast:n_cast + n_in_refs]
    cast_out = refs[n_cast + n_in_refs:2 * n_cast + n_in_refs]
    o_ref, state_ref = refs[2 * n_cast + n_in_refs:]
    for src, dst in zip(cast_in, cast_out):
        dst[...] = src[...].astype(BF16)

    tile = x_ref.shape[1]
    chunk = RET_CHUNK
    n_chunks = tile // chunk
    half = RET_DK // 2
    n_in = win_ref.shape[2] // IN_PIECE_A
    n_out = wout_ref.shape[2] // OUT_PIECE_A
    o_v = 2 * RET_QK_W
    o_g = o_v + RET_V_W
    o_mq = o_g + RET_V_W

    @pl.when(pl.program_id(1) == 0)
    def _():
        state_ref[...] = jnp.zeros_like(state_ref)

    tables = {}
    states = [state_ref[h] for h in range(RET_HEADS)]
    xs, xns = {}, {}
    u = {}
    mixed = {}
    cats = {}

    def rotary_tables():
        ang_t = invf_ref[...] * pos_ref[0]
        cos_sin = jnp.concatenate([jnp.cos(ang_t), jnp.sin(ang_t)], axis=0).T
        sin_cos = pltpu.roll(cos_sin, half, 1)
        low_lanes = lax.broadcasted_iota(jnp.int32, (tile, RET_DK), 1) < half
        tables["cos"] = jnp.where(low_lanes, cos_sin, sin_cos)
        tables["sin"] = jnp.where(low_lanes, -sin_cos, cos_sin)

    def decay_tables(h):
        def task():
            lg = RET_LOG_GAMMA[h]
            rel = (lax.broadcasted_iota(jnp.int32, (chunk, chunk), 0)
                   - lax.broadcasted_iota(jnp.int32, (chunk, chunk), 1)).astype(F32)
            idx = lax.broadcasted_iota(jnp.int32, (chunk, 1), 0).astype(F32)
            tables[h] = (jnp.where(rel >= 0.0, jnp.exp(jnp.maximum(rel, 0.0) * lg), 0.0),
                         jnp.exp((idx + 1.0) * lg),
                         jnp.exp((chunk - 1.0 - idx) * lg))
        return task

    def normalize(c):
        xs[c] = x_ref[0, c * chunk:(c + 1) * chunk, :]
        xns[c] = _rms(xs[c], gn_ref[0]).astype(BF16)

    def in_piece(c, p):
        def task():
            u[c, p] = _dot(xns[c], win_ref[0, :, p * IN_PIECE_A:(p + 1) * IN_PIECE_A])
        return task

    def cols(c, lo, width):
        p, off = divmod(lo, IN_PIECE_A)
        return u[c, p][:, off:off + width]

    part = {}

    def retention_scores(c, h):
        def task():
            rows = slice(c * chunk, (c + 1) * chunk)
            cos_c, sin_c = tables["cos"][rows], tables["sin"][rows]

            def rotary(t):
                return t * cos_c + pltpu.roll(t, half, 1) * sin_c

            q = rotary(cols(c, h * RET_DK, RET_DK)).astype(BF16)
            k = rotary(cols(c, RET_QK_W + h * RET_DK, RET_DK)) * (RET_DK ** -0.5)
            part[c, h, "q"], part[c, h, "k"] = q, k
            part[c, h, "scores"] = _dot_nt(q, k.astype(BF16))
        return task

    def retention_values(c, h):
        def task():
            decay, cross_decay, state_decay = tables[h]
            q, k = part.pop((c, h, "q")), part.pop((c, h, "k"))
            v = cols(c, o_v + h * RET_DV, RET_DV).astype(BF16)
            intra = _dot((part.pop((c, h, "scores")) * decay).astype(BF16), v)
            cross = _dot(q, states[h].astype(BF16)) * cross_decay
            states[h] = (math.exp(chunk * RET_LOG_GAMMA[h]) * states[h]
                         + _dot((k * state_decay).T.astype(BF16), v))
            part[c, h, "ret"] = intra + cross
        return task

    def retention_gated(c, h):
        def task():
            ret = _rms(part.pop((c, h, "ret")), gret_ref[0])
            mixed[c, h] = (ret * _silu(cols(c, o_g + h * RET_DV, RET_DV))).astype(BF16)
        return task

    def memory_logits(c, h):
        def task():
            part[c, h, "logits"] = _mem_logits(h, cols(c, o_mq + h * MEM_DH, MEM_DH),
                                               kt_ref, gq_ref[0])
        return task

    def memory_values(c, h):
        def task():
            part[c, h, "values"] = _mem_values(h, part.pop((c, h, "logits")), mv_ref)
        return task

    def memory_gated(c, h):
        def task():
            mixed[c, RET_HEADS + h] = _mem_gated(part.pop((c, h, "values")),
                                                 cols(c, o_mq + MEM_W + h * MEM_DH, MEM_DH))
        return task

    def mixer_groups(c):
        groups = [[retention_scores(c, h) for h in range(RET_HEADS)],
                  [memory_logits(c, h) for h in range(MEM_HEADS)],
                  [retention_values(c, h) for h in range(RET_HEADS)],
                  [memory_values(c, h) for h in range(MEM_HEADS)],
                  [retention_gated(c, h) for h in range(RET_HEADS)]
                  + [memory_gated(c, h) for h in range(MEM_HEADS)]]

        def run(group):
            def task():
                for t in group:
                    t()
            return task
        return [run(g) for g in groups]

    def out_piece(c, j):
        def task():
            if c not in cats:
                cats[c] = jnp.concatenate(
                    [mixed.pop((c, s)) for s in range(RET_HEADS + MEM_HEADS)], axis=1)
            sl = slice(j * OUT_PIECE_A, (j + 1) * OUT_PIECE_A)
            o_ref[0, c * chunk:(c + 1) * chunk, sl] = xs[c][:, sl] + _dot(cats[c], wout_ref[0, :, sl])
        return task

    setup = [rotary_tables] + [decay_tables(h) for h in range(RET_HEADS)]
    for step in range(n_chunks + 2):
        stages = [setup] if step == 0 else []
        if step < n_chunks:
            normalize(step)
            stages.append([in_piece(step, p) for p in range(n_in)])
        if 0 <= step - 1 < n_chunks:
            stages.append(mixer_groups(step - 1))
        if 0 <= step - 2 < n_chunks:
            stages.append([out_piece(step - 2, j) for j in range(n_out)])
        for task in _interleaved(*stages):
            task()

    for h in range(RET_HEADS):
        state_ref[h] = states[h]


def _layer_a(x, pos_f32, invf, g_norm, w_in_bf16, g_ret_head, w_out_bf16, kt, mv, g_mem_q,
             la, layer, later_weights=()):
    batch, seq, _ = x.shape
    tile = ROW_TILE_A
    n_a = g_norm.shape[0]
    depth = g_mem_q.shape[0]
    a_in_w = w_in_bf16.shape[2]
    a_out_in = w_out_bf16.shape[1]
    mem_len = mv.shape[2]
    steps_per_batch = seq // tile
    n_steps = batch * steps_per_batch
    cast_specs = [pl.BlockSpec((w.shape[0] // n_steps, w.shape[1]),
                               lambda b, t: (b * steps_per_batch + t, 0)) for w in later_weights]
    outs = pl.pallas_call(
        functools.partial(_layer_a_kernel, n_cast=len(later_weights)),
        grid=(batch, steps_per_batch),
        in_specs=cast_specs + [
            pl.BlockSpec((1, tile, D_MODEL), lambda b, t: (b, t, 0)),
            pl.BlockSpec((1, 1, tile), lambda b, t: (b, 0, t)),
            _const_spec((RET_DK // 2, 1), lambda b, t: (0, 0)),
            _const_spec((1, 1, D_MODEL), lambda b, t: (la, 0, 0)),
            _const_spec((1, D_MODEL, a_in_w), lambda b, t: (la, 0, 0)),
            _const_spec((1, 1, RET_DV), lambda b, t: (la, 0, 0)),
            _const_spec((1, a_out_in, D_MODEL), lambda b, t: (la, 0, 0)),
            pl.BlockSpec((1, 1, MEM_W, mem_len), lambda b, t: (layer, b, 0, 0)),
            pl.BlockSpec((1, 1, mem_len, MEM_W), lambda b, t: (layer, b, 0, 0)),
            _const_spec((1, 1, MEM_DH), lambda b, t: (layer, 0, 0)),
        ],
        out_specs=cast_specs + [pl.BlockSpec((1, tile, D_MODEL), lambda b, t: (b, t, 0))],
        out_shape=[jax.ShapeDtypeStruct(w.shape, BF16) for w in later_weights]
        + [jax.ShapeDtypeStruct(x.shape, F32)],
        scratch_shapes=[pltpu.VMEM((RET_HEADS, RET_DK, RET_DV), F32)],
        compiler_params=pltpu.CompilerParams(
            dimension_semantics=("arbitrary", "arbitrary"),
            vmem_limit_bytes=VMEM_LIMIT_BYTES),
        name="layer_a",
    )(*later_weights, x, pos_f32, invf, g_norm.reshape(n_a, 1, D_MODEL), w_in_bf16,
      g_ret_head.reshape(n_a, 1, RET_DV), w_out_bf16, kt, mv, g_mem_q.reshape(depth, 1, MEM_DH))
    return outs[-1], outs[:-1]


def _proj_b_kernel(x_ref, gkv_ref, gnb_ref, wkv_ref, win_ref, kt_ref, mv_ref, gq_ref,
                   q_ref, k_ref, v_ref, gate_ref, mo_ref):
    x = x_ref[0]
    xr = x * lax.rsqrt(jnp.mean(x * x, axis=-1, keepdims=True) + EPS)
    xkv = (xr * gkv_ref[...]).astype(BF16)
    xb = (xr * gnb_ref[0]).astype(BF16)
    heads = range(MEM_HEADS)
    head = lambda t, h: t[:, h * MEM_DH:(h + 1) * MEM_DH]
    o_mq = 2 * SB_W
    memq = _dot(xb, win_ref[0, :, o_mq:o_mq + MEM_W])
    memgate = _dot(xb, win_ref[0, :, o_mq + MEM_W:])
    logits = [_mem_logits(h, head(memq, h), kt_ref, gq_ref[0]) for h in heads]
    gate_ref[0] = _silu(_dot(xb, win_ref[0, :, SB_W:o_mq])).astype(BF16)
    values = [_mem_values(h, logits[h], mv_ref) for h in heads]
    k_ref[0] = _dot(xkv, wkv_ref[:, :SB_W]).astype(BF16)
    mo_ref[0] = jnp.concatenate([_mem_gated(values[h], head(memgate, h)) for h in heads], axis=1)
    v_ref[0] = _dot(xkv, wkv_ref[:, SB_W:]).astype(BF16)
    q_ref[0] = (_dot(xb, win_ref[0, :, :SB_W]) * SB_Q_SCALE).astype(BF16)


def _proj_b(x, g_kv, g_norm_b, w_kv_bf16, w_in_bf16, kt, mv, g_mem_q, lb, layer):
    batch, seq, _ = x.shape
    tile = ROW_TILE_B
    n_b = g_norm_b.shape[0]
    depth = g_mem_q.shape[0]
    mem_len = mv.shape[2]
    b_in_w = w_in_bf16.shape[2]
    row_spec = lambda w: pl.BlockSpec((1, tile, w), lambda b, t: (b, t, 0))
    widths = (SB_W, SB_W, SB_W, SB_W, MEM_W)
    return pl.pallas_call(
        _proj_b_kernel,
        grid=(batch, seq // tile),
        in_specs=[
            row_spec(D_MODEL),
            _const_spec((1, D_MODEL), lambda b, t: (0, 0)),
            _const_spec((1, 1, D_MODEL), lambda b, t: (lb, 0, 0)),
            _const_spec((D_MODEL, 2 * SB_W), lambda b, t: (0, 0)),
            _const_spec((1, D_MODEL, b_in_w), lambda b, t: (lb, 0, 0)),
            pl.BlockSpec((1, 1, MEM_W, mem_len), lambda b, t: (layer, b, 0, 0)),
            pl.BlockSpec((1, 1, mem_len, MEM_W), lambda b, t: (layer, b, 0, 0)),
            _const_spec((1, 1, MEM_DH), lambda b, t: (layer, 0, 0)),
        ],
        out_specs=[row_spec(w) for w in widths],
        out_shape=[jax.ShapeDtypeStruct((batch, seq, w), BF16) for w in widths],
        compiler_params=pltpu.CompilerParams(
            dimension_semantics=("arbitrary", "arbitrary"),
            vmem_limit_bytes=VMEM_LIMIT_LARGE_BYTES),
        name="proj_b",
    )(x, g_kv.reshape(1, D_MODEL), g_norm_b.reshape(n_b, 1, D_MODEL), w_kv_bf16, w_in_bf16,
      kt, mv, g_mem_q.reshape(depth, 1, MEM_DH))


def _sb_out_kernel(x_ref, q_ref, kprev_ref, kcur_ref, vprev_ref, vcur_ref, k_hbm, v_hbm,
                   gate_ref, mo_ref, wout_ref, o_ref, cat_ref, kfar_ref, vfar_ref, far_sem, *,
                   n_steps):
    tile = SB_TILE
    tiles_per_step = q_ref.shape[1] // tile
    heads = q_ref.shape[2] // SB_DH
    batch_idx = pl.program_id(0)
    step = pl.program_id(1)

    row = lax.broadcasted_iota(jnp.int32, (tile, tile), 0)
    col = lax.broadcasted_iota(jnp.int32, (tile, tile), 1)
    causal = col < row
    suffix_mat = jnp.where(row > col, 1.0, 0.0).astype(BF16)


    def window_keys(prev_ref, cur_ref, far_ref, t, g, kind):
        hs = slice(g * SB_DH, (g + 1) * SB_DH)
        if kind == "far":
            return far_ref[:, hs]
        if kind == "diag":
            return cur_ref[0, t * tile:(t + 1) * tile, hs]
        if t == 0:
            return jnp.concatenate([prev_ref[0, :, hs], cur_ref[0, :tile, hs]], axis=0)
        return cur_ref[0, (t - 1) * tile:(t + 1) * tile, hs]

    def stage_scores(t, g, kind):
        nblk = 2 if kind == "near" else 1
        q = q_ref[0, t * tile:(t + 1) * tile, g * SB_DH:(g + 1) * SB_DH]
        z = _dot_nt(q, window_keys(kprev_ref, kcur_ref, kfar_ref, t, g, kind))
        if kind != "far":
            diag = jnp.where(causal, z[:, -tile:], SB_MASKED_LOGIT)
            z = diag if nblk == 1 else jnp.concatenate([z[:, :-tile], diag], axis=1)
        sp = jnp.maximum(z, jnp.log(1.0 + jnp.exp2(jnp.minimum(z, SB_EXP2_CLAMP))) * LOG2E)
        blocks = [sp[:, j * tile:(j + 1) * tile] for j in range(nblk)]
        return z, blocks, jnp.concatenate([blk.astype(BF16) for blk in blocks], axis=0)

    def stage_weights(scores, kind, penalty):
        z, blocks, sp_bf16 = scores
        nblk = len(blocks)
        suffix = _dot(sp_bf16, suffix_mat)
        weights = [None] * nblk
        for j in reversed(range(nblk)):
            sfx = suffix[j * tile:(j + 1) * tile]
            log_a = (z[:, j * tile:(j + 1) * tile] - blocks[j]) - sfx
            if penalty is not None:
                log_a = log_a - penalty
            weights[j] = jnp.exp2(log_a).astype(BF16)
            total = sfx[:, 0:1] + blocks[j][:, 0:1]
            penalty = total if penalty is None else penalty + total
        return jnp.concatenate(weights, axis=1), penalty

    def stage_values(weights, t, g, kind):
        return _dot(weights, window_keys(vprev_ref, vcur_ref, vfar_ref, t, g, kind))

    def run_windows(wins, penalties=None, other_tasks=()):
        n = len(wins)
        scores, weights, out = {}, {}, [None] * n

        def stagger(s):
            def task():
                if s < n:
                    scores[s] = stage_scores(*wins[s])
                w = s - 1
                if 0 <= w < n:
                    weights[w] = stage_weights(scores.pop(w), wins[w][2],
                                               None if penalties is None else penalties[w])
                w = s - 2
                if 0 <= w < n:
                    a, penalty = weights.pop(w)
                    out[w] = (stage_values(a, *wins[w]), penalty)
            return task

        steps = [stagger(s) for s in range(n + 2)]
        others = list(other_tasks)
        n_front = (len(others) + 1) // 2
        for task in steps[:1] + others[:n_front] + steps[1:-1] + others[n_front:] + steps[-1:]:
            task()
        return out

    cur = lax.rem(step, 2)
    n_out = o_ref.shape[2] // OUT_PIECE_B

    def out_piece(j):
        def task():
            sl = slice(j * OUT_PIECE_B, (j + 1) * OUT_PIECE_B)
            o_ref[0, :, sl] = x_ref[0, :, sl] + _dot(cat_ref[1 - cur], wout_ref[0, :, sl])
        return task

    out_tasks = [out_piece(j) for j in range(n_out)]

    def store(t, accs):
        rows = slice(t * tile, (t + 1) * tile)
        for g in range(heads):
            hs = slice(g * SB_DH, (g + 1) * SB_DH)
            cat_ref[cur, rows, hs] = (accs[g] * gate_ref[0, rows, hs].astype(F32)).astype(BF16)

    def log_weight_bound(pens):
        return -functools.reduce(jnp.minimum, [jnp.min(p) for p in pens])

    def fetch_far(start):
        copies = [pltpu.make_async_copy(src.at[batch_idx, pl.ds(start, tile), :], dst, far_sem.at[n])
                  for n, (src, dst) in enumerate(((k_hbm, kfar_ref), (v_hbm, vfar_ref)))]
        for c in copies:
            c.start()
        for c in copies:
            c.wait()

    def finish_tile(t, results):
        accs = [r[0] for r in results]
        penalties = [r[1] for r in results]
        store(t, accs)
        tile_idx = step * tiles_per_step + t
        bound = log_weight_bound(penalties)

        def cond(carry):
            return jnp.logical_and(carry[0] <= tile_idx, carry[1] > SB_LOG_WEIGHT_FLOOR)

        def body(carry):
            n, _, accs, pens = carry
            fetch_far(pl.multiple_of((tile_idx - n) * tile, tile))
            far = run_windows([(t, g, "far") for g in range(heads)], pens)
            accs = [accs[g] + far[g][0] for g in range(heads)]
            pens = [far[g][1] for g in range(heads)]
            return n + 1, log_weight_bound(pens), accs, pens

        @pl.when(jnp.logical_and(tile_idx >= 2, bound > SB_LOG_WEIGHT_FLOOR))
        def _():
            _, _, accs_far, _ = lax.while_loop(cond, body, (jnp.int32(2), bound, accs, penalties))
            store(t, accs_far)

    def attention(first_kind, other_tasks):
        wins = [(t, g, first_kind if t == 0 else "near")
                for t in range(tiles_per_step) for g in range(heads)]
        results = run_windows(wins, None, other_tasks)
        cat_ref[cur, :, SB_W:] = mo_ref[0]
        for t in range(tiles_per_step):
            finish_tile(t, results[t * heads:(t + 1) * heads])

    @pl.when(step == 0)
    def _():
        attention("diag", ())

    @pl.when(jnp.logical_and(step > 0, step < n_steps))
    def _():
        attention("near", out_tasks)

    @pl.when(step == n_steps)
    def _():
        for task in out_tasks:
            task()


def _sb_out(x, q, k, v, gate, mo, w_out_bf16, lb):
    batch, seq, _ = x.shape
    rows = ROW_TILE_SB
    n_steps = seq // rows
    tiles_per_step = rows // SB_TILE
    b_out_in = w_out_bf16.shape[1]
    attn_idx = lambda i: jnp.minimum(i, n_steps - 1)
    attn_spec = lambda w: pl.BlockSpec((1, rows, w), lambda b, i: (b, attn_idx(i), 0))
    proj_spec = lambda w: pl.BlockSpec((1, rows, w), lambda b, i: (b, jnp.maximum(i - 1, 0), 0))
    prev_spec = pl.BlockSpec(
        (1, SB_TILE, SB_W), lambda b, i: (b, jnp.maximum(attn_idx(i) * tiles_per_step - 1, 0), 0))
    any_spec = pl.BlockSpec(memory_space=pl.ANY)
    return pl.pallas_call(
        functools.partial(_sb_out_kernel, n_steps=n_steps),
        grid=(batch, n_steps + 1),
        in_specs=[
            proj_spec(D_MODEL), attn_spec(SB_W), prev_spec, attn_spec(SB_W), prev_spec,
            attn_spec(SB_W), any_spec, any_spec, attn_spec(SB_W), attn_spec(MEM_W),
            _const_spec((1, b_out_in, D_MODEL), lambda b, i: (lb, 0, 0)),
        ],
        out_specs=proj_spec(D_MODEL),
        out_shape=jax.ShapeDtypeStruct(x.shape, F32),
        scratch_shapes=[pltpu.VMEM((2, rows, b_out_in), BF16),
                        pltpu.VMEM((SB_TILE, SB_W), BF16),
                        pltpu.VMEM((SB_TILE, SB_W), BF16),
                        pltpu.SemaphoreType.DMA((2,))],
        compiler_params=pltpu.CompilerParams(
            dimension_semantics=("arbitrary", "arbitrary"),
            vmem_limit_bytes=VMEM_LIMIT_BYTES),
        name="sb_out",
    )(x, q, k, k, v, v, k, v, gate, mo, w_out_bf16)


def kernel(x, mem, positions, g_norm_a, w_in_a, g_ret_head, w_out_a, g_kv, w_kv, g_norm_b, w_in_b,
           w_out_b, g_mem, w_mem_kv, g_mem_q, g_mem_k):
    n_a = g_norm_a.shape[0]
    n_b = g_norm_b.shape[0]
    batch, seq, d_model = x.shape
    assert d_model == D_MODEL and n_a >= 1 and n_b >= 1
    assert seq % ROW_TILE_A == 0 and seq % ROW_TILE_B == 0 and seq % ROW_TILE_SB == 0
    assert ROW_TILE_A % RET_CHUNK == 0 and ROW_TILE_SB % SB_TILE == 0

    flat = lambda w: w.reshape(-1, w.shape[-1])
    kt, mv, casted = _memory_kv(mem, g_mem, w_mem_kv, g_mem_k, [flat(w_in_a), flat(w_out_a)])
    w_in_a_bf16, w_out_a_bf16 = (c.reshape(w.shape) for c, w in zip(casted, (w_in_a, w_out_a)))

    inv_freq = ROPE_BASE ** (-jnp.arange(0, RET_DK // 2, dtype=F32) * 2.0 / RET_DK)
    invf = inv_freq.reshape(RET_DK // 2, 1)
    pos_f32 = positions.astype(F32).reshape(batch, 1, seq)

    later = [w_in_b, w_out_b, w_kv]
    later_bf16 = None
    for la in range(n_a):
        x, casted = _layer_a(x, pos_f32, invf, g_norm_a, w_in_a_bf16, g_ret_head, w_out_a_bf16,
                             kt, mv, g_mem_q, la, la,
                             [flat(w) for w in later] if la == 0 else ())
        later_bf16 = later_bf16 or casted
    w_in_b_bf16, w_out_b_bf16, w_kv_bf16 = (c.reshape(w.shape) for c, w in zip(later_bf16, later))

    k_shared = v_shared = None
    for lb in range(n_b):
        layer = n_a + lb
        q, k_new, v_new, gate, mo = _proj_b(x, g_kv, g_norm_b, w_kv_bf16, w_in_b_bf16, kt, mv,
                                            g_mem_q, lb, layer)
        if lb == 0:
            k_shared, v_shared = k_new, v_new
        x = _sb_out(x, q, k_shared, v_shared, gate, mo, w_out_b_bf16, lb)
    return x
```

```python
import functools
import math

import numpy as np
import jax
import jax.numpy as jnp
from jax import lax
from jax.experimental import pallas as pl
from jax.experimental.pallas import tpu as pltpu

F32 = jnp.float32
BF16 = jnp.bfloat16

D_MODEL = 1024
RET_HEADS = 4
RET_DK = 128
RET_DV = 256
RET_QK_W = RET_HEADS * RET_DK
RET_V_W = RET_HEADS * RET_DV
ROPE_BASE = 10000.0
SB_HEADS = 8
SB_DH = 128
SB_W = SB_HEADS * SB_DH
MEM_HEADS = 4
MEM_DH = 128
MEM_W = MEM_HEADS * MEM_DH
EPS = 1e-6

RET_LOG_GAMMA = tuple(
    math.log(float(np.float32(1.0 - 2.0 ** (-5.0 - h)))) for h in range(RET_HEADS))

VMEM_LIMIT_BYTES = 48 * 1024 * 1024
VMEM_LIMIT_LARGE_BYTES = 58 * 1024 * 1024

ROW_TILE_A = 1024
RET_CHUNK = 256
IN_PIECE_A = 512
OUT_PIECE_A = 256
ROW_TILE_B = 1024
OUT_PIECE_B = 256
SB_TILE = 256
ROW_TILE_SB = 512
SB_LOG_WEIGHT_FLOOR = -152.0
SB_EXP2_CLAMP = 64.0
SB_MASKED_LOGIT = -1e30
LOG2E = math.log2(math.e)
SB_Q_SCALE = LOG2E * SB_DH ** -0.5


def _rms(x, g):
    return x * lax.rsqrt(jnp.mean(x * x, axis=-1, keepdims=True) + EPS) * g


def _silu(x):
    return x * (1.0 / (1.0 + jnp.exp(-x)))


def _dot(a, b):
    return jnp.dot(a, b, preferred_element_type=F32)


def _dot_nt(a, b):
    return lax.dot_general(a, b, (((1,), (1,)), ((), ())), preferred_element_type=F32)


def _const_spec(shape, index_map):
    return pl.BlockSpec(shape, index_map, pipeline_mode=pl.Buffered(1))


def _interleaved(*task_lists):
    tagged = [((i + 0.5) / len(tasks), k, task)
              for k, tasks in enumerate(task_lists) for i, task in enumerate(tasks)]
    return [task for _, _, task in sorted(tagged, key=lambda e: e[:2])]


def _memkv_kernel(*refs, n_cast):
    cast_in = refs[:n_cast]
    mem_ref, g_ref, w_ref, gk_ref = refs[n_cast:n_cast + 4]
    cast_out = refs[n_cast + 4:2 * n_cast + 4]
    kt_ref, v_ref = refs[2 * n_cast + 4:]
    for src, dst in zip(cast_in, cast_out):
        dst[...] = src[...].astype(BF16)

    mn = _rms(mem_ref[0], g_ref[0]).astype(BF16)
    kv = _dot(mn, w_ref[0].astype(BF16))
    gk = gk_ref[0]
    for h in range(MEM_HEADS):
        kh = _rms(kv[:, h * MEM_DH:(h + 1) * MEM_DH], gk)
        kt_ref[0, 0, h * MEM_DH:(h + 1) * MEM_DH, :] = kh.T.astype(BF16)
    v_ref[0, 0] = kv[:, MEM_W:].astype(BF16)


def _memory_kv(mem, g_mem, w_mem_kv, g_mem_k, later_weights=()):
    depth = g_mem.shape[0]
    batch, mem_len, _ = mem.shape
    n_steps = depth * batch
    cast_specs = [pl.BlockSpec((w.shape[0] // n_steps, w.shape[1]), lambda l, b: (l * batch + b, 0))
                  for w in later_weights]
    outs = pl.pallas_call(
        functools.partial(_memkv_kernel, n_cast=len(later_weights)),
        grid=(depth, batch),
        in_specs=cast_specs + [
            pl.BlockSpec((1, mem_len, D_MODEL), lambda l, b: (b, 0, 0)),
            pl.BlockSpec((1, 1, D_MODEL), lambda l, b: (l, 0, 0)),
            pl.BlockSpec((1, D_MODEL, 2 * MEM_W), lambda l, b: (l, 0, 0)),
            pl.BlockSpec((1, 1, MEM_DH), lambda l, b: (l, 0, 0)),
        ],
        out_specs=cast_specs + [
            pl.BlockSpec((1, 1, MEM_W, mem_len), lambda l, b: (l, b, 0, 0)),
            pl.BlockSpec((1, 1, mem_len, MEM_W), lambda l, b: (l, b, 0, 0)),
        ],
        out_shape=[jax.ShapeDtypeStruct(w.shape, BF16) for w in later_weights] + [
            jax.ShapeDtypeStruct((depth, batch, MEM_W, mem_len), BF16),
            jax.ShapeDtypeStruct((depth, batch, mem_len, MEM_W), BF16),
        ],
        compiler_params=pltpu.CompilerParams(
            dimension_semantics=("arbitrary", "arbitrary"),
            vmem_limit_bytes=VMEM_LIMIT_BYTES),
        name="memkv",
    )(*later_weights, mem, g_mem.reshape(depth, 1, D_MODEL), w_mem_kv,
      g_mem_k.reshape(depth, 1, MEM_DH))
    return outs[-2], outs[-1], outs[:-2]


def _mem_logits(h, memq, kt_ref, gq):
    qn = _rms(memq, gq).astype(BF16)
    return _dot(qn, kt_ref[0, 0, h * MEM_DH:(h + 1) * MEM_DH, :]) * (MEM_DH ** -0.5)


def _mem_values(h, logits, mv_ref):
    e = jnp.exp(logits - jnp.max(logits, axis=-1, keepdims=True))
    return _dot(e.astype(BF16), mv_ref[0, 0, :, h * MEM_DH:(h + 1) * MEM_DH]), \
        jnp.sum(e, axis=-1, keepdims=True)


def _mem_gated(values, memgate):
    o, denom = values
    return (o / denom * _silu(memgate)).astype(BF16)


def _layer_a_kernel(*refs, n_cast):
    n_in_refs = 10
    cast_in = refs[:n_cast]
    (x_ref, pos_ref, invf_ref, gn_ref, win_ref, gret_ref, wout_ref,
     kt_ref, mv_ref, gq_ref) = refs[n_cast:n_cast + n_in_refs]
    cast_out = refs[n_cast + n_in_refs:2 * n_cast + n_in_refs]
    o_ref, state_ref, decay_ref = refs[2 * n_cast + n_in_refs:]
    for src, dst in zip(cast_in, cast_out):
        dst[...] = src[...].astype(BF16)

    tile = x_ref.shape[1]
    chunk = RET_CHUNK
    n_chunks = tile // chunk
    half = RET_DK // 2
    n_in = win_ref.shape[2] // IN_PIECE_A
    n_out = wout_ref.shape[2] // OUT_PIECE_A
    o_v = 2 * RET_QK_W
    o_g = o_v + RET_V_W
    o_mq = o_g + RET_V_W

    @pl.when(pl.program_id(1) == 0)
    def _():
        state_ref[...] = jnp.zeros_like(state_ref)

    tables = {}
    states = [state_ref[h] for h in range(RET_HEADS)]
    xs, xns = {}, {}
    u = {}
    mixed = {}
    cats = {}

    def rotary_tables():
        ang_t = invf_ref[...] * pos_ref[0]
        cos_sin = jnp.concatenate([jnp.cos(ang_t), jnp.sin(ang_t)], axis=0).T
        sin_cos = pltpu.roll(cos_sin, half, 1)
        low_lanes = lax.broadcasted_iota(jnp.int32, (tile, RET_DK), 1) < half
        tables["cos"] = jnp.where(low_lanes, cos_sin, sin_cos)
        tables["sin"] = jnp.where(low_lanes, -sin_cos, cos_sin)

    @pl.when(jnp.logical_and(pl.program_id(0) == 0, pl.program_id(1) == 0))
    def _():
        rel = (lax.broadcasted_iota(jnp.int32, (chunk, chunk), 0)
               - lax.broadcasted_iota(jnp.int32, (chunk, chunk), 1)).astype(F32)
        for h, lg in enumerate(RET_LOG_GAMMA):
            decay_ref[h] = jnp.where(rel >= 0.0, jnp.exp(jnp.maximum(rel, 0.0) * lg), 0.0)

    def decay_tables(h):
        def task():
            lg = RET_LOG_GAMMA[h]
            idx = lax.broadcasted_iota(jnp.int32, (chunk, 1), 0).astype(F32)
            tables[h] = (decay_ref[h],
                         jnp.exp((idx + 1.0) * lg),
                         jnp.exp((chunk - 1.0 - idx) * lg))
        return task

    def normalize(c):
        xs[c] = x_ref[0, c * chunk:(c + 1) * chunk, :]
        xns[c] = _rms(xs[c], gn_ref[0]).astype(BF16)

    def in_piece(c, p):
        def task():
            u[c, p] = _dot(xns[c], win_ref[0, :, p * IN_PIECE_A:(p + 1) * IN_PIECE_A])
        return task

    def cols(c, lo, width):
        p, off = divmod(lo, IN_PIECE_A)
        return u[c, p][:, off:off + width]

    part = {}

    def retention_scores(c, h):
        def task():
            rows = slice(c * chunk, (c + 1) * chunk)
            cos_c, sin_c = tables["cos"][rows], tables["sin"][rows]

            def rotary(t):
                return t * cos_c + pltpu.roll(t, half, 1) * sin_c

            q = rotary(cols(c, h * RET_DK, RET_DK)).astype(BF16)
            k = rotary(cols(c, RET_QK_W + h * RET_DK, RET_DK)) * (RET_DK ** -0.5)
            part[c, h, "q"], part[c, h, "k"] = q, k
            part[c, h, "scores"] = _dot_nt(q, k.astype(BF16))
        return task

    def retention_values(c, h):
        def task():
            decay, cross_decay, state_decay = tables[h]
            q, k = part.pop((c, h, "q")), part.pop((c, h, "k"))
            v = cols(c, o_v + h * RET_DV, RET_DV).astype(BF16)
            intra = _dot((part.pop((c, h, "scores")) * decay).astype(BF16), v)
            cross = _dot(q, states[h].astype(BF16)) * cross_decay
            states[h] = (math.exp(chunk * RET_LOG_GAMMA[h]) * states[h]
                         + _dot((k * state_decay).T.astype(BF16), v))
            part[c, h, "ret"] = intra + cross
        return task

    def retention_gated(c, h):
        def task():
            ret = _rms(part.pop((c, h, "ret")), gret_ref[0])
            mixed[c, h] = (ret * _silu(cols(c, o_g + h * RET_DV, RET_DV))).astype(BF16)
        return task

    def memory_logits(c, h):
        def task():
            part[c, h, "logits"] = _mem_logits(h, cols(c, o_mq + h * MEM_DH, MEM_DH),
                                               kt_ref, gq_ref[0])
        return task

    def memory_values(c, h):
        def task():
            part[c, h, "values"] = _mem_values(h, part.pop((c, h, "logits")), mv_ref)
        return task

    def memory_gated(c, h):
        def task():
            mixed[c, RET_HEADS + h] = _mem_gated(part.pop((c, h, "values")),
                                                 cols(c, o_mq + MEM_W + h * MEM_DH, MEM_DH))
        return task

    def mixer_groups(c):
        groups = [[retention_scores(c, h) for h in range(RET_HEADS)],
                  [memory_logits(c, h) for h in range(MEM_HEADS)],
                  [retention_values(c, h) for h in range(RET_HEADS)],
                  [memory_values(c, h) for h in range(MEM_HEADS)],
                  [retention_gated(c, h) for h in range(RET_HEADS)]
                  + [memory_gated(c, h) for h in range(MEM_HEADS)]]

        def run(group):
            def task():
                for t in group:
                    t()
            return task
        return [run(g) for g in groups]

    def out_piece(c, j):
        def task():
            if c not in cats:
                cats[c] = jnp.concatenate(
                    [mixed.pop((c, s)) for s in range(RET_HEADS + MEM_HEADS)], axis=1)
            sl = slice(j * OUT_PIECE_A, (j + 1) * OUT_PIECE_A)
            o_ref[0, c * chunk:(c + 1) * chunk, sl] = xs[c][:, sl] + _dot(cats[c], wout_ref[0, :, sl])
        return task

    setup = [rotary_tables] + [decay_tables(h) for h in range(RET_HEADS)]
    for step in range(n_chunks + 2):
        stages = [setup] if step == 0 else []
        if step < n_chunks:
            normalize(step)
            stages.append([in_piece(step, p) for p in range(n_in)])
        if 0 <= step - 1 < n_chunks:
            stages.append(mixer_groups(step - 1))
        if 0 <= step - 2 < n_chunks:
            stages.append([out_piece(step - 2, j) for j in range(n_out)])
        for task in _interleaved(*stages):
            task()

    for h in range(RET_HEADS):
        state_ref[h] = states[h]


def _layer_a(x, pos_f32, invf, g_norm, w_in_bf16, g_ret_head, w_out_bf16, kt, mv, g_mem_q,
             la, layer, later_weights=()):
    batch, seq, _ = x.shape
    tile = ROW_TILE_A
    n_a = g_norm.shape[0]
    depth = g_mem_q.shape[0]
    a_in_w = w_in_bf16.shape[2]
    a_out_in = w_out_bf16.shape[1]
    mem_len = mv.shape[2]
    steps_per_batch = seq // tile
    n_steps = batch * steps_per_batch
    cast_specs = [pl.BlockSpec((w.shape[0] // n_steps, w.shape[1]),
                               lambda b, t: (b * steps_per_batch + t, 0)) for w in later_weights]
    outs = pl.pallas_call(
        functools.partial(_layer_a_kernel, n_cast=len(later_weights)),
        grid=(batch, steps_per_batch),
        in_specs=cast_specs + [
            pl.BlockSpec((1, tile, D_MODEL), lambda b, t: (b, t, 0)),
            pl.BlockSpec((1, 1, tile), lambda b, t: (b, 0, t)),
            _const_spec((RET_DK // 2, 1), lambda b, t: (0, 0)),
            _const_spec((1, 1, D_MODEL), lambda b, t: (la, 0, 0)),
            _const_spec((1, D_MODEL, a_in_w), lambda b, t: (la, 0, 0)),
            _const_spec((1, 1, RET_DV), lambda b, t: (la, 0, 0)),
            _const_spec((1, a_out_in, D_MODEL), lambda b, t: (la, 0, 0)),
            pl.BlockSpec((1, 1, MEM_W, mem_len), lambda b, t: (layer, b, 0, 0)),
            pl.BlockSpec((1, 1, mem_len, MEM_W), lambda b, t: (layer, b, 0, 0)),
            _const_spec((1, 1, MEM_DH), lambda b, t: (layer, 0, 0)),
        ],
        out_specs=cast_specs + [pl.BlockSpec((1, tile, D_MODEL), lambda b, t: (b, t, 0))],
        out_shape=[jax.ShapeDtypeStruct(w.shape, BF16) for w in later_weights]
        + [jax.ShapeDtypeStruct(x.shape, F32)],
        scratch_shapes=[pltpu.VMEM((RET_HEADS, RET_DK, RET_DV), F32),
                        pltpu.VMEM((RET_HEADS, RET_CHUNK, RET_CHUNK), F32)],
        compiler_params=pltpu.CompilerParams(
            dimension_semantics=("arbitrary", "arbitrary"),
            vmem_limit_bytes=VMEM_LIMIT_LARGE_BYTES),
        name="layer_a",
    )(*later_weights, x, pos_f32, invf, g_norm.reshape(n_a, 1, D_MODEL), w_in_bf16,
      g_ret_head.reshape(n_a, 1, RET_DV), w_out_bf16, kt, mv, g_mem_q.reshape(depth, 1, MEM_DH))
    return outs[-1], outs[:-1]


def _proj_b_kernel(x_ref, gkv_ref, gnb_ref, wkv_ref, win_ref, kt_ref, mv_ref, gq_ref,
                   q_ref, k_ref, v_ref, gate_ref, mo_ref):
    x = x_ref[0]
    xr = x * lax.rsqrt(jnp.mean(x * x, axis=-1, keepdims=True) + EPS)
    xkv = (xr * gkv_ref[...]).astype(BF16)
    xb = (xr * gnb_ref[0]).astype(BF16)
    heads = range(MEM_HEADS)
    head = lambda t, h: t[:, h * MEM_DH:(h + 1) * MEM_DH]
    o_mq = 2 * SB_W
    memq = _dot(xb, win_ref[0, :, o_mq:o_mq + MEM_W])
    memgate = _dot(xb, win_ref[0, :, o_mq + MEM_W:])
    logits = [_mem_logits(h, head(memq, h), kt_ref, gq_ref[0]) for h in heads]
    gate_ref[0] = _silu(_dot(xb, win_ref[0, :, SB_W:o_mq])).astype(BF16)
    values = [_mem_values(h, logits[h], mv_ref) for h in heads]
    k_ref[0] = _dot(xkv, wkv_ref[:, :SB_W]).astype(BF16)
    mo_ref[0] = jnp.concatenate([_mem_gated(values[h], head(memgate, h)) for h in heads], axis=1)
    v_ref[0] = _dot(xkv, wkv_ref[:, SB_W:]).astype(BF16)
    q_ref[0] = (_dot(xb, win_ref[0, :, :SB_W]) * SB_Q_SCALE).astype(BF16)


def _proj_b(x, g_kv, g_norm_b, w_kv_bf16, w_in_bf16, kt, mv, g_mem_q, lb, layer):
    batch, seq, _ = x.shape
    tile = ROW_TILE_B
    n_b = g_norm_b.shape[0]
    depth = g_mem_q.shape[0]
    mem_len = mv.shape[2]
    b_in_w = w_in_bf16.shape[2]
    row_spec = lambda w: pl.BlockSpec((1, tile, w), lambda b, t: (b, t, 0))
    widths = (SB_W, SB_W, SB_W, SB_W, MEM_W)
    return pl.pallas_call(
        _proj_b_kernel,
        grid=(batch, seq // tile),
        in_specs=[
            row_spec(D_MODEL),
            _const_spec((1, D_MODEL), lambda b, t: (0, 0)),
            _const_spec((1, 1, D_MODEL), lambda b, t: (lb, 0, 0)),
            _const_spec((D_MODEL, 2 * SB_W), lambda b, t: (0, 0)),
            _const_spec((1, D_MODEL, b_in_w), lambda b, t: (lb, 0, 0)),
            pl.BlockSpec((1, 1, MEM_W, mem_len), lambda b, t: (layer, b, 0, 0)),
            pl.BlockSpec((1, 1, mem_len, MEM_W), lambda b, t: (layer, b, 0, 0)),
            _const_spec((1, 1, MEM_DH), lambda b, t: (layer, 0, 0)),
        ],
        out_specs=[row_spec(w) for w in widths],
        out_shape=[jax.ShapeDtypeStruct((batch, seq, w), BF16) for w in widths],
        compiler_params=pltpu.CompilerParams(
            dimension_semantics=("arbitrary", "arbitrary"),
            vmem_limit_bytes=VMEM_LIMIT_LARGE_BYTES),
        name="proj_b",
    )(x, g_kv.reshape(1, D_MODEL), g_norm_b.reshape(n_b, 1, D_MODEL), w_kv_bf16, w_in_bf16,
      kt, mv, g_mem_q.reshape(depth, 1, MEM_DH))


def _sb_out_kernel(x_ref, q_ref, kprev_ref, kcur_ref, vprev_ref, vcur_ref, k_hbm, v_hbm,
                   gate_ref, mo_ref, wout_ref, o_ref, cat_ref, kfar_ref, vfar_ref, far_sem, *,
                   n_steps):
    tile = SB_TILE
    tiles_per_step = q_ref.shape[1] // tile
    heads = q_ref.shape[2] // SB_DH
    batch_idx = pl.program_id(0)
    step = pl.program_id(1)

    row = lax.broadcasted_iota(jnp.int32, (tile, tile), 0)
    col = lax.broadcasted_iota(jnp.int32, (tile, tile), 1)
    causal = col < row
    suffix_mat = jnp.where(row > col, 1.0, 0.0).astype(BF16)


    def window_keys(prev_ref, cur_ref, far_ref, t, g, kind):
        hs = slice(g * SB_DH, (g + 1) * SB_DH)
        if kind == "far":
            return far_ref[:, hs]
        if kind == "diag":
            return cur_ref[0, t * tile:(t + 1) * tile, hs]
        if t == 0:
            return jnp.concatenate([prev_ref[0, :, hs], cur_ref[0, :tile, hs]], axis=0)
        return cur_ref[0, (t - 1) * tile:(t + 1) * tile, hs]

    def stage_scores(t, g, kind):
        nblk = 2 if kind == "near" else 1
        q = q_ref[0, t * tile:(t + 1) * tile, g * SB_DH:(g + 1) * SB_DH]
        z = _dot_nt(q, window_keys(kprev_ref, kcur_ref, kfar_ref, t, g, kind))
        if kind != "far":
            diag = jnp.where(causal, z[:, -tile:], SB_MASKED_LOGIT)
            z = diag if nblk == 1 else jnp.concatenate([z[:, :-tile], diag], axis=1)
        sp = jnp.maximum(z, jnp.log(1.0 + jnp.exp2(jnp.minimum(z, SB_EXP2_CLAMP))) * LOG2E)
        blocks = [sp[:, j * tile:(j + 1) * tile] for j in range(nblk)]
        return z, blocks, jnp.concatenate([blk.astype(BF16) for blk in blocks], axis=0)

    def stage_weights(scores, kind, penalty):
        z, blocks, sp_bf16 = scores
        nblk = len(blocks)
        suffix = _dot(sp_bf16, suffix_mat)
        weights = [None] * nblk
        for j in reversed(range(nblk)):
            sfx = suffix[j * tile:(j + 1) * tile]
            log_a = (z[:, j * tile:(j + 1) * tile] - blocks[j]) - sfx
            if penalty is not None:
                log_a = log_a - penalty
            weights[j] = jnp.exp2(log_a).astype(BF16)
            total = sfx[:, 0:1] + blocks[j][:, 0:1]
            penalty = total if penalty is None else penalty + total
        return jnp.concatenate(weights, axis=1), penalty

    def stage_values(weights, t, g, kind):
        return _dot(weights, window_keys(vprev_ref, vcur_ref, vfar_ref, t, g, kind))

    def run_windows(wins, penalties=None, other_tasks=()):
        n = len(wins)
        scores, weights, out = {}, {}, [None] * n

        def stagger(s):
            def task():
                if s < n:
                    scores[s] = stage_scores(*wins[s])
                w = s - 1
                if 0 <= w < n:
                    weights[w] = stage_weights(scores.pop(w), wins[w][2],
                                               None if penalties is None else penalties[w])
                w = s - 2
                if 0 <= w < n:
                    a, penalty = weights.pop(w)
                    out[w] = (stage_values(a, *wins[w]), penalty)
            return task

        steps = [stagger(s) for s in range(n + 2)]
        others = list(other_tasks)
        n_front = (len(others) + 1) // 2
        for task in steps[:1] + others[:n_front] + steps[1:-1] + others[n_front:] + steps[-1:]:
            task()
        return out

    cur = lax.rem(step, 2)
    n_out = o_ref.shape[2] // OUT_PIECE_B

    def out_piece(j):
        def task():
            sl = slice(j * OUT_PIECE_B, (j + 1) * OUT_PIECE_B)
            o_ref[0, :, sl] = x_ref[0, :, sl] + _dot(cat_ref[1 - cur], wout_ref[0, :, sl])
        return task

    out_tasks = [out_piece(j) for j in range(n_out)]

    def store(t, accs):
        rows = slice(t * tile, (t + 1) * tile)
        for g in range(heads):
            hs = slice(g * SB_DH, (g + 1) * SB_DH)
            cat_ref[cur, rows, hs] = (accs[g] * gate_ref[0, rows, hs].astype(F32)).astype(BF16)

    def log_weight_bound(pens):
        return -functools.reduce(jnp.minimum, [jnp.min(p) for p in pens])

    def fetch_far(start):
        copies = [pltpu.make_async_copy(src.at[batch_idx, pl.ds(start, tile), :], dst, far_sem.at[n])
                  for n, (src, dst) in enumerate(((k_hbm, kfar_ref), (v_hbm, vfar_ref)))]
        for c in copies:
            c.start()
        for c in copies:
            c.wait()

    def finish_tile(t, results):
        accs = [r[0] for r in results]
        penalties = [r[1] for r in results]
        store(t, accs)
        tile_idx = step * tiles_per_step + t
        bound = log_weight_bound(penalties)

        def cond(carry):
            return jnp.logical_and(carry[0] <= tile_idx, carry[1] > SB_LOG_WEIGHT_FLOOR)

        def body(carry):
            n, _, accs, pens = carry
            fetch_far(pl.multiple_of((tile_idx - n) * tile, tile))
            far = run_windows([(t, g, "far") for g in range(heads)], pens)
            accs = [accs[g] + far[g][0] for g in range(heads)]
            pens = [far[g][1] for g in range(heads)]
            return n + 1, log_weight_bound(pens), accs, pens

        @pl.when(jnp.logical_and(tile_idx >= 2, bound > SB_LOG_WEIGHT_FLOOR))
        def _():
            _, _, accs_far, _ = lax.while_loop(cond, body, (jnp.int32(2), bound, accs, penalties))
            store(t, accs_far)

    def attention(first_kind, other_tasks):
        wins = [(t, g, first_kind if t == 0 else "near")
                for t in range(tiles_per_step) for g in range(heads)]
        results = run_windows(wins, None, other_tasks)
        cat_ref[cur, :, SB_W:] = mo_ref[0]
        for t in range(tiles_per_step):
            finish_tile(t, results[t * heads:(t + 1) * heads])

    @pl.when(step == 0)
    def _():
        attention("diag", ())

    @pl.when(jnp.logical_and(step > 0, step < n_steps))
    def _():
        attention("near", out_tasks)

    @pl.when(step == n_steps)
    def _():
        for task in out_tasks:
            task()


def _sb_out(x, q, k, v, gate, mo, w_out_bf16, lb):
    batch, seq, _ = x.shape
    rows = ROW_TILE_SB
    n_steps = seq // rows
    tiles_per_step = rows // SB_TILE
    b_out_in = w_out_bf16.shape[1]
    attn_idx = lambda i: jnp.minimum(i, n_steps - 1)
    attn_spec = lambda w: pl.BlockSpec((1, rows, w), lambda b, i: (b, attn_idx(i), 0))
    proj_spec = lambda w: pl.BlockSpec((1, rows, w), lambda b, i: (b, jnp.maximum(i - 1, 0), 0))
    prev_spec = pl.BlockSpec(
        (1, SB_TILE, SB_W), lambda b, i: (b, jnp.maximum(attn_idx(i) * tiles_per_step - 1, 0), 0))
    any_spec = pl.BlockSpec(memory_space=pl.ANY)
    return pl.pallas_call(
        functools.partial(_sb_out_kernel, n_steps=n_steps),
        grid=(batch, n_steps + 1),
        in_specs=[
            proj_spec(D_MODEL), attn_spec(SB_W), prev_spec, attn_spec(SB_W), prev_spec,
            attn_spec(SB_W), any_spec, any_spec, attn_spec(SB_W), attn_spec(MEM_W),
            _const_spec((1, b_out_in, D_MODEL), lambda b, i: (lb, 0, 0)),
        ],
        out_specs=proj_spec(D_MODEL),
        out_shape=jax.ShapeDtypeStruct(x.shape, F32),
        scratch_shapes=[pltpu.VMEM((2, rows, b_out_in), BF16),
                        pltpu.VMEM((SB_TILE, SB_W), BF16),
                        pltpu.VMEM((SB_TILE, SB_W), BF16),
                        pltpu.SemaphoreType.DMA((2,))],
        compiler_params=pltpu.CompilerParams(
            dimension_semantics=("arbitrary", "arbitrary"),
            vmem_limit_bytes=VMEM_LIMIT_BYTES),
        name="sb_out",
    )(x, q, k, k, v, v, k, v, gate, mo, w_out_bf16)


def kernel(x, mem, positions, g_norm_a, w_in_a, g_ret_head, w_out_a, g_kv, w_kv, g_norm_b, w_in_b,
           w_out_b, g_mem, w_mem_kv, g_mem_q, g_mem_k):
    n_a = g_norm_a.shape[0]
    n_b = g_norm_b.shape[0]
    batch, seq, d_model = x.shape
    assert d_model == D_MODEL and n_a >= 1 and n_b >= 1
    assert seq % ROW_TILE_A == 0 and seq % ROW_TILE_B == 0 and seq % ROW_TILE_SB == 0
    assert ROW_TILE_A % RET_CHUNK == 0 and ROW_TILE_SB % SB_TILE == 0

    flat = lambda w: w.reshape(-1, w.shape[-1])
    kt, mv, casted = _memory_kv(mem, g_mem, w_mem_kv, g_mem_k, [flat(w_in_a), flat(w_out_a)])
    w_in_a_bf16, w_out_a_bf16 = (c.reshape(w.shape) for c, w in zip(casted, (w_in_a, w_out_a)))

    inv_freq = ROPE_BASE ** (-jnp.arange(0, RET_DK // 2, dtype=F32) * 2.0 / RET_DK)
    invf = inv_freq.reshape(RET_DK // 2, 1)
    pos_f32 = positions.astype(F32).reshape(batch, 1, seq)

    later = [w_in_b, w_out_b, w_kv]
    later_bf16 = None
    for la in range(n_a):
        x, casted = _layer_a(x, pos_f32, invf, g_norm_a, w_in_a_bf16, g_ret_head, w_out_a_bf16,
                             kt, mv, g_mem_q, la, la,
                             [flat(w) for w in later] if la == 0 else ())
        later_bf16 = later_bf16 or casted
    w_in_b_bf16, w_out_b_bf16, w_kv_bf16 = (c.reshape(w.shape) for c, w in zip(later_bf16, later))

    k_shared = v_shared = None
    for lb in range(n_b):
        layer = n_a + lb
        q, k_new, v_new, gate, mo = _proj_b(x, g_kv, g_norm_b, w_kv_bf16, w_in_b_bf16, kt, mv,
                                            g_mem_q, lb, layer)
        if lb == 0:
            k_shared, v_shared = k_new, v_new
        x = _sb_out(x, q, k_shared, v_shared, gate, mo, w_out_b_bf16, lb)
    return x
```

```python
import functools
import math

import numpy as np
import jax
import jax.numpy as jnp
from jax import lax
from jax.experimental import pallas as pl
from jax.experimental.pallas import tpu as pltpu

F32 = jnp.float32
BF16 = jnp.bfloat16

D_MODEL = 1024
RET_HEADS = 4
RET_DK = 128
RET_DV = 256
RET_QK_W = RET_HEADS * RET_DK
RET_V_W = RET_HEADS * RET_DV
ROPE_BASE = 10000.0
SB_HEADS = 8
SB_DH = 128
SB_W = SB_HEADS * SB_DH
MEM_HEADS = 4
MEM_DH = 128
MEM_W = MEM_HEADS * MEM_DH
EPS = 1e-6

RET_LOG_GAMMA = tuple(
    math.log(float(np.float32(1.0 - 2.0 ** (-5.0 - h)))) for h in range(RET_HEADS))

VMEM_LIMIT_BYTES = 48 * 1024 * 1024
VMEM_LIMIT_LARGE_BYTES = 58 * 1024 * 1024

ROW_TILE_A = 1024
RET_CHUNK = 256
IN_PIECE_A = 512
OUT_PIECE_A = 256
ROW_TILE_B = 1024
NORM_CHUNK_B = 256
OUT_PIECE_B = 256
SB_TILE = 256
ROW_TILE_SB = 512
SB_LOG_WEIGHT_FLOOR = -152.0
SB_EXP2_CLAMP = 64.0
SB_MASKED_LOGIT = -1e30
LOG2E = math.log2(math.e)
SB_Q_SCALE = LOG2E * SB_DH ** -0.5


def _rms(x, g):
    return x * lax.rsqrt(jnp.mean(x * x, axis=-1, keepdims=True) + EPS) * g


def _silu(x):
    return x * (1.0 / (1.0 + jnp.exp(-x)))


def _dot(a, b):
    return jnp.dot(a, b, preferred_element_type=F32)


def _dot_nt(a, b):
    return lax.dot_general(a, b, (((1,), (1,)), ((), ())), preferred_element_type=F32)


def _const_spec(shape, index_map):
    return pl.BlockSpec(shape, index_map, pipeline_mode=pl.Buffered(1))


def _interleaved(*task_lists):
    tagged = [((i + 0.5) / len(tasks), k, task)
              for k, tasks in enumerate(task_lists) for i, task in enumerate(tasks)]
    return [task for _, _, task in sorted(tagged, key=lambda e: e[:2])]


def _memkv_kernel(*refs, n_cast):
    cast_in = refs[:n_cast]
    mem_ref, g_ref, w_ref, gk_ref = refs[n_cast:n_cast + 4]
    cast_out = refs[n_cast + 4:2 * n_cast + 4]
    kt_ref, v_ref = refs[2 * n_cast + 4:]
    for src, dst in zip(cast_in, cast_out):
        dst[...] = src[...].astype(BF16)

    mn = _rms(mem_ref[0], g_ref[0]).astype(BF16)
    kv = _dot(mn, w_ref[0].astype(BF16))
    gk = gk_ref[0]
    for h in range(MEM_HEADS):
        kh = _rms(kv[:, h * MEM_DH:(h + 1) * MEM_DH], gk)
        kt_ref[0, 0, h * MEM_DH:(h + 1) * MEM_DH, :] = kh.T.astype(BF16)
    v_ref[0, 0] = kv[:, MEM_W:].astype(BF16)


def _memory_kv(mem, g_mem, w_mem_kv, g_mem_k, later_weights=()):
    depth = g_mem.shape[0]
    batch, mem_len, _ = mem.shape
    n_steps = depth * batch
    cast_specs = [pl.BlockSpec((w.shape[0] // n_steps, w.shape[1]), lambda l, b: (l * batch + b, 0))
                  for w in later_weights]
    outs = pl.pallas_call(
        functools.partial(_memkv_kernel, n_cast=len(later_weights)),
        grid=(depth, batch),
        in_specs=cast_specs + [
            pl.BlockSpec((1, mem_len, D_MODEL), lambda l, b: (b, 0, 0)),
            pl.BlockSpec((1, 1, D_MODEL), lambda l, b: (l, 0, 0)),
            pl.BlockSpec((1, D_MODEL, 2 * MEM_W), lambda l, b: (l, 0, 0)),
            pl.BlockSpec((1, 1, MEM_DH), lambda l, b: (l, 0, 0)),
        ],
        out_specs=cast_specs + [
            pl.BlockSpec((1, 1, MEM_W, mem_len), lambda l, b: (l, b, 0, 0)),
            pl.BlockSpec((1, 1, mem_len, MEM_W), lambda l, b: (l, b, 0, 0)),
        ],
        out_shape=[jax.ShapeDtypeStruct(w.shape, BF16) for w in later_weights] + [
            jax.ShapeDtypeStruct((depth, batch, MEM_W, mem_len), BF16),
            jax.ShapeDtypeStruct((depth, batch, mem_len, MEM_W), BF16),
        ],
        compiler_params=pltpu.CompilerParams(
            dimension_semantics=("arbitrary", "arbitrary"),
            vmem_limit_bytes=VMEM_LIMIT_BYTES),
        name="memkv",
    )(*later_weights, mem, g_mem.reshape(depth, 1, D_MODEL), w_mem_kv,
      g_mem_k.reshape(depth, 1, MEM_DH))
    return outs[-2], outs[-1], outs[:-2]


def _mem_logits(h, memq, kt_ref, gq):
    qn = _rms(memq, gq).astype(BF16)
    return _dot(qn, kt_ref[0, 0, h * MEM_DH:(h + 1) * MEM_DH, :]) * (MEM_DH ** -0.5)


def _mem_values(h, logits, mv_ref):
    e = jnp.exp(logits - jnp.max(logits, axis=-1, keepdims=True))
    return _dot(e.astype(BF16), mv_ref[0, 0, :, h * MEM_DH:(h + 1) * MEM_DH]), \
        jnp.sum(e, axis=-1, keepdims=True)


def _mem_gated(values, memgate):
    o, denom = values
    return (o / denom * _silu(memgate)).astype(BF16)


def _layer_a_kernel(*refs, n_cast):
    n_in_refs = 10
    cast_in = refs[:n_cast]
    (x_ref, pos_ref, invf_ref, gn_ref, win_ref, gret_ref, wout_ref,
     kt_ref, mv_ref, gq_ref) = refs[n_cast:n_cast + n_in_refs]
    cast_out = refs[n_cast + n_in_refs:2 * n_cast + n_in_refs]
    o_ref, state_ref = refs[2 * n_cast + n_in_refs:]
    for src, dst in zip(cast_in, cast_out):
        dst[...] = src[...].astype(BF16)

    tile = x_ref.shape[1]
    chunk = RET_CHUNK
    n_chunks = tile // chunk
    half = RET_DK // 2
    n_in = win_ref.shape[2] // IN_PIECE_A
    n_out = wout_ref.shape[2] // OUT_PIECE_A
    o_v = 2 * RET_QK_W
    o_g = o_v + RET_V_W
    o_mq = o_g + RET_V_W

    @pl.when(pl.program_id(1) == 0)
    def _():
        state_ref[...] = jnp.zeros_like(state_ref)

    tables = {}
    states = [state_ref[h] for h in range(RET_HEADS)]
    xs, xns = {}, {}
    u = {}
    mixed = {}
    cats = {}

    def rotary_tables():
        ang_t = invf_ref[...] * pos_ref[0]
        cos_sin = jnp.concatenate([jnp.cos(ang_t), jnp.sin(ang_t)], axis=0).T
        sin_cos = pltpu.roll(cos_sin, half, 1)
        low_lanes = lax.broadcasted_iota(jnp.int32, (tile, RET_DK), 1) < half
        tables["cos"] = jnp.where(low_lanes, cos_sin, sin_cos)
        tables["sin"] = jnp.where(low_lanes, -sin_cos, cos_sin)

    def decay_tables(h):
        def task():
            lg = RET_LOG_GAMMA[h]
            rel = (lax.broadcasted_iota(jnp.int32, (chunk, chunk), 0)
                   - lax.broadcasted_iota(jnp.int32, (chunk, chunk), 1)).astype(F32)
            idx = lax.broadcasted_iota(jnp.int32, (chunk, 1), 0).astype(F32)
            tables[h] = (jnp.where(rel >= 0.0, jnp.exp(jnp.maximum(rel, 0.0) * lg), 0.0),
                         jnp.exp((idx + 1.0) * lg),
                         jnp.exp((chunk - 1.0 - idx) * lg))
        return task

    def normalize(c):
        xs[c] = x_ref[0, c * chunk:(c + 1) * chunk, :]
        xns[c] = _rms(xs[c], gn_ref[0]).astype(BF16)

    def in_piece(c, p):
        def task():
            u[c, p] = _dot(xns[c], win_ref[0, :, p * IN_PIECE_A:(p + 1) * IN_PIECE_A])
        return task

    def cols(c, lo, width):
        p, off = divmod(lo, IN_PIECE_A)
        return u[c, p][:, off:off + width]

    part = {}

    def retention_scores(c, h):
        def task():
            rows = slice(c * chunk, (c + 1) * chunk)
            cos_c, sin_c = tables["cos"][rows], tables["sin"][rows]

            def rotary(t):
                return t * cos_c + pltpu.roll(t, half, 1) * sin_c

            q = rotary(cols(c, h * RET_DK, RET_DK)).astype(BF16)
            k = rotary(cols(c, RET_QK_W + h * RET_DK, RET_DK)) * (RET_DK ** -0.5)
            part[c, h, "q"], part[c, h, "k"] = q, k
            part[c, h, "scores"] = _dot_nt(q, k.astype(BF16))
        return task

    def retention_values(c, h):
        def task():
            decay, cross_decay, state_decay = tables[h]
            q, k = part.pop((c, h, "q")), part.pop((c, h, "k"))
            v = cols(c, o_v + h * RET_DV, RET_DV).astype(BF16)
            intra = _dot((part.pop((c, h, "scores")) * decay).astype(BF16), v)
            cross = _dot(q, states[h].astype(BF16)) * cross_decay
            states[h] = (math.exp(chunk * RET_LOG_GAMMA[h]) * states[h]
                         + _dot((k * state_decay).T.astype(BF16), v))
            part[c, h, "ret"] = intra + cross
        return task

    def retention_gated(c, h):
        def task():
            ret = _rms(part.pop((c, h, "ret")), gret_ref[0])
            mixed[c, h] = (ret * _silu(cols(c, o_g + h * RET_DV, RET_DV))).astype(BF16)
        return task

    def memory_logits(c, h):
        def task():
            part[c, h, "logits"] = _mem_logits(h, cols(c, o_mq + h * MEM_DH, MEM_DH),
                                               kt_ref, gq_ref[0])
        return task

    def memory_values(c, h):
        def task():
            part[c, h, "values"] = _mem_values(h, part.pop((c, h, "logits")), mv_ref)
        return task

    def memory_gated(c, h):
        def task():
            mixed[c, RET_HEADS + h] = _mem_gated(part.pop((c, h, "values")),
                                                 cols(c, o_mq + MEM_W + h * MEM_DH, MEM_DH))
        return task

    def mixer_groups(c):
        groups = [[retention_scores(c, h) for h in range(RET_HEADS)],
                  [memory_logits(c, h) for h in range(MEM_HEADS)],
                  [retention_values(c, h) for h in range(RET_HEADS)],
                  [memory_values(c, h) for h in range(MEM_HEADS)],
                  [retention_gated(c, h) for h in range(RET_HEADS)]
                  + [memory_gated(c, h) for h in range(MEM_HEADS)]]

        def run(group):
            def task():
                for t in group:
                    t()
            return task
        return [run(g) for g in groups]

    def out_piece(c, j):
        def task():
            if c not in cats:
                cats[c] = jnp.concatenate(
                    [mixed.pop((c, s)) for s in range(RET_HEADS + MEM_HEADS)], axis=1)
            sl = slice(j * OUT_PIECE_A, (j + 1) * OUT_PIECE_A)
            o_ref[0, c * chunk:(c + 1) * chunk, sl] = xs[c][:, sl] + _dot(cats[c], wout_ref[0, :, sl])
        return task

    setup = [rotary_tables] + [decay_tables(h) for h in range(RET_HEADS)]
    for step in range(n_chunks + 2):
        stages = [setup] if step == 0 else []
        if step < n_chunks:
            normalize(step)
            stages.append([in_piece(step, p) for p in range(n_in)])
        if 0 <= step - 1 < n_chunks:
            stages.append(mixer_groups(step - 1))
        if 0 <= step - 2 < n_chunks:
            stages.append([out_piece(step - 2, j) for j in range(n_out)])
        for task in _interleaved(*stages):
            task()

    for h in range(RET_HEADS):
        state_ref[h] = states[h]


def _layer_a(x, pos_f32, invf, g_norm, w_in_bf16, g_ret_head, w_out_bf16, kt, mv, g_mem_q,
             la, layer, later_weights=()):
    batch, seq, _ = x.shape
    tile = ROW_TILE_A
    n_a = g_norm.shape[0]
    depth = g_mem_q.shape[0]
    a_in_w = w_in_bf16.shape[2]
    a_out_in = w_out_bf16.shape[1]
    mem_len = mv.shape[2]
    steps_per_batch = seq // tile
    n_steps = batch * steps_per_batch
    cast_specs = [pl.BlockSpec((w.shape[0] // n_steps, w.shape[1]),
                               lambda b, t: (b * steps_per_batch + t, 0)) for w in later_weights]
    outs = pl.pallas_call(
        functools.partial(_layer_a_kernel, n_cast=len(later_weights)),
        grid=(batch, steps_per_batch),
        in_specs=cast_specs + [
            pl.BlockSpec((1, tile, D_MODEL), lambda b, t: (b, t, 0)),
            pl.BlockSpec((1, 1, tile), lambda b, t: (b, 0, t)),
            _const_spec((RET_DK // 2, 1), lambda b, t: (0, 0)),
            _const_spec((1, 1, D_MODEL), lambda b, t: (la, 0, 0)),
            _const_spec((1, D_MODEL, a_in_w), lambda b, t: (la, 0, 0)),
            _const_spec((1, 1, RET_DV), lambda b, t: (la, 0, 0)),
            _const_spec((1, a_out_in, D_MODEL), lambda b, t: (la, 0, 0)),
            pl.BlockSpec((1, 1, MEM_W, mem_len), lambda b, t: (layer, b, 0, 0)),
            pl.BlockSpec((1, 1, mem_len, MEM_W), lambda b, t: (layer, b, 0, 0)),
            _const_spec((1, 1, MEM_DH), lambda b, t: (layer, 0, 0)),
        ],
        out_specs=cast_specs + [pl.BlockSpec((1, tile, D_MODEL), lambda b, t: (b, t, 0))],
        out_shape=[jax.ShapeDtypeStruct(w.shape, BF16) for w in later_weights]
        + [jax.ShapeDtypeStruct(x.shape, F32)],
        scratch_shapes=[pltpu.VMEM((RET_HEADS, RET_DK, RET_DV), F32)],
        compiler_params=pltpu.CompilerParams(
            dimension_semantics=("arbitrary", "arbitrary"),
            vmem_limit_bytes=VMEM_LIMIT_BYTES),
        name="layer_a",
    )(*later_weights, x, pos_f32, invf, g_norm.reshape(n_a, 1, D_MODEL), w_in_bf16,
      g_ret_head.reshape(n_a, 1, RET_DV), w_out_bf16, kt, mv, g_mem_q.reshape(depth, 1, MEM_DH))
    return outs[-1], outs[:-1]


def _proj_b_kernel(x_ref, gkv_ref, gnb_ref, wkv_ref, win_ref, kt_ref, mv_ref, gq_ref,
                   q_ref, k_ref, v_ref, gate_ref, mo_ref):
    xkv_parts, xb_parts = [], []
    for c in range(x_ref.shape[1] // NORM_CHUNK_B):
        x = x_ref[0, c * NORM_CHUNK_B:(c + 1) * NORM_CHUNK_B, :]
        xr = x * lax.rsqrt(jnp.mean(x * x, axis=-1, keepdims=True) + EPS)
        xkv_parts.append((xr * gkv_ref[...]).astype(BF16))
        xb_parts.append((xr * gnb_ref[0]).astype(BF16))
    xkv = jnp.concatenate(xkv_parts, axis=0)
    xb = jnp.concatenate(xb_parts, axis=0)
    heads = range(MEM_HEADS)
    head = lambda t, h: t[:, h * MEM_DH:(h + 1) * MEM_DH]
    o_mq = 2 * SB_W
    memq = _dot(xb, win_ref[0, :, o_mq:o_mq + MEM_W])
    memgate = _dot(xb, win_ref[0, :, o_mq + MEM_W:])
    logits = [_mem_logits(h, head(memq, h), kt_ref, gq_ref[0]) for h in heads]
    gate_ref[0] = _silu(_dot(xb, win_ref[0, :, SB_W:o_mq])).astype(BF16)
    values = [_mem_values(h, logits[h], mv_ref) for h in heads]
    k_ref[0] = _dot(xkv, wkv_ref[:, :SB_W]).astype(BF16)
    mo_ref[0] = jnp.concatenate([_mem_gated(values[h], head(memgate, h)) for h in heads], axis=1)
    v_ref[0] = _dot(xkv, wkv_ref[:, SB_W:]).astype(BF16)
    q_ref[0] = (_dot(xb, win_ref[0, :, :SB_W]) * SB_Q_SCALE).astype(BF16)


def _proj_b(x, g_kv, g_norm_b, w_kv_bf16, w_in_bf16, kt, mv, g_mem_q, lb, layer):
    batch, seq, _ = x.shape
    tile = ROW_TILE_B
    n_b = g_norm_b.shape[0]
    depth = g_mem_q.shape[0]
    mem_len = mv.shape[2]
    b_in_w = w_in_bf16.shape[2]
    row_spec = lambda w: pl.BlockSpec((1, tile, w), lambda b, t: (b, t, 0))
    widths = (SB_W, SB_W, SB_W, SB_W, MEM_W)
    return pl.pallas_call(
        _proj_b_kernel,
        grid=(batch, seq // tile),
        in_specs=[
            row_spec(D_MODEL),
            _const_spec((1, D_MODEL), lambda b, t: (0, 0)),
            _const_spec((1, 1, D_MODEL), lambda b, t: (lb, 0, 0)),
            _const_spec((D_MODEL, 2 * SB_W), lambda b, t: (0, 0)),
            _const_spec((1, D_MODEL, b_in_w), lambda b, t: (lb, 0, 0)),
            pl.BlockSpec((1, 1, MEM_W, mem_len), lambda b, t: (layer, b, 0, 0)),
            pl.BlockSpec((1, 1, mem_len, MEM_W), lambda b, t: (layer, b, 0, 0)),
            _const_spec((1, 1, MEM_DH), lambda b, t: (layer, 0, 0)),
        ],
        out_specs=[row_spec(w) for w in widths],
        out_shape=[jax.ShapeDtypeStruct((batch, seq, w), BF16) for w in widths],
        compiler_params=pltpu.CompilerParams(
            dimension_semantics=("arbitrary", "arbitrary"),
            vmem_limit_bytes=VMEM_LIMIT_LARGE_BYTES),
        name="proj_b",
    )(x, g_kv.reshape(1, D_MODEL), g_norm_b.reshape(n_b, 1, D_MODEL), w_kv_bf16, w_in_bf16,
      kt, mv, g_mem_q.reshape(depth, 1, MEM_DH))


def _sb_out_kernel(x_ref, q_ref, kprev_ref, kcur_ref, vprev_ref, vcur_ref, k_hbm, v_hbm,
                   gate_ref, mo_ref, wout_ref, o_ref, cat_ref, kfar_ref, vfar_ref, far_sem, *,
                   n_steps):
    tile = SB_TILE
    tiles_per_step = q_ref.shape[1] // tile
    heads = q_ref.shape[2] // SB_DH
    batch_idx = pl.program_id(0)
    step = pl.program_id(1)

    row = lax.broadcasted_iota(jnp.int32, (tile, tile), 0)
    col = lax.broadcasted_iota(jnp.int32, (tile, tile), 1)
    causal = col < row
    suffix_mat = jnp.where(row > col, 1.0, 0.0).astype(BF16)


    def window_keys(prev_ref, cur_ref, far_ref, t, g, kind):
        hs = slice(g * SB_DH, (g + 1) * SB_DH)
        if kind == "far":
            return far_ref[:, hs]
        if kind == "diag":
            return cur_ref[0, t * tile:(t + 1) * tile, hs]
        if t == 0:
            return jnp.concatenate([prev_ref[0, :, hs], cur_ref[0, :tile, hs]], axis=0)
        return cur_ref[0, (t - 1) * tile:(t + 1) * tile, hs]

    def stage_scores(t, g, kind):
        nblk = 2 if kind == "near" else 1
        q = q_ref[0, t * tile:(t + 1) * tile, g * SB_DH:(g + 1) * SB_DH]
        z = _dot_nt(q, window_keys(kprev_ref, kcur_ref, kfar_ref, t, g, kind))
        if kind != "far":
            diag = jnp.where(causal, z[:, -tile:], SB_MASKED_LOGIT)
            z = diag if nblk == 1 else jnp.concatenate([z[:, :-tile], diag], axis=1)
        sp = jnp.maximum(z, jnp.log(1.0 + jnp.exp2(jnp.minimum(z, SB_EXP2_CLAMP))) * LOG2E)
        blocks = [sp[:, j * tile:(j + 1) * tile] for j in range(nblk)]
        return z, blocks, jnp.concatenate([blk.astype(BF16) for blk in blocks], axis=0)

    def stage_weights(scores, kind, penalty):
        z, blocks, sp_bf16 = scores
        nblk = len(blocks)
        suffix = _dot(sp_bf16, suffix_mat)
        weights = [None] * nblk
        for j in reversed(range(nblk)):
            sfx = suffix[j * tile:(j + 1) * tile]
            log_a = (z[:, j * tile:(j + 1) * tile] - blocks[j]) - sfx
            if penalty is not None:
                log_a = log_a - penalty
            weights[j] = jnp.exp2(log_a).astype(BF16)
            total = sfx[:, 0:1] + blocks[j][:, 0:1]
            penalty = total if penalty is None else penalty + total
        return jnp.concatenate(weights, axis=1), penalty

    def stage_values(weights, t, g, kind):
        return _dot(weights, window_keys(vprev_ref, vcur_ref, vfar_ref, t, g, kind))

    def run_windows(wins, penalties=None, other_tasks=()):
        n = len(wins)
        scores, weights, out = {}, {}, [None] * n

        def stagger(s):
            def task():
                if s < n:
                    scores[s] = stage_scores(*wins[s])
                w = s - 1
                if 0 <= w < n:
                    weights[w] = stage_weights(scores.pop(w), wins[w][2],
                                               None if penalties is None else penalties[w])
                w = s - 2
                if 0 <= w < n:
                    a, penalty = weights.pop(w)
                    out[w] = (stage_values(a, *wins[w]), penalty)
            return task

        steps = [stagger(s) for s in range(n + 2)]
        others = list(other_tasks)
        n_front = (len(others) + 1) // 2
        for task in steps[:1] + others[:n_front] + steps[1:-1] + others[n_front:] + steps[-1:]:
            task()
        return out

    cur = lax.rem(step, 2)
    n_out = o_ref.shape[2] // OUT_PIECE_B

    def out_piece(j):
        def task():
            sl = slice(j * OUT_PIECE_B, (j + 1) * OUT_PIECE_B)
            o_ref[0, :, sl] = x_ref[0, :, sl] + _dot(cat_ref[1 - cur], wout_ref[0, :, sl])
        return task

    out_tasks = [out_piece(j) for j in range(n_out)]

    def store(t, accs):
        rows = slice(t * tile, (t + 1) * tile)
        for g in range(heads):
            hs = slice(g * SB_DH, (g + 1) * SB_DH)
            cat_ref[cur, rows, hs] = (accs[g] * gate_ref[0, rows, hs].astype(F32)).astype(BF16)

    def log_weight_bound(pens):
        return -functools.reduce(jnp.minimum, [jnp.min(p) for p in pens])

    def fetch_far(start):
        copies = [pltpu.make_async_copy(src.at[batch_idx, pl.ds(start, tile), :], dst, far_sem.at[n])
                  for n, (src, dst) in enumerate(((k_hbm, kfar_ref), (v_hbm, vfar_ref)))]
        for c in copies:
            c.start()
        for c in copies:
            c.wait()

    def finish_tile(t, results):
        accs = [r[0] for r in results]
        penalties = [r[1] for r in results]
        store(t, accs)
        tile_idx = step * tiles_per_step + t
        bound = log_weight_bound(penalties)

        def cond(carry):
            return jnp.logical_and(carry[0] <= tile_idx, carry[1] > SB_LOG_WEIGHT_FLOOR)

        def body(carry):
            n, _, accs, pens = carry
            fetch_far(pl.multiple_of((tile_idx - n) * tile, tile))
            far = run_windows([(t, g, "far") for g in range(heads)], pens)
            accs = [accs[g] + far[g][0] for g in range(heads)]
            pens = [far[g][1] for g in range(heads)]
            return n + 1, log_weight_bound(pens), accs, pens

        @pl.when(jnp.logical_and(tile_idx >= 2, bound > SB_LOG_WEIGHT_FLOOR))
        def _():
            _, _, accs_far, _ = lax.while_loop(cond, body, (jnp.int32(2), bound, accs, penalties))
            store(t, accs_far)

    def attention(first_kind, other_tasks):
        wins = [(t, g, first_kind if t == 0 else "near")
                for t in range(tiles_per_step) for g in range(heads)]
        results = run_windows(wins, None, other_tasks)
        cat_ref[cur, :, SB_W:] = mo_ref[0]
        for t in range(tiles_per_step):
            finish_tile(t, results[t * heads:(t + 1) * heads])

    @pl.when(step == 0)
    def _():
        attention("diag", ())

    @pl.when(jnp.logical_and(step > 0, step < n_steps))
    def _():
        attention("near", out_tasks)

    @pl.when(step == n_steps)
    def _():
        for task in out_tasks:
            task()


def _sb_out(x, q, k, v, gate, mo, w_out_bf16, lb):
    batch, seq, _ = x.shape
    rows = ROW_TILE_SB
    n_steps = seq // rows
    tiles_per_step = rows // SB_TILE
    b_out_in = w_out_bf16.shape[1]
    attn_idx = lambda i: jnp.minimum(i, n_steps - 1)
    attn_spec = lambda w: pl.BlockSpec((1, rows, w), lambda b, i: (b, attn_idx(i), 0))
    proj_spec = lambda w: pl.BlockSpec((1, rows, w), lambda b, i: (b, jnp.maximum(i - 1, 0), 0))
    prev_spec = pl.BlockSpec(
        (1, SB_TILE, SB_W), lambda b, i: (b, jnp.maximum(attn_idx(i) * tiles_per_step - 1, 0), 0))
    any_spec = pl.BlockSpec(memory_space=pl.ANY)
    return pl.pallas_call(
        functools.partial(_sb_out_kernel, n_steps=n_steps),
        grid=(batch, n_steps + 1),
        in_specs=[
            proj_spec(D_MODEL), attn_spec(SB_W), prev_spec, attn_spec(SB_W), prev_spec,
            attn_spec(SB_W), any_spec, any_spec, attn_spec(SB_W), attn_spec(MEM_W),
            _const_spec((1, b_out_in, D_MODEL), lambda b, i: (lb, 0, 0)),
        ],
        out_specs=proj_spec(D_MODEL),
        out_shape=jax.ShapeDtypeStruct(x.shape, F32),
        scratch_shapes=[pltpu.VMEM((2, rows, b_out_in), BF16),
                        pltpu.VMEM((SB_TILE, SB_W), BF16),
                        pltpu.VMEM((SB_TILE, SB_W), BF16),
                        pltpu.SemaphoreType.DMA((2,))],
        compiler_params=pltpu.CompilerParams(
            dimension_semantics=("arbitrary", "arbitrary"),
            vmem_limit_bytes=VMEM_LIMIT_BYTES),
        name="sb_out",
    )(x, q, k, k, v, v, k, v, gate, mo, w_out_bf16)


def kernel(x, mem, positions, g_norm_a, w_in_a, g_ret_head, w_out_a, g_kv, w_kv, g_norm_b, w_in_b,
           w_out_b, g_mem, w_mem_kv, g_mem_q, g_mem_k):
    n_a = g_norm_a.shape[0]
    n_b = g_norm_b.shape[0]
    batch, seq, d_model = x.shape
    assert d_model == D_MODEL and n_a >= 1 and n_b >= 1
    assert seq % ROW_TILE_A == 0 and seq % ROW_TILE_B == 0 and seq % ROW_TILE_SB == 0
    assert ROW_TILE_A % RET_CHUNK == 0 and ROW_TILE_SB % SB_TILE == 0

    flat = lambda w: w.reshape(-1, w.shape[-1])
    kt, mv, casted = _memory_kv(mem, g_mem, w_mem_kv, g_mem_k, [flat(w_in_a), flat(w_out_a)])
    w_in_a_bf16, w_out_a_bf16 = (c.reshape(w.shape) for c, w in zip(casted, (w_in_a, w_out_a)))

    inv_freq = ROPE_BASE ** (-jnp.arange(0, RET_DK // 2, dtype=F32) * 2.0 / RET_DK)
    invf = inv_freq.reshape(RET_DK // 2, 1)
    pos_f32 = positions.astype(F32).reshape(batch, 1, seq)

    later = [w_in_b, w_out_b, w_kv]
    later_bf16 = None
    for la in range(n_a):
        x, casted = _layer_a(x, pos_f32, invf, g_norm_a, w_in_a_bf16, g_ret_head, w_out_a_bf16,
                             kt, mv, g_mem_q, la, la,
                             [flat(w) for w in later] if la == 0 else ())
        later_bf16 = later_bf16 or casted
    w_in_b_bf16, w_out_b_bf16, w_kv_bf16 = (c.reshape(w.shape) for c, w in zip(later_bf16, later))

    k_shared = v_shared = None
    for lb in range(n_b):
        layer = n_a + lb
        q, k_new, v_new, gate, mo = _proj_b(x, g_kv, g_norm_b, w_kv_bf16, w_in_b_bf16, kt, mv,
                                            g_mem_q, lb, layer)
        if lb == 0:
            k_shared, v_shared = k_new, v_new
        x = _sb_out(x, q, k_shared, v_shared, gate, mo, w_out_b_bf16, lb)
    return x
```

```python
import functools
import math

import numpy as np
import jax
import jax.numpy as jnp
from jax import lax
from jax.experimental import pallas as pl
from jax.experimental.pallas import tpu as pltpu

F32 = jnp.float32
BF16 = jnp.bfloat16

D_MODEL = 1024
RET_HEADS = 4
RET_DK = 128
RET_DV = 256
RET_QK_W = RET_HEADS * RET_DK
RET_V_W = RET_HEADS * RET_DV
ROPE_BASE = 10000.0
SB_HEADS = 8
SB_DH = 128
SB_W = SB_HEADS * SB_DH
MEM_HEADS = 4
MEM_DH = 128
MEM_W = MEM_HEADS * MEM_DH
EPS = 1e-6

RET_LOG_GAMMA = tuple(
    math.log(float(np.float32(1.0 - 2.0 ** (-5.0 - h)))) for h in range(RET_HEADS))

VMEM_LIMIT_BYTES = 48 * 1024 * 1024
VMEM_LIMIT_LARGE_BYTES = 58 * 1024 * 1024

ROW_TILE_A = 1024
RET_CHUNK = 256
IN_PIECE_A = 512
OUT_PIECE_A = 256
ROW_TILE_B = 1024
OUT_PIECE_B = 256
SB_TILE = 256
ROW_TILE_SB = 512
SB_LOG_WEIGHT_FLOOR = -152.0
SB_EXP2_CLAMP = 64.0
SB_MASKED_LOGIT = -1e30
LOG2E = math.log2(math.e)
SB_Q_SCALE = LOG2E * SB_DH ** -0.5


def _rms(x, g):
    return x * lax.rsqrt(jnp.mean(x * x, axis=-1, keepdims=True) + EPS) * g


def _silu(x):
    return x * (1.0 / (1.0 + jnp.exp(-x)))


def _dot(a, b):
    return jnp.dot(a, b, preferred_element_type=F32)


def _dot_nt(a, b):
    return lax.dot_general(a, b, (((1,), (1,)), ((), ())), preferred_element_type=F32)


def _const_spec(shape, index_map):
    return pl.BlockSpec(shape, index_map, pipeline_mode=pl.Buffered(1))


def _interleaved(*task_lists):
    tagged = [((i + 0.5) / len(tasks), k, task)
              for k, tasks in enumerate(task_lists) for i, task in enumerate(tasks)]
    return [task for _, _, task in sorted(tagged, key=lambda e: e[:2])]


def _memkv_kernel(*refs, n_cast):
    cast_in = refs[:n_cast]
    mem_ref, g_ref, w_ref, gk_ref = refs[n_cast:n_cast + 4]
    cast_out = refs[n_cast + 4:2 * n_cast + 4]
    kt_ref, v_ref = refs[2 * n_cast + 4:]
    for src, dst in zip(cast_in, cast_out):
        dst[...] = src[...].astype(BF16)

    mn = _rms(mem_ref[0], g_ref[0]).astype(BF16)
    kv = _dot(mn, w_ref[0].astype(BF16))
    gk = gk_ref[0]
    for h in range(MEM_HEADS):
        kh = _rms(kv[:, h * MEM_DH:(h + 1) * MEM_DH], gk)
        kt_ref[0, 0, h * MEM_DH:(h + 1) * MEM_DH, :] = kh.T.astype(BF16)
    v_ref[0, 0] = kv[:, MEM_W:].astype(BF16)


def _memory_kv(mem, g_mem, w_mem_kv, g_mem_k, later_weights=()):
    depth = g_mem.shape[0]
    batch, mem_len, _ = mem.shape
    n_steps = depth * batch
    cast_specs = [pl.BlockSpec((w.shape[0] // n_steps, w.shape[1]), lambda l, b: (l * batch + b, 0))
                  for w in later_weights]
    outs = pl.pallas_call(
        functools.partial(_memkv_kernel, n_cast=len(later_weights)),
        grid=(depth, batch),
        in_specs=cast_specs + [
            pl.BlockSpec((1, mem_len, D_MODEL), lambda l, b: (b, 0, 0)),
            pl.BlockSpec((1, 1, D_MODEL), lambda l, b: (l, 0, 0)),
            pl.BlockSpec((1, D_MODEL, 2 * MEM_W), lambda l, b: (l, 0, 0)),
            pl.BlockSpec((1, 1, MEM_DH), lambda l, b: (l, 0, 0)),
        ],
        out_specs=cast_specs + [
            pl.BlockSpec((1, 1, MEM_W, mem_len), lambda l, b: (l, b, 0, 0)),
            pl.BlockSpec((1, 1, mem_len, MEM_W), lambda l, b: (l, b, 0, 0)),
        ],
        out_shape=[jax.ShapeDtypeStruct(w.shape, BF16) for w in later_weights] + [
            jax.ShapeDtypeStruct((depth, batch, MEM_W, mem_len), BF16),
            jax.ShapeDtypeStruct((depth, batch, mem_len, MEM_W), BF16),
        ],
        compiler_params=pltpu.CompilerParams(
            dimension_semantics=("arbitrary", "arbitrary"),
            vmem_limit_bytes=VMEM_LIMIT_BYTES),
        name="memkv",
    )(*later_weights, mem, g_mem.reshape(depth, 1, D_MODEL), w_mem_kv,
      g_mem_k.reshape(depth, 1, MEM_DH))
    return outs[-2], outs[-1], outs[:-2]


def _mem_logits(h, memq, kt_ref, gq):
    qn = _rms(memq, gq).astype(BF16)
    return _dot(qn, kt_ref[0, 0, h * MEM_DH:(h + 1) * MEM_DH, :]) * (MEM_DH ** -0.5)


def _mem_values(h, logits, mv_ref):
    e = jnp.exp(logits - jnp.max(logits, axis=-1, keepdims=True))
    return _dot(e.astype(BF16), mv_ref[0, 0, :, h * MEM_DH:(h + 1) * MEM_DH]), \
        jnp.sum(e, axis=-1, keepdims=True)


def _mem_gated(values, memgate):
    o, denom = values
    return (o / denom * _silu(memgate)).astype(BF16)


def _layer_a_kernel(*refs, n_cast):
    n_in_refs = 10
    cast_in = refs[:n_cast]
    (x_ref, pos_ref, invf_ref, gn_ref, win_ref, gret_ref, wout_ref,
     kt_ref, mv_ref, gq_ref) = refs[n_cast:n_cast + n_in_refs]
    cast_out = refs[n_cast + n_in_refs:2 * n_cast + n_in_refs]
    o_ref, state_ref = refs[2 * n_cast + n_in_refs:]
    for src, dst in zip(cast_in, cast_out):
        dst[...] = src[...].astype(BF16)

    tile = x_ref.shape[1]
    chunk = RET_CHUNK
    n_chunks = tile // chunk
    half = RET_DK // 2
    n_in = win_ref.shape[2] // IN_PIECE_A
    n_out = wout_ref.shape[2] // OUT_PIECE_A
    o_v = 2 * RET_QK_W
    o_g = o_v + RET_V_W
    o_mq = o_g + RET_V_W

    @pl.when(pl.program_id(1) == 0)
    def _():
        state_ref[...] = jnp.zeros_like(state_ref)

    tables = {}
    states = [state_ref[h] for h in range(RET_HEADS)]
    xs, xns = {}, {}
    u = {}
    mixed = {}
    cats = {}

    def rotary_tables():
        ang_t = invf_ref[...] * pos_ref[0]
        cos_sin = jnp.concatenate([jnp.cos(ang_t), jnp.sin(ang_t)], axis=0).T
        sin_cos = pltpu.roll(cos_sin, half, 1)
        low_lanes = lax.broadcasted_iota(jnp.int32, (tile, RET_DK), 1) < half
        tables["cos"] = jnp.where(low_lanes, cos_sin, sin_cos)
        tables["sin"] = jnp.where(low_lanes, -sin_cos, cos_sin)

    def decay_tables(h):
        def task():
            lg = RET_LOG_GAMMA[h]
            rel = (lax.broadcasted_iota(jnp.int32, (chunk, chunk), 0)
                   - lax.broadcasted_iota(jnp.int32, (chunk, chunk), 1)).astype(F32)
            idx = lax.broadcasted_iota(jnp.int32, (chunk, 1), 0).astype(F32)
            tables[h] = (jnp.where(rel >= 0.0, jnp.exp(jnp.maximum(rel, 0.0) * lg), 0.0),
                         jnp.exp((idx + 1.0) * lg),
                         jnp.exp((chunk - 1.0 - idx) * lg))
        return task

    def normalize(c):
        xs[c] = x_ref[0, c * chunk:(c + 1) * chunk, :]
        xns[c] = _rms(xs[c], gn_ref[0]).astype(BF16)

    def in_piece(c, p):
        def task():
            u[c, p] = _dot(xns[c], win_ref[0, :, p * IN_PIECE_A:(p + 1) * IN_PIECE_A])
        return task

    def cols(c, lo, width):
        p, off = divmod(lo, IN_PIECE_A)
        return u[c, p][:, off:off + width]

    part = {}

    def retention_scores(c, h):
        def task():
            rows = slice(c * chunk, (c + 1) * chunk)
            cos_c, sin_c = tables["cos"][rows], tables["sin"][rows]

            def rotary(t):
                return t * cos_c + pltpu.roll(t, half, 1) * sin_c

            q = rotary(cols(c, h * RET_DK, RET_DK)).astype(BF16)
            k = rotary(cols(c, RET_QK_W + h * RET_DK, RET_DK)) * (RET_DK ** -0.5)
            part[c, h, "q"], part[c, h, "k"] = q, k
            part[c, h, "scores"] = _dot_nt(q, k.astype(BF16))
        return task

    def retention_values(c, h):
        def task():
            decay, cross_decay, state_decay = tables[h]
            q, k = part.pop((c, h, "q")), part.pop((c, h, "k"))
            v = cols(c, o_v + h * RET_DV, RET_DV).astype(BF16)
            intra = _dot((part.pop((c, h, "scores")) * decay).astype(BF16), v)
            cross = _dot(q, states[h].astype(BF16)) * cross_decay
            states[h] = (math.exp(chunk * RET_LOG_GAMMA[h]) * states[h]
                         + _dot((k * state_decay).T.astype(BF16), v))
            part[c, h, "ret"] = intra + cross
        return task

    def retention_gated(c, h):
        def task():
            ret = _rms(part.pop((c, h, "ret")), gret_ref[0])
            mixed[c, h] = (ret * _silu(cols(c, o_g + h * RET_DV, RET_DV))).astype(BF16)
        return task

    def memory_logits(c, h):
        def task():
            part[c, h, "logits"] = _mem_logits(h, cols(c, o_mq + h * MEM_DH, MEM_DH),
                                               kt_ref, gq_ref[0])
        return task

    def memory_values(c, h):
        def task():
            part[c, h, "values"] = _mem_values(h, part.pop((c, h, "logits")), mv_ref)
        return task

    def memory_gated(c, h):
        def task():
            mixed[c, RET_HEADS + h] = _mem_gated(part.pop((c, h, "values")),
                                                 cols(c, o_mq + MEM_W + h * MEM_DH, MEM_DH))
        return task

    def mixer_groups(c):
        groups = [[retention_scores(c, h) for h in range(RET_HEADS)],
                  [memory_logits(c, h) for h in range(MEM_HEADS)],
                  [retention_values(c, h) for h in range(RET_HEADS)],
                  [memory_values(c, h) for h in range(MEM_HEADS)],
                  [retention_gated(c, h) for h in range(RET_HEADS)]
                  + [memory_gated(c, h) for h in range(MEM_HEADS)]]

        def run(group):
            def task():
                for t in group:
                    t()
            return task
        return [run(g) for g in groups]

    def out_piece(c, j):
        def task():
            if c not in cats:
                cats[c] = jnp.concatenate(
                    [mixed.pop((c, s)) for s in range(RET_HEADS + MEM_HEADS)], axis=1)
            sl = slice(j * OUT_PIECE_A, (j + 1) * OUT_PIECE_A)
            o_ref[0, c * chunk:(c + 1) * chunk, sl] = xs[c][:, sl] + _dot(cats[c], wout_ref[0, :, sl])
        return task

    setup = [rotary_tables] + [decay_tables(h) for h in range(RET_HEADS)]
    for step in range(n_chunks + 2):
        stages = [setup] if step == 0 else []
        if step < n_chunks:
            normalize(step)
            stages.append([in_piece(step, p) for p in range(n_in)])
        if 0 <= step - 1 < n_chunks:
            stages.append(mixer_groups(step - 1))
        if 0 <= step - 2 < n_chunks:
            stages.append([out_piece(step - 2, j) for j in range(n_out)])
        for task in _interleaved(*stages):
            task()

    for h in range(RET_HEADS):
        state_ref[h] = states[h]


def _layer_a(x, pos_f32, invf, g_norm, w_in_bf16, g_ret_head, w_out_bf16, kt, mv, g_mem_q,
             la, layer, later_weights=()):
    batch, seq, _ = x.shape
    tile = ROW_TILE_A
    n_a = g_norm.shape[0]
    depth = g_mem_q.shape[0]
    a_in_w = w_in_bf16.shape[2]
    a_out_in = w_out_bf16.shape[1]
    mem_len = mv.shape[2]
    steps_per_batch = seq // tile
    n_steps = batch * steps_per_batch
    cast_specs = [pl.BlockSpec((w.shape[0] // n_steps, w.shape[1]),
                               lambda b, t: (b * steps_per_batch + t, 0)) for w in later_weights]
    outs = pl.pallas_call(
        functools.partial(_layer_a_kernel, n_cast=len(later_weights)),
        grid=(batch, steps_per_batch),
        in_specs=cast_specs + [
            pl.BlockSpec((1, tile, D_MODEL), lambda b, t: (b, t, 0)),
            pl.BlockSpec((1, 1, tile), lambda b, t: (b, 0, t)),
            _const_spec((RET_DK // 2, 1), lambda b, t: (0, 0)),
            _const_spec((1, 1, D_MODEL), lambda b, t: (la, 0, 0)),
            _const_spec((1, D_MODEL, a_in_w), lambda b, t: (la, 0, 0)),
            _const_spec((1, 1, RET_DV), lambda b, t: (la, 0, 0)),
            _const_spec((1, a_out_in, D_MODEL), lambda b, t: (la, 0, 0)),
            pl.BlockSpec((1, 1, MEM_W, mem_len), lambda b, t: (layer, b, 0, 0)),
            pl.BlockSpec((1, 1, mem_len, MEM_W), lambda b, t: (layer, b, 0, 0)),
            _const_spec((1, 1, MEM_DH), lambda b, t: (layer, 0, 0)),
        ],
        out_specs=cast_specs + [pl.BlockSpec((1, tile, D_MODEL), lambda b, t: (b, t, 0))],
        out_shape=[jax.ShapeDtypeStruct(w.shape, BF16) for w in later_weights]
        + [jax.ShapeDtypeStruct(x.shape, F32)],
        scratch_shapes=[pltpu.VMEM((RET_HEADS, RET_DK, RET_DV), F32)],
        compiler_params=pltpu.CompilerParams(
            dimension_semantics=("arbitrary", "arbitrary"),
            vmem_limit_bytes=VMEM_LIMIT_BYTES),
        name="layer_a",
    )(*later_weights, x, pos_f32, invf, g_norm.reshape(n_a, 1, D_MODEL), w_in_bf16,
      g_ret_head.reshape(n_a, 1, RET_DV), w_out_bf16, kt, mv, g_mem_q.reshape(depth, 1, MEM_DH))
    return outs[-1], outs[:-1]


def _proj_b_kernel(x_ref, gkv_ref, gnb_ref, wkv_ref, win_ref, kt_ref, mv_ref, gq_ref,
                   q_ref, k_ref, v_ref, gate_ref, mo_ref):
    x = x_ref[0]
    xr = x * lax.rsqrt(jnp.mean(x * x, axis=-1, keepdims=True) + EPS)
    xkv = (xr * gkv_ref[...]).astype(BF16)
    xb = (xr * gnb_ref[0]).astype(BF16)
    heads = range(MEM_HEADS)
    head = lambda t, h: t[:, h * MEM_DH:(h + 1) * MEM_DH]
    o_mq = 2 * SB_W
    memq = _dot(xb, win_ref[0, :, o_mq:o_mq + MEM_W])
    memgate = _dot(xb, win_ref[0, :, o_mq + MEM_W:])
    logits = [_mem_logits(h, head(memq, h), kt_ref, gq_ref[0]) for h in heads]
    gate_ref[0] = _silu(_dot(xb, win_ref[0, :, SB_W:o_mq])).astype(BF16)
    values = [_mem_values(h, logits[h], mv_ref) for h in heads]
    k_ref[0] = _dot(xkv, wkv_ref[:, :SB_W]).astype(BF16)
    mo_ref[0] = jnp.concatenate([_mem_gated(values[h], head(memgate, h)) for h in heads], axis=1)
    v_ref[0] = _dot(xkv, wkv_ref[:, SB_W:]).astype(BF16)
    q_ref[0] = (_dot(xb, win_ref[0, :, :SB_W]) * SB_Q_SCALE).astype(BF16)


def _proj_b(x, g_kv, g_norm_b, w_kv_bf16, w_in_bf16, kt, mv, g_mem_q, lb, layer):
    batch, seq, _ = x.shape
    tile = ROW_TILE_B
    n_b = g_norm_b.shape[0]
    depth = g_mem_q.shape[0]
    mem_len = mv.shape[2]
    b_in_w = w_in_bf16.shape[2]
    row_spec = lambda w: pl.BlockSpec((1, tile, w), lambda b, t: (b, t, 0))
    widths = (SB_W, SB_W, SB_W, SB_W, MEM_W)
    return pl.pallas_call(
        _proj_b_kernel,
        grid=(batch, seq // tile),
        in_specs=[
            row_spec(D_MODEL),
            _const_spec((1, D_MODEL), lambda b, t: (0, 0)),
            _const_spec((1, 1, D_MODEL), lambda b, t: (lb, 0, 0)),
            _const_spec((D_MODEL, 2 * SB_W), lambda b, t: (0, 0)),
            _const_spec((1, D_MODEL, b_in_w), lambda b, t: (lb, 0, 0)),
            pl.BlockSpec((1, 1, MEM_W, mem_len), lambda b, t: (layer, b, 0, 0)),
            pl.BlockSpec((1, 1, mem_len, MEM_W), lambda b, t: (layer, b, 0, 0)),
            _const_spec((1, 1, MEM_DH), lambda b, t: (layer, 0, 0)),
        ],
        out_specs=[row_spec(w) for w in widths],
        out_shape=[jax.ShapeDtypeStruct((batch, seq, w), BF16) for w in widths],
        compiler_params=pltpu.CompilerParams(
            dimension_semantics=("parallel", "parallel"),
            vmem_limit_bytes=VMEM_LIMIT_LARGE_BYTES),
        name="proj_b",
    )(x, g_kv.reshape(1, D_MODEL), g_norm_b.reshape(n_b, 1, D_MODEL), w_kv_bf16, w_in_bf16,
      kt, mv, g_mem_q.reshape(depth, 1, MEM_DH))


def _sb_out_kernel(x_ref, q_ref, kprev_ref, kcur_ref, vprev_ref, vcur_ref, k_hbm, v_hbm,
                   gate_ref, mo_ref, wout_ref, o_ref, cat_ref, kfar_ref, vfar_ref, far_sem, *,
                   n_steps):
    tile = SB_TILE
    tiles_per_step = q_ref.shape[1] // tile
    heads = q_ref.shape[2] // SB_DH
    batch_idx = pl.program_id(0)
    step = pl.program_id(1)

    row = lax.broadcasted_iota(jnp.int32, (tile, tile), 0)
    col = lax.broadcasted_iota(jnp.int32, (tile, tile), 1)
    causal = col < row
    suffix_mat = jnp.where(row > col, 1.0, 0.0).astype(BF16)


    def window_keys(prev_ref, cur_ref, far_ref, t, g, kind):
        hs = slice(g * SB_DH, (g + 1) * SB_DH)
        if kind == "far":
            return far_ref[:, hs]
        if kind == "diag":
            return cur_ref[0, t * tile:(t + 1) * tile, hs]
        if t == 0:
            return jnp.concatenate([prev_ref[0, :, hs], cur_ref[0, :tile, hs]], axis=0)
        return cur_ref[0, (t - 1) * tile:(t + 1) * tile, hs]

    def stage_scores(t, g, kind):
        nblk = 2 if kind == "near" else 1
        q = q_ref[0, t * tile:(t + 1) * tile, g * SB_DH:(g + 1) * SB_DH]
        z = _dot_nt(q, window_keys(kprev_ref, kcur_ref, kfar_ref, t, g, kind))
        if kind != "far":
            diag = jnp.where(causal, z[:, -tile:], SB_MASKED_LOGIT)
            z = diag if nblk == 1 else jnp.concatenate([z[:, :-tile], diag], axis=1)
        sp = jnp.maximum(z, jnp.log(1.0 + jnp.exp2(jnp.minimum(z, SB_EXP2_CLAMP))) * LOG2E)
        blocks = [sp[:, j * tile:(j + 1) * tile] for j in range(nblk)]
        return z, blocks, jnp.concatenate([blk.astype(BF16) for blk in blocks], axis=0)

    def stage_weights(scores, kind, penalty):
        z, blocks, sp_bf16 = scores
        nblk = len(blocks)
        suffix = _dot(sp_bf16, suffix_mat)
        weights = [None] * nblk
        for j in reversed(range(nblk)):
            sfx = suffix[j * tile:(j + 1) * tile]
            log_a = (z[:, j * tile:(j + 1) * tile] - blocks[j]) - sfx
            if penalty is not None:
                log_a = log_a - penalty
            weights[j] = jnp.exp2(log_a).astype(BF16)
            total = sfx[:, 0:1] + blocks[j][:, 0:1]
            penalty = total if penalty is None else penalty + total
        return jnp.concatenate(weights, axis=1), penalty

    def stage_values(weights, t, g, kind):
        return _dot(weights, window_keys(vprev_ref, vcur_ref, vfar_ref, t, g, kind))

    def run_windows(wins, penalties=None, other_tasks=()):
        n = len(wins)
        scores, weights, out = {}, {}, [None] * n

        def stagger(s):
            def task():
                if s < n:
                    scores[s] = stage_scores(*wins[s])
                w = s - 1
                if 0 <= w < n:
                    weights[w] = stage_weights(scores.pop(w), wins[w][2],
                                               None if penalties is None else penalties[w])
                w = s - 2
                if 0 <= w < n:
                    a, penalty = weights.pop(w)
                    out[w] = (stage_values(a, *wins[w]), penalty)
            return task

        steps = [stagger(s) for s in range(n + 2)]
        others = list(other_tasks)
        n_front = (len(others) + 1) // 2
        for task in steps[:1] + others[:n_front] + steps[1:-1] + others[n_front:] + steps[-1:]:
            task()
        return out

    cur = lax.rem(step, 2)
    n_out = o_ref.shape[2] // OUT_PIECE_B

    def out_piece(j):
        def task():
            sl = slice(j * OUT_PIECE_B, (j + 1) * OUT_PIECE_B)
            o_ref[0, :, sl] = x_ref[0, :, sl] + _dot(cat_ref[1 - cur], wout_ref[0, :, sl])
        return task

    out_tasks = [out_piece(j) for j in range(n_out)]

    def store(t, accs):
        rows = slice(t * tile, (t + 1) * tile)
        for g in range(heads):
            hs = slice(g * SB_DH, (g + 1) * SB_DH)
            cat_ref[cur, rows, hs] = (accs[g] * gate_ref[0, rows, hs].astype(F32)).astype(BF16)

    def log_weight_bound(pens):
        return -functools.reduce(jnp.minimum, [jnp.min(p) for p in pens])

    def fetch_far(start):
        copies = [pltpu.make_async_copy(src.at[batch_idx, pl.ds(start, tile), :], dst, far_sem.at[n])
                  for n, (src, dst) in enumerate(((k_hbm, kfar_ref), (v_hbm, vfar_ref)))]
        for c in copies:
            c.start()
        for c in copies:
            c.wait()

    def finish_tile(t, results):
        accs = [r[0] for r in results]
        penalties = [r[1] for r in results]
        store(t, accs)
        tile_idx = step * tiles_per_step + t
        bound = log_weight_bound(penalties)

        def cond(carry):
            return jnp.logical_and(carry[0] <= tile_idx, carry[1] > SB_LOG_WEIGHT_FLOOR)

        def body(carry):
            n, _, accs, pens = carry
            fetch_far(pl.multiple_of((tile_idx - n) * tile, tile))
            far = run_windows([(t, g, "far") for g in range(heads)], pens)
            accs = [accs[g] + far[g][0] for g in range(heads)]
            pens = [far[g][1] for g in range(heads)]
            return n + 1, log_weight_bound(pens), accs, pens

        @pl.when(jnp.logical_and(tile_idx >= 2, bound > SB_LOG_WEIGHT_FLOOR))
        def _():
            _, _, accs_far, _ = lax.while_loop(cond, body, (jnp.int32(2), bound, accs, penalties))
            store(t, accs_far)

    def attention(first_kind, other_tasks):
        wins = [(t, g, first_kind if t == 0 else "near")
                for t in range(tiles_per_step) for g in range(heads)]
        results = run_windows(wins, None, other_tasks)
        cat_ref[cur, :, SB_W:] = mo_ref[0]
        for t in range(tiles_per_step):
            finish_tile(t, results[t * heads:(t + 1) * heads])

    @pl.when(step == 0)
    def _():
        attention("diag", ())

    @pl.when(jnp.logical_and(step > 0, step < n_steps))
    def _():
        attention("near", out_tasks)

    @pl.when(step == n_steps)
    def _():
        for task in out_tasks:
            task()


def _sb_out(x, q, k, v, gate, mo, w_out_bf16, lb):
    batch, seq, _ = x.shape
    rows = ROW_TILE_SB
    n_steps = seq // rows
    tiles_per_step = rows // SB_TILE
    b_out_in = w_out_bf16.shape[1]
    attn_idx = lambda i: jnp.minimum(i, n_steps - 1)
    attn_spec = lambda w: pl.BlockSpec((1, rows, w), lambda b, i: (b, attn_idx(i), 0))
    proj_spec = lambda w: pl.BlockSpec((1, rows, w), lambda b, i: (b, jnp.maximum(i - 1, 0), 0))
    prev_spec = pl.BlockSpec(
        (1, SB_TILE, SB_W), lambda b, i: (b, jnp.maximum(attn_idx(i) * tiles_per_step - 1, 0), 0))
    any_spec = pl.BlockSpec(memory_space=pl.ANY)
    return pl.pallas_call(
        functools.partial(_sb_out_kernel, n_steps=n_steps),
        grid=(batch, n_steps + 1),
        in_specs=[
            proj_spec(D_MODEL), attn_spec(SB_W), prev_spec, attn_spec(SB_W), prev_spec,
            attn_spec(SB_W), any_spec, any_spec, attn_spec(SB_W), attn_spec(MEM_W),
            _const_spec((1, b_out_in, D_MODEL), lambda b, i: (lb, 0, 0)),
        ],
        out_specs=proj_spec(D_MODEL),
        out_shape=jax.ShapeDtypeStruct(x.shape, F32),
        scratch_shapes=[pltpu.VMEM((2, rows, b_out_in), BF16),
                        pltpu.VMEM((SB_TILE, SB_W), BF16),
                        pltpu.VMEM((SB_TILE, SB_W), BF16),
                        pltpu.SemaphoreType.DMA((2,))],
        compiler_params=pltpu.CompilerParams(
            dimension_semantics=("arbitrary", "arbitrary"),
            vmem_limit_bytes=VMEM_LIMIT_BYTES),
        name="sb_out",
    )(x, q, k, k, v, v, k, v, gate, mo, w_out_bf16)


def kernel(x, mem, positions, g_norm_a, w_in_a, g_ret_head, w_out_a, g_kv, w_kv, g_norm_b, w_in_b,
           w_out_b, g_mem, w_mem_kv, g_mem_q, g_mem_k):
    n_a = g_norm_a.shape[0]
    n_b = g_norm_b.shape[0]
    batch, seq, d_model = x.shape
    assert d_model == D_MODEL and n_a >= 1 and n_b >= 1
    assert seq % ROW_TILE_A == 0 and seq % ROW_TILE_B == 0 and seq % ROW_TILE_SB == 0
    assert ROW_TILE_A % RET_CHUNK == 0 and ROW_TILE_SB % SB_TILE == 0

    flat = lambda w: w.reshape(-1, w.shape[-1])
    kt, mv, casted = _memory_kv(mem, g_mem, w_mem_kv, g_mem_k, [flat(w_in_a), flat(w_out_a)])
    w_in_a_bf16, w_out_a_bf16 = (c.reshape(w.shape) for c, w in zip(casted, (w_in_a, w_out_a)))

    inv_freq = ROPE_BASE ** (-jnp.arange(0, RET_DK // 2, dtype=F32) * 2.0 / RET_DK)
    invf = inv_freq.reshape(RET_DK // 2, 1)
    pos_f32 = positions.astype(F32).reshape(batch, 1, seq)

    later = [w_in_b, w_out_b, w_kv]
    later_bf16 = None
    for la in range(n_a):
        x, casted = _layer_a(x, pos_f32, invf, g_norm_a, w_in_a_bf16, g_ret_head, w_out_a_bf16,
                             kt, mv, g_mem_q, la, la,
                             [flat(w) for w in later] if la == 0 else ())
        later_bf16 = later_bf16 or casted
    w_in_b_bf16, w_out_b_bf16, w_kv_bf16 = (c.reshape(w.shape) for c, w in zip(later_bf16, later))

    k_shared = v_shared = None
    for lb in range(n_b):
        layer = n_a + lb
        q, k_new, v_new, gate, mo = _proj_b(x, g_kv, g_norm_b, w_kv_bf16, w_in_b_bf16, kt, mv,
                                            g_mem_q, lb, layer)
        if lb == 0:
            k_shared, v_shared = k_new, v_new
        x = _sb_out(x, q, k_shared, v_shared, gate, mo, w_out_b_bf16, lb)
    return x
```
